```python
import jax
import jax.numpy as jnp
from jax import lax
import numpy as np

D_MODEL = 1024
BATCH = 16
SEQ = 2048
DEPTH = 2

GRID_W = 64
CTX_LEN = 256

RET_HEADS = 4
RET_DK = 128
RET_DV = 128
RET_CHUNK = 128
NA_HEADS = 8
NA_DH = 64
NA_WIN_R = 8
NA_WIN_C = 16
NA_QB = NA_WIN_C
NA_KBW = 2 * NA_WIN_C
POOL_SIZES = (2, 4, 8, 16)
POOL_GROUP = D_MODEL // len(POOL_SIZES)
MOE_GROUPS = 4
MOE_PER_GROUP = 8
MOE_EXPERTS = MOE_GROUPS * MOE_PER_GROUP
MOE_HIDDEN = D_MODEL // 2
MOE_TOPK = 2

ROPE_BASE = 10000.0
LN_EPS = 1e-5
N_MOD = 6
DEEPNORM_ALPHA = (2 * DEPTH) ** 0.25
DEEPNORM_BETA = (8 * DEPTH) ** -0.25

RET_QKW = RET_HEADS * RET_DK
RET_VW = RET_HEADS * RET_DV
NA_W = NA_HEADS * NA_DH
MIX_W = RET_VW + NA_W
OFF_RK = 0
OFF_RV = OFF_RK + RET_QKW
OFF_NK = OFF_RV + RET_VW
OFF_NV = OFF_NK + NA_W
KV_COLS = OFF_NV + NA_W
OFF_RQ = KV_COLS
OFF_RG = OFF_RQ + RET_QKW
OFF_NQ = OFF_RG + RET_VW
IN_COLS = OFF_NQ + NA_W

kernel_name = 'hybrid_retention_natten_pool_hmoe_dit'


def _layer_norm(x, g, b):
    xf = x.astype(jnp.float32)
    mu = jnp.mean(xf, -1, keepdims=True)
    var = jnp.mean(jnp.square(xf - mu), -1, keepdims=True)
    return ((xf - mu) * lax.rsqrt(var + LN_EPS)).astype(x.dtype) * g + b


def _post_norm(h, y, g, b):
    return _layer_norm(DEEPNORM_ALPHA * h + y, g, b)


def _modulation(cvec, w_mod, b_mod):
    m = jax.nn.silu(cvec) @ w_mod + b_mod
    return [p[..., None, :] for p in jnp.split(m, N_MOD, axis=-1)]


def _modulate(h, shift, scale):
    return h * (1 + scale) + shift


def _split_heads(t, n_heads):
    b, n, _ = t.shape
    return jnp.transpose(t.reshape(b, n, n_heads, -1), (0, 2, 1, 3))


def _merge_heads(t):
    b, h, n, d = t.shape
    return jnp.transpose(t, (0, 2, 1, 3)).reshape(b, n, h * d)


def _axial_rope(n_tokens, head_dim):
    t = jnp.arange(n_tokens)
    rows = (t // GRID_W).astype(jnp.float32)
    cols = (t % GRID_W).astype(jnp.float32)
    n_freq = head_dim // 4
    inv_freq = ROPE_BASE ** (-jnp.arange(n_freq, dtype=jnp.float32) / n_freq)
    ang = jnp.concatenate([rows[:, None] * inv_freq, cols[:, None] * inv_freq], axis=-1)
    return jnp.cos(ang), jnp.sin(ang)


def _apply_rope(x, cos, sin):
    half = x.shape[-1] // 2
    x1, x2 = x[..., :half], x[..., half:]
    cos = cos.astype(x.dtype)
    sin = sin.astype(x.dtype)
    return jnp.concatenate([x1 * cos - x2 * sin, x1 * sin + x2 * cos], axis=-1)


def _retention_chunked(q, k, v, log_gamma, s0):
    b, h, n, _ = q.shape
    dv = v.shape[-1]
    nc = n // RET_CHUNK
    idx = jnp.arange(RET_CHUNK, dtype=jnp.float32)
    lg = log_gamma[:, None]
    diff = idx[:, None] - idx[None, :]
    intra = jnp.where(diff >= 0, jnp.exp(lg[..., None] * jnp.maximum(diff, 0.0)), 0.0)
    q_dec = jnp.exp(lg * (idx + 1.0))[..., None]
    k_dec = jnp.exp(lg * (RET_CHUNK - 1.0 - idx))[..., None]
    chunk_dec = jnp.exp(log_gamma * RET_CHUNK)[:, None, None]

    def to_chunks(t):
        return jnp.moveaxis(t.astype(jnp.float32).reshape(b, h, nc, RET_CHUNK, t.shape[-1]), 2, 0)

    def step(state, blk):
        qi, ki, vi = blk
        att = jnp.einsum('bhid,bhjd->bhij', qi, ki) * intra
        o = (jnp.einsum('bhij,bhje->bhie', att, vi)
             + jnp.einsum('bhid,bhde->bhie', qi, state) * q_dec)
        state = state * chunk_dec + jnp.einsum('bhjd,bhje->bhde', ki * k_dec, vi)
        return state, o

    state, o = lax.scan(step, s0, (to_chunks(q), to_chunks(k), to_chunks(v)))
    return jnp.moveaxis(o, 0, 2).reshape(b, h, n, dv), state


def _bidir_retention(q, k, v, log_gamma2, s0_fwd, s0_bwd):
    o_f, s_f = _retention_chunked(q, k, v, log_gamma2[0], s0_fwd)
    flip = lambda t: jnp.flip(t, axis=2)
    o_b, s_b = _retention_chunked(flip(q), flip(k), flip(v), log_gamma2[1], s0_bwd)
    return o_f + flip(o_b), s_f, s_b


def _retention_final_states(k, v, log_gamma2):
    n = k.shape[2]
    pos = jnp.arange(n, dtype=jnp.float32)
    w_f = jnp.exp(log_gamma2[0][:, None] * (n - 1.0 - pos))
    w_b = jnp.exp(log_gamma2[1][:, None] * pos)
    kf = k.astype(jnp.float32)
    vf = v.astype(jnp.float32)
    s_f = jnp.einsum('bhld,hl,bhle->bhde', kf, w_f, vf)
    s_b = jnp.einsum('bhld,hl,bhle->bhde', kf, w_b, vf)
    return s_f, s_b


def _head_group_norm(o):
    mu = jnp.mean(o, -1, keepdims=True)
    var = jnp.mean(jnp.square(o - mu), -1, keepdims=True)
    return (o - mu) * lax.rsqrt(var + LN_EPS)


def _neighbourhood_attention(q, k, v, k_ctx, v_ctx, rpb):
    b, h, n, d = q.shape
    rows = n // GRID_W
    wr = min(NA_WIN_R, rows)
    ncb = GRID_W // NA_QB
    qcol = np.arange(ncb)[:, None] * NA_QB + np.arange(NA_QB)[None, :]
    kstart = np.clip(np.arange(ncb) * NA_QB - NA_WIN_C // 2, 0, GRID_W - NA_KBW)
    kcol = kstart[:, None] + np.arange(NA_KBW)[None, :]
    wstart = np.clip(qcol - NA_WIN_C // 2, 0, GRID_W - NA_WIN_C)[..., None]
    col_ok = (kcol[:, None, :] >= wstart) & (kcol[:, None, :] < wstart + NA_WIN_C)
    col_bias_idx = np.clip(kcol[:, None, :] - qcol[..., None] + NA_WIN_C - 1, 0, 2 * NA_WIN_C - 2)
    qg = (q * NA_DH ** -0.5).reshape(b, h, rows, GRID_W, d)
    kg = k.reshape(b, h, rows, GRID_W, d)
    vg = v.reshape(b, h, rows, GRID_W, d)
    n_win = wr * NA_KBW

    def row_block(r):
        r0 = jnp.clip(r - wr // 2, 0, rows - wr)
        k_blk = lax.dynamic_slice_in_dim(kg, r0, wr, axis=2)[:, :, :, kcol]
        v_blk = lax.dynamic_slice_in_dim(vg, r0, wr, axis=2)[:, :, :, kcol]
        q_r = lax.dynamic_index_in_dim(qg, r, axis=2, keepdims=False).reshape(b, h, ncb, NA_QB, d)
        row_bias_idx = r0 + jnp.arange(wr) - r + NA_WIN_R - 1
        bias = rpb[:, row_bias_idx[:, None, None, None], col_bias_idx[None]]
        s_win = jnp.einsum('bhcqd,bhrckd->bhcqrk', q_r, k_blk).astype(jnp.float32)
        s_win = s_win + jnp.transpose(bias, (0, 2, 3, 1, 4)).astype(jnp.float32)
        s_win = jnp.where(col_ok[:, :, None, :], s_win, -jnp.inf)
        s_ctx = jnp.einsum('bhcqd,bhld->bhcql', q_r, k_ctx).astype(jnp.float32)
        logits = jnp.concatenate([s_win.reshape(b, h, ncb, NA_QB, n_win), s_ctx], axis=-1)
        p = jax.nn.softmax(logits, axis=-1).astype(v.dtype)
        p_win = p[..., :n_win].reshape(b, h, ncb, NA_QB, wr, NA_KBW)
        o = (jnp.einsum('bhcqrk,bhrckd->bhcqd', p_win, v_blk)
             + jnp.einsum('bhcql,bhld->bhcqd', p[..., n_win:], v_ctx))
        return o.reshape(b, h, GRID_W, d)

    o = lax.map(row_block, jnp.arange(rows))
    return jnp.transpose(o, (1, 2, 0, 3, 4)).reshape(b, h, n, d)


def _context_attention(q, k, v):
    s = jnp.einsum('bhqd,bhkd->bhqk', q, k).astype(jnp.float32) * NA_DH ** -0.5
    p = jax.nn.softmax(s, axis=-1).astype(v.dtype)
    return jnp.einsum('bhqk,bhkd->bhqd', p, v)


def _mixer_retention_na(h, hc, w_in, w_out, log_decay, rpb, update_ctx):
    b, n, _ = h.shape
    dtype = h.dtype
    log_gamma2 = jnp.log1p(-jnp.exp(log_decay.astype(jnp.float32)))
    cols = lambda t, off, width: t[..., off:off + width]
    p = h @ w_in
    pc = hc @ (w_in if update_ctx else w_in[:, :KV_COLS])
    rk_c = _split_heads(cols(pc, OFF_RK, RET_QKW), RET_HEADS)
    rv_c = _split_heads(cols(pc, OFF_RV, RET_VW), RET_HEADS)
    nk_c = _split_heads(cols(pc, OFF_NK, NA_W), NA_HEADS)
    nv_c = _split_heads(cols(pc, OFF_NV, NA_W), NA_HEADS)
    cos, sin = _axial_rope(n, RET_DK)
    rq = _apply_rope(_split_heads(cols(p, OFF_RQ, RET_QKW), RET_HEADS), cos, sin) * RET_DK ** -0.5
    rk = _apply_rope(_split_heads(cols(p, OFF_RK, RET_QKW), RET_HEADS), cos, sin)
    rv = _split_heads(cols(p, OFF_RV, RET_VW), RET_HEADS)
    rg = cols(p, OFF_RG, RET_VW)
    nq = _split_heads(cols(p, OFF_NQ, NA_W), NA_HEADS)
    nk = _split_heads(cols(p, OFF_NK, NA_W), NA_HEADS)
    nv = _split_heads(cols(p, OFF_NV, NA_W), NA_HEADS)
    if update_ctx:
        rq_c = _split_heads(cols(pc, OFF_RQ, RET_QKW), RET_HEADS) * RET_DK ** -0.5
        zeros = jnp.zeros((b, RET_HEADS, RET_DK, RET_DV), jnp.float32)
        o_ret_c, s_f, s_b = _bidir_retention(rq_c, rk_c, rv_c, log_gamma2, zeros, zeros)
    else:
        s_f, s_b = _retention_final_states(rk_c, rv_c, log_gamma2)
    o_ret, _, _ = _bidir_retention(rq, rk, rv, log_gamma2, s_f, s_b)
    y_ret = jax.nn.silu(rg) * _merge_heads(_head_group_norm(o_ret)).astype(dtype)
    y_na = _merge_heads(_neighbourhood_attention(nq, nk, nv, nk_c, nv_c, rpb))
    y = jnp.concatenate([y_ret, y_na], axis=-1) @ w_out
    if not update_ctx:
        return y, None
    rg_c = cols(pc, OFF_RG, RET_VW)
    nq_c = _split_heads(cols(pc, OFF_NQ, NA_W), NA_HEADS)
    y_ret_c = jax.nn.silu(rg_c) * _merge_heads(_head_group_norm(o_ret_c)).astype(dtype)
    y_na_c = _merge_heads(_context_attention(nq_c, nk_c, nv_c))
    yc = jnp.concatenate([y_ret_c, y_na_c], axis=-1) @ w_out
    return y, yc


def _pool_mixer(h, pool_w, pool_scale):
    b, n, d = h.shape
    hf = h.astype(jnp.float32).reshape(b, n, len(POOL_SIZES), POOL_GROUP)
    csum = jnp.pad(jnp.cumsum(hf, axis=1), ((0, 0), (1, 0), (0, 0), (0, 0)))
    pos = jnp.arange(n)
    outs = []
    for g, w in enumerate(POOL_SIZES):
        lo = jnp.clip(pos - w // 2, 0, n)
        hi = jnp.clip(pos + (w - w // 2), 0, n)
        cg = csum[:, :, g]
        mean = (cg[:, hi] - cg[:, lo]) / (hi - lo).astype(jnp.float32)[None, :, None]
        outs.append(mean - hf[:, :, g])
    z = jnp.stack(outs, axis=2).astype(h.dtype)
    y = jnp.einsum('bngc,gce->bnge', z, pool_w).reshape(b, n, d)
    return y * pool_scale


def _hier_moe(h, w_r1, b_r1, w_r2, b_r2, w_gate, w_up, w_down):
    b, n, d = h.shape
    t = h.reshape(b * n, d)
    logit_g = (t @ w_r1).astype(jnp.float32) + b_r1.astype(jnp.float32)
    grp = jnp.argmax(logit_g, axis=-1)
    gate_g = jnp.take_along_axis(jax.nn.softmax(logit_g, axis=-1), grp[:, None], axis=-1)
    logit_e = jnp.einsum('td,gde->tge', t, w_r2).astype(jnp.float32) + b_r2.astype(jnp.float32)
    logit_e = jnp.take_along_axis(logit_e, grp[:, None, None], axis=1)[:, 0]
    top_v, top_i = lax.top_k(logit_e, MOE_TOPK)
    w_sel = jax.nn.softmax(top_v, axis=-1) * gate_g
    eid = grp[:, None] * MOE_PER_GROUP + top_i
    combine = jnp.einsum('tk,tke->te', w_sel, jax.nn.one_hot(eid, MOE_EXPERTS, dtype=jnp.float32))

    def expert_step(acc, xs):
        wg, wu, wd, gcol = xs
        y = (jax.nn.silu(t @ wg) * (t @ wu)) @ wd
        return acc + y.astype(jnp.float32) * gcol[:, None], None

    acc, _ = lax.scan(expert_step, jnp.zeros((b * n, d), jnp.float32),
                      (w_gate, w_up, w_down, combine.T))
    return acc.astype(h.dtype).reshape(b, n, d)


def setup_inputs(seed: int = 0) -> dict:
    key = jax.random.key(seed)
    ks = jax.random.split(key, 21)
    f32 = jnp.float32

    def nrm(k, shape, scale):
        return jax.random.normal(k, shape, f32) * scale

    n_ab = (DEPTH + 1) // 2
    n_pool = DEPTH // 2
    base_log_decay = -(5.0 + jnp.arange(RET_HEADS, dtype=f32)) * jnp.log(2.0)
    return {
        'x': nrm(ks[0], (BATCH, SEQ, D_MODEL), 1.0),
        'c': nrm(ks[1], (BATCH, D_MODEL), 1.0),
        'ctx': nrm(ks[2], (BATCH, CTX_LEN, D_MODEL), 1.0),
        'c_ctx': nrm(ks[3], (D_MODEL,), 1.0),
        'w_mod': nrm(ks[4], (DEPTH, D_MODEL, N_MOD * D_MODEL), 0.5 * D_MODEL ** -0.5),
        'b_mod': nrm(ks[5], (DEPTH, N_MOD * D_MODEL), 0.02),
        'ln_g': 1.0 + nrm(ks[6], (DEPTH, 2, D_MODEL), 0.02),
        'ln_b': nrm(ks[7], (DEPTH, 2, D_MODEL), 0.02),
        'ab_w_in': nrm(ks[8], (n_ab, D_MODEL, IN_COLS), D_MODEL ** -0.5),
        'ab_w_out': nrm(ks[9], (n_ab, MIX_W, D_MODEL), DEEPNORM_BETA * MIX_W ** -0.5),
        'ab_log_decay': base_log_decay + nrm(ks[10], (n_ab, 2, RET_HEADS), 0.05),
        'ab_rpb': nrm(ks[11], (n_ab, NA_HEADS, 2 * NA_WIN_R - 1, 2 * NA_WIN_C - 1), 0.1),
        'pool_w': nrm(ks[12], (n_pool, len(POOL_SIZES), POOL_GROUP, POOL_GROUP), DEEPNORM_BETA * POOL_GROUP ** -0.5),
        'pool_scale': 1.0 + nrm(ks[13], (n_pool, D_MODEL), 0.1),
        'moe_w_r1': nrm(ks[14], (DEPTH, D_MODEL, MOE_GROUPS), D_MODEL ** -0.5),
        'moe_b_r1': nrm(ks[15], (DEPTH, MOE_GROUPS), 0.01),
        'moe_w_r2': nrm(ks[16], (DEPTH, MOE_GROUPS, D_MODEL, MOE_PER_GROUP), D_MODEL ** -0.5),
        'moe_b_r2': nrm(ks[17], (DEPTH, MOE_GROUPS, MOE_PER_GROUP), 0.01),
        'moe_w_gate': nrm(ks[18], (DEPTH, MOE_EXPERTS, D_MODEL, MOE_HIDDEN), D_MODEL ** -0.5),
        'moe_w_up': nrm(ks[19], (DEPTH, MOE_EXPERTS, D_MODEL, MOE_HIDDEN), D_MODEL ** -0.5),
        'moe_w_down': nrm(ks[20], (DEPTH, MOE_EXPERTS, MOE_HIDDEN, D_MODEL), DEEPNORM_BETA * MOE_HIDDEN ** -0.5),
    }


def reference(x, c, ctx, c_ctx, w_mod, b_mod, ln_g, ln_b, ab_w_in, ab_w_out, ab_log_decay, ab_rpb,
              pool_w, pool_scale, moe_w_r1, moe_b_r1, moe_w_r2, moe_b_r2, moe_w_gate, moe_w_up, moe_w_down):
    h, hc = x, ctx
    for i in range(DEPTH):
        j = i // 2
        carry_ctx = any(later % 2 == 0 for later in range(i + 1, DEPTH))
        moe_p = (moe_w_r1[i], moe_b_r1[i], moe_w_r2[i], moe_b_r2[i], moe_w_gate[i], moe_w_up[i], moe_w_down[i])
        m = _modulation(c, w_mod[i], b_mod[i])
        if i % 2 == 0 or carry_ctx:
            mc = _modulation(c_ctx, w_mod[i], b_mod[i])
        if i % 2 == 0:
            y, yc = _mixer_retention_na(_modulate(h, m[0], m[1]), _modulate(hc, mc[0], mc[1]),
                                        ab_w_in[j], ab_w_out[j], ab_log_decay[j], ab_rpb[j], carry_ctx)
        else:
            y = _pool_mixer(_modulate(h, m[0], m[1]), pool_w[j], pool_scale[j])
            if carry_ctx:
                yc = _pool_mixer(_modulate(hc, mc[0], mc[1]), pool_w[j], pool_scale[j])
        h = _post_norm(h, m[2] * y, ln_g[i, 0], ln_b[i, 0])
        h = _post_norm(h, m[5] * _hier_moe(_modulate(h, m[3], m[4]), *moe_p), ln_g[i, 1], ln_b[i, 1])
        if carry_ctx:
            hc = _post_norm(hc, mc[2] * yc, ln_g[i, 0], ln_b[i, 0])
            hc = _post_norm(hc, mc[5] * _hier_moe(_modulate(hc, mc[3], mc[4]), *moe_p), ln_g[i, 1], ln_b[i, 1])
    return h
```

```python
import functools

import numpy as np
import jax
import jax.numpy as jnp
from jax import lax
from jax.experimental import pallas as pl
from jax.experimental.pallas import tpu as pltpu

F32 = jnp.float32
BF16 = jnp.bfloat16
HIGHEST = lax.Precision.HIGHEST

D_MODEL = 1024
DEPTH = 2
GRID_W = 64
RET_HEADS = 4
RET_DK = 128
RET_CHUNK = 128
NA_HEADS = 8
NA_DH = 64
NA_WIN_R = 8
NA_WIN_C = 16
POOL_SIZES = (2, 4, 8, 16)
POOL_GROUP = D_MODEL // len(POOL_SIZES)
MOE_GROUPS = 4
MOE_PER_GROUP = 8
MOE_EXPERTS = MOE_GROUPS * MOE_PER_GROUP
MOE_HIDDEN = D_MODEL // 2
ROPE_BASE = 10000.0
LN_EPS = 1e-5
N_MOD = 6
DEEPNORM_ALPHA = (2 * DEPTH) ** 0.25
HEAD_W = 512
N_IN_GROUPS = 7
N_KV_GROUPS = 4
MASK_VALUE = -1e30

LANES = 128
VMEM_LIMIT = 56 * 1024 * 1024

NA_ROWS_PER_BLOCK = 4
ROW_TILE = 512
MOE_TILE = 256
MOD_ROWS = 24
MOD_COL_TILE = 1536


def _cparams(*sem):
    return pltpu.CompilerParams(dimension_semantics=sem, vmem_limit_bytes=VMEM_LIMIT)


def _silu(v):
    return v / (1.0 + jnp.exp(-v))


def _dot(a, b):
    return jnp.dot(a, b, preferred_element_type=F32)


def _dot_nt(a, b):
    return lax.dot_general(a, b, (((1,), (1,)), ((), ())), preferred_element_type=F32)


def _dot_tn(a, b):
    return lax.dot_general(a, b, (((0,), (0,)), ((), ())), preferred_element_type=F32)


def _mod_kernel(c_ref, w_ref, b_ref, o_ref):
    s = _silu(c_ref[...])
    o_ref[0] = jnp.dot(s, w_ref[0], precision=HIGHEST, preferred_element_type=F32) + b_ref[0]


def _modulation(cc, w_mod, b_mod):
    depth, d, n = w_mod.shape
    return pl.pallas_call(
        _mod_kernel,
        out_shape=jax.ShapeDtypeStruct((depth, MOD_ROWS, n), F32),
        grid=(depth, n // MOD_COL_TILE),
        in_specs=[
            pl.BlockSpec((MOD_ROWS, d), lambda i, j: (0, 0)),
            pl.BlockSpec((1, d, MOD_COL_TILE), lambda i, j: (i, 0, j)),
            pl.BlockSpec((1, 1, MOD_COL_TILE), lambda i, j: (i, 0, j)),
        ],
        out_specs=pl.BlockSpec((1, MOD_ROWS, MOD_COL_TILE), lambda i, j: (i, 0, j)),
        compiler_params=_cparams("arbitrary", "arbitrary"),
        name="modulation",
    )(cc, w_mod, b_mod.reshape(depth, 1, n))


def _rope(v, cos2, sin2):
    return v * cos2 + pltpu.roll(v, RET_DK // 2, axis=1) * sin2


def _in_proj_kernel(x_ref, m_ref, w_ref, cq_ref, sq_ref, ck_ref, sk_ref,
                    rk_ref, rv_ref, nk_ref, nv_ref, rq_ref, rg_ref, nq_ref):
    shift = m_ref[0, 0:1, :]
    scale = m_ref[0, 1:2, :]
    hm = (x_ref[0] * (1.0 + scale) + shift).astype(BF16)
    outs = (rk_ref, rv_ref, nk_ref, nv_ref, rq_ref, rg_ref, nq_ref)
    for g, o_ref in enumerate(outs):
        p = _dot(hm, w_ref[:, g * HEAD_W:(g + 1) * HEAD_W])
        if g == 0 or g == 4:
            cos2 = (ck_ref if g == 0 else cq_ref)[...]
            sin2 = (sk_ref if g == 0 else sq_ref)[...]
            for hd in range(RET_HEADS):
                sl = slice(hd * RET_DK, (hd + 1) * RET_DK)
                o_ref[0, :, sl] = _rope(p[:, sl], cos2, sin2).astype(BF16)
        elif g == 6:
            o_ref[0] = (p * NA_DH ** -0.5).astype(BF16)
        else:
            o_ref[0] = p.astype(BF16)


def _in_proj(x, m, w_in_bf, rope_tabs):
    b, n, d = x.shape
    tm = ROW_TILE
    tab_spec = pl.BlockSpec((tm, RET_DK), lambda i, j: (j, 0))
    out_spec = pl.BlockSpec((1, tm, HEAD_W), lambda i, j: (i, j, 0))
    return pl.pallas_call(
        _in_proj_kernel,
        out_shape=[jax.ShapeDtypeStruct((b, n, HEAD_W), BF16)] * N_IN_GROUPS,
        grid=(b, n // tm),
        in_specs=[
            pl.BlockSpec((1, tm, d), lambda i, j: (i, j, 0)),
            pl.BlockSpec((1, N_MOD, d), lambda i, j: (i, 0, 0)),
            pl.BlockSpec((d, N_IN_GROUPS * HEAD_W), lambda i, j: (0, 0)),
            tab_spec, tab_spec, tab_spec, tab_spec,
        ],
        out_specs=[out_spec] * N_IN_GROUPS,
        compiler_params=_cparams("arbitrary", "arbitrary"),
        name="in_proj",
    )(x, m, w_in_bf, *rope_tabs)


def _ctx_proj_kernel(x_ref, m_ref, w_ref, rk_ref, rv_ref, nk_ref, nv_ref):
    shift = m_ref[0, 0:1, :]
    scale = m_ref[0, 1:2, :]
    hm = (x_ref[0] * (1.0 + scale) + shift).astype(BF16)
    for g, o_ref in enumerate((rk_ref, rv_ref, nk_ref, nv_ref)):
        o_ref[0] = _dot(hm, w_ref[:, g * HEAD_W:(g + 1) * HEAD_W]).astype(BF16)


def _ctx_proj(ctx, m_ctx, w_in_bf):
    b, l, d = ctx.shape
    out_spec = pl.BlockSpec((1, l, HEAD_W), lambda i: (i, 0, 0))
    return pl.pallas_call(
        _ctx_proj_kernel,
        out_shape=[jax.ShapeDtypeStruct((b, l, HEAD_W), BF16)] * N_KV_GROUPS,
        grid=(b,),
        in_specs=[
            pl.BlockSpec((1, l, d), lambda i: (i, 0, 0)),
            pl.BlockSpec((1, N_MOD, d), lambda i: (0, 0, 0)),
            pl.BlockSpec((d, N_KV_GROUPS * HEAD_W), lambda i: (0, 0)),
        ],
        out_specs=[out_spec] * N_KV_GROUPS,
        compiler_params=_cparams("arbitrary"),
        name="ctx_proj",
    )(ctx, m_ctx, w_in_bf)


def _retention_kernel(lg_ref, q_ref, k_ref, v_ref, g_ref, kc_ref, vc_ref, o_ref, sb_ref):
    hd = pl.program_id(1)
    n = q_ref.shape[1]
    c = RET_CHUNK
    nc = n // c
    l = kc_ref.shape[1]
    lgf = lg_ref[0, hd]
    lgb = lg_ref[1, hd]

    ii = lax.broadcasted_iota(jnp.int32, (c, c), 0).astype(F32)
    jj = lax.broadcasted_iota(jnp.int32, (c, c), 1).astype(F32)
    diff = ii - jj
    decay = (jnp.where(diff >= 0, jnp.exp(lgf * jnp.maximum(diff, 0.0)), 0.0)
             + jnp.where(diff <= 0, jnp.exp(lgb * jnp.maximum(-diff, 0.0)), 0.0))
    idx = lax.broadcasted_iota(jnp.int32, (c, 1), 0).astype(F32)
    q_dec_f = jnp.exp(lgf * (idx + 1.0))
    k_dec_f = jnp.exp(lgf * (c - 1.0 - idx))
    q_dec_b = jnp.exp(lgb * (c - idx))
    k_dec_b = jnp.exp(lgb * idx)
    ones = jnp.ones((1, RET_DK), F32)
    chunk_dec_f = jnp.exp(ones * (lgf * c))
    chunk_dec_b = jnp.exp(ones * (lgb * c))

    pos = lax.broadcasted_iota(jnp.int32, (l, 1), 0).astype(F32)
    kc = kc_ref[0].astype(F32)
    vc = vc_ref[0]
    s_f0 = _dot_tn((kc * jnp.exp(lgf * (l - 1.0 - pos))).astype(BF16), vc)
    s_b0 = _dot_tn((kc * jnp.exp(lgb * pos)).astype(BF16), vc)

    def bwd_step(t, s_b):
        i = nc - 1 - t
        rows = pl.ds(pl.multiple_of(i * c, c), c)
        sb_ref[i] = s_b.astype(BF16)
        k_i = k_ref[0, rows, :].astype(F32)
        return s_b * chunk_dec_b + _dot_tn((k_i * k_dec_b).astype(BF16), v_ref[0, rows, :])

    lax.fori_loop(0, nc, bwd_step, s_b0)

    def fwd_step(i, s_f):
        rows = pl.ds(pl.multiple_of(i * c, c), c)
        q_i = q_ref[0, rows, :]
        k_i = k_ref[0, rows, :]
        v_i = v_ref[0, rows, :]
        att = (_dot_nt(q_i, k_i) * decay).astype(BF16)
        o = (_dot(att, v_i)
             + _dot(q_i, s_f.astype(BF16)) * q_dec_f
             + _dot(q_i, sb_ref[i]) * q_dec_b)
        mu = jnp.mean(o, axis=-1, keepdims=True)
        var = jnp.mean(jnp.square(o - mu), axis=-1, keepdims=True)
        o_n = (o - mu) * lax.rsqrt(var + LN_EPS)
        gate = g_ref[0, rows, :].astype(F32)
        o_ref[0, rows, :] = (_silu(gate) * o_n).astype(BF16)
        return s_f * chunk_dec_f + _dot_tn((k_i.astype(F32) * k_dec_f).astype(BF16), v_i)

    lax.fori_loop(0, nc, fwd_step, s_f0)


def _retention(log_gamma2, rq, rk, rv, rg, rk_c, rv_c):
    b, n, _ = rq.shape
    l = rk_c.shape[1]
    seq_spec = pl.BlockSpec((1, n, RET_DK), lambda i, h: (i, 0, h))
    ctx_spec = pl.BlockSpec((1, l, RET_DK), lambda i, h: (i, 0, h))
    return pl.pallas_call(
        _retention_kernel,
        out_shape=jax.ShapeDtypeStruct((b, n, HEAD_W), BF16),
        grid=(b, RET_HEADS),
        in_specs=[pl.BlockSpec(memory_space=pltpu.SMEM),
                  seq_spec, seq_spec, seq_spec, seq_spec, ctx_spec, ctx_spec],
        out_specs=seq_spec,
        scratch_shapes=[pltpu.VMEM((n // RET_CHUNK, RET_DK, RET_DK), BF16)],
        compiler_params=_cparams("arbitrary", "arbitrary"),
        name="retention",
    )(log_gamma2, rq, rk, rv, rg, rk_c, rv_c)


def _na_geometry(rows):
    rb = NA_ROWS_PER_BLOCK
    wr = min(NA_WIN_R, rows)
    key_rows = min(rows, rb + wr - 1)
    n_blocks = rows // rb
    variants, block_variant, block_start = [], [], []
    for kb in range(n_blocks):
        q_rows = kb * rb + np.arange(rb)
        r0 = np.clip(q_rows - wr // 2, 0, rows - wr)
        ks = int(np.clip(r0.min(), 0, rows - key_rows))
        assert r0.max() + wr <= ks + key_rows
        qr = np.repeat(q_rows, GRID_W)[:, None]
        qc = np.tile(np.arange(GRID_W), rb)[:, None]
        q_r0 = np.repeat(r0, GRID_W)[:, None]
        kr = np.repeat(ks + np.arange(key_rows), GRID_W)[None, :]
        kc = np.tile(np.arange(GRID_W), key_rows)[None, :]
        wstart = np.clip(qc - NA_WIN_C // 2, 0, GRID_W - NA_WIN_C)
        valid = (kr >= q_r0) & (kr < q_r0 + wr) & (kc >= wstart) & (kc < wstart + NA_WIN_C)
        ridx = np.clip(kr - qr + NA_WIN_R - 1, 0, 2 * NA_WIN_R - 2) * np.ones_like(kc)
        cidx = np.clip(kc - qc + NA_WIN_C - 1, 0, 2 * NA_WIN_C - 2) * np.ones_like(kr)
        ridx = np.where(valid, ridx, 0).astype(np.int32)
        cidx = np.where(valid, cidx, 0).astype(np.int32)
        for vi, (vr, vc, vv) in enumerate(variants):
            if np.array_equal(vr, ridx) and np.array_equal(vc, cidx) and np.array_equal(vv, valid):
                block_variant.append(vi)
                break
        else:
            variants.append((ridx, cidx, valid))
            block_variant.append(len(variants) - 1)
        block_start.append(ks)
    return key_rows, variants, block_variant, block_start


def _na_bias_table(rpb, variants):
    tabs = []
    for ridx, cidx, valid in variants:
        tabs.append(jnp.where(valid[None], rpb[:, ridx, cidx], MASK_VALUE))
    return jnp.stack(tabs, axis=1).astype(F32)


def _na_kernel(q_ref, k_ref, v_ref, kc_ref, vc_ref, bias_ref, o_ref, *, block_variant, block_start, key_rows):
    qb = NA_ROWS_PER_BLOCK * GRID_W
    nk = key_rows * GRID_W
    lane = lax.broadcasted_iota(jnp.int32, (qb, LANES), 1)
    kc = kc_ref[0]
    vc = vc_ref[0]
    for kb, (var, ks) in enumerate(zip(block_variant, block_start)):
        q = q_ref[0, kb * qb:(kb + 1) * qb, :]
        k_win = k_ref[0, ks * GRID_W:ks * GRID_W + nk, :]
        v_win = v_ref[0, ks * GRID_W:ks * GRID_W + nk, :]
        outs = []
        for j in range(2):
            in_head = (lane >= j * NA_DH) & (lane < (j + 1) * NA_DH)
            qm = jnp.where(in_head, q, jnp.zeros_like(q))
            s_win = _dot_nt(qm, k_win) + bias_ref[j, var]
            s_ctx = _dot_nt(qm, kc)
            m = jnp.maximum(jnp.max(s_win, axis=-1, keepdims=True), jnp.max(s_ctx, axis=-1, keepdims=True))
            p_win = jnp.exp(s_win - m)
            p_ctx = jnp.exp(s_ctx - m)
            denom = jnp.sum(p_win, axis=-1, keepdims=True) + jnp.sum(p_ctx, axis=-1, keepdims=True)
            o = _dot(p_win.astype(BF16), v_win) + _dot(p_ctx.astype(BF16), vc)
            outs.append(o / denom)
        o_ref[0, kb * qb:(kb + 1) * qb, :] = jnp.where(lane < NA_DH, outs[0], outs[1]).astype(BF16)


def _neighbourhood_attention(nq, nk, nv, nk_c, nv_c, rpb):
    b, n, _ = nq.shape
    l = nk_c.shape[1]
    rows = n // GRID_W
    key_rows, variants, block_variant, block_start = _na_geometry(rows)
    bias = _na_bias_table(rpb, variants)
    nvar = len(variants)
    qb = NA_ROWS_PER_BLOCK * GRID_W
    nkeys = key_rows * GRID_W
    seq_spec = pl.BlockSpec((1, n, LANES), lambda h, i: (i, 0, h))
    ctx_spec = pl.BlockSpec((1, l, LANES), lambda h, i: (i, 0, h))
    kern = functools.partial(_na_kernel, block_variant=tuple(block_variant), block_start=tuple(block_start),
                             key_rows=key_rows)
    return pl.pallas_call(
        kern,
        out_shape=jax.ShapeDtypeStruct((b, n, HEAD_W), BF16),
        grid=(NA_HEADS // 2, b),
        in_specs=[seq_spec, seq_spec, seq_spec, ctx_spec, ctx_spec,
                  pl.BlockSpec((2, nvar, qb, nkeys), lambda h, i: (h, 0, 0, 0))],
        out_specs=seq_spec,
        compiler_params=_cparams("arbitrary", "arbitrary"),
        name="neighbourhood_attention",
    )(nq, nk, nv, nk_c, nv_c, bias)


def _layer_norm(z, g, b):
    mu = jnp.mean(z, axis=-1, keepdims=True)
    var = jnp.mean(jnp.square(z - mu), axis=-1, keepdims=True)
    return (z - mu) * lax.rsqrt(var + LN_EPS) * g + b


def _route(t, wr_ref, br_ref):
    logits = jnp.dot(t, wr_ref[...], precision=HIGHEST, preferred_element_type=F32) + br_ref[...]
    lane = lax.broadcasted_iota(jnp.int32, logits.shape, 1).astype(F32)
    neg = -jnp.inf
    big = float(LANES)

    def first_max(vals):
        vmax = jnp.max(vals, axis=-1, keepdims=True)
        return vmax, jnp.min(jnp.where(vals == vmax, lane, big), axis=-1, keepdims=True)

    is_grp = lane < MOE_GROUPS
    g_max, grp = first_max(jnp.where(is_grp, logits, neg))
    g_sum = jnp.sum(jnp.where(is_grp, jnp.exp(logits - g_max), 0.0), axis=-1, keepdims=True)
    gate_g = 1.0 / g_sum
    lo = MOE_GROUPS + grp * MOE_PER_GROUP
    in_grp = (lane >= lo) & (lane < lo + MOE_PER_GROUP)
    le = jnp.where(in_grp, logits, neg)
    v1, i1 = first_max(le)
    v2, i2 = first_max(jnp.where(lane == i1, neg, le))
    e21 = jnp.exp(v2 - v1)
    w1 = gate_g / (1.0 + e21)
    w2 = gate_g * e21 / (1.0 + e21)
    return jnp.where(lane == 0, w1,
                     jnp.where(lane == 1, w2,
                               jnp.where(lane == 2, i1 - MOE_GROUPS,
                                         jnp.where(lane == 3, i2 - MOE_GROUPS, 0.0))))


def _post_norm_route(h, y, m_ref, ln_ref, wr_ref, br_ref, h_out, t_out, r_out):
    gate = m_ref[0, 2:3, :]
    shift = m_ref[0, 3:4, :]
    scale = m_ref[0, 4:5, :]
    h1 = _layer_norm(DEEPNORM_ALPHA * h + gate * y, ln_ref[0:1, :], ln_ref[1:2, :])
    t = h1 * (1.0 + scale) + shift
    h_out[0] = h1
    t_out[0] = t.astype(BF16)
    r_out[0] = _route(t, wr_ref, br_ref)


def _epilogue_specs(b, n, d, tm):
    in_specs = [
        pl.BlockSpec((1, N_MOD, d), lambda i, j: (i, 0, 0)),
        pl.BlockSpec((2, d), lambda i, j: (0, 0)),
        pl.BlockSpec((d, LANES), lambda i, j: (0, 0)),
        pl.BlockSpec((1, LANES), lambda i, j: (0, 0)),
    ]
    out_shape = [jax.ShapeDtypeStruct((b, n, d), F32), jax.ShapeDtypeStruct((b, n, d), BF16),
                 jax.ShapeDtypeStruct((b, n, LANES), F32)]
    out_specs = [pl.BlockSpec((1, tm, d), lambda i, j: (i, j, 0)),
                 pl.BlockSpec((1, tm, d), lambda i, j: (i, j, 0)),
                 pl.BlockSpec((1, tm, LANES), lambda i, j: (i, j, 0))]
    return in_specs, out_shape, out_specs


def _out_proj_kernel(x_ref, yr_ref, yn_ref, w_ref, m_ref, ln_ref, wr_ref, br_ref, h_out, t_out, r_out):
    y = _dot(yr_ref[0], w_ref[0:HEAD_W, :]) + _dot(yn_ref[0], w_ref[HEAD_W:2 * HEAD_W, :])
    _post_norm_route(x_ref[0], y, m_ref, ln_ref, wr_ref, br_ref, h_out, t_out, r_out)


def _out_proj(x, y_ret, y_na, w_out_bf, m, ln, wr, br):
    b, n, d = x.shape
    tm = ROW_TILE
    ep_in, out_shape, out_specs = _epilogue_specs(b, n, d, tm)
    return pl.pallas_call(
        _out_proj_kernel,
        out_shape=out_shape,
        grid=(b, n // tm),
        in_specs=[
            pl.BlockSpec((1, tm, d), lambda i, j: (i, j, 0)),
            pl.BlockSpec((1, tm, HEAD_W), lambda i, j: (i, j, 0)),
            pl.BlockSpec((1, tm, HEAD_W), lambda i, j: (i, j, 0)),
            pl.BlockSpec((2 * HEAD_W, d), lambda i, j: (0, 0)),
        ] + ep_in,
        out_specs=out_specs,
        compiler_params=_cparams("arbitrary", "arbitrary"),
        name="out_proj",
    )(x, y_ret, y_na, w_out_bf, m, ln, wr, br)


POOL_HALO = max(POOL_SIZES) // 2
POOL_TILE = 256


def _pool_kernel(x_ref, prev_ref, next_ref, pw_ref, ps_ref, m_ref, ln_ref, wr_ref, br_ref,
                 h_out, t_out, r_out, *, n):
    j = pl.program_id(1)
    nj = pl.num_programs(1)
    tm = x_ref.shape[1]
    halo = POOL_HALO
    shift = m_ref[0, 0:1, :]
    scale = m_ref[0, 1:2, :]
    x = x_ref[0]
    hm = x * (1.0 + scale) + shift
    prev = jnp.where(j > 0, prev_ref[0] * (1.0 + scale) + shift, 0.0)
    nxt = jnp.where(j < nj - 1, next_ref[0] * (1.0 + scale) + shift, 0.0)
    ext = jnp.concatenate([prev, hm, nxt], axis=0)
    pos = (j * tm + lax.broadcasted_iota(jnp.int32, (tm, 1), 0))
    ys = []
    for g, w in enumerate(POOL_SIZES):
        cols = slice(g * POOL_GROUP, (g + 1) * POOL_GROUP)
        s = ext[:, cols]
        span = 1
        while span < w:
            s = s[:s.shape[0] - span] + s[span:]
            span *= 2
        off = halo - w // 2
        win = s[off:off + tm]
        cnt = (jnp.minimum(pos + (w - w // 2), n) - jnp.maximum(pos - w // 2, 0)).astype(F32)
        z = (win / cnt - hm[:, cols]).astype(BF16)
        ys.append(_dot(z, pw_ref[g]))
    y = jnp.concatenate(ys, axis=-1) * ps_ref[...]
    _post_norm_route(x, y, m_ref, ln_ref, wr_ref, br_ref, h_out, t_out, r_out)


def _pool_mixer(h, pool_w_bf, pool_scale, m, ln, wr, br):
    b, n, d = h.shape
    tm = POOL_TILE
    halo = POOL_HALO
    blocks_per_tile = tm // halo
    n_halo_blocks = n // halo
    ep_in, out_shape, out_specs = _epilogue_specs(b, n, d, tm)
    return pl.pallas_call(
        functools.partial(_pool_kernel, n=n),
        out_shape=out_shape,
        grid=(b, n // tm),
        in_specs=[
            pl.BlockSpec((1, tm, d), lambda i, j: (i, j, 0)),
            pl.BlockSpec((1, halo, d), lambda i, j: (i, jnp.maximum(j * blocks_per_tile - 1, 0), 0)),
            pl.BlockSpec((1, halo, d),
                         lambda i, j: (i, jnp.minimum((j + 1) * blocks_per_tile, n_halo_blocks - 1), 0)),
            pl.BlockSpec((len(POOL_SIZES), POOL_GROUP, POOL_GROUP), lambda i, j: (0, 0, 0)),
            pl.BlockSpec((1, d), lambda i, j: (0, 0)),
        ] + ep_in,
        out_specs=out_specs,
        compiler_params=_cparams("arbitrary", "arbitrary"),
        name="pool_mixer",
    )(h, h, h, pool_w_bf, pool_scale.reshape(1, d), m, ln, wr, br)


def _expert_kernel(te_ref, tv_ref, x_ref, cw_ref, wg_ref, wu_ref, wd_ref, o_ref, wg_s, wu_s, wd_s):
    j = pl.program_id(0)
    prev_e = te_ref[jnp.maximum(j - 1, 0)]

    @pl.when((j == 0) | (te_ref[j] != prev_e))
    def _():
        wg_s[...] = wg_ref[0].astype(BF16)
        wu_s[...] = wu_ref[0].astype(BF16)
        wd_s[...] = wd_ref[0].astype(BF16)

    @pl.when(tv_ref[j] == 1)
    def _():
        x = x_ref[...]
        gate = _dot(x, wg_s[...])
        up = _dot(x, wu_s[...])
        act = (_silu(gate) * up).astype(BF16)
        o_ref[...] = (_dot(act, wd_s[...]) * cw_ref[...]).astype(BF16)

    @pl.when(tv_ref[j] == 0)
    def _():
        o_ref[...] = jnp.zeros_like(o_ref)


def _experts(tile_expert, tile_valid, x_sorted, cw_sorted, w_gate, w_up, w_down):
    p, d = x_sorted.shape
    tm = MOE_TILE
    hid = w_gate.shape[-1]
    grid_spec = pltpu.PrefetchScalarGridSpec(
        num_scalar_prefetch=2,
        grid=(p // tm,),
        in_specs=[
            pl.BlockSpec((tm, d), lambda j, te, tv: (j, 0)),
            pl.BlockSpec((tm, 1), lambda j, te, tv: (j, 0)),
            pl.BlockSpec((1, d, hid), lambda j, te, tv: (te[j], 0, 0)),
            pl.BlockSpec((1, d, hid), lambda j, te, tv: (te[j], 0, 0)),
            pl.BlockSpec((1, hid, d), lambda j, te, tv: (te[j], 0, 0)),
        ],
        out_specs=pl.BlockSpec((tm, d), lambda j, te, tv: (j, 0)),
        scratch_shapes=[pltpu.VMEM((d, hid), BF16), pltpu.VMEM((d, hid), BF16), pltpu.VMEM((hid, d), BF16)],
    )
    return pl.pallas_call(
        _expert_kernel,
        out_shape=jax.ShapeDtypeStruct((p, d), BF16),
        grid_spec=grid_spec,
        compiler_params=_cparams("arbitrary"),
        name="experts",
    )(tile_expert, tile_valid, x_sorted, cw_sorted, w_gate, w_up, w_down)


def _moe(t, route, w_gate, w_up, w_down):
    b, n, d = t.shape
    tok = b * n
    tm = MOE_TILE
    n_assign = 2 * tok
    p = n_assign + MOE_EXPERTS * tm
    route = route.reshape(tok, LANES)
    cw = route[:, 0:2].reshape(n_assign)
    eid = route[:, 2:4].astype(jnp.int32).reshape(n_assign)
    order = jnp.argsort(eid, stable=True).astype(jnp.int32)
    counts = jnp.bincount(eid, length=MOE_EXPERTS).astype(jnp.int32)
    padded = ((counts + tm - 1) // tm) * tm
    pad_end = jnp.cumsum(padded)
    pad_start = pad_end - padded
    start = jnp.cumsum(counts) - counts
    sorted_e = eid[order]
    dest = pad_start[sorted_e] + jnp.arange(n_assign, dtype=jnp.int32) - start[sorted_e]
    tok_of_pos = jnp.zeros((p,), jnp.int32).at[dest].set(order // 2)
    cw_of_pos = jnp.zeros((p,), F32).at[dest].set(cw[order])
    pos_of_assign = jnp.zeros((n_assign,), jnp.int32).at[order].set(dest)
    tile_row = jnp.arange(p // tm, dtype=jnp.int32) * tm
    tile_expert = jnp.minimum(jnp.searchsorted(pad_end, tile_row, side="right"), MOE_EXPERTS - 1).astype(jnp.int32)
    tile_valid = (tile_row < pad_end[-1]).astype(jnp.int32)
    x_sorted = t.reshape(tok, d)[tok_of_pos]
    y_sorted = _experts(tile_expert, tile_valid, x_sorted, cw_of_pos.reshape(p, 1), w_gate, w_up, w_down)
    y = y_sorted[pos_of_assign].reshape(tok, 2, d)
    return (y[:, 0].astype(F32) + y[:, 1].astype(F32)).reshape(b, n, d)


def _post_norm_kernel(h_ref, y_ref, m_ref, ln_ref, o_ref):
    gate = m_ref[0, 5:6, :]
    o_ref[0] = _layer_norm(DEEPNORM_ALPHA * h_ref[0] + gate * y_ref[0], ln_ref[0:1, :], ln_ref[1:2, :])


def _post_norm(h, y, m, ln):
    b, n, d = h.shape
    tm = ROW_TILE
    row_spec = pl.BlockSpec((1, tm, d), lambda i, j: (i, j, 0))
    return pl.pallas_call(
        _post_norm_kernel,
        out_shape=jax.ShapeDtypeStruct((b, n, d), F32),
        grid=(b, n // tm),
        in_specs=[row_spec, row_spec,
                  pl.BlockSpec((1, N_MOD, d), lambda i, j: (i, 0, 0)),
                  pl.BlockSpec((2, d), lambda i, j: (0, 0))],
        out_specs=row_spec,
        compiler_params=_cparams("arbitrary", "arbitrary"),
        name="post_norm",
    )(h, y, m, ln)


def _rope_tables(n):
    t = jnp.arange(n)
    rows = (t // GRID_W).astype(F32)
    cols = (t % GRID_W).astype(F32)
    n_freq = RET_DK // 4
    inv_freq = ROPE_BASE ** (-jnp.arange(n_freq, dtype=F32) / n_freq)
    ang = jnp.concatenate([rows[:, None] * inv_freq, cols[:, None] * inv_freq], axis=-1)
    cos, sin = jnp.cos(ang), jnp.sin(ang)
    cos2 = jnp.concatenate([cos, cos], axis=-1)
    sin2 = jnp.concatenate([-sin, sin], axis=-1)
    q_scale = RET_DK ** -0.5
    return cos2 * q_scale, sin2 * q_scale, cos2, sin2


def _router_params(w_r1, b_r1, w_r2, b_r2):
    d = w_r1.shape[0]
    w2 = jnp.transpose(w_r2, (1, 0, 2)).reshape(d, MOE_EXPERTS)
    pad = LANES - MOE_GROUPS - MOE_EXPERTS
    wr = jnp.concatenate([w_r1, w2, jnp.zeros((d, pad), F32)], axis=-1)
    br = jnp.concatenate([b_r1, b_r2.reshape(MOE_EXPERTS), jnp.zeros((pad,), F32)]).reshape(1, LANES)
    return wr, br


def kernel(x, c, ctx, c_ctx, w_mod, b_mod, ln_g, ln_b, ab_w_in, ab_w_out, ab_log_decay, ab_rpb, pool_w, pool_scale, moe_w_r1, moe_b_r1, moe_w_r2, moe_b_r2, moe_w_gate, moe_w_up, moe_w_down):
    b, n, d = x.shape
    cc = jnp.concatenate([c, c_ctx[None, :], jnp.zeros((MOD_ROWS - b - 1, d), F32)], axis=0)
    mod = _modulation(cc, w_mod, b_mod)
    h = x
    for i in range(DEPTH):
        j = i // 2
        m = mod[i, :b].reshape(b, N_MOD, d)
        ln1 = jnp.stack([ln_g[i, 0], ln_b[i, 0]])
        ln2 = jnp.stack([ln_g[i, 1], ln_b[i, 1]])
        wr, br = _router_params(moe_w_r1[i], moe_b_r1[i], moe_w_r2[i], moe_b_r2[i])
        if i % 2 == 0:
            m_ctx = mod[i, b].reshape(1, N_MOD, d)
            w_in_bf = ab_w_in[j].astype(BF16)
            log_gamma2 = jnp.log1p(-jnp.exp(ab_log_decay[j].astype(F32)))
            rk, rv, nk, nv, rq, rg, nq = _in_proj(h, m, w_in_bf, _rope_tables(n))
            rk_c, rv_c, nk_c, nv_c = _ctx_proj(ctx, m_ctx, w_in_bf[:, :N_KV_GROUPS * HEAD_W])
            y_ret = _retention(log_gamma2, rq, rk, rv, rg, rk_c, rv_c)
            y_na = _neighbourhood_attention(nq, nk, nv, nk_c, nv_c, ab_rpb[j])
            h1, t, route = _out_proj(h, y_ret, y_na, ab_w_out[j].astype(BF16), m, ln1, wr, br)
        else:
            h1, t, route = _pool_mixer(h, pool_w[j].astype(BF16), pool_scale[j], m, ln1, wr, br)
        y_moe = _moe(t, route, moe_w_gate[i], moe_w_up[i], moe_w_down[i])
        h = _post_norm(h1, y_moe, m, ln2)
    return h
```

```python
import functools

import numpy as np
import jax
import jax.numpy as jnp
from jax import lax
from jax.experimental import pallas as pl
from jax.experimental.pallas import tpu as pltpu

F32 = jnp.float32
BF16 = jnp.bfloat16
HIGHEST = lax.Precision.HIGHEST

D_MODEL = 1024
DEPTH = 2
GRID_W = 64
RET_HEADS = 4
RET_DK = 128
RET_CHUNK = 128
NA_HEADS = 8
NA_DH = 64
NA_WIN_R = 8
NA_WIN_C = 16
N_BIAS_ROWS = 2 * NA_WIN_R - 1
POOL_SIZES = (2, 4, 8, 16)
POOL_GROUP = D_MODEL // len(POOL_SIZES)
MOE_GROUPS = 4
MOE_PER_GROUP = 8
MOE_EXPERTS = MOE_GROUPS * MOE_PER_GROUP
MOE_HIDDEN = D_MODEL // 2
ROPE_BASE = 10000.0
LN_EPS = 1e-5
N_MOD = 6
DEEPNORM_ALPHA = (2 * DEPTH) ** 0.25
HEAD_W = 512
N_IN_GROUPS = 7
N_KV_GROUPS = 4
MASK_VALUE = -1e30

LANES = 128
VMEM_LIMIT = 56 * 1024 * 1024

NA_ROWS_PER_BLOCK = 4
ROW_TILE = 512
MOE_TILE = 256
MOD_ROWS = 24
MOD_COL_TILE = 1536


def _cparams(*sem):
    return pltpu.CompilerParams(dimension_semantics=sem, vmem_limit_bytes=VMEM_LIMIT)


def _silu(v):
    return v / (1.0 + jnp.exp(-v))


def _dot(a, b):
    return jnp.dot(a, b, preferred_element_type=F32)


def _dot_nt(a, b):
    return lax.dot_general(a, b, (((1,), (1,)), ((), ())), preferred_element_type=F32)


def _dot_tn(a, b):
    return lax.dot_general(a, b, (((0,), (0,)), ((), ())), preferred_element_type=F32)


def _mod_kernel(c_ref, w_ref, b_ref, o_ref):
    s = _silu(c_ref[...])
    o_ref[0] = jnp.dot(s, w_ref[0], precision=HIGHEST, preferred_element_type=F32) + b_ref[0]


def _modulation(cc, w_mod, b_mod):
    depth, d, n = w_mod.shape
    return pl.pallas_call(
        _mod_kernel,
        out_shape=jax.ShapeDtypeStruct((depth, MOD_ROWS, n), F32),
        grid=(depth, n // MOD_COL_TILE),
        in_specs=[
            pl.BlockSpec((MOD_ROWS, d), lambda i, j: (0, 0)),
            pl.BlockSpec((1, d, MOD_COL_TILE), lambda i, j: (i, 0, j)),
            pl.BlockSpec((1, 1, MOD_COL_TILE), lambda i, j: (i, 0, j)),
        ],
        out_specs=pl.BlockSpec((1, MOD_ROWS, MOD_COL_TILE), lambda i, j: (i, 0, j)),
        compiler_params=_cparams("arbitrary", "arbitrary"),
        name="modulation",
    )(cc, w_mod, b_mod.reshape(depth, 1, n))


def _rope(v, cos2, sin2):
    return v * cos2 + pltpu.roll(v, RET_DK // 2, axis=1) * sin2


def _in_proj_kernel(x_ref, m_ref, w_ref, cq_ref, sq_ref, ck_ref, sk_ref,
                    rk_ref, rv_ref, nk_ref, nv_ref, rq_ref, rg_ref, nq_ref):
    shift = m_ref[0, 0:1, :]
    scale = m_ref[0, 1:2, :]
    hm = (x_ref[0] * (1.0 + scale) + shift).astype(BF16)
    outs = (rk_ref, rv_ref, nk_ref, nv_ref, rq_ref, rg_ref, nq_ref)
    for g, o_ref in enumerate(outs):
        p = _dot(hm, w_ref[:, g * HEAD_W:(g + 1) * HEAD_W])
        if g == 0 or g == 4:
            cos2 = (ck_ref if g == 0 else cq_ref)[...]
            sin2 = (sk_ref if g == 0 else sq_ref)[...]
            for hd in range(RET_HEADS):
                sl = slice(hd * RET_DK, (hd + 1) * RET_DK)
                o_ref[0, :, sl] = _rope(p[:, sl], cos2, sin2).astype(BF16)
        elif g == 6:
            o_ref[0] = (p * NA_DH ** -0.5).astype(BF16)
        else:
            o_ref[0] = p.astype(BF16)


def _in_proj(x, m, w_in_bf, rope_tabs):
    b, n, d = x.shape
    tm = ROW_TILE
    tab_spec = pl.BlockSpec((tm, RET_DK), lambda i, j: (j, 0))
    out_spec = pl.BlockSpec((1, tm, HEAD_W), lambda i, j: (i, j, 0))
    return pl.pallas_call(
        _in_proj_kernel,
        out_shape=[jax.ShapeDtypeStruct((b, n, HEAD_W), BF16)] * N_IN_GROUPS,
        grid=(b, n // tm),
        in_specs=[
            pl.BlockSpec((1, tm, d), lambda i, j: (i, j, 0)),
            pl.BlockSpec((1, N_MOD, d), lambda i, j: (i, 0, 0)),
            pl.BlockSpec((d, N_IN_GROUPS * HEAD_W), lambda i, j: (0, 0)),
            tab_spec, tab_spec, tab_spec, tab_spec,
        ],
        out_specs=[out_spec] * N_IN_GROUPS,
        compiler_params=_cparams("arbitrary", "arbitrary"),
        name="in_proj",
    )(x, m, w_in_bf, *rope_tabs)


def _ctx_proj_kernel(x_ref, m_ref, w_ref, rk_ref, rv_ref, nk_ref, nv_ref):
    shift = m_ref[0, 0:1, :]
    scale = m_ref[0, 1:2, :]
    hm = (x_ref[0] * (1.0 + scale) + shift).astype(BF16)
    for g, o_ref in enumerate((rk_ref, rv_ref, nk_ref, nv_ref)):
        o_ref[0] = _dot(hm, w_ref[:, g * HEAD_W:(g + 1) * HEAD_W]).astype(BF16)


def _ctx_proj(ctx, m_ctx, w_in_bf):
    b, l, d = ctx.shape
    out_spec = pl.BlockSpec((1, l, HEAD_W), lambda i: (i, 0, 0))
    return pl.pallas_call(
        _ctx_proj_kernel,
        out_shape=[jax.ShapeDtypeStruct((b, l, HEAD_W), BF16)] * N_KV_GROUPS,
        grid=(b,),
        in_specs=[
            pl.BlockSpec((1, l, d), lambda i: (i, 0, 0)),
            pl.BlockSpec((1, N_MOD, d), lambda i: (0, 0, 0)),
            pl.BlockSpec((d, N_KV_GROUPS * HEAD_W), lambda i: (0, 0)),
        ],
        out_specs=[out_spec] * N_KV_GROUPS,
        compiler_params=_cparams("arbitrary"),
        name="ctx_proj",
    )(ctx, m_ctx, w_in_bf)


def _retention_kernel(lg_ref, q_ref, k_ref, v_ref, g_ref, kc_ref, vc_ref, o_ref, u_ref, s_ref):
    hd = pl.program_id(1)
    n = q_ref.shape[1]
    c = RET_CHUNK
    dk = RET_DK
    nc = n // c
    l = kc_ref.shape[1]
    lgf = lg_ref[0, hd]
    lgb = lg_ref[1, hd]

    ii = lax.broadcasted_iota(jnp.int32, (c, c), 0).astype(F32)
    jj = lax.broadcasted_iota(jnp.int32, (c, c), 1).astype(F32)
    diff = ii - jj
    decay = (jnp.where(diff >= 0, jnp.exp(lgf * jnp.maximum(diff, 0.0)), 0.0)
             + jnp.where(diff <= 0, jnp.exp(lgb * jnp.maximum(-diff, 0.0)), 0.0))
    idx = lax.broadcasted_iota(jnp.int32, (c, 1), 0).astype(F32)
    q_dec_f = jnp.exp(lgf * (idx + 1.0))
    k_dec_f = jnp.exp(lgf * (c - 1.0 - idx))
    q_dec_b = jnp.exp(lgb * (c - idx))
    k_dec_b = jnp.exp(lgb * idx)
    ones = jnp.ones((1, dk), F32)
    chunk_dec_f = jnp.exp(ones * (lgf * c))
    chunk_dec_b = jnp.exp(ones * (lgb * c))

    pos = lax.broadcasted_iota(jnp.int32, (l, 1), 0).astype(F32)
    kc = kc_ref[0].astype(F32)
    vc = vc_ref[0]
    s_f0 = _dot_tn((kc * jnp.exp(lgf * (l - 1.0 - pos))).astype(BF16), vc)
    s_b0 = _dot_tn((kc * jnp.exp(lgb * pos)).astype(BF16), vc)

    def chunk_rows(i):
        return pl.ds(pl.multiple_of(i * c, c), c)

    def kv_step(i, carry):
        rows = chunk_rows(i)
        k_i = k_ref[0, rows, :].astype(F32)
        kk = jnp.concatenate([(k_i * k_dec_f).astype(BF16), (k_i * k_dec_b).astype(BF16)], axis=1)
        u_ref[i] = _dot_tn(kk, v_ref[0, rows, :])
        return carry

    lax.fori_loop(0, nc, kv_step, 0, unroll=4)

    def scan_f(i, s):
        s_ref[i, :, 0:dk] = s.astype(BF16)
        return s * chunk_dec_f + u_ref[i, 0:dk, :]

    lax.fori_loop(0, nc, scan_f, s_f0, unroll=True)

    def scan_b(t, s):
        i = nc - 1 - t
        s_ref[i, :, dk:2 * dk] = s.astype(BF16)
        return s * chunk_dec_b + u_ref[i, dk:2 * dk, :]

    lax.fori_loop(0, nc, scan_b, s_b0, unroll=True)

    def out_step(i, carry):
        rows = chunk_rows(i)
        q_i = q_ref[0, rows, :]
        v_i = v_ref[0, rows, :]
        att = (_dot_nt(q_i, k_ref[0, rows, :]) * decay).astype(BF16)
        inter = _dot(q_i, s_ref[i])
        o = _dot(att, v_i) + inter[:, 0:dk] * q_dec_f + inter[:, dk:2 * dk] * q_dec_b
        mu = jnp.mean(o, axis=-1, keepdims=True)
        var = jnp.mean(jnp.square(o - mu), axis=-1, keepdims=True)
        o_n = (o - mu) * lax.rsqrt(var + LN_EPS)
        gate = g_ref[0, rows, :].astype(F32)
        o_ref[0, rows, :] = (_silu(gate) * o_n).astype(BF16)
        return carry

    lax.fori_loop(0, nc, out_step, 0, unroll=2)


def _retention(log_gamma2, rq, rk, rv, rg, rk_c, rv_c):
    b, n, _ = rq.shape
    l = rk_c.shape[1]
    nc = n // RET_CHUNK
    seq_spec = pl.BlockSpec((1, n, RET_DK), lambda i, h: (i, 0, h))
    ctx_spec = pl.BlockSpec((1, l, RET_DK), lambda i, h: (i, 0, h))
    return pl.pallas_call(
        _retention_kernel,
        out_shape=jax.ShapeDtypeStruct((b, n, HEAD_W), BF16),
        grid=(b, RET_HEADS),
        in_specs=[pl.BlockSpec(memory_space=pltpu.SMEM),
                  seq_spec, seq_spec, seq_spec, seq_spec, ctx_spec, ctx_spec],
        out_specs=seq_spec,
        scratch_shapes=[pltpu.VMEM((nc, 2 * RET_DK, RET_DK), F32),
                        pltpu.VMEM((nc, RET_DK, 2 * RET_DK), BF16)],
        compiler_params=_cparams("arbitrary", "arbitrary"),
        name="retention",
    )(log_gamma2, rq, rk, rv, rg, rk_c, rv_c)


def _na_geometry(rows):
    rb = NA_ROWS_PER_BLOCK
    wr = min(NA_WIN_R, rows)
    key_rows = min(rows, rb + wr - 1)
    variants, block_variant, block_start = [], [], []
    for kb in range(rows // rb):
        q_rows = kb * rb + np.arange(rb)
        r0 = np.clip(q_rows - wr // 2, 0, rows - wr)
        ks = int(np.clip(r0.min(), 0, rows - key_rows))
        assert r0.max() + wr <= ks + key_rows
        kr = ks + np.arange(key_rows)
        valid = (kr[None, :] >= r0[:, None]) & (kr[None, :] < r0[:, None] + wr)
        ridx = np.where(valid, kr[None, :] - q_rows[:, None] + NA_WIN_R - 1, N_BIAS_ROWS).astype(np.int32)
        for vi, v in enumerate(variants):
            if np.array_equal(v, ridx):
                block_variant.append(vi)
                break
        else:
            variants.append(ridx)
            block_variant.append(len(variants) - 1)
        block_start.append(ks)
    return key_rows, np.stack(variants), block_variant, block_start


def _na_bias_table(rpb, variant_rows):
    n_heads = rpb.shape[0]
    qc = np.arange(GRID_W)[:, None]
    kc = np.arange(GRID_W)[None, :]
    wstart = np.clip(qc - NA_WIN_C // 2, 0, GRID_W - NA_WIN_C)
    col_ok = (kc >= wstart) & (kc < wstart + NA_WIN_C)
    cidx = np.clip(kc - qc + NA_WIN_C - 1, 0, 2 * NA_WIN_C - 2)
    onehot = ((cidx[None] == np.arange(2 * NA_WIN_C - 1)[:, None, None]) & col_ok[None]).astype(np.float32)
    blocks = jnp.einsum("hrd,dqk->hrqk", rpb.astype(F32), jnp.asarray(onehot), precision=HIGHEST)
    blocks = jnp.where(col_ok[None, None], blocks, MASK_VALUE)
    blocks = jnp.concatenate([blocks, jnp.full((n_heads, 1, GRID_W, GRID_W), MASK_VALUE, F32)], axis=1)
    nvar, rb, key_rows = variant_rows.shape
    tab = blocks[:, variant_rows.reshape(-1)].reshape(n_heads, nvar, rb, key_rows, GRID_W, GRID_W)
    return jnp.transpose(tab, (0, 1, 2, 4, 3, 5)).reshape(n_heads, nvar, rb * GRID_W, key_rows * GRID_W)


def _na_kernel(q_ref, k_ref, v_ref, kc_ref, vc_ref, bias_ref, o_ref, *, block_variant, block_start, key_rows):
    qb = NA_ROWS_PER_BLOCK * GRID_W
    nk = key_rows * GRID_W
    lane = lax.broadcasted_iota(jnp.int32, (qb, LANES), 1)
    kc = kc_ref[0]
    vc = vc_ref[0]
    for kb, (var, ks) in enumerate(zip(block_variant, block_start)):
        q = q_ref[0, kb * qb:(kb + 1) * qb, :]
        k_win = k_ref[0, ks * GRID_W:ks * GRID_W + nk, :]
        v_win = v_ref[0, ks * GRID_W:ks * GRID_W + nk, :]
        outs = []
        for j in range(2):
            in_head = (lane >= j * NA_DH) & (lane < (j + 1) * NA_DH)
            qm = jnp.where(in_head, q, jnp.zeros_like(q))
            s_win = _dot_nt(qm, k_win) + bias_ref[j, var]
            s_ctx = _dot_nt(qm, kc)
            m = jnp.maximum(jnp.max(s_win, axis=-1, keepdims=True), jnp.max(s_ctx, axis=-1, keepdims=True))
            p_win = jnp.exp(s_win - m)
            p_ctx = jnp.exp(s_ctx - m)
            denom = jnp.sum(p_win, axis=-1, keepdims=True) + jnp.sum(p_ctx, axis=-1, keepdims=True)
            o = _dot(p_win.astype(BF16), v_win) + _dot(p_ctx.astype(BF16), vc)
            outs.append(o / denom)
        o_ref[0, kb * qb:(kb + 1) * qb, :] = jnp.where(lane < NA_DH, outs[0], outs[1]).astype(BF16)


def _neighbourhood_attention(nq, nk, nv, nk_c, nv_c, rpb):
    b, n, _ = nq.shape
    l = nk_c.shape[1]
    rows = n // GRID_W
    key_rows, variant_rows, block_variant, block_start = _na_geometry(rows)
    bias = _na_bias_table(rpb, variant_rows)
    nvar = variant_rows.shape[0]
    qb = NA_ROWS_PER_BLOCK * GRID_W
    nkeys = key_rows * GRID_W
    seq_spec = pl.BlockSpec((1, n, LANES), lambda h, i: (i, 0, h))
    ctx_spec = pl.BlockSpec((1, l, LANES), lambda h, i: (i, 0, h))
    kern = functools.partial(_na_kernel, block_variant=tuple(block_variant), block_start=tuple(block_start),
                             key_rows=key_rows)
    return pl.pallas_call(
        kern,
        out_shape=jax.ShapeDtypeStruct((b, n, HEAD_W), BF16),
        grid=(NA_HEADS // 2, b),
        in_specs=[seq_spec, seq_spec, seq_spec, ctx_spec, ctx_spec,
                  pl.BlockSpec((2, nvar, qb, nkeys), lambda h, i: (h, 0, 0, 0))],
        out_specs=seq_spec,
        compiler_params=_cparams("arbitrary", "arbitrary"),
        name="neighbourhood_attention",
    )(nq, nk, nv, nk_c, nv_c, bias)


def _layer_norm(z, g, b):
    mu = jnp.mean(z, axis=-1, keepdims=True)
    var = jnp.mean(jnp.square(z - mu), axis=-1, keepdims=True)
    return (z - mu) * lax.rsqrt(var + LN_EPS) * g + b


def _route(t, wr_hi_ref, wr_lo_ref, br_ref):
    t_hi = t.astype(BF16)
    t_lo = (t - t_hi.astype(F32)).astype(BF16)
    w_hi = wr_hi_ref[...]
    logits = _dot(t_hi, w_hi) + (_dot(t_lo, w_hi) + _dot(t_hi, wr_lo_ref[...])) + br_ref[...]
    lane = lax.broadcasted_iota(jnp.int32, logits.shape, 1).astype(F32)
    neg = -jnp.inf
    big = float(LANES)

    def first_max(vals):
        vmax = jnp.max(vals, axis=-1, keepdims=True)
        return vmax, jnp.min(jnp.where(vals == vmax, lane, big), axis=-1, keepdims=True)

    is_grp = lane < MOE_GROUPS
    g_max, grp = first_max(jnp.where(is_grp, logits, neg))
    g_sum = jnp.sum(jnp.where(is_grp, jnp.exp(logits - g_max), 0.0), axis=-1, keepdims=True)
    gate_g = 1.0 / g_sum
    lo = MOE_GROUPS + grp * MOE_PER_GROUP
    in_grp = (lane >= lo) & (lane < lo + MOE_PER_GROUP)
    le = jnp.where(in_grp, logits, neg)
    v1, i1 = first_max(le)
    v2, i2 = first_max(jnp.where(lane == i1, neg, le))
    e21 = jnp.exp(v2 - v1)
    w1 = gate_g / (1.0 + e21)
    w2 = gate_g * e21 / (1.0 + e21)
    return jnp.where(lane == 0, w1,
                     jnp.where(lane == 1, w2,
                               jnp.where(lane == 2, i1 - MOE_GROUPS,
                                         jnp.where(lane == 3, i2 - MOE_GROUPS, 0.0))))


def _post_norm_route(h, y, m_ref, ln_ref, wr_hi_ref, wr_lo_ref, br_ref, h_out, t_out, r_out):
    gate = m_ref[0, 2:3, :]
    shift = m_ref[0, 3:4, :]
    scale = m_ref[0, 4:5, :]
    h1 = _layer_norm(DEEPNORM_ALPHA * h + gate * y, ln_ref[0:1, :], ln_ref[1:2, :])
    t = h1 * (1.0 + scale) + shift
    h_out[0] = h1
    t_out[0] = t.astype(BF16)
    r_out[0] = _route(t, wr_hi_ref, wr_lo_ref, br_ref)


def _epilogue_specs(b, n, d, tm):
    in_specs = [
        pl.BlockSpec((1, N_MOD, d), lambda i, j: (i, 0, 0)),
        pl.BlockSpec((2, d), lambda i, j: (0, 0)),
        pl.BlockSpec((d, LANES), lambda i, j: (0, 0)),
        pl.BlockSpec((d, LANES), lambda i, j: (0, 0)),
        pl.BlockSpec((1, LANES), lambda i, j: (0, 0)),
    ]
    out_shape = [jax.ShapeDtypeStruct((b, n, d), F32), jax.ShapeDtypeStruct((b, n, d), BF16),
                 jax.ShapeDtypeStruct((b, n, LANES), F32)]
    out_specs = [pl.BlockSpec((1, tm, d), lambda i, j: (i, j, 0)),
                 pl.BlockSpec((1, tm, d), lambda i, j: (i, j, 0)),
                 pl.BlockSpec((1, tm, LANES), lambda i, j: (i, j, 0))]
    return in_specs, out_shape, out_specs


def _out_proj_kernel(x_ref, yr_ref, yn_ref, w_ref, m_ref, ln_ref, wr_hi_ref, wr_lo_ref, br_ref,
                     h_out, t_out, r_out):
    y = _dot(yr_ref[0], w_ref[0:HEAD_W, :]) + _dot(yn_ref[0], w_ref[HEAD_W:2 * HEAD_W, :])
    _post_norm_route(x_ref[0], y, m_ref, ln_ref, wr_hi_ref, wr_lo_ref, br_ref, h_out, t_out, r_out)


def _out_proj(x, y_ret, y_na, w_out_bf, m, ln, router):
    b, n, d = x.shape
    tm = ROW_TILE
    ep_in, out_shape, out_specs = _epilogue_specs(b, n, d, tm)
    return pl.pallas_call(
        _out_proj_kernel,
        out_shape=out_shape,
        grid=(b, n // tm),
        in_specs=[
            pl.BlockSpec((1, tm, d), lambda i, j: (i, j, 0)),
            pl.BlockSpec((1, tm, HEAD_W), lambda i, j: (i, j, 0)),
            pl.BlockSpec((1, tm, HEAD_W), lambda i, j: (i, j, 0)),
            pl.BlockSpec((2 * HEAD_W, d), lambda i, j: (0, 0)),
        ] + ep_in,
        out_specs=out_specs,
        compiler_params=_cparams("arbitrary", "arbitrary"),
        name="out_proj",
    )(x, y_ret, y_na, w_out_bf, m, ln, *router)


POOL_HALO = max(POOL_SIZES) // 2
POOL_TILE = 256


def _pool_kernel(x_ref, prev_ref, next_ref, pw_ref, ps_ref, m_ref, ln_ref, wr_hi_ref, wr_lo_ref, br_ref,
                 h_out, t_out, r_out, *, n):
    j = pl.program_id(1)
    nj = pl.num_programs(1)
    tm = x_ref.shape[1]
    halo = POOL_HALO
    shift = m_ref[0, 0:1, :]
    scale = m_ref[0, 1:2, :]
    x = x_ref[0]
    hm = x * (1.0 + scale) + shift
    prev = jnp.where(j > 0, prev_ref[0] * (1.0 + scale) + shift, 0.0)
    nxt = jnp.where(j < nj - 1, next_ref[0] * (1.0 + scale) + shift, 0.0)
    ext = jnp.concatenate([prev, hm, nxt], axis=0)
    pos = (j * tm + lax.broadcasted_iota(jnp.int32, (tm, 1), 0))
    ys = []
    for g, w in enumerate(POOL_SIZES):
        cols = slice(g * POOL_GROUP, (g + 1) * POOL_GROUP)
        s = ext[:, cols]
        span = 1
        while span < w:
            s = s[:s.shape[0] - span] + s[span:]
            span *= 2
        off = halo - w // 2
        win = s[off:off + tm]
        cnt = (jnp.minimum(pos + (w - w // 2), n) - jnp.maximum(pos - w // 2, 0)).astype(F32)
        z = (win / cnt - hm[:, cols]).astype(BF16)
        ys.append(_dot(z, pw_ref[g]))
    y = jnp.concatenate(ys, axis=-1) * ps_ref[...]
    _post_norm_route(x, y, m_ref, ln_ref, wr_hi_ref, wr_lo_ref, br_ref, h_out, t_out, r_out)


def _pool_mixer(h, pool_w_bf, pool_scale, m, ln, router):
    b, n, d = h.shape
    tm = POOL_TILE
    halo = POOL_HALO
    blocks_per_tile = tm // halo
    n_halo_blocks = n // halo
    ep_in, out_shape, out_specs = _epilogue_specs(b, n, d, tm)
    return pl.pallas_call(
        functools.partial(_pool_kernel, n=n),
        out_shape=out_shape,
        grid=(b, n // tm),
        in_specs=[
            pl.BlockSpec((1, tm, d), lambda i, j: (i, j, 0)),
            pl.BlockSpec((1, halo, d), lambda i, j: (i, jnp.maximum(j * blocks_per_tile - 1, 0), 0)),
            pl.BlockSpec((1, halo, d),
                         lambda i, j: (i, jnp.minimum((j + 1) * blocks_per_tile, n_halo_blocks - 1), 0)),
            pl.BlockSpec((len(POOL_SIZES), POOL_GROUP, POOL_GROUP), lambda i, j: (0, 0, 0)),
            pl.BlockSpec((1, d), lambda i, j: (0, 0)),
        ] + ep_in,
        out_specs=out_specs,
        compiler_params=_cparams("arbitrary", "arbitrary"),
        name="pool_mixer",
    )(h, h, h, pool_w_bf, pool_scale.reshape(1, d), m, ln, *router)


def _expert_kernel(te_ref, tv_ref, x_ref, wg_ref, wu_ref, wd_ref, o_ref, wg_s, wu_s, wd_s):
    j = pl.program_id(0)
    prev_e = te_ref[jnp.maximum(j - 1, 0)]

    @pl.when((j == 0) | (te_ref[j] != prev_e))
    def _():
        wg_s[...] = wg_ref[0].astype(BF16)
        wu_s[...] = wu_ref[0].astype(BF16)
        wd_s[...] = wd_ref[0].astype(BF16)

    @pl.when(tv_ref[j] == 1)
    def _():
        x = x_ref[...]
        gate = _dot(x, wg_s[...])
        up = _dot(x, wu_s[...])
        act = (_silu(gate) * up).astype(BF16)
        o_ref[...] = _dot(act, wd_s[...]).astype(BF16)

    @pl.when(tv_ref[j] == 0)
    def _():
        o_ref[...] = jnp.zeros_like(o_ref)


def _experts(tile_expert, tile_valid, x_sorted, w_gate, w_up, w_down):
    p, d = x_sorted.shape
    tm = MOE_TILE
    hid = w_gate.shape[-1]
    grid_spec = pltpu.PrefetchScalarGridSpec(
        num_scalar_prefetch=2,
        grid=(p // tm,),
        in_specs=[
            pl.BlockSpec((tm, d), lambda j, te, tv: (j, 0)),
            pl.BlockSpec((1, d, hid), lambda j, te, tv: (te[j], 0, 0)),
            pl.BlockSpec((1, d, hid), lambda j, te, tv: (te[j], 0, 0)),
            pl.BlockSpec((1, hid, d), lambda j, te, tv: (te[j], 0, 0)),
        ],
        out_specs=pl.BlockSpec((tm, d), lambda j, te, tv: (j, 0)),
        scratch_shapes=[pltpu.VMEM((d, hid), BF16), pltpu.VMEM((d, hid), BF16), pltpu.VMEM((hid, d), BF16)],
    )
    return pl.pallas_call(
        _expert_kernel,
        out_shape=jax.ShapeDtypeStruct((p, d), BF16),
        grid_spec=grid_spec,
        compiler_params=_cparams("arbitrary"),
        name="experts",
    )(tile_expert, tile_valid, x_sorted, w_gate, w_up, w_down)


def _dispatch_plan(route, tm):
    tok = route.shape[0]
    n_assign = 2 * tok
    p = n_assign + MOE_EXPERTS * tm
    n_tiles = p // tm
    i32 = jnp.int32
    eid = route[:, 2:4].astype(i32).reshape(n_assign)
    sorted_e, order = lax.sort((eid, jnp.arange(n_assign, dtype=i32)), num_keys=1)
    experts = jnp.arange(MOE_EXPERTS, dtype=i32)
    counts = jnp.sum((eid[:, None] == experts[None, :]).astype(i32), axis=0)
    padded = ((counts + tm - 1) // tm) * tm
    pad_end = jnp.cumsum(padded)
    pad_start = pad_end - padded
    start = jnp.cumsum(counts) - counts
    tile_row = jnp.arange(n_tiles, dtype=i32) * tm
    tile_expert = jnp.minimum(jnp.sum((pad_end[None, :] <= tile_row[:, None]).astype(i32), axis=1), MOE_EXPERTS - 1)
    tile_valid = (tile_row < pad_end[-1]).astype(i32)
    rank = (tile_row - pad_start[tile_expert])[:, None] + jnp.arange(tm, dtype=i32)[None, :]
    row_valid = (rank < counts[tile_expert][:, None]).reshape(p)
    src = jnp.clip(start[tile_expert][:, None] + rank, 0, n_assign - 1).reshape(p)
    tok_of_pos = jnp.where(row_valid, order[src] // 2, 0)
    offs = pad_start - start
    dest_sorted = jnp.arange(n_assign, dtype=i32) + jnp.sum(
        jnp.where(sorted_e[:, None] == experts[None, :], offs[None, :], 0), axis=1)
    _, pos_of_assign = lax.sort((order, dest_sorted), num_keys=1)
    return tile_expert, tile_valid, tok_of_pos, pos_of_assign


def _moe(t, route, w_gate, w_up, w_down):
    b, n, d = t.shape
    tok = b * n
    tile_expert, tile_valid, tok_of_pos, pos_of_assign = _dispatch_plan(route.reshape(tok, LANES), MOE_TILE)
    x_sorted = jnp.take(t.reshape(tok, d), tok_of_pos, axis=0)
    y_sorted = _experts(tile_expert, tile_valid, x_sorted, w_gate, w_up, w_down)
    return jnp.take(y_sorted, pos_of_assign, axis=0).reshape(b, n, 2 * d)


def _post_norm_kernel(h_ref, y_ref, r_ref, m_ref, ln_ref, o_ref):
    d = h_ref.shape[2]
    gate = m_ref[0, 5:6, :]
    r = r_ref[0]
    y = r[:, 0:1] * y_ref[0, :, 0:d].astype(F32) + r[:, 1:2] * y_ref[0, :, d:2 * d].astype(F32)
    o_ref[0] = _layer_norm(DEEPNORM_ALPHA * h_ref[0] + gate * y, ln_ref[0:1, :], ln_ref[1:2, :])


def _post_norm(h, y2, route, m, ln):
    b, n, d = h.shape
    tm = ROW_TILE
    row_spec = pl.BlockSpec((1, tm, d), lambda i, j: (i, j, 0))
    return pl.pallas_call(
        _post_norm_kernel,
        out_shape=jax.ShapeDtypeStruct((b, n, d), F32),
        grid=(b, n // tm),
        in_specs=[row_spec,
                  pl.BlockSpec((1, tm, 2 * d), lambda i, j: (i, j, 0)),
                  pl.BlockSpec((1, tm, LANES), lambda i, j: (i, j, 0)),
                  pl.BlockSpec((1, N_MOD, d), lambda i, j: (i, 0, 0)),
                  pl.BlockSpec((2, d), lambda i, j: (0, 0))],
        out_specs=row_spec,
        compiler_params=_cparams("arbitrary", "arbitrary"),
        name="post_norm",
    )(h, y2, route, m, ln)


def _rope_tables(n):
    t = jnp.arange(n)
    rows = (t // GRID_W).astype(F32)
    cols = (t % GRID_W).astype(F32)
    n_freq = RET_DK // 4
    inv_freq = ROPE_BASE ** (-jnp.arange(n_freq, dtype=F32) / n_freq)
    ang = jnp.concatenate([rows[:, None] * inv_freq, cols[:, None] * inv_freq], axis=-1)
    cos, sin = jnp.cos(ang), jnp.sin(ang)
    cos2 = jnp.concatenate([cos, cos], axis=-1)
    sin2 = jnp.concatenate([-sin, sin], axis=-1)
    q_scale = RET_DK ** -0.5
    return cos2 * q_scale, sin2 * q_scale, cos2, sin2


def _router_params(w_r1, b_r1, w_r2, b_r2):
    d = w_r1.shape[0]
    w2 = jnp.transpose(w_r2, (1, 0, 2)).reshape(d, MOE_EXPERTS)
    pad = LANES - MOE_GROUPS - MOE_EXPERTS
    wr = jnp.concatenate([w_r1, w2, jnp.zeros((d, pad), F32)], axis=-1)
    br = jnp.concatenate([b_r1, b_r2.reshape(MOE_EXPERTS), jnp.zeros((pad,), F32)]).reshape(1, LANES)
    wr_hi = wr.astype(BF16)
    wr_lo = (wr - wr_hi.astype(F32)).astype(BF16)
    return wr_hi, wr_lo, br


def kernel(x, c, ctx, c_ctx, w_mod, b_mod, ln_g, ln_b, ab_w_in, ab_w_out, ab_log_decay, ab_rpb, pool_w, pool_scale, moe_w_r1, moe_b_r1, moe_w_r2, moe_b_r2, moe_w_gate, moe_w_up, moe_w_down):
    b, n, d = x.shape
    cc = jnp.concatenate([c, c_ctx[None, :], jnp.zeros((MOD_ROWS - b - 1, d), F32)], axis=0)
    mod = _modulation(cc, w_mod, b_mod)
    h = x
    for i in range(DEPTH):
        j = i // 2
        m = mod[i, :b].reshape(b, N_MOD, d)
        ln1 = jnp.stack([ln_g[i, 0], ln_b[i, 0]])
        ln2 = jnp.stack([ln_g[i, 1], ln_b[i, 1]])
        router = _router_params(moe_w_r1[i], moe_b_r1[i], moe_w_r2[i], moe_b_r2[i])
        if i % 2 == 0:
            m_ctx = mod[i, b].reshape(1, N_MOD, d)
            w_in_bf = ab_w_in[j].astype(BF16)
            log_gamma2 = jnp.log1p(-jnp.exp(ab_log_decay[j].astype(F32)))
            rk, rv, nk, nv, rq, rg, nq = _in_proj(h, m, w_in_bf, _rope_tables(n))
            rk_c, rv_c, nk_c, nv_c = _ctx_proj(ctx, m_ctx, w_in_bf[:, :N_KV_GROUPS * HEAD_W])
            y_ret = _retention(log_gamma2, rq, rk, rv, rg, rk_c, rv_c)
            y_na = _neighbourhood_attention(nq, nk, nv, nk_c, nv_c, ab_rpb[j])
            h1, t, route = _out_proj(h, y_ret, y_na, ab_w_out[j].astype(BF16), m, ln1, router)
        else:
            h1, t, route = _pool_mixer(h, pool_w[j].astype(BF16), pool_scale[j], m, ln1, router)
        y2 = _moe(t, route, moe_w_gate[i], moe_w_up[i], moe_w_down[i])
        h = _post_norm(h1, y2, route, m, ln2)
    return h
```

```python
import functools

import numpy as np
import jax
import jax.numpy as jnp
from jax import lax
from jax.experimental import pallas as pl
from jax.experimental.pallas import tpu as pltpu

F32 = jnp.float32
BF16 = jnp.bfloat16
HIGHEST = lax.Precision.HIGHEST

D_MODEL = 1024
DEPTH = 2
GRID_W = 64
RET_HEADS = 4
RET_DK = 128
RET_CHUNK = 128
NA_HEADS = 8
NA_DH = 64
NA_WIN_R = 8
NA_WIN_C = 16
N_BIAS_ROWS = 2 * NA_WIN_R - 1
POOL_SIZES = (2, 4, 8, 16)
POOL_GROUP = D_MODEL // len(POOL_SIZES)
MOE_GROUPS = 4
MOE_PER_GROUP = 8
MOE_EXPERTS = MOE_GROUPS * MOE_PER_GROUP
MOE_HIDDEN = D_MODEL // 2
ROPE_BASE = 10000.0
LN_EPS = 1e-5
N_MOD = 6
DEEPNORM_ALPHA = (2 * DEPTH) ** 0.25
HEAD_W = 512
N_IN_GROUPS = 7
N_KV_GROUPS = 4
MASK_VALUE = -1e30

LANES = 128
VMEM_LIMIT = 56 * 1024 * 1024

NA_ROWS_PER_BLOCK = 4
ROW_TILE = 512
MOE_TILE = 512
MOD_ROWS = 24
MOD_COL_TILE = 1536


def _cparams(*sem):
    return pltpu.CompilerParams(dimension_semantics=sem, vmem_limit_bytes=VMEM_LIMIT)


def _silu(v):
    return v / (1.0 + jnp.exp(-v))


def _dot(a, b):
    return jnp.dot(a, b, preferred_element_type=F32)


def _dot_nt(a, b):
    return lax.dot_general(a, b, (((1,), (1,)), ((), ())), preferred_element_type=F32)


def _dot_tn(a, b):
    return lax.dot_general(a, b, (((0,), (0,)), ((), ())), preferred_element_type=F32)


def _mod_kernel(c_ref, w_ref, b_ref, o_ref):
    s = _silu(c_ref[...])
    o_ref[0] = jnp.dot(s, w_ref[0], precision=HIGHEST, preferred_element_type=F32) + b_ref[0]


def _modulation(cc, w_mod, b_mod):
    depth, d, n = w_mod.shape
    return pl.pallas_call(
        _mod_kernel,
        out_shape=jax.ShapeDtypeStruct((depth, MOD_ROWS, n), F32),
        grid=(depth, n // MOD_COL_TILE),
        in_specs=[
            pl.BlockSpec((MOD_ROWS, d), lambda i, j: (0, 0)),
            pl.BlockSpec((1, d, MOD_COL_TILE), lambda i, j: (i, 0, j)),
            pl.BlockSpec((1, 1, MOD_COL_TILE), lambda i, j: (i, 0, j)),
        ],
        out_specs=pl.BlockSpec((1, MOD_ROWS, MOD_COL_TILE), lambda i, j: (i, 0, j)),
        compiler_params=_cparams("arbitrary", "arbitrary"),
        name="modulation",
    )(cc, w_mod, b_mod.reshape(depth, 1, n))


def _rope(v, cos2, sin2):
    return v * cos2 + pltpu.roll(v, RET_DK // 2, axis=1) * sin2


def _in_proj_kernel(x_ref, m_ref, w_ref, cq_ref, sq_ref, ck_ref, sk_ref,
                    rk_ref, rv_ref, nk_ref, nv_ref, rq_ref, rg_ref, nq_ref):
    shift = m_ref[0, 0:1, :]
    scale = m_ref[0, 1:2, :]
    hm = (x_ref[0] * (1.0 + scale) + shift).astype(BF16)
    outs = (rk_ref, rv_ref, nk_ref, nv_ref, rq_ref, rg_ref, nq_ref)
    for g, o_ref in enumerate(outs):
        p = _dot(hm, w_ref[:, g * HEAD_W:(g + 1) * HEAD_W])
        if g == 0 or g == 4:
            cos2 = (ck_ref if g == 0 else cq_ref)[...]
            sin2 = (sk_ref if g == 0 else sq_ref)[...]
            for hd in range(RET_HEADS):
                sl = slice(hd * RET_DK, (hd + 1) * RET_DK)
                o_ref[0, :, sl] = _rope(p[:, sl], cos2, sin2).astype(BF16)
        elif g == 6:
            o_ref[0] = (p * NA_DH ** -0.5).astype(BF16)
        else:
            o_ref[0] = p.astype(BF16)


def _in_proj(x, m, w_in_bf, rope_tabs):
    b, n, d = x.shape
    tm = ROW_TILE
    tab_spec = pl.BlockSpec((tm, RET_DK), lambda i, j: (j, 0))
    out_spec = pl.BlockSpec((1, tm, HEAD_W), lambda i, j: (i, j, 0))
    return pl.pallas_call(
        _in_proj_kernel,
        out_shape=[jax.ShapeDtypeStruct((b, n, HEAD_W), BF16)] * N_IN_GROUPS,
        grid=(b, n // tm),
        in_specs=[
            pl.BlockSpec((1, tm, d), lambda i, j: (i, j, 0)),
            pl.BlockSpec((1, N_MOD, d), lambda i, j: (i, 0, 0)),
            pl.BlockSpec((d, N_IN_GROUPS * HEAD_W), lambda i, j: (0, 0)),
            tab_spec, tab_spec, tab_spec, tab_spec,
        ],
        out_specs=[out_spec] * N_IN_GROUPS,
        compiler_params=_cparams("arbitrary", "arbitrary"),
        name="in_proj",
    )(x, m, w_in_bf, *rope_tabs)


def _ctx_proj_kernel(x_ref, m_ref, w_ref, rk_ref, rv_ref, nk_ref, nv_ref):
    shift = m_ref[0, 0:1, :]
    scale = m_ref[0, 1:2, :]
    hm = (x_ref[0] * (1.0 + scale) + shift).astype(BF16)
    for g, o_ref in enumerate((rk_ref, rv_ref, nk_ref, nv_ref)):
        o_ref[0] = _dot(hm, w_ref[:, g * HEAD_W:(g + 1) * HEAD_W]).astype(BF16)


def _ctx_proj(ctx, m_ctx, w_in_bf):
    b, l, d = ctx.shape
    out_spec = pl.BlockSpec((1, l, HEAD_W), lambda i: (i, 0, 0))
    return pl.pallas_call(
        _ctx_proj_kernel,
        out_shape=[jax.ShapeDtypeStruct((b, l, HEAD_W), BF16)] * N_KV_GROUPS,
        grid=(b,),
        in_specs=[
            pl.BlockSpec((1, l, d), lambda i: (i, 0, 0)),
            pl.BlockSpec((1, N_MOD, d), lambda i: (0, 0, 0)),
            pl.BlockSpec((d, N_KV_GROUPS * HEAD_W), lambda i: (0, 0)),
        ],
        out_specs=[out_spec] * N_KV_GROUPS,
        compiler_params=_cparams("arbitrary"),
        name="ctx_proj",
    )(ctx, m_ctx, w_in_bf)


def _retention_kernel(lg_ref, q_ref, k_ref, v_ref, g_ref, kc_ref, vc_ref, o_ref, u_ref, s_ref):
    hd = pl.program_id(1)
    n = q_ref.shape[1]
    c = RET_CHUNK
    dk = RET_DK
    nc = n // c
    l = kc_ref.shape[1]
    lgf = lg_ref[0, hd]
    lgb = lg_ref[1, hd]

    ii = lax.broadcasted_iota(jnp.int32, (c, c), 0).astype(F32)
    jj = lax.broadcasted_iota(jnp.int32, (c, c), 1).astype(F32)
    diff = ii - jj
    decay = (jnp.where(diff >= 0, jnp.exp(lgf * jnp.maximum(diff, 0.0)), 0.0)
             + jnp.where(diff <= 0, jnp.exp(lgb * jnp.maximum(-diff, 0.0)), 0.0))
    idx = lax.broadcasted_iota(jnp.int32, (c, 1), 0).astype(F32)
    q_dec_f = jnp.exp(lgf * (idx + 1.0))
    k_dec_f = jnp.exp(lgf * (c - 1.0 - idx))
    q_dec_b = jnp.exp(lgb * (c - idx))
    k_dec_b = jnp.exp(lgb * idx)
    ones = jnp.ones((1, dk), F32)
    chunk_dec_f = jnp.exp(ones * (lgf * c))
    chunk_dec_b = jnp.exp(ones * (lgb * c))

    pos = lax.broadcasted_iota(jnp.int32, (l, 1), 0).astype(F32)
    kc = kc_ref[0].astype(F32)
    vc = vc_ref[0]
    s_f0 = _dot_tn((kc * jnp.exp(lgf * (l - 1.0 - pos))).astype(BF16), vc)
    s_b0 = _dot_tn((kc * jnp.exp(lgb * pos)).astype(BF16), vc)

    def chunk_rows(i):
        return pl.ds(pl.multiple_of(i * c, c), c)

    def kv_step(i, carry):
        rows = chunk_rows(i)
        k_i = k_ref[0, rows, :].astype(F32)
        kk = jnp.concatenate([(k_i * k_dec_f).astype(BF16), (k_i * k_dec_b).astype(BF16)], axis=1)
        u_ref[i] = _dot_tn(kk, v_ref[0, rows, :])
        return carry

    lax.fori_loop(0, nc, kv_step, 0, unroll=4)

    def scan_f(i, s):
        s_ref[i, :, 0:dk] = s.astype(BF16)
        return s * chunk_dec_f + u_ref[i, 0:dk, :]

    lax.fori_loop(0, nc, scan_f, s_f0, unroll=True)

    def scan_b(t, s):
        i = nc - 1 - t
        s_ref[i, :, dk:2 * dk] = s.astype(BF16)
        return s * chunk_dec_b + u_ref[i, dk:2 * dk, :]

    lax.fori_loop(0, nc, scan_b, s_b0, unroll=True)

    def out_step(i, carry):
        rows = chunk_rows(i)
        q_i = q_ref[0, rows, :]
        v_i = v_ref[0, rows, :]
        att = (_dot_nt(q_i, k_ref[0, rows, :]) * decay).astype(BF16)
        inter = _dot(q_i, s_ref[i])
        o = _dot(att, v_i) + inter[:, 0:dk] * q_dec_f + inter[:, dk:2 * dk] * q_dec_b
        mu = jnp.mean(o, axis=-1, keepdims=True)
        var = jnp.mean(jnp.square(o - mu), axis=-1, keepdims=True)
        o_n = (o - mu) * lax.rsqrt(var + LN_EPS)
        gate = g_ref[0, rows, :].astype(F32)
        o_ref[0, rows, :] = (_silu(gate) * o_n).astype(BF16)
        return carry

    lax.fori_loop(0, nc, out_step, 0, unroll=2)


def _retention(log_gamma2, rq, rk, rv, rg, rk_c, rv_c):
    b, n, _ = rq.shape
    l = rk_c.shape[1]
    nc = n // RET_CHUNK
    seq_spec = pl.BlockSpec((1, n, RET_DK), lambda i, h: (i, 0, h))
    ctx_spec = pl.BlockSpec((1, l, RET_DK), lambda i, h: (i, 0, h))
    return pl.pallas_call(
        _retention_kernel,
        out_shape=jax.ShapeDtypeStruct((b, n, HEAD_W), BF16),
        grid=(b, RET_HEADS),
        in_specs=[pl.BlockSpec(memory_space=pltpu.SMEM),
                  seq_spec, seq_spec, seq_spec, seq_spec, ctx_spec, ctx_spec],
        out_specs=seq_spec,
        scratch_shapes=[pltpu.VMEM((nc, 2 * RET_DK, RET_DK), F32),
                        pltpu.VMEM((nc, RET_DK, 2 * RET_DK), BF16)],
        compiler_params=_cparams("arbitrary", "arbitrary"),
        name="retention",
    )(log_gamma2, rq, rk, rv, rg, rk_c, rv_c)


def _na_geometry(rows):
    rb = NA_ROWS_PER_BLOCK
    wr = min(NA_WIN_R, rows)
    key_rows = min(rows, rb + wr - 1)
    variants, block_variant, block_start = [], [], []
    for kb in range(rows // rb):
        q_rows = kb * rb + np.arange(rb)
        r0 = np.clip(q_rows - wr // 2, 0, rows - wr)
        ks = int(np.clip(r0.min(), 0, rows - key_rows))
        assert r0.max() + wr <= ks + key_rows
        kr = ks + np.arange(key_rows)
        valid = (kr[None, :] >= r0[:, None]) & (kr[None, :] < r0[:, None] + wr)
        ridx = np.where(valid, kr[None, :] - q_rows[:, None] + NA_WIN_R - 1, N_BIAS_ROWS).astype(np.int32)
        for vi, v in enumerate(variants):
            if np.array_equal(v, ridx):
                block_variant.append(vi)
                break
        else:
            variants.append(ridx)
            block_variant.append(len(variants) - 1)
        block_start.append(ks)
    return key_rows, np.stack(variants), block_variant, block_start


def _na_bias_table(rpb, variant_rows):
    n_heads = rpb.shape[0]
    qc = np.arange(GRID_W)[:, None]
    kc = np.arange(GRID_W)[None, :]
    wstart = np.clip(qc - NA_WIN_C // 2, 0, GRID_W - NA_WIN_C)
    col_ok = (kc >= wstart) & (kc < wstart + NA_WIN_C)
    cidx = np.clip(kc - qc + NA_WIN_C - 1, 0, 2 * NA_WIN_C - 2)
    onehot = ((cidx[None] == np.arange(2 * NA_WIN_C - 1)[:, None, None]) & col_ok[None]).astype(np.float32)
    blocks = jnp.einsum("hrd,dqk->hrqk", rpb.astype(F32), jnp.asarray(onehot), precision=HIGHEST)
    blocks = jnp.where(col_ok[None, None], blocks, MASK_VALUE)
    blocks = jnp.concatenate([blocks, jnp.full((n_heads, 1, GRID_W, GRID_W), MASK_VALUE, F32)], axis=1)
    nvar, rb, key_rows = variant_rows.shape
    tab = blocks[:, variant_rows.reshape(-1)].reshape(n_heads, nvar, rb, key_rows, GRID_W, GRID_W)
    return jnp.transpose(tab, (0, 1, 2, 4, 3, 5)).reshape(n_heads, nvar, rb * GRID_W, key_rows * GRID_W)


def _na_kernel(q_ref, k_ref, v_ref, kc_ref, vc_ref, bias_ref, o_ref, *, block_variant, block_start, key_rows):
    qb = NA_ROWS_PER_BLOCK * GRID_W
    nk = key_rows * GRID_W
    lane = lax.broadcasted_iota(jnp.int32, (qb, LANES), 1)
    kc = kc_ref[0]
    vc = vc_ref[0]
    for kb, (var, ks) in enumerate(zip(block_variant, block_start)):
        q = q_ref[0, kb * qb:(kb + 1) * qb, :]
        k_win = k_ref[0, ks * GRID_W:ks * GRID_W + nk, :]
        v_win = v_ref[0, ks * GRID_W:ks * GRID_W + nk, :]
        outs = []
        for j in range(2):
            in_head = (lane >= j * NA_DH) & (lane < (j + 1) * NA_DH)
            qm = jnp.where(in_head, q, jnp.zeros_like(q))
            s_win = _dot_nt(qm, k_win) + bias_ref[j, var]
            s_ctx = _dot_nt(qm, kc)
            m = jnp.maximum(jnp.max(s_win, axis=-1, keepdims=True), jnp.max(s_ctx, axis=-1, keepdims=True))
            p_win = jnp.exp(s_win - m)
            p_ctx = jnp.exp(s_ctx - m)
            denom = jnp.sum(p_win, axis=-1, keepdims=True) + jnp.sum(p_ctx, axis=-1, keepdims=True)
            o = _dot(p_win.astype(BF16), v_win) + _dot(p_ctx.astype(BF16), vc)
            outs.append(o / denom)
        o_ref[0, kb * qb:(kb + 1) * qb, :] = jnp.where(lane < NA_DH, outs[0], outs[1]).astype(BF16)


def _neighbourhood_attention(nq, nk, nv, nk_c, nv_c, rpb):
    b, n, _ = nq.shape
    l = nk_c.shape[1]
    rows = n // GRID_W
    key_rows, variant_rows, block_variant, block_start = _na_geometry(rows)
    bias = _na_bias_table(rpb, variant_rows)
    nvar = variant_rows.shape[0]
    qb = NA_ROWS_PER_BLOCK * GRID_W
    nkeys = key_rows * GRID_W
    seq_spec = pl.BlockSpec((1, n, LANES), lambda h, i: (i, 0, h))
    ctx_spec = pl.BlockSpec((1, l, LANES), lambda h, i: (i, 0, h))
    kern = functools.partial(_na_kernel, block_variant=tuple(block_variant), block_start=tuple(block_start),
                             key_rows=key_rows)
    return pl.pallas_call(
        kern,
        out_shape=jax.ShapeDtypeStruct((b, n, HEAD_W), BF16),
        grid=(NA_HEADS // 2, b),
        in_specs=[seq_spec, seq_spec, seq_spec, ctx_spec, ctx_spec,
                  pl.BlockSpec((2, nvar, qb, nkeys), lambda h, i: (h, 0, 0, 0))],
        out_specs=seq_spec,
        compiler_params=_cparams("arbitrary", "arbitrary"),
        name="neighbourhood_attention",
    )(nq, nk, nv, nk_c, nv_c, bias)


def _layer_norm(z, g, b):
    mu = jnp.mean(z, axis=-1, keepdims=True)
    var = jnp.mean(jnp.square(z - mu), axis=-1, keepdims=True)
    return (z - mu) * lax.rsqrt(var + LN_EPS) * g + b


def _split_bf16(v):
    hi = v.astype(BF16)
    return hi, (v - hi.astype(F32)).astype(BF16)


def _route(t, wr_ref, br_ref):
    t_hi, t_lo = _split_bf16(t)
    w_hi, w_lo = _split_bf16(wr_ref[...])
    logits = _dot(t_hi, w_hi) + (_dot(t_lo, w_hi) + _dot(t_hi, w_lo)) + br_ref[...]
    lane = lax.broadcasted_iota(jnp.int32, logits.shape, 1).astype(F32)
    neg = -jnp.inf
    big = float(LANES)

    def first_max(vals):
        vmax = jnp.max(vals, axis=-1, keepdims=True)
        return vmax, jnp.min(jnp.where(vals == vmax, lane, big), axis=-1, keepdims=True)

    is_grp = lane < MOE_GROUPS
    g_max, grp = first_max(jnp.where(is_grp, logits, neg))
    g_sum = jnp.sum(jnp.where(is_grp, jnp.exp(logits - g_max), 0.0), axis=-1, keepdims=True)
    gate_g = 1.0 / g_sum
    lo = MOE_GROUPS + grp * MOE_PER_GROUP
    in_grp = (lane >= lo) & (lane < lo + MOE_PER_GROUP)
    le = jnp.where(in_grp, logits, neg)
    v1, i1 = first_max(le)
    v2, i2 = first_max(jnp.where(lane == i1, neg, le))
    e21 = jnp.exp(v2 - v1)
    w1 = gate_g / (1.0 + e21)
    w2 = gate_g * e21 / (1.0 + e21)
    return jnp.where(lane == 0, w1,
                     jnp.where(lane == 1, w2,
                               jnp.where(lane == 2, i1 - MOE_GROUPS,
                                         jnp.where(lane == 3, i2 - MOE_GROUPS, 0.0))))


def _post_norm_route(h, y, m_ref, ln_ref, wr_ref, br_ref, h_out, t_out, r_out):
    gate = m_ref[0, 2:3, :]
    shift = m_ref[0, 3:4, :]
    scale = m_ref[0, 4:5, :]
    h1 = _layer_norm(DEEPNORM_ALPHA * h + gate * y, ln_ref[0:1, :], ln_ref[1:2, :])
    t = h1 * (1.0 + scale) + shift
    h_out[0] = h1
    t_out[0] = t.astype(BF16)
    r_out[0] = _route(t, wr_ref, br_ref)


def _epilogue_specs(b, n, d, tm):
    in_specs = [
        pl.BlockSpec((1, N_MOD, d), lambda i, j: (i, 0, 0)),
        pl.BlockSpec((2, d), lambda i, j: (0, 0)),
        pl.BlockSpec((d, LANES), lambda i, j: (0, 0)),
        pl.BlockSpec((1, LANES), lambda i, j: (0, 0)),
    ]
    out_shape = [jax.ShapeDtypeStruct((b, n, d), F32), jax.ShapeDtypeStruct((b, n, d), BF16),
                 jax.ShapeDtypeStruct((b, n, LANES), F32)]
    out_specs = [pl.BlockSpec((1, tm, d), lambda i, j: (i, j, 0)),
                 pl.BlockSpec((1, tm, d), lambda i, j: (i, j, 0)),
                 pl.BlockSpec((1, tm, LANES), lambda i, j: (i, j, 0))]
    return in_specs, out_shape, out_specs


def _out_proj_kernel(x_ref, yr_ref, yn_ref, w_ref, m_ref, ln_ref, wr_ref, br_ref, h_out, t_out, r_out):
    y = _dot(yr_ref[0], w_ref[0:HEAD_W, :]) + _dot(yn_ref[0], w_ref[HEAD_W:2 * HEAD_W, :])
    _post_norm_route(x_ref[0], y, m_ref, ln_ref, wr_ref, br_ref, h_out, t_out, r_out)


def _out_proj(x, y_ret, y_na, w_out_bf, m, ln, router):
    b, n, d = x.shape
    tm = ROW_TILE
    ep_in, out_shape, out_specs = _epilogue_specs(b, n, d, tm)
    return pl.pallas_call(
        _out_proj_kernel,
        out_shape=out_shape,
        grid=(b, n // tm),
        in_specs=[
            pl.BlockSpec((1, tm, d), lambda i, j: (i, j, 0)),
            pl.BlockSpec((1, tm, HEAD_W), lambda i, j: (i, j, 0)),
            pl.BlockSpec((1, tm, HEAD_W), lambda i, j: (i, j, 0)),
            pl.BlockSpec((2 * HEAD_W, d), lambda i, j: (0, 0)),
        ] + ep_in,
        out_specs=out_specs,
        compiler_params=_cparams("arbitrary", "arbitrary"),
        name="out_proj",
    )(x, y_ret, y_na, w_out_bf, m, ln, *router)


POOL_HALO = max(POOL_SIZES) // 2
POOL_TILE = 256


def _pool_kernel(x_ref, prev_ref, next_ref, pw_ref, ps_ref, m_ref, ln_ref, wr_ref, br_ref,
                 h_out, t_out, r_out, *, n):
    j = pl.program_id(1)
    nj = pl.num_programs(1)
    tm = x_ref.shape[1]
    halo = POOL_HALO
    shift = m_ref[0, 0:1, :]
    scale = m_ref[0, 1:2, :]
    x = x_ref[0]
    hm = x * (1.0 + scale) + shift
    prev = jnp.where(j > 0, prev_ref[0] * (1.0 + scale) + shift, 0.0)
    nxt = jnp.where(j < nj - 1, next_ref[0] * (1.0 + scale) + shift, 0.0)
    ext = jnp.concatenate([prev, hm, nxt], axis=0)
    pos = (j * tm + lax.broadcasted_iota(jnp.int32, (tm, 1), 0))
    ys = []
    for g, w in enumerate(POOL_SIZES):
        cols = slice(g * POOL_GROUP, (g + 1) * POOL_GROUP)
        s = ext[:, cols]
        span = 1
        while span < w:
            s = s[:s.shape[0] - span] + s[span:]
            span *= 2
        off = halo - w // 2
        win = s[off:off + tm]
        cnt = (jnp.minimum(pos + (w - w // 2), n) - jnp.maximum(pos - w // 2, 0)).astype(F32)
        z = (win / cnt - hm[:, cols]).astype(BF16)
        ys.append(_dot(z, pw_ref[g]))
    y = jnp.concatenate(ys, axis=-1) * ps_ref[...]
    _post_norm_route(x, y, m_ref, ln_ref, wr_ref, br_ref, h_out, t_out, r_out)


def _pool_mixer(h, pool_w_bf, pool_scale, m, ln, router):
    b, n, d = h.shape
    tm = POOL_TILE
    halo = POOL_HALO
    blocks_per_tile = tm // halo
    n_halo_blocks = n // halo
    ep_in, out_shape, out_specs = _epilogue_specs(b, n, d, tm)
    return pl.pallas_call(
        functools.partial(_pool_kernel, n=n),
        out_shape=out_shape,
        grid=(b, n // tm),
        in_specs=[
            pl.BlockSpec((1, tm, d), lambda i, j: (i, j, 0)),
            pl.BlockSpec((1, halo, d), lambda i, j: (i, jnp.maximum(j * blocks_per_tile - 1, 0), 0)),
            pl.BlockSpec((1, halo, d),
                         lambda i, j: (i, jnp.minimum((j + 1) * blocks_per_tile, n_halo_blocks - 1), 0)),
            pl.BlockSpec((len(POOL_SIZES), POOL_GROUP, POOL_GROUP), lambda i, j: (0, 0, 0)),
            pl.BlockSpec((1, d), lambda i, j: (0, 0)),
        ] + ep_in,
        out_specs=out_specs,
        compiler_params=_cparams("arbitrary", "arbitrary"),
        name="pool_mixer",
    )(h, h, h, pool_w_bf, pool_scale.reshape(1, d), m, ln, *router)


def _expert_kernel(te_ref, tv_ref, x_ref, wg_ref, wu_ref, wd_ref, o_ref, wg_s, wu_s, wd_s):
    j = pl.program_id(0)
    prev_e = te_ref[jnp.maximum(j - 1, 0)]

    @pl.when((j == 0) | (te_ref[j] != prev_e))
    def _():
        wg_s[...] = wg_ref[0, 0].astype(BF16)
        wu_s[...] = wu_ref[0, 0].astype(BF16)
        wd_s[...] = wd_ref[0, 0].astype(BF16)

    @pl.when(tv_ref[j] == 1)
    def _():
        x = x_ref[...]
        gate = _dot(x, wg_s[...])
        up = _dot(x, wu_s[...])
        act = (_silu(gate) * up).astype(BF16)
        o_ref[...] = _dot(act, wd_s[...]).astype(BF16)

    @pl.when(tv_ref[j] == 0)
    def _():
        o_ref[...] = jnp.zeros_like(o_ref)


def _experts(tile_expert, tile_valid, x_sorted, w_gate, w_up, w_down, layer):
    p, d = x_sorted.shape
    tm = MOE_TILE
    hid = w_gate.shape[-1]
    grid_spec = pltpu.PrefetchScalarGridSpec(
        num_scalar_prefetch=2,
        grid=(p // tm,),
        in_specs=[
            pl.BlockSpec((tm, d), lambda j, te, tv: (j, 0)),
            pl.BlockSpec((1, 1, d, hid), lambda j, te, tv: (layer, te[j], 0, 0)),
            pl.BlockSpec((1, 1, d, hid), lambda j, te, tv: (layer, te[j], 0, 0)),
            pl.BlockSpec((1, 1, hid, d), lambda j, te, tv: (layer, te[j], 0, 0)),
        ],
        out_specs=pl.BlockSpec((tm, d), lambda j, te, tv: (j, 0)),
        scratch_shapes=[pltpu.VMEM((d, hid), BF16), pltpu.VMEM((d, hid), BF16), pltpu.VMEM((hid, d), BF16)],
    )
    return pl.pallas_call(
        _expert_kernel,
        out_shape=jax.ShapeDtypeStruct((p, d), BF16),
        grid_spec=grid_spec,
        compiler_params=_cparams("arbitrary"),
        name="experts",
    )(tile_expert, tile_valid, x_sorted, w_gate, w_up, w_down)


def _dispatch_plan(route, tm):
    tok = route.shape[0]
    n_assign = 2 * tok
    p = n_assign + MOE_EXPERTS * tm
    n_tiles = p // tm
    i32 = jnp.int32
    eid = jnp.transpose(route[:, 2:4]).astype(i32).reshape(n_assign)
    sorted_e, order = lax.sort((eid, jnp.arange(n_assign, dtype=i32)), num_keys=1)
    experts = jnp.arange(MOE_EXPERTS, dtype=i32)
    counts = jnp.sum((eid[:, None] == experts[None, :]).astype(i32), axis=0)
    padded = ((counts + tm - 1) // tm) * tm
    pad_end = jnp.cumsum(padded)
    pad_start = pad_end - padded
    start = jnp.cumsum(counts) - counts
    tile_row = jnp.arange(n_tiles, dtype=i32) * tm
    tile_expert = jnp.minimum(jnp.sum((pad_end[None, :] <= tile_row[:, None]).astype(i32), axis=1), MOE_EXPERTS - 1)
    tile_valid = (tile_row < pad_end[-1]).astype(i32)
    rank = (tile_row - pad_start[tile_expert])[:, None] + jnp.arange(tm, dtype=i32)[None, :]
    row_valid = (rank < counts[tile_expert][:, None]).reshape(p)
    src = jnp.clip(start[tile_expert][:, None] + rank, 0, n_assign - 1).reshape(p)
    src_assign = order.at[src].get(mode="promise_in_bounds")
    tok_of_pos = jnp.where(row_valid, src_assign, jnp.arange(p, dtype=i32)) % tok
    offs = pad_start - start
    dest_sorted = jnp.arange(n_assign, dtype=i32) + jnp.sum(
        jnp.where(sorted_e[:, None] == experts[None, :], offs[None, :], 0), axis=1)
    _, pos_of_assign = lax.sort((order, dest_sorted), num_keys=1)
    return tile_expert, tile_valid, tok_of_pos, pos_of_assign


def _moe(t, route, w_gate, w_up, w_down, layer):
    b, n, d = t.shape
    tok = b * n
    tile_expert, tile_valid, tok_of_pos, pos_of_assign = _dispatch_plan(route.reshape(tok, LANES), MOE_TILE)
    x_sorted = t.reshape(tok, d).at[tok_of_pos].get(mode="promise_in_bounds")
    y_sorted = _experts(tile_expert, tile_valid, x_sorted, w_gate, w_up, w_down, layer)
    return y_sorted.at[pos_of_assign].get(mode="promise_in_bounds")


def _post_norm_kernel(h_ref, y0_ref, y1_ref, r_ref, m_ref, ln_ref, o_ref):
    gate = m_ref[0, 5:6, :]
    r = r_ref[0]
    y = r[:, 0:1] * y0_ref[...].astype(F32) + r[:, 1:2] * y1_ref[...].astype(F32)
    o_ref[0] = _layer_norm(DEEPNORM_ALPHA * h_ref[0] + gate * y, ln_ref[0:1, :], ln_ref[1:2, :])


def _post_norm(h, y2, route, m, ln):
    b, n, d = h.shape
    tm = ROW_TILE
    tiles_per_seq = n // tm
    tiles_per_k = b * tiles_per_seq
    row_spec = pl.BlockSpec((1, tm, d), lambda i, j: (i, j, 0))
    return pl.pallas_call(
        _post_norm_kernel,
        out_shape=jax.ShapeDtypeStruct((b, n, d), F32),
        grid=(b, tiles_per_seq),
        in_specs=[row_spec,
                  pl.BlockSpec((tm, d), lambda i, j: (i * tiles_per_seq + j, 0)),
                  pl.BlockSpec((tm, d), lambda i, j: (tiles_per_k + i * tiles_per_seq + j, 0)),
                  pl.BlockSpec((1, tm, LANES), lambda i, j: (i, j, 0)),
                  pl.BlockSpec((1, N_MOD, d), lambda i, j: (i, 0, 0)),
                  pl.BlockSpec((2, d), lambda i, j: (0, 0))],
        out_specs=row_spec,
        compiler_params=_cparams("arbitrary", "arbitrary"),
        name="post_norm",
    )(h, y2, y2, route, m, ln)


def _rope_tables(n):
    t = jnp.arange(n)
    rows = (t // GRID_W).astype(F32)
    cols = (t % GRID_W).astype(F32)
    n_freq = RET_DK // 4
    inv_freq = ROPE_BASE ** (-jnp.arange(n_freq, dtype=F32) / n_freq)
    ang = jnp.concatenate([rows[:, None] * inv_freq, cols[:, None] * inv_freq], axis=-1)
    cos, sin = jnp.cos(ang), jnp.sin(ang)
    cos2 = jnp.concatenate([cos, cos], axis=-1)
    sin2 = jnp.concatenate([-sin, sin], axis=-1)
    q_scale = RET_DK ** -0.5
    return cos2 * q_scale, sin2 * q_scale, cos2, sin2


def _router_params(w_r1, b_r1, w_r2, b_r2):
    d = w_r1.shape[0]
    w2 = jnp.transpose(w_r2, (1, 0, 2)).reshape(d, MOE_EXPERTS)
    pad = LANES - MOE_GROUPS - MOE_EXPERTS
    wr = jnp.concatenate([w_r1, w2, jnp.zeros((d, pad), F32)], axis=-1)
    br = jnp.concatenate([b_r1, b_r2.reshape(MOE_EXPERTS), jnp.zeros((pad,), F32)]).reshape(1, LANES)
    return wr, br


def kernel(x, c, ctx, c_ctx, w_mod, b_mod, ln_g, ln_b, ab_w_in, ab_w_out, ab_log_decay, ab_rpb, pool_w, pool_scale, moe_w_r1, moe_b_r1, moe_w_r2, moe_b_r2, moe_w_gate, moe_w_up, moe_w_down):
    b, n, d = x.shape
    cc = jnp.concatenate([c, c_ctx[None, :], jnp.zeros((MOD_ROWS - b - 1, d), F32)], axis=0)
    mod = _modulation(cc, w_mod, b_mod)
    h = x
    for i in range(DEPTH):
        j = i // 2
        m = mod[i, :b].reshape(b, N_MOD, d)
        ln1 = jnp.stack([ln_g[i, 0], ln_b[i, 0]])
        ln2 = jnp.stack([ln_g[i, 1], ln_b[i, 1]])
        router = _router_params(moe_w_r1[i], moe_b_r1[i], moe_w_r2[i], moe_b_r2[i])
        if i % 2 == 0:
            m_ctx = mod[i, b].reshape(1, N_MOD, d)
            w_in_bf = ab_w_in[j].astype(BF16)
            log_gamma2 = jnp.log1p(-jnp.exp(ab_log_decay[j].astype(F32)))
            rk, rv, nk, nv, rq, rg, nq = _in_proj(h, m, w_in_bf, _rope_tables(n))
            rk_c, rv_c, nk_c, nv_c = _ctx_proj(ctx, m_ctx, w_in_bf[:, :N_KV_GROUPS * HEAD_W])
            y_ret = _retention(log_gamma2, rq, rk, rv, rg, rk_c, rv_c)
            y_na = _neighbourhood_attention(nq, nk, nv, nk_c, nv_c, ab_rpb[j])
            h1, t, route = _out_proj(h, y_ret, y_na, ab_w_out[j].astype(BF16), m, ln1, router)
        else:
            h1, t, route = _pool_mixer(h, pool_w[j].astype(BF16), pool_scale[j], m, ln1, router)
        y2 = _moe(t, route, moe_w_gate, moe_w_up, moe_w_down, i)
        h = _post_norm(h1, y2, route, m, ln2)
    return h
```

```python
import functools

import numpy as np
import jax
import jax.numpy as jnp
from jax import lax
from jax.experimental import pallas as pl
from jax.experimental.pallas import tpu as pltpu

F32 = jnp.float32
BF16 = jnp.bfloat16
HIGHEST = lax.Precision.HIGHEST

D_MODEL = 1024
DEPTH = 2
GRID_W = 64
RET_HEADS = 4
RET_DK = 128
RET_CHUNK = 128
NA_HEADS = 8
NA_DH = 64
NA_WIN_R = 8
NA_WIN_C = 16
N_BIAS_ROWS = 2 * NA_WIN_R - 1
POOL_SIZES = (2, 4, 8, 16)
POOL_GROUP = D_MODEL // len(POOL_SIZES)
MOE_GROUPS = 4
MOE_PER_GROUP = 8
MOE_EXPERTS = MOE_GROUPS * MOE_PER_GROUP
MOE_HIDDEN = D_MODEL // 2
ROPE_BASE = 10000.0
LN_EPS = 1e-5
N_MOD = 6
DEEPNORM_ALPHA = (2 * DEPTH) ** 0.25
HEAD_W = 512
N_IN_GROUPS = 7
N_KV_GROUPS = 4
MASK_VALUE = -1e30

LANES = 128
VMEM_LIMIT = 56 * 1024 * 1024

NA_ROWS_PER_BLOCK = 4
ROW_TILE = 512
MOE_TILE = 512
MOD_ROWS = 24
MOD_COL_TILE = 1536


def _cparams(*sem):
    return pltpu.CompilerParams(dimension_semantics=sem, vmem_limit_bytes=VMEM_LIMIT)


def _silu(v):
    return v / (1.0 + jnp.exp(-v))


def _dot(a, b):
    return jnp.dot(a, b, preferred_element_type=F32)


def _dot_nt(a, b):
    return lax.dot_general(a, b, (((1,), (1,)), ((), ())), preferred_element_type=F32)


def _dot_tn(a, b):
    return lax.dot_general(a, b, (((0,), (0,)), ((), ())), preferred_element_type=F32)


def _mod_kernel(c_ref, w_ref, b_ref, o_ref):
    s = _silu(c_ref[...])
    o_ref[0] = jnp.dot(s, w_ref[0], precision=HIGHEST, preferred_element_type=F32) + b_ref[0]


def _modulation(cc, w_mod, b_mod):
    depth, d, n = w_mod.shape
    return pl.pallas_call(
        _mod_kernel,
        out_shape=jax.ShapeDtypeStruct((depth, MOD_ROWS, n), F32),
        grid=(depth, n // MOD_COL_TILE),
        in_specs=[
            pl.BlockSpec((MOD_ROWS, d), lambda i, j: (0, 0)),
            pl.BlockSpec((1, d, MOD_COL_TILE), lambda i, j: (i, 0, j)),
            pl.BlockSpec((1, 1, MOD_COL_TILE), lambda i, j: (i, 0, j)),
        ],
        out_specs=pl.BlockSpec((1, MOD_ROWS, MOD_COL_TILE), lambda i, j: (i, 0, j)),
        compiler_params=_cparams("arbitrary", "arbitrary"),
        name="modulation",
    )(cc, w_mod, b_mod.reshape(depth, 1, n))


def _rope(v, cos2, sin2):
    return v * cos2 + pltpu.roll(v, RET_DK // 2, axis=1) * sin2


def _in_proj_kernel(x_ref, m_ref, w_ref, cq_ref, sq_ref, ck_ref, sk_ref,
                    rk_ref, rv_ref, nk_ref, nv_ref, rq_ref, rg_ref, nq_ref):
    shift = m_ref[0, 0:1, :]
    scale = m_ref[0, 1:2, :]
    hm = (x_ref[0] * (1.0 + scale) + shift).astype(BF16)
    outs = (rk_ref, rv_ref, nk_ref, nv_ref, rq_ref, rg_ref, nq_ref)
    for g, o_ref in enumerate(outs):
        p = _dot(hm, w_ref[:, g * HEAD_W:(g + 1) * HEAD_W])
        if g == 0 or g == 4:
            cos2 = (ck_ref if g == 0 else cq_ref)[...]
            sin2 = (sk_ref if g == 0 else sq_ref)[...]
            for hd in range(RET_HEADS):
                sl = slice(hd * RET_DK, (hd + 1) * RET_DK)
                o_ref[0, :, sl] = _rope(p[:, sl], cos2, sin2).astype(BF16)
        elif g == 6:
            o_ref[0] = (p * NA_DH ** -0.5).astype(BF16)
        else:
            o_ref[0] = p.astype(BF16)


def _in_proj(x, m, w_in_bf, rope_tabs):
    b, n, d = x.shape
    tm = ROW_TILE
    tab_spec = pl.BlockSpec((tm, RET_DK), lambda i, j: (j, 0))
    out_spec = pl.BlockSpec((1, tm, HEAD_W), lambda i, j: (i, j, 0))
    return pl.pallas_call(
        _in_proj_kernel,
        out_shape=[jax.ShapeDtypeStruct((b, n, HEAD_W), BF16)] * N_IN_GROUPS,
        grid=(b, n // tm),
        in_specs=[
            pl.BlockSpec((1, tm, d), lambda i, j: (i, j, 0)),
            pl.BlockSpec((1, N_MOD, d), lambda i, j: (i, 0, 0)),
            pl.BlockSpec((d, N_IN_GROUPS * HEAD_W), lambda i, j: (0, 0)),
            tab_spec, tab_spec, tab_spec, tab_spec,
        ],
        out_specs=[out_spec] * N_IN_GROUPS,
        compiler_params=_cparams("arbitrary", "arbitrary"),
        name="in_proj",
    )(x, m, w_in_bf, *rope_tabs)


def _ctx_proj_kernel(x_ref, m_ref, w_ref, rk_ref, rv_ref, nk_ref, nv_ref):
    shift = m_ref[0, 0:1, :]
    scale = m_ref[0, 1:2, :]
    hm = (x_ref[0] * (1.0 + scale) + shift).astype(BF16)
    for g, o_ref in enumerate((rk_ref, rv_ref, nk_ref, nv_ref)):
        o_ref[0] = _dot(hm, w_ref[:, g * HEAD_W:(g + 1) * HEAD_W]).astype(BF16)


def _ctx_proj(ctx, m_ctx, w_in_bf):
    b, l, d = ctx.shape
    out_spec = pl.BlockSpec((1, l, HEAD_W), lambda i: (i, 0, 0))
    return pl.pallas_call(
        _ctx_proj_kernel,
        out_shape=[jax.ShapeDtypeStruct((b, l, HEAD_W), BF16)] * N_KV_GROUPS,
        grid=(b,),
        in_specs=[
            pl.BlockSpec((1, l, d), lambda i: (i, 0, 0)),
            pl.BlockSpec((1, N_MOD, d), lambda i: (0, 0, 0)),
            pl.BlockSpec((d, N_KV_GROUPS * HEAD_W), lambda i: (0, 0)),
        ],
        out_specs=[out_spec] * N_KV_GROUPS,
        compiler_params=_cparams("arbitrary"),
        name="ctx_proj",
    )(ctx, m_ctx, w_in_bf)


def _retention_kernel(lg_ref, q_ref, k_ref, v_ref, g_ref, kc_ref, vc_ref, o_ref, u_ref, s_ref):
    hd = pl.program_id(1)
    n = q_ref.shape[1]
    c = RET_CHUNK
    dk = RET_DK
    nc = n // c
    l = kc_ref.shape[1]
    lgf = lg_ref[0, hd]
    lgb = lg_ref[1, hd]

    ii = lax.broadcasted_iota(jnp.int32, (c, c), 0).astype(F32)
    jj = lax.broadcasted_iota(jnp.int32, (c, c), 1).astype(F32)
    diff = ii - jj
    decay = (jnp.where(diff >= 0, jnp.exp(lgf * jnp.maximum(diff, 0.0)), 0.0)
             + jnp.where(diff <= 0, jnp.exp(lgb * jnp.maximum(-diff, 0.0)), 0.0))
    idx = lax.broadcasted_iota(jnp.int32, (c, 1), 0).astype(F32)
    q_dec_f = jnp.exp(lgf * (idx + 1.0))
    k_dec_f = jnp.exp(lgf * (c - 1.0 - idx))
    q_dec_b = jnp.exp(lgb * (c - idx))
    k_dec_b = jnp.exp(lgb * idx)
    ones = jnp.ones((1, dk), F32)
    chunk_dec_f = jnp.exp(ones * (lgf * c))
    chunk_dec_b = jnp.exp(ones * (lgb * c))

    pos = lax.broadcasted_iota(jnp.int32, (l, 1), 0).astype(F32)
    kc = kc_ref[0].astype(F32)
    vc = vc_ref[0]
    s_f0 = _dot_tn((kc * jnp.exp(lgf * (l - 1.0 - pos))).astype(BF16), vc)
    s_b0 = _dot_tn((kc * jnp.exp(lgb * pos)).astype(BF16), vc)

    def chunk_rows(i):
        return pl.ds(pl.multiple_of(i * c, c), c)

    def kv_step(i, carry):
        rows = chunk_rows(i)
        k_i = k_ref[0, rows, :].astype(F32)
        kk = jnp.concatenate([(k_i * k_dec_f).astype(BF16), (k_i * k_dec_b).astype(BF16)], axis=1)
        u_ref[i] = _dot_tn(kk, v_ref[0, rows, :])
        return carry

    lax.fori_loop(0, nc, kv_step, 0, unroll=4)

    def scan_f(i, s):
        s_ref[i, :, 0:dk] = s.astype(BF16)
        return s * chunk_dec_f + u_ref[i, 0:dk, :]

    lax.fori_loop(0, nc, scan_f, s_f0, unroll=True)

    def scan_b(t, s):
        i = nc - 1 - t
        s_ref[i, :, dk:2 * dk] = s.astype(BF16)
        return s * chunk_dec_b + u_ref[i, dk:2 * dk, :]

    lax.fori_loop(0, nc, scan_b, s_b0, unroll=True)

    def out_step(i, carry):
        rows = chunk_rows(i)
        q_i = q_ref[0, rows, :]
        v_i = v_ref[0, rows, :]
        att = (_dot_nt(q_i, k_ref[0, rows, :]) * decay).astype(BF16)
        inter = _dot(q_i, s_ref[i])
        o = _dot(att, v_i) + inter[:, 0:dk] * q_dec_f + inter[:, dk:2 * dk] * q_dec_b
        mu = jnp.mean(o, axis=-1, keepdims=True)
        var = jnp.mean(jnp.square(o - mu), axis=-1, keepdims=True)
        o_n = (o - mu) * lax.rsqrt(var + LN_EPS)
        gate = g_ref[0, rows, :].astype(F32)
        o_ref[0, rows, :] = (_silu(gate) * o_n).astype(BF16)
        return carry

    lax.fori_loop(0, nc, out_step, 0, unroll=2)


def _retention(log_gamma2, rq, rk, rv, rg, rk_c, rv_c):
    b, n, _ = rq.shape
    l = rk_c.shape[1]
    nc = n // RET_CHUNK
    seq_spec = pl.BlockSpec((1, n, RET_DK), lambda i, h: (i, 0, h))
    ctx_spec = pl.BlockSpec((1, l, RET_DK), lambda i, h: (i, 0, h))
    return pl.pallas_call(
        _retention_kernel,
        out_shape=jax.ShapeDtypeStruct((b, n, HEAD_W), BF16),
        grid=(b, RET_HEADS),
        in_specs=[pl.BlockSpec(memory_space=pltpu.SMEM),
                  seq_spec, seq_spec, seq_spec, seq_spec, ctx_spec, ctx_spec],
        out_specs=seq_spec,
        scratch_shapes=[pltpu.VMEM((nc, 2 * RET_DK, RET_DK), F32),
                        pltpu.VMEM((nc, RET_DK, 2 * RET_DK), BF16)],
        compiler_params=_cparams("arbitrary", "arbitrary"),
        name="retention",
    )(log_gamma2, rq, rk, rv, rg, rk_c, rv_c)


def _na_geometry(rows):
    rb = NA_ROWS_PER_BLOCK
    wr = min(NA_WIN_R, rows)
    key_rows = min(rows, rb + wr - 1)
    variants, block_variant, block_start = [], [], []
    for kb in range(rows // rb):
        q_rows = kb * rb + np.arange(rb)
        r0 = np.clip(q_rows - wr // 2, 0, rows - wr)
        ks = int(np.clip(r0.min(), 0, rows - key_rows))
        assert r0.max() + wr <= ks + key_rows
        kr = ks + np.arange(key_rows)
        valid = (kr[None, :] >= r0[:, None]) & (kr[None, :] < r0[:, None] + wr)
        ridx = np.where(valid, kr[None, :] - q_rows[:, None] + NA_WIN_R - 1, N_BIAS_ROWS).astype(np.int32)
        for vi, v in enumerate(variants):
            if np.array_equal(v, ridx):
                block_variant.append(vi)
                break
        else:
            variants.append(ridx)
            block_variant.append(len(variants) - 1)
        block_start.append(ks)
    return key_rows, np.stack(variants), block_variant, block_start


def _na_bias_table(rpb, variant_rows):
    n_heads = rpb.shape[0]
    qc = np.arange(GRID_W)[:, None]
    kc = np.arange(GRID_W)[None, :]
    wstart = np.clip(qc - NA_WIN_C // 2, 0, GRID_W - NA_WIN_C)
    col_ok = (kc >= wstart) & (kc < wstart + NA_WIN_C)
    cidx = np.clip(kc - qc + NA_WIN_C - 1, 0, 2 * NA_WIN_C - 2)
    onehot = ((cidx[None] == np.arange(2 * NA_WIN_C - 1)[:, None, None]) & col_ok[None]).astype(np.float32)
    blocks = jnp.einsum("hrd,dqk->hrqk", rpb.astype(F32), jnp.asarray(onehot), precision=HIGHEST)
    blocks = jnp.where(col_ok[None, None], blocks, MASK_VALUE)
    blocks = jnp.concatenate([blocks, jnp.full((n_heads, 1, GRID_W, GRID_W), MASK_VALUE, F32)], axis=1)
    nvar, rb, key_rows = variant_rows.shape
    tab = blocks[:, variant_rows.reshape(-1)].reshape(n_heads, nvar, rb, key_rows, GRID_W, GRID_W)
    return jnp.transpose(tab, (0, 1, 2, 4, 3, 5)).reshape(n_heads, nvar, rb * GRID_W, key_rows * GRID_W)


def _na_kernel(q_ref, k_ref, v_ref, kc_ref, vc_ref, bias_ref, o_ref, *, block_variant, block_start, key_rows):
    qb = NA_ROWS_PER_BLOCK * GRID_W
    nk = key_rows * GRID_W
    lane = lax.broadcasted_iota(jnp.int32, (qb, LANES), 1)
    kc = kc_ref[0]
    vc = vc_ref[0]
    for kb, (var, ks) in enumerate(zip(block_variant, block_start)):
        q = q_ref[0, kb * qb:(kb + 1) * qb, :]
        k_win = k_ref[0, ks * GRID_W:ks * GRID_W + nk, :]
        v_win = v_ref[0, ks * GRID_W:ks * GRID_W + nk, :]
        outs = []
        for j in range(2):
            in_head = (lane >= j * NA_DH) & (lane < (j + 1) * NA_DH)
            qm = jnp.where(in_head, q, jnp.zeros_like(q))
            s_win = _dot_nt(qm, k_win) + bias_ref[j, var]
            s_ctx = _dot_nt(qm, kc)
            m = jnp.maximum(jnp.max(s_win, axis=-1, keepdims=True), jnp.max(s_ctx, axis=-1, keepdims=True))
            p_win = jnp.exp(s_win - m)
            p_ctx = jnp.exp(s_ctx - m)
            denom = jnp.sum(p_win, axis=-1, keepdims=True) + jnp.sum(p_ctx, axis=-1, keepdims=True)
            o = _dot(p_win.astype(BF16), v_win) + _dot(p_ctx.astype(BF16), vc)
            outs.append(o / denom)
        o_ref[0, kb * qb:(kb + 1) * qb, :] = jnp.where(lane < NA_DH, outs[0], outs[1]).astype(BF16)


def _neighbourhood_attention(nq, nk, nv, nk_c, nv_c, rpb):
    b, n, _ = nq.shape
    l = nk_c.shape[1]
    rows = n // GRID_W
    key_rows, variant_rows, block_variant, block_start = _na_geometry(rows)
    bias = _na_bias_table(rpb, variant_rows)
    nvar = variant_rows.shape[0]
    qb = NA_ROWS_PER_BLOCK * GRID_W
    nkeys = key_rows * GRID_W
    seq_spec = pl.BlockSpec((1, n, LANES), lambda h, i: (i, 0, h))
    ctx_spec = pl.BlockSpec((1, l, LANES), lambda h, i: (i, 0, h))
    kern = functools.partial(_na_kernel, block_variant=tuple(block_variant), block_start=tuple(block_start),
                             key_rows=key_rows)
    return pl.pallas_call(
        kern,
        out_shape=jax.ShapeDtypeStruct((b, n, HEAD_W), BF16),
        grid=(NA_HEADS // 2, b),
        in_specs=[seq_spec, seq_spec, seq_spec, ctx_spec, ctx_spec,
                  pl.BlockSpec((2, nvar, qb, nkeys), lambda h, i: (h, 0, 0, 0))],
        out_specs=seq_spec,
        compiler_params=_cparams("arbitrary", "arbitrary"),
        name="neighbourhood_attention",
    )(nq, nk, nv, nk_c, nv_c, bias)


def _layer_norm(z, g, b):
    mu = jnp.mean(z, axis=-1, keepdims=True)
    var = jnp.mean(jnp.square(z - mu), axis=-1, keepdims=True)
    return (z - mu) * lax.rsqrt(var + LN_EPS) * g + b


def _split_bf16(v):
    hi = v.astype(BF16)
    return hi, (v - hi.astype(F32)).astype(BF16)


def _route(t, wr_ref, br_ref):
    t_hi, t_lo = _split_bf16(t)
    w_hi, w_lo = _split_bf16(wr_ref[...])
    logits = _dot(t_hi, w_hi) + (_dot(t_lo, w_hi) + _dot(t_hi, w_lo)) + br_ref[...]
    lane = lax.broadcasted_iota(jnp.int32, logits.shape, 1).astype(F32)
    neg = -jnp.inf
    big = float(LANES)

    def first_max(vals):
        vmax = jnp.max(vals, axis=-1, keepdims=True)
        return vmax, jnp.min(jnp.where(vals == vmax, lane, big), axis=-1, keepdims=True)

    is_grp = lane < MOE_GROUPS
    g_max, grp = first_max(jnp.where(is_grp, logits, neg))
    g_sum = jnp.sum(jnp.where(is_grp, jnp.exp(logits - g_max), 0.0), axis=-1, keepdims=True)
    gate_g = 1.0 / g_sum
    lo = MOE_GROUPS + grp * MOE_PER_GROUP
    in_grp = (lane >= lo) & (lane < lo + MOE_PER_GROUP)
    le = jnp.where(in_grp, logits, neg)
    v1, i1 = first_max(le)
    v2, i2 = first_max(jnp.where(lane == i1, neg, le))
    e21 = jnp.exp(v2 - v1)
    w1 = gate_g / (1.0 + e21)
    w2 = gate_g * e21 / (1.0 + e21)
    return jnp.where(lane == 0, w1,
                     jnp.where(lane == 1, w2,
                               jnp.where(lane == 2, i1 - MOE_GROUPS,
                                         jnp.where(lane == 3, i2 - MOE_GROUPS, 0.0))))


def _post_norm_route(h, y, m_ref, ln_ref, wr_ref, br_ref, h_out, t_out, r_out):
    gate = m_ref[0, 2:3, :]
    shift = m_ref[0, 3:4, :]
    scale = m_ref[0, 4:5, :]
    h1 = _layer_norm(DEEPNORM_ALPHA * h + gate * y, ln_ref[0:1, :], ln_ref[1:2, :])
    t = h1 * (1.0 + scale) + shift
    h_out[0] = h1
    t_out[0] = t.astype(BF16)
    r_out[0] = _route(t, wr_ref, br_ref)


def _epilogue_specs(b, n, d, tm):
    in_specs = [
        pl.BlockSpec((1, N_MOD, d), lambda i, j: (i, 0, 0)),
        pl.BlockSpec((2, d), lambda i, j: (0, 0)),
        pl.BlockSpec((d, LANES), lambda i, j: (0, 0)),
        pl.BlockSpec((1, LANES), lambda i, j: (0, 0)),
    ]
    out_shape = [jax.ShapeDtypeStruct((b, n, d), F32), jax.ShapeDtypeStruct((b, n, d), BF16),
                 jax.ShapeDtypeStruct((b, n, LANES), F32)]
    out_specs = [pl.BlockSpec((1, tm, d), lambda i, j: (i, j, 0)),
                 pl.BlockSpec((1, tm, d), lambda i, j: (i, j, 0)),
                 pl.BlockSpec((1, tm, LANES), lambda i, j: (i, j, 0))]
    return in_specs, out_shape, out_specs


def _out_proj_kernel(x_ref, yr_ref, yn_ref, w_ref, m_ref, ln_ref, wr_ref, br_ref, h_out, t_out, r_out):
    y = _dot(yr_ref[0], w_ref[0:HEAD_W, :]) + _dot(yn_ref[0], w_ref[HEAD_W:2 * HEAD_W, :])
    _post_norm_route(x_ref[0], y, m_ref, ln_ref, wr_ref, br_ref, h_out, t_out, r_out)


def _out_proj(x, y_ret, y_na, w_out_bf, m, ln, router):
    b, n, d = x.shape
    tm = ROW_TILE
    ep_in, out_shape, out_specs = _epilogue_specs(b, n, d, tm)
    return pl.pallas_call(
        _out_proj_kernel,
        out_shape=out_shape,
        grid=(b, n // tm),
        in_specs=[
            pl.BlockSpec((1, tm, d), lambda i, j: (i, j, 0)),
            pl.BlockSpec((1, tm, HEAD_W), lambda i, j: (i, j, 0)),
            pl.BlockSpec((1, tm, HEAD_W), lambda i, j: (i, j, 0)),
            pl.BlockSpec((2 * HEAD_W, d), lambda i, j: (0, 0)),
        ] + ep_in,
        out_specs=out_specs,
        compiler_params=_cparams("arbitrary", "arbitrary"),
        name="out_proj",
    )(x, y_ret, y_na, w_out_bf, m, ln, *router)


POOL_HALO = max(POOL_SIZES) // 2
POOL_TILE = 256


def _pool_kernel(x_ref, prev_ref, next_ref, pw_ref, ps_ref, m_ref, ln_ref, wr_ref, br_ref,
                 h_out, t_out, r_out, *, n):
    j = pl.program_id(1)
    nj = pl.num_programs(1)
    tm = x_ref.shape[1]
    halo = POOL_HALO
    shift = m_ref[0, 0:1, :]
    scale = m_ref[0, 1:2, :]
    x = x_ref[0]
    hm = x * (1.0 + scale) + shift
    prev = jnp.where(j > 0, prev_ref[0] * (1.0 + scale) + shift, 0.0)
    nxt = jnp.where(j < nj - 1, next_ref[0] * (1.0 + scale) + shift, 0.0)
    ext = jnp.concatenate([prev, hm, nxt], axis=0)
    pos = (j * tm + lax.broadcasted_iota(jnp.int32, (tm, 1), 0))
    ys = []
    for g, w in enumerate(POOL_SIZES):
        cols = slice(g * POOL_GROUP, (g + 1) * POOL_GROUP)
        s = ext[:, cols]
        span = 1
        while span < w:
            s = s[:s.shape[0] - span] + s[span:]
            span *= 2
        off = halo - w // 2
        win = s[off:off + tm]
        cnt = (jnp.minimum(pos + (w - w // 2), n) - jnp.maximum(pos - w // 2, 0)).astype(F32)
        z = (win / cnt - hm[:, cols]).astype(BF16)
        ys.append(_dot(z, pw_ref[g]))
    y = jnp.concatenate(ys, axis=-1) * ps_ref[...]
    _post_norm_route(x, y, m_ref, ln_ref, wr_ref, br_ref, h_out, t_out, r_out)


def _pool_mixer(h, pool_w_bf, pool_scale, m, ln, router):
    b, n, d = h.shape
    tm = POOL_TILE
    halo = POOL_HALO
    blocks_per_tile = tm // halo
    n_halo_blocks = n // halo
    ep_in, out_shape, out_specs = _epilogue_specs(b, n, d, tm)
    return pl.pallas_call(
        functools.partial(_pool_kernel, n=n),
        out_shape=out_shape,
        grid=(b, n // tm),
        in_specs=[
            pl.BlockSpec((1, tm, d), lambda i, j: (i, j, 0)),
            pl.BlockSpec((1, halo, d), lambda i, j: (i, jnp.maximum(j * blocks_per_tile - 1, 0), 0)),
            pl.BlockSpec((1, halo, d),
                         lambda i, j: (i, jnp.minimum((j + 1) * blocks_per_tile, n_halo_blocks - 1), 0)),
            pl.BlockSpec((len(POOL_SIZES), POOL_GROUP, POOL_GROUP), lambda i, j: (0, 0, 0)),
            pl.BlockSpec((1, d), lambda i, j: (0, 0)),
        ] + ep_in,
        out_specs=out_specs,
        compiler_params=_cparams("arbitrary", "arbitrary"),
        name="pool_mixer",
    )(h, h, h, pool_w_bf, pool_scale.reshape(1, d), m, ln, *router)


def _expert_kernel(te_ref, tv_ref, x_ref, wg_ref, wu_ref, wd_ref, *rest):
    o_ref, wg_s, wu_s, wd_s = rest[-4:]
    j = pl.program_id(0)
    prev_e = te_ref[jnp.maximum(j - 1, 0)]

    @pl.when((j == 0) | (te_ref[j] != prev_e))
    def _():
        wg_s[...] = wg_ref[0, 0].astype(BF16)
        wu_s[...] = wu_ref[0, 0].astype(BF16)
        wd_s[...] = wd_ref[0, 0].astype(BF16)

    @pl.when(tv_ref[j] == 1)
    def _():
        x = x_ref[...]
        gate = _dot(x, wg_s[...])
        up = _dot(x, wu_s[...])
        act = (_silu(gate) * up).astype(BF16)
        o_ref[...] = _dot(act, wd_s[...]).astype(BF16)

    @pl.when(tv_ref[j] == 0)
    def _():
        o_ref[...] = jnp.zeros_like(o_ref)


def _experts(tile_expert, tile_valid, x_chunk, y_prev, w_gate, w_up, w_down, layer, chunk, n_chunks):
    pc, d = x_chunk.shape
    tm = MOE_TILE
    tiles = pc // tm
    hid = w_gate.shape[-1]
    in_specs = [
        pl.BlockSpec((tm, d), lambda j, te, tv: (j, 0)),
        pl.BlockSpec((1, 1, d, hid), lambda j, te, tv: (layer, te[j], 0, 0)),
        pl.BlockSpec((1, 1, d, hid), lambda j, te, tv: (layer, te[j], 0, 0)),
        pl.BlockSpec((1, 1, hid, d), lambda j, te, tv: (layer, te[j], 0, 0)),
    ]
    args = [tile_expert, tile_valid, x_chunk, w_gate, w_up, w_down]
    aliases = {}
    if y_prev is not None:
        in_specs.append(pl.BlockSpec(memory_space=pl.ANY))
        aliases = {len(args): 0}
        args.append(y_prev)
    grid_spec = pltpu.PrefetchScalarGridSpec(
        num_scalar_prefetch=2,
        grid=(tiles,),
        in_specs=in_specs,
        out_specs=pl.BlockSpec((tm, d), lambda j, te, tv: (chunk * tiles + j, 0)),
        scratch_shapes=[pltpu.VMEM((d, hid), BF16), pltpu.VMEM((d, hid), BF16), pltpu.VMEM((hid, d), BF16)],
    )
    return pl.pallas_call(
        _expert_kernel,
        out_shape=jax.ShapeDtypeStruct((pc * n_chunks, d), BF16),
        grid_spec=grid_spec,
        input_output_aliases=aliases,
        compiler_params=_cparams("arbitrary"),
        name="experts",
    )(*args)


def _dispatch_plan(route, tm):
    tok = route.shape[0]
    n_assign = 2 * tok
    p = n_assign + MOE_EXPERTS * tm
    n_tiles = p // tm
    i32 = jnp.int32
    eid = jnp.transpose(route[:, 2:4]).astype(i32).reshape(n_assign)
    sorted_e, order = lax.sort((eid, jnp.arange(n_assign, dtype=i32)), num_keys=1)
    experts = jnp.arange(MOE_EXPERTS, dtype=i32)
    counts = jnp.sum((eid[:, None] == experts[None, :]).astype(i32), axis=0)
    padded = ((counts + tm - 1) // tm) * tm
    pad_end = jnp.cumsum(padded)
    pad_start = pad_end - padded
    start = jnp.cumsum(counts) - counts
    tile_row = jnp.arange(n_tiles, dtype=i32) * tm
    tile_expert = jnp.minimum(jnp.sum((pad_end[None, :] <= tile_row[:, None]).astype(i32), axis=1), MOE_EXPERTS - 1)
    tile_valid = (tile_row < pad_end[-1]).astype(i32)
    rank = (tile_row - pad_start[tile_expert])[:, None] + jnp.arange(tm, dtype=i32)[None, :]
    row_valid = (rank < counts[tile_expert][:, None]).reshape(p)
    src = jnp.clip(start[tile_expert][:, None] + rank, 0, n_assign - 1).reshape(p)
    src_assign = order.at[src].get(mode="promise_in_bounds")
    tok_of_pos = jnp.where(row_valid, src_assign, jnp.arange(p, dtype=i32)) % tok
    offs = pad_start - start
    dest_sorted = jnp.arange(n_assign, dtype=i32) + jnp.sum(
        jnp.where(sorted_e[:, None] == experts[None, :], offs[None, :], 0), axis=1)
    _, pos_of_assign = lax.sort((order, dest_sorted), num_keys=1)
    return tile_expert, tile_valid, tok_of_pos, pos_of_assign


def _n_chunks(b):
    return 4 if b % 4 == 0 else (2 if b % 2 == 0 else 1)


def _post_norm_kernel(h_ref, y0_ref, y1_ref, r_ref, m_ref, ln_ref, *rest):
    o_ref = rest[-1]
    gate = m_ref[0, 5:6, :]
    r = r_ref[0]
    y = r[:, 0:1] * y0_ref[...].astype(F32) + r[:, 1:2] * y1_ref[...].astype(F32)
    o_ref[0] = _layer_norm(DEEPNORM_ALPHA * h_ref[0] + gate * y, ln_ref[0:1, :], ln_ref[1:2, :])


def _post_norm(h, y2, route, m, ln, out_prev, chunk, n_chunks):
    b, n, d = h.shape
    tm = ROW_TILE
    bc = b // n_chunks
    b0 = chunk * bc
    tiles_per_seq = n // tm
    tiles_per_k = bc * tiles_per_seq
    row_spec = pl.BlockSpec((1, tm, d), lambda i, j: (b0 + i, j, 0))
    in_specs = [row_spec,
                pl.BlockSpec((tm, d), lambda i, j: (i * tiles_per_seq + j, 0)),
                pl.BlockSpec((tm, d), lambda i, j: (tiles_per_k + i * tiles_per_seq + j, 0)),
                pl.BlockSpec((1, tm, LANES), lambda i, j: (b0 + i, j, 0)),
                pl.BlockSpec((1, N_MOD, d), lambda i, j: (b0 + i, 0, 0)),
                pl.BlockSpec((2, d), lambda i, j: (0, 0))]
    args = [h, y2, y2, route, m, ln]
    aliases = {}
    if out_prev is not None:
        in_specs.append(pl.BlockSpec(memory_space=pl.ANY))
        aliases = {len(args): 0}
        args.append(out_prev)
    return pl.pallas_call(
        _post_norm_kernel,
        out_shape=jax.ShapeDtypeStruct((b, n, d), F32),
        grid=(bc, tiles_per_seq),
        in_specs=in_specs,
        out_specs=row_spec,
        input_output_aliases=aliases,
        compiler_params=_cparams("arbitrary", "arbitrary"),
        name="post_norm",
    )(*args)


def _moe_post_norm(h1, t, route, m, ln, w_gate, w_up, w_down, layer):
    b, n, d = t.shape
    tok = b * n
    nch = _n_chunks(b)
    tile_expert, tile_valid, tok_of_pos, pos_of_assign = _dispatch_plan(route.reshape(tok, LANES), MOE_TILE)
    t2 = t.reshape(tok, d)
    tiles = tile_expert.shape[0] // nch
    pc = tiles * MOE_TILE
    y_sorted = None
    for c in range(nch):
        x_c = t2.at[tok_of_pos[c * pc:(c + 1) * pc]].get(mode="promise_in_bounds")
        y_sorted = _experts(tile_expert[c * tiles:(c + 1) * tiles], tile_valid[c * tiles:(c + 1) * tiles],
                            x_c, y_sorted, w_gate, w_up, w_down, layer, c, nch)
    bc = b // nch
    pos3 = pos_of_assign.reshape(2, b, n)
    out = None
    for c in range(nch):
        pos_c = pos3[:, c * bc:(c + 1) * bc].reshape(2 * bc * n)
        y2_c = y_sorted.at[pos_c].get(mode="promise_in_bounds")
        out = _post_norm(h1, y2_c, route, m, ln, out, c, nch)
    return out


def _rope_tables(n):
    t = jnp.arange(n)
    rows = (t // GRID_W).astype(F32)
    cols = (t % GRID_W).astype(F32)
    n_freq = RET_DK // 4
    inv_freq = ROPE_BASE ** (-jnp.arange(n_freq, dtype=F32) / n_freq)
    ang = jnp.concatenate([rows[:, None] * inv_freq, cols[:, None] * inv_freq], axis=-1)
    cos, sin = jnp.cos(ang), jnp.sin(ang)
    cos2 = jnp.concatenate([cos, cos], axis=-1)
    sin2 = jnp.concatenate([-sin, sin], axis=-1)
    q_scale = RET_DK ** -0.5
    return cos2 * q_scale, sin2 * q_scale, cos2, sin2


def _router_params(w_r1, b_r1, w_r2, b_r2):
    d = w_r1.shape[0]
    w2 = jnp.transpose(w_r2, (1, 0, 2)).reshape(d, MOE_EXPERTS)
    pad = LANES - MOE_GROUPS - MOE_EXPERTS
    wr = jnp.concatenate([w_r1, w2, jnp.zeros((d, pad), F32)], axis=-1)
    br = jnp.concatenate([b_r1, b_r2.reshape(MOE_EXPERTS), jnp.zeros((pad,), F32)]).reshape(1, LANES)
    return wr, br


def kernel(x, c, ctx, c_ctx, w_mod, b_mod, ln_g, ln_b, ab_w_in, ab_w_out, ab_log_decay, ab_rpb, pool_w, pool_scale, moe_w_r1, moe_b_r1, moe_w_r2, moe_b_r2, moe_w_gate, moe_w_up, moe_w_down):
    b, n, d = x.shape
    cc = jnp.concatenate([c, c_ctx[None, :], jnp.zeros((MOD_ROWS - b - 1, d), F32)], axis=0)
    mod = _modulation(cc, w_mod, b_mod)
    h = x
    for i in range(DEPTH):
        j = i // 2
        m = mod[i, :b].reshape(b, N_MOD, d)
        ln1 = jnp.stack([ln_g[i, 0], ln_b[i, 0]])
        ln2 = jnp.stack([ln_g[i, 1], ln_b[i, 1]])
        router = _router_params(moe_w_r1[i], moe_b_r1[i], moe_w_r2[i], moe_b_r2[i])
        if i % 2 == 0:
            m_ctx = mod[i, b].reshape(1, N_MOD, d)
            w_in_bf = ab_w_in[j].astype(BF16)
            log_gamma2 = jnp.log1p(-jnp.exp(ab_log_decay[j].astype(F32)))
            rk, rv, nk, nv, rq, rg, nq = _in_proj(h, m, w_in_bf, _rope_tables(n))
            rk_c, rv_c, nk_c, nv_c = _ctx_proj(ctx, m_ctx, w_in_bf[:, :N_KV_GROUPS * HEAD_W])
            y_ret = _retention(log_gamma2, rq, rk, rv, rg, rk_c, rv_c)
            y_na = _neighbourhood_attention(nq, nk, nv, nk_c, nv_c, ab_rpb[j])
            h1, t, route = _out_proj(h, y_ret, y_na, ab_w_out[j].astype(BF16), m, ln1, router)
        else:
            h1, t, route = _pool_mixer(h, pool_w[j].astype(BF16), pool_scale[j], m, ln1, router)
        h = _moe_post_norm(h1, t, route, m, ln2, moe_w_gate, moe_w_up, moe_w_down, i)
    return h
```

```python
import functools

import numpy as np
import jax
import jax.numpy as jnp
from jax import lax
from jax.experimental import pallas as pl
from jax.experimental.pallas import tpu as pltpu

F32 = jnp.float32
BF16 = jnp.bfloat16
HIGHEST = lax.Precision.HIGHEST

D_MODEL = 1024
DEPTH = 2
GRID_W = 64
RET_HEADS = 4
RET_DK = 128
RET_CHUNK = 128
NA_HEADS = 8
NA_DH = 64
NA_WIN_R = 8
NA_WIN_C = 16
N_BIAS_ROWS = 2 * NA_WIN_R - 1
POOL_SIZES = (2, 4, 8, 16)
POOL_GROUP = D_MODEL // len(POOL_SIZES)
MOE_GROUPS = 4
MOE_PER_GROUP = 8
MOE_EXPERTS = MOE_GROUPS * MOE_PER_GROUP
MOE_HIDDEN = D_MODEL // 2
ROPE_BASE = 10000.0
LN_EPS = 1e-5
N_MOD = 6
DEEPNORM_ALPHA = (2 * DEPTH) ** 0.25
HEAD_W = 512
N_IN_GROUPS = 7
N_KV_GROUPS = 4
MASK_VALUE = -1e30

LANES = 128
VMEM_LIMIT = 56 * 1024 * 1024

NA_ROWS_PER_BLOCK = 4
ROW_TILE = 512
MOE_TILE = 512
MOD_ROWS = 24
MOD_COL_TILE = 1536


def _cparams(*sem):
    return pltpu.CompilerParams(dimension_semantics=sem, vmem_limit_bytes=VMEM_LIMIT)


def _silu(v):
    return v / (1.0 + jnp.exp(-v))


def _dot(a, b):
    return jnp.dot(a, b, preferred_element_type=F32)


def _dot_nt(a, b):
    return lax.dot_general(a, b, (((1,), (1,)), ((), ())), preferred_element_type=F32)


def _dot_tn(a, b):
    return lax.dot_general(a, b, (((0,), (0,)), ((), ())), preferred_element_type=F32)


def _mod_kernel(c_ref, w_ref, b_ref, o_ref):
    s = _silu(c_ref[...])
    o_ref[0] = jnp.dot(s, w_ref[0], precision=HIGHEST, preferred_element_type=F32) + b_ref[0]


def _modulation(cc, w_mod, b_mod):
    depth, d, n = w_mod.shape
    return pl.pallas_call(
        _mod_kernel,
        out_shape=jax.ShapeDtypeStruct((depth, MOD_ROWS, n), F32),
        grid=(depth, n // MOD_COL_TILE),
        in_specs=[
            pl.BlockSpec((MOD_ROWS, d), lambda i, j: (0, 0)),
            pl.BlockSpec((1, d, MOD_COL_TILE), lambda i, j: (i, 0, j)),
            pl.BlockSpec((1, 1, MOD_COL_TILE), lambda i, j: (i, 0, j)),
        ],
        out_specs=pl.BlockSpec((1, MOD_ROWS, MOD_COL_TILE), lambda i, j: (i, 0, j)),
        compiler_params=_cparams("arbitrary", "arbitrary"),
        name="modulation",
    )(cc, w_mod, b_mod.reshape(depth, 1, n))


def _rope(v, cos2, sin2):
    return v * cos2 + pltpu.roll(v, RET_DK // 2, axis=1) * sin2


def _in_proj_kernel(x_ref, m_ref, w_ref, cq_ref, sq_ref, ck_ref, sk_ref,
                    rk_ref, rv_ref, nk_ref, nv_ref, rq_ref, rg_ref, nq_ref):
    shift = m_ref[0, 0:1, :]
    scale = m_ref[0, 1:2, :]
    hm = (x_ref[0] * (1.0 + scale) + shift).astype(BF16)
    outs = (rk_ref, rv_ref, nk_ref, nv_ref, rq_ref, rg_ref, nq_ref)
    for g, o_ref in enumerate(outs):
        p = _dot(hm, w_ref[:, g * HEAD_W:(g + 1) * HEAD_W])
        if g == 0 or g == 4:
            cos2 = (ck_ref if g == 0 else cq_ref)[...]
            sin2 = (sk_ref if g == 0 else sq_ref)[...]
            for hd in range(RET_HEADS):
                sl = slice(hd * RET_DK, (hd + 1) * RET_DK)
                o_ref[0, :, sl] = _rope(p[:, sl], cos2, sin2).astype(BF16)
        elif g == 6:
            o_ref[0] = (p * NA_DH ** -0.5).astype(BF16)
        else:
            o_ref[0] = p.astype(BF16)


def _in_proj(x, m, w_in_bf, rope_tabs):
    b, n, d = x.shape
    tm = ROW_TILE
    tab_spec = pl.BlockSpec((tm, RET_DK), lambda i, j: (j, 0))
    out_spec = pl.BlockSpec((1, tm, HEAD_W), lambda i, j: (i, j, 0))
    return pl.pallas_call(
        _in_proj_kernel,
        out_shape=[jax.ShapeDtypeStruct((b, n, HEAD_W), BF16)] * N_IN_GROUPS,
        grid=(b, n // tm),
        in_specs=[
            pl.BlockSpec((1, tm, d), lambda i, j: (i, j, 0)),
            pl.BlockSpec((1, N_MOD, d), lambda i, j: (i, 0, 0)),
            pl.BlockSpec((d, N_IN_GROUPS * HEAD_W), lambda i, j: (0, 0)),
            tab_spec, tab_spec, tab_spec, tab_spec,
        ],
        out_specs=[out_spec] * N_IN_GROUPS,
        compiler_params=_cparams("arbitrary", "arbitrary"),
        name="in_proj",
    )(x, m, w_in_bf, *rope_tabs)


def _ctx_proj_kernel(x_ref, m_ref, w_ref, rk_ref, rv_ref, nk_ref, nv_ref):
    shift = m_ref[0, 0:1, :]
    scale = m_ref[0, 1:2, :]
    hm = (x_ref[0] * (1.0 + scale) + shift).astype(BF16)
    for g, o_ref in enumerate((rk_ref, rv_ref, nk_ref, nv_ref)):
        o_ref[0] = _dot(hm, w_ref[:, g * HEAD_W:(g + 1) * HEAD_W]).astype(BF16)


def _ctx_proj(ctx, m_ctx, w_in_bf):
    b, l, d = ctx.shape
    out_spec = pl.BlockSpec((1, l, HEAD_W), lambda i: (i, 0, 0))
    return pl.pallas_call(
        _ctx_proj_kernel,
        out_shape=[jax.ShapeDtypeStruct((b, l, HEAD_W), BF16)] * N_KV_GROUPS,
        grid=(b,),
        in_specs=[
            pl.BlockSpec((1, l, d), lambda i: (i, 0, 0)),
            pl.BlockSpec((1, N_MOD, d), lambda i: (0, 0, 0)),
            pl.BlockSpec((d, N_KV_GROUPS * HEAD_W), lambda i: (0, 0)),
        ],
        out_specs=[out_spec] * N_KV_GROUPS,
        compiler_params=_cparams("arbitrary"),
        name="ctx_proj",
    )(ctx, m_ctx, w_in_bf)


def _retention_kernel(lg_ref, q_ref, k_ref, v_ref, g_ref, kc_ref, vc_ref, o_ref, u_ref, s_ref):
    hd = pl.program_id(1)
    n = q_ref.shape[1]
    c = RET_CHUNK
    dk = RET_DK
    nc = n // c
    l = kc_ref.shape[1]
    lgf = lg_ref[0, hd]
    lgb = lg_ref[1, hd]

    ii = lax.broadcasted_iota(jnp.int32, (c, c), 0).astype(F32)
    jj = lax.broadcasted_iota(jnp.int32, (c, c), 1).astype(F32)
    diff = ii - jj
    decay = (jnp.where(diff >= 0, jnp.exp(lgf * jnp.maximum(diff, 0.0)), 0.0)
             + jnp.where(diff <= 0, jnp.exp(lgb * jnp.maximum(-diff, 0.0)), 0.0))
    idx = lax.broadcasted_iota(jnp.int32, (c, 1), 0).astype(F32)
    q_dec_f = jnp.exp(lgf * (idx + 1.0))
    k_dec_f = jnp.exp(lgf * (c - 1.0 - idx))
    q_dec_b = jnp.exp(lgb * (c - idx))
    k_dec_b = jnp.exp(lgb * idx)
    ones = jnp.ones((1, dk), F32)
    chunk_dec_f = jnp.exp(ones * (lgf * c))
    chunk_dec_b = jnp.exp(ones * (lgb * c))

    pos = lax.broadcasted_iota(jnp.int32, (l, 1), 0).astype(F32)
    kc = kc_ref[0].astype(F32)
    vc = vc_ref[0]
    s_f0 = _dot_tn((kc * jnp.exp(lgf * (l - 1.0 - pos))).astype(BF16), vc)
    s_b0 = _dot_tn((kc * jnp.exp(lgb * pos)).astype(BF16), vc)

    def chunk_rows(i):
        return pl.ds(pl.multiple_of(i * c, c), c)

    def kv_step(i, carry):
        rows = chunk_rows(i)
        k_i = k_ref[0, rows, :].astype(F32)
        kk = jnp.concatenate([(k_i * k_dec_f).astype(BF16), (k_i * k_dec_b).astype(BF16)], axis=1)
        u_ref[i] = _dot_tn(kk, v_ref[0, rows, :])
        return carry

    lax.fori_loop(0, nc, kv_step, 0, unroll=True)

    def scan_f(i, s):
        s_ref[i, :, 0:dk] = s.astype(BF16)
        return s * chunk_dec_f + u_ref[i, 0:dk, :]

    lax.fori_loop(0, nc, scan_f, s_f0, unroll=True)

    def scan_b(t, s):
        i = nc - 1 - t
        s_ref[i, :, dk:2 * dk] = s.astype(BF16)
        return s * chunk_dec_b + u_ref[i, dk:2 * dk, :]

    lax.fori_loop(0, nc, scan_b, s_b0, unroll=True)

    def out_step(i, carry):
        rows = chunk_rows(i)
        q_i = q_ref[0, rows, :]
        v_i = v_ref[0, rows, :]
        att = (_dot_nt(q_i, k_ref[0, rows, :]) * decay).astype(BF16)
        inter = _dot(q_i, s_ref[i])
        o = _dot(att, v_i) + inter[:, 0:dk] * q_dec_f + inter[:, dk:2 * dk] * q_dec_b
        mu = jnp.mean(o, axis=-1, keepdims=True)
        var = jnp.mean(jnp.square(o - mu), axis=-1, keepdims=True)
        o_n = (o - mu) * lax.rsqrt(var + LN_EPS)
        gate = g_ref[0, rows, :].astype(F32)
        o_ref[0, rows, :] = (_silu(gate) * o_n).astype(BF16)
        return carry

    lax.fori_loop(0, nc, out_step, 0, unroll=True)


def _retention(log_gamma2, rq, rk, rv, rg, rk_c, rv_c):
    b, n, _ = rq.shape
    l = rk_c.shape[1]
    nc = n // RET_CHUNK
    seq_spec = pl.BlockSpec((1, n, RET_DK), lambda i, h: (i, 0, h))
    ctx_spec = pl.BlockSpec((1, l, RET_DK), lambda i, h: (i, 0, h))
    return pl.pallas_call(
        _retention_kernel,
        out_shape=jax.ShapeDtypeStruct((b, n, HEAD_W), BF16),
        grid=(b, RET_HEADS),
        in_specs=[pl.BlockSpec(memory_space=pltpu.SMEM),
                  seq_spec, seq_spec, seq_spec, seq_spec, ctx_spec, ctx_spec],
        out_specs=seq_spec,
        scratch_shapes=[pltpu.VMEM((nc, 2 * RET_DK, RET_DK), F32),
                        pltpu.VMEM((nc, RET_DK, 2 * RET_DK), BF16)],
        compiler_params=_cparams("arbitrary", "arbitrary"),
        name="retention",
    )(log_gamma2, rq, rk, rv, rg, rk_c, rv_c)


def _na_geometry(rows):
    rb = NA_ROWS_PER_BLOCK
    wr = min(NA_WIN_R, rows)
    key_rows = min(rows, rb + wr - 1)
    variants, block_variant, block_start = [], [], []
    for kb in range(rows // rb):
        q_rows = kb * rb + np.arange(rb)
        r0 = np.clip(q_rows - wr // 2, 0, rows - wr)
        ks = int(np.clip(r0.min(), 0, rows - key_rows))
        assert r0.max() + wr <= ks + key_rows
        kr = ks + np.arange(key_rows)
        valid = (kr[None, :] >= r0[:, None]) & (kr[None, :] < r0[:, None] + wr)
        ridx = np.where(valid, kr[None, :] - q_rows[:, None] + NA_WIN_R - 1, N_BIAS_ROWS).astype(np.int32)
        for vi, v in enumerate(variants):
            if np.array_equal(v, ridx):
                block_variant.append(vi)
                break
        else:
            variants.append(ridx)
            block_variant.append(len(variants) - 1)
        block_start.append(ks)
    return key_rows, np.stack(variants), block_variant, block_start


def _na_bias_table(rpb, variant_rows):
    n_heads = rpb.shape[0]
    qc = np.arange(GRID_W)[:, None]
    kc = np.arange(GRID_W)[None, :]
    wstart = np.clip(qc - NA_WIN_C // 2, 0, GRID_W - NA_WIN_C)
    col_ok = (kc >= wstart) & (kc < wstart + NA_WIN_C)
    cidx = np.clip(kc - qc + NA_WIN_C - 1, 0, 2 * NA_WIN_C - 2)
    onehot = ((cidx[None] == np.arange(2 * NA_WIN_C - 1)[:, None, None]) & col_ok[None]).astype(np.float32)
    blocks = jnp.einsum("hrd,dqk->hrqk", rpb.astype(F32), jnp.asarray(onehot), precision=HIGHEST)
    blocks = jnp.where(col_ok[None, None], blocks, MASK_VALUE)
    blocks = jnp.concatenate([blocks, jnp.full((n_heads, 1, GRID_W, GRID_W), MASK_VALUE, F32)], axis=1)
    nvar, rb, key_rows = variant_rows.shape
    tab = blocks[:, variant_rows.reshape(-1)].reshape(n_heads, nvar, rb, key_rows, GRID_W, GRID_W)
    return jnp.transpose(tab, (0, 1, 2, 4, 3, 5)).reshape(n_heads, nvar, rb * GRID_W, key_rows * GRID_W)


def _na_kernel(q_ref, k_ref, v_ref, kc_ref, vc_ref, bias_ref, o_ref, *, block_variant, block_start, key_rows):
    qb = NA_ROWS_PER_BLOCK * GRID_W
    nk = key_rows * GRID_W
    lane = lax.broadcasted_iota(jnp.int32, (qb, LANES), 1)
    kc = kc_ref[0]
    vc = vc_ref[0]
    for kb, (var, ks) in enumerate(zip(block_variant, block_start)):
        q = q_ref[0, kb * qb:(kb + 1) * qb, :]
        k_win = k_ref[0, ks * GRID_W:ks * GRID_W + nk, :]
        v_win = v_ref[0, ks * GRID_W:ks * GRID_W + nk, :]
        outs = []
        for j in range(2):
            in_head = (lane >= j * NA_DH) & (lane < (j + 1) * NA_DH)
            qm = jnp.where(in_head, q, jnp.zeros_like(q))
            s_win = _dot_nt(qm, k_win) + bias_ref[j, var]
            s_ctx = _dot_nt(qm, kc)
            m = jnp.maximum(jnp.max(s_win, axis=-1, keepdims=True), jnp.max(s_ctx, axis=-1, keepdims=True))
            p_win = jnp.exp(s_win - m)
            p_ctx = jnp.exp(s_ctx - m)
            denom = jnp.sum(p_win, axis=-1, keepdims=True) + jnp.sum(p_ctx, axis=-1, keepdims=True)
            o = _dot(p_win.astype(BF16), v_win) + _dot(p_ctx.astype(BF16), vc)
            outs.append(o / denom)
        o_ref[0, kb * qb:(kb + 1) * qb, :] = jnp.where(lane < NA_DH, outs[0], outs[1]).astype(BF16)


def _neighbourhood_attention(nq, nk, nv, nk_c, nv_c, rpb):
    b, n, _ = nq.shape
    l = nk_c.shape[1]
    rows = n // GRID_W
    key_rows, variant_rows, block_variant, block_start = _na_geometry(rows)
    bias = _na_bias_table(rpb, variant_rows)
    nvar = variant_rows.shape[0]
    qb = NA_ROWS_PER_BLOCK * GRID_W
    nkeys = key_rows * GRID_W
    seq_spec = pl.BlockSpec((1, n, LANES), lambda h, i: (i, 0, h))
    ctx_spec = pl.BlockSpec((1, l, LANES), lambda h, i: (i, 0, h))
    kern = functools.partial(_na_kernel, block_variant=tuple(block_variant), block_start=tuple(block_start),
                             key_rows=key_rows)
    return pl.pallas_call(
        kern,
        out_shape=jax.ShapeDtypeStruct((b, n, HEAD_W), BF16),
        grid=(NA_HEADS // 2, b),
        in_specs=[seq_spec, seq_spec, seq_spec, ctx_spec, ctx_spec,
                  pl.BlockSpec((2, nvar, qb, nkeys), lambda h, i: (h, 0, 0, 0))],
        out_specs=seq_spec,
        compiler_params=_cparams("arbitrary", "arbitrary"),
        name="neighbourhood_attention",
    )(nq, nk, nv, nk_c, nv_c, bias)


def _layer_norm(z, g, b):
    mu = jnp.mean(z, axis=-1, keepdims=True)
    var = jnp.mean(jnp.square(z - mu), axis=-1, keepdims=True)
    return (z - mu) * lax.rsqrt(var + LN_EPS) * g + b


def _split_bf16(v):
    hi = v.astype(BF16)
    return hi, (v - hi.astype(F32)).astype(BF16)


def _route(t, wr_ref, br_ref):
    t_hi, t_lo = _split_bf16(t)
    w_hi, w_lo = _split_bf16(wr_ref[...])
    hi = _dot(t_hi, jnp.concatenate([w_hi, w_lo], axis=1))
    logits = hi[:, 0:LANES] + (_dot(t_lo, w_hi) + hi[:, LANES:2 * LANES]) + br_ref[...]
    lane = lax.broadcasted_iota(jnp.int32, logits.shape, 1).astype(F32)
    neg = -jnp.inf
    big = float(LANES)

    def first_max(vals):
        vmax = jnp.max(vals, axis=-1, keepdims=True)
        return vmax, jnp.min(jnp.where(vals == vmax, lane, big), axis=-1, keepdims=True)

    is_grp = lane < MOE_GROUPS
    g_max, grp = first_max(jnp.where(is_grp, logits, neg))
    g_sum = jnp.sum(jnp.where(is_grp, jnp.exp(logits - g_max), 0.0), axis=-1, keepdims=True)
    gate_g = 1.0 / g_sum
    lo = MOE_GROUPS + grp * MOE_PER_GROUP
    in_grp = (lane >= lo) & (lane < lo + MOE_PER_GROUP)
    le = jnp.where(in_grp, logits, neg)
    v1, i1 = first_max(le)
    v2, i2 = first_max(jnp.where(lane == i1, neg, le))
    e21 = jnp.exp(v2 - v1)
    w1 = gate_g / (1.0 + e21)
    w2 = gate_g * e21 / (1.0 + e21)
    return jnp.where(lane == 0, w1,
                     jnp.where(lane == 1, w2,
                               jnp.where(lane == 2, i1 - MOE_GROUPS,
                                         jnp.where(lane == 3, i2 - MOE_GROUPS, 0.0))))


def _post_norm_route(h, y, m_ref, ln_ref, wr_ref, br_ref, h_out, t_out, r_out):
    gate = m_ref[0, 2:3, :]
    shift = m_ref[0, 3:4, :]
    scale = m_ref[0, 4:5, :]
    h1 = _layer_norm(DEEPNORM_ALPHA * h + gate * y, ln_ref[0:1, :], ln_ref[1:2, :])
    t = h1 * (1.0 + scale) + shift
    h_out[0] = h1
    t_out[0] = t.astype(BF16)
    r_out[0] = _route(t, wr_ref, br_ref)


def _epilogue_specs(b, n, d, tm):
    in_specs = [
        pl.BlockSpec((1, N_MOD, d), lambda i, j: (i, 0, 0)),
        pl.BlockSpec((2, d), lambda i, j: (0, 0)),
        pl.BlockSpec((d, LANES), lambda i, j: (0, 0)),
        pl.BlockSpec((1, LANES), lambda i, j: (0, 0)),
    ]
    out_shape = [jax.ShapeDtypeStruct((b, n, d), F32), jax.ShapeDtypeStruct((b, n, d), BF16),
                 jax.ShapeDtypeStruct((b, n, LANES), F32)]
    out_specs = [pl.BlockSpec((1, tm, d), lambda i, j: (i, j, 0)),
                 pl.BlockSpec((1, tm, d), lambda i, j: (i, j, 0)),
                 pl.BlockSpec((1, tm, LANES), lambda i, j: (i, j, 0))]
    return in_specs, out_shape, out_specs


def _out_proj_kernel(x_ref, yr_ref, yn_ref, w_ref, m_ref, ln_ref, wr_ref, br_ref, h_out, t_out, r_out):
    y = _dot(yr_ref[0], w_ref[0:HEAD_W, :]) + _dot(yn_ref[0], w_ref[HEAD_W:2 * HEAD_W, :])
    _post_norm_route(x_ref[0], y, m_ref, ln_ref, wr_ref, br_ref, h_out, t_out, r_out)


def _out_proj(x, y_ret, y_na, w_out_bf, m, ln, router):
    b, n, d = x.shape
    tm = ROW_TILE
    ep_in, out_shape, out_specs = _epilogue_specs(b, n, d, tm)
    return pl.pallas_call(
        _out_proj_kernel,
        out_shape=out_shape,
        grid=(b, n // tm),
        in_specs=[
            pl.BlockSpec((1, tm, d), lambda i, j: (i, j, 0)),
            pl.BlockSpec((1, tm, HEAD_W), lambda i, j: (i, j, 0)),
            pl.BlockSpec((1, tm, HEAD_W), lambda i, j: (i, j, 0)),
            pl.BlockSpec((2 * HEAD_W, d), lambda i, j: (0, 0)),
        ] + ep_in,
        out_specs=out_specs,
        compiler_params=_cparams("arbitrary", "arbitrary"),
        name="out_proj",
    )(x, y_ret, y_na, w_out_bf, m, ln, *router)


POOL_HALO = max(POOL_SIZES) // 2
POOL_TILE = 256


def _pool_kernel(x_ref, prev_ref, next_ref, pw_ref, ps_ref, m_ref, ln_ref, wr_ref, br_ref,
                 h_out, t_out, r_out, *, n):
    j = pl.program_id(1)
    nj = pl.num_programs(1)
    tm = x_ref.shape[1]
    halo = POOL_HALO
    shift = m_ref[0, 0:1, :]
    scale = m_ref[0, 1:2, :]
    x = x_ref[0]
    hm = x * (1.0 + scale) + shift
    prev = jnp.where(j > 0, prev_ref[0] * (1.0 + scale) + shift, 0.0)
    nxt = jnp.where(j < nj - 1, next_ref[0] * (1.0 + scale) + shift, 0.0)
    ext = jnp.concatenate([prev, hm, nxt], axis=0)
    pos = (j * tm + lax.broadcasted_iota(jnp.int32, (tm, 1), 0))
    ys = []
    for g, w in enumerate(POOL_SIZES):
        cols = slice(g * POOL_GROUP, (g + 1) * POOL_GROUP)
        s = ext[:, cols]
        span = 1
        while span < w:
            s = s[:s.shape[0] - span] + s[span:]
            span *= 2
        off = halo - w // 2
        win = s[off:off + tm]
        cnt = (jnp.minimum(pos + (w - w // 2), n) - jnp.maximum(pos - w // 2, 0)).astype(F32)
        z = (win / cnt - hm[:, cols]).astype(BF16)
        ys.append(_dot(z, pw_ref[g]))
    y = jnp.concatenate(ys, axis=-1) * ps_ref[...]
    _post_norm_route(x, y, m_ref, ln_ref, wr_ref, br_ref, h_out, t_out, r_out)


def _pool_mixer(h, pool_w_bf, pool_scale, m, ln, router):
    b, n, d = h.shape
    tm = POOL_TILE
    halo = POOL_HALO
    blocks_per_tile = tm // halo
    n_halo_blocks = n // halo
    ep_in, out_shape, out_specs = _epilogue_specs(b, n, d, tm)
    return pl.pallas_call(
        functools.partial(_pool_kernel, n=n),
        out_shape=out_shape,
        grid=(b, n // tm),
        in_specs=[
            pl.BlockSpec((1, tm, d), lambda i, j: (i, j, 0)),
            pl.BlockSpec((1, halo, d), lambda i, j: (i, jnp.maximum(j * blocks_per_tile - 1, 0), 0)),
            pl.BlockSpec((1, halo, d),
                         lambda i, j: (i, jnp.minimum((j + 1) * blocks_per_tile, n_halo_blocks - 1), 0)),
            pl.BlockSpec((len(POOL_SIZES), POOL_GROUP, POOL_GROUP), lambda i, j: (0, 0, 0)),
            pl.BlockSpec((1, d), lambda i, j: (0, 0)),
        ] + ep_in,
        out_specs=out_specs,
        compiler_params=_cparams("arbitrary", "arbitrary"),
        name="pool_mixer",
    )(h, h, h, pool_w_bf, pool_scale.reshape(1, d), m, ln, *router)


def _expert_kernel(te_ref, tv_ref, x_ref, wg_ref, wu_ref, wd_ref, *rest):
    o_ref, wg_s, wu_s, wd_s = rest[-4:]
    j = pl.program_id(0)
    prev_e = te_ref[jnp.maximum(j - 1, 0)]

    @pl.when((j == 0) | (te_ref[j] != prev_e))
    def _():
        wg_s[...] = wg_ref[0, 0].astype(BF16)
        wu_s[...] = wu_ref[0, 0].astype(BF16)
        wd_s[...] = wd_ref[0, 0].astype(BF16)

    @pl.when(tv_ref[j] == 1)
    def _():
        x = x_ref[...]
        gate = _dot(x, wg_s[...])
        up = _dot(x, wu_s[...])
        act = (_silu(gate) * up).astype(BF16)
        o_ref[...] = _dot(act, wd_s[...]).astype(BF16)

    @pl.when(tv_ref[j] == 0)
    def _():
        o_ref[...] = jnp.zeros_like(o_ref)


def _experts(tile_expert, tile_valid, x_chunk, y_prev, w_gate, w_up, w_down, layer, chunk, n_chunks):
    pc, d = x_chunk.shape
    tm = MOE_TILE
    tiles = pc // tm
    hid = w_gate.shape[-1]
    in_specs = [
        pl.BlockSpec((tm, d), lambda j, te, tv: (j, 0)),
        pl.BlockSpec((1, 1, d, hid), lambda j, te, tv: (layer, te[j], 0, 0)),
        pl.BlockSpec((1, 1, d, hid), lambda j, te, tv: (layer, te[j], 0, 0)),
        pl.BlockSpec((1, 1, hid, d), lambda j, te, tv: (layer, te[j], 0, 0)),
    ]
    args = [tile_expert, tile_valid, x_chunk, w_gate, w_up, w_down]
    aliases = {}
    if y_prev is not None:
        in_specs.append(pl.BlockSpec(memory_space=pl.ANY))
        aliases = {len(args): 0}
        args.append(y_prev)
    grid_spec = pltpu.PrefetchScalarGridSpec(
        num_scalar_prefetch=2,
        grid=(tiles,),
        in_specs=in_specs,
        out_specs=pl.BlockSpec((tm, d), lambda j, te, tv: (chunk * tiles + j, 0)),
        scratch_shapes=[pltpu.VMEM((d, hid), BF16), pltpu.VMEM((d, hid), BF16), pltpu.VMEM((hid, d), BF16)],
    )
    return pl.pallas_call(
        _expert_kernel,
        out_shape=jax.ShapeDtypeStruct((pc * n_chunks, d), BF16),
        grid_spec=grid_spec,
        input_output_aliases=aliases,
        compiler_params=_cparams("arbitrary"),
        name="experts",
    )(*args)


def _dispatch_plan(route, tm):
    tok = route.shape[0]
    n_assign = 2 * tok
    p = n_assign + MOE_EXPERTS * tm
    n_tiles = p // tm
    i32 = jnp.int32
    eid = jnp.transpose(route[:, 2:4]).astype(i32).reshape(n_assign)
    sorted_e, order = lax.sort((eid, jnp.arange(n_assign, dtype=i32)), num_keys=1)
    experts = jnp.arange(MOE_EXPERTS, dtype=i32)
    counts = jnp.sum((eid[:, None] == experts[None, :]).astype(i32), axis=0)
    padded = ((counts + tm - 1) // tm) * tm
    pad_end = jnp.cumsum(padded)
    pad_start = pad_end - padded
    start = jnp.cumsum(counts) - counts
    tile_row = jnp.arange(n_tiles, dtype=i32) * tm
    tile_expert = jnp.minimum(jnp.sum((pad_end[None, :] <= tile_row[:, None]).astype(i32), axis=1), MOE_EXPERTS - 1)
    tile_valid = (tile_row < pad_end[-1]).astype(i32)
    rank = (tile_row - pad_start[tile_expert])[:, None] + jnp.arange(tm, dtype=i32)[None, :]
    row_valid = (rank < counts[tile_expert][:, None]).reshape(p)
    src = jnp.clip(start[tile_expert][:, None] + rank, 0, n_assign - 1).reshape(p)
    src_assign = order.at[src].get(mode="promise_in_bounds")
    tok_of_pos = jnp.where(row_valid, src_assign, jnp.arange(p, dtype=i32)) % tok
    offs = pad_start - start
    dest_sorted = jnp.arange(n_assign, dtype=i32) + jnp.sum(
        jnp.where(sorted_e[:, None] == experts[None, :], offs[None, :], 0), axis=1)
    _, pos_of_assign = lax.sort((order, dest_sorted), num_keys=1)
    return tile_expert, tile_valid, tok_of_pos, pos_of_assign


def _n_chunks(b):
    return 4 if b % 4 == 0 else (2 if b % 2 == 0 else 1)


def _post_norm_kernel(h_ref, y0_ref, y1_ref, r_ref, m_ref, ln_ref, *rest):
    o_ref = rest[-1]
    gate = m_ref[0, 5:6, :]
    r = r_ref[0]
    y = r[:, 0:1] * y0_ref[...].astype(F32) + r[:, 1:2] * y1_ref[...].astype(F32)
    o_ref[0] = _layer_norm(DEEPNORM_ALPHA * h_ref[0] + gate * y, ln_ref[0:1, :], ln_ref[1:2, :])


def _post_norm(h, y2, route, m, ln, out_prev, chunk, n_chunks):
    b, n, d = h.shape
    tm = ROW_TILE
    bc = b // n_chunks
    b0 = chunk * bc
    tiles_per_seq = n // tm
    tiles_per_k = bc * tiles_per_seq
    row_spec = pl.BlockSpec((1, tm, d), lambda i, j: (b0 + i, j, 0))
    in_specs = [row_spec,
                pl.BlockSpec((tm, d), lambda i, j: (i * tiles_per_seq + j, 0)),
                pl.BlockSpec((tm, d), lambda i, j: (tiles_per_k + i * tiles_per_seq + j, 0)),
                pl.BlockSpec((1, tm, LANES), lambda i, j: (b0 + i, j, 0)),
                pl.BlockSpec((1, N_MOD, d), lambda i, j: (b0 + i, 0, 0)),
                pl.BlockSpec((2, d), lambda i, j: (0, 0))]
    args = [h, y2, y2, route, m, ln]
    aliases = {}
    if out_prev is not None:
        in_specs.append(pl.BlockSpec(memory_space=pl.ANY))
        aliases = {len(args): 0}
        args.append(out_prev)
    return pl.pallas_call(
        _post_norm_kernel,
        out_shape=jax.ShapeDtypeStruct((b, n, d), F32),
        grid=(bc, tiles_per_seq),
        in_specs=in_specs,
        out_specs=row_spec,
        input_output_aliases=aliases,
        compiler_params=_cparams("arbitrary", "arbitrary"),
        name="post_norm",
    )(*args)


def _moe_post_norm(h1, t, route, m, ln, w_gate, w_up, w_down, layer):
    b, n, d = t.shape
    tok = b * n
    nch = _n_chunks(b)
    tile_expert, tile_valid, tok_of_pos, pos_of_assign = _dispatch_plan(route.reshape(tok, LANES), MOE_TILE)
    t2 = t.reshape(tok, d)
    tiles = tile_expert.shape[0] // nch
    pc = tiles * MOE_TILE
    y_sorted = None
    for c in range(nch):
        x_c = t2.at[tok_of_pos[c * pc:(c + 1) * pc]].get(mode="promise_in_bounds")
        y_sorted = _experts(tile_expert[c * tiles:(c + 1) * tiles], tile_valid[c * tiles:(c + 1) * tiles],
                            x_c, y_sorted, w_gate, w_up, w_down, layer, c, nch)
    bc = b // nch
    pos3 = pos_of_assign.reshape(2, b, n)
    out = None
    for c in range(nch):
        pos_c = pos3[:, c * bc:(c + 1) * bc].reshape(2 * bc * n)
        y2_c = y_sorted.at[pos_c].get(mode="promise_in_bounds")
        out = _post_norm(h1, y2_c, route, m, ln, out, c, nch)
    return out


def _rope_tables(n):
    t = jnp.arange(n)
    rows = (t // GRID_W).astype(F32)
    cols = (t % GRID_W).astype(F32)
    n_freq = RET_DK // 4
    inv_freq = ROPE_BASE ** (-jnp.arange(n_freq, dtype=F32) / n_freq)
    ang = jnp.concatenate([rows[:, None] * inv_freq, cols[:, None] * inv_freq], axis=-1)
    cos, sin = jnp.cos(ang), jnp.sin(ang)
    cos2 = jnp.concatenate([cos, cos], axis=-1)
    sin2 = jnp.concatenate([-sin, sin], axis=-1)
    q_scale = RET_DK ** -0.5
    return cos2 * q_scale, sin2 * q_scale, cos2, sin2


def _router_params(w_r1, b_r1, w_r2, b_r2):
    d = w_r1.shape[0]
    w2 = jnp.transpose(w_r2, (1, 0, 2)).reshape(d, MOE_EXPERTS)
    pad = LANES - MOE_GROUPS - MOE_EXPERTS
    wr = jnp.concatenate([w_r1, w2, jnp.zeros((d, pad), F32)], axis=-1)
    br = jnp.concatenate([b_r1, b_r2.reshape(MOE_EXPERTS), jnp.zeros((pad,), F32)]).reshape(1, LANES)
    return wr, br


def kernel(x, c, ctx, c_ctx, w_mod, b_mod, ln_g, ln_b, ab_w_in, ab_w_out, ab_log_decay, ab_rpb, pool_w, pool_scale, moe_w_r1, moe_b_r1, moe_w_r2, moe_b_r2, moe_w_gate, moe_w_up, moe_w_down):
    b, n, d = x.shape
    cc = jnp.concatenate([c, c_ctx[None, :], jnp.zeros((MOD_ROWS - b - 1, d), F32)], axis=0)
    mod = _modulation(cc, w_mod, b_mod)
    h = x
    for i in range(DEPTH):
        j = i // 2
        m = mod[i, :b].reshape(b, N_MOD, d)
        ln1 = jnp.stack([ln_g[i, 0], ln_b[i, 0]])
        ln2 = jnp.stack([ln_g[i, 1], ln_b[i, 1]])
        router = _router_params(moe_w_r1[i], moe_b_r1[i], moe_w_r2[i], moe_b_r2[i])
        if i % 2 == 0:
            m_ctx = mod[i, b].reshape(1, N_MOD, d)
            w_in_bf = ab_w_in[j].astype(BF16)
            log_gamma2 = jnp.log1p(-jnp.exp(ab_log_decay[j].astype(F32)))
            rk, rv, nk, nv, rq, rg, nq = _in_proj(h, m, w_in_bf, _rope_tables(n))
            rk_c, rv_c, nk_c, nv_c = _ctx_proj(ctx, m_ctx, w_in_bf[:, :N_KV_GROUPS * HEAD_W])
            y_ret = _retention(log_gamma2, rq, rk, rv, rg, rk_c, rv_c)
            y_na = _neighbourhood_attention(nq, nk, nv, nk_c, nv_c, ab_rpb[j])
            h1, t, route = _out_proj(h, y_ret, y_na, ab_w_out[j].astype(BF16), m, ln1, router)
        else:
            h1, t, route = _pool_mixer(h, pool_w[j].astype(BF16), pool_scale[j], m, ln1, router)
        h = _moe_post_norm(h1, t, route, m, ln2, moe_w_gate, moe_w_up, moe_w_down, i)
    return h
```

```python
import functools

import numpy as np
import jax
import jax.numpy as jnp
from jax import lax
from jax.experimental import pallas as pl
from jax.experimental.pallas import tpu as pltpu

F32 = jnp.float32
BF16 = jnp.bfloat16
HIGHEST = lax.Precision.HIGHEST

D_MODEL = 1024
DEPTH = 2
GRID_W = 64
RET_HEADS = 4
RET_DK = 128
RET_CHUNK = 128
NA_HEADS = 8
NA_DH = 64
NA_WIN_R = 8
NA_WIN_C = 16
N_BIAS_ROWS = 2 * NA_WIN_R - 1
POOL_SIZES = (2, 4, 8, 16)
POOL_GROUP = D_MODEL // len(POOL_SIZES)
MOE_GROUPS = 4
MOE_PER_GROUP = 8
MOE_EXPERTS = MOE_GROUPS * MOE_PER_GROUP
MOE_HIDDEN = D_MODEL // 2
ROPE_BASE = 10000.0
LN_EPS = 1e-5
N_MOD = 6
DEEPNORM_ALPHA = (2 * DEPTH) ** 0.25
HEAD_W = 512
N_IN_GROUPS = 7
N_KV_GROUPS = 4
MASK_VALUE = -1e30

LANES = 128
VMEM_LIMIT = 56 * 1024 * 1024

NA_ROWS_PER_BLOCK = 4
ROW_TILE = 512
MOE_TILE = 512
MOD_ROWS = 24
MOD_COL_TILE = 1536


def _cparams(*sem):
    return pltpu.CompilerParams(dimension_semantics=sem, vmem_limit_bytes=VMEM_LIMIT)


def _silu(v):
    return v / (1.0 + jnp.exp(-v))


def _dot(a, b):
    return jnp.dot(a, b, preferred_element_type=F32)


def _dot_nt(a, b):
    return lax.dot_general(a, b, (((1,), (1,)), ((), ())), preferred_element_type=F32)


def _dot_tn(a, b):
    return lax.dot_general(a, b, (((0,), (0,)), ((), ())), preferred_element_type=F32)


def _mod_kernel(c_ref, w_ref, b_ref, o_ref):
    s = _silu(c_ref[...])
    o_ref[0] = jnp.dot(s, w_ref[0], precision=HIGHEST, preferred_element_type=F32) + b_ref[0]


def _modulation(cc, w_mod, b_mod):
    depth, d, n = w_mod.shape
    return pl.pallas_call(
        _mod_kernel,
        out_shape=jax.ShapeDtypeStruct((depth, MOD_ROWS, n), F32),
        grid=(depth, n // MOD_COL_TILE),
        in_specs=[
            pl.BlockSpec((MOD_ROWS, d), lambda i, j: (0, 0)),
            pl.BlockSpec((1, d, MOD_COL_TILE), lambda i, j: (i, 0, j)),
            pl.BlockSpec((1, 1, MOD_COL_TILE), lambda i, j: (i, 0, j)),
        ],
        out_specs=pl.BlockSpec((1, MOD_ROWS, MOD_COL_TILE), lambda i, j: (i, 0, j)),
        compiler_params=_cparams("arbitrary", "arbitrary"),
        name="modulation",
    )(cc, w_mod, b_mod.reshape(depth, 1, n))


def _rope(v, cos2, sin2):
    return v * cos2 + pltpu.roll(v, RET_DK // 2, axis=1) * sin2


def _in_proj_kernel(x_ref, m_ref, w_ref, cq_ref, sq_ref, ck_ref, sk_ref,
                    rk_ref, rv_ref, nk_ref, nv_ref, rq_ref, rg_ref, nq_ref):
    shift = m_ref[0, 0:1, :]
    scale = m_ref[0, 1:2, :]
    hm = (x_ref[0] * (1.0 + scale) + shift).astype(BF16)
    outs = (rk_ref, rv_ref, nk_ref, nv_ref, rq_ref, rg_ref, nq_ref)
    for g, o_ref in enumerate(outs):
        p = _dot(hm, w_ref[:, g * HEAD_W:(g + 1) * HEAD_W])
        if g == 0 or g == 4:
            cos2 = (ck_ref if g == 0 else cq_ref)[...]
            sin2 = (sk_ref if g == 0 else sq_ref)[...]
            for hd in range(RET_HEADS):
                sl = slice(hd * RET_DK, (hd + 1) * RET_DK)
                o_ref[0, :, sl] = _rope(p[:, sl], cos2, sin2).astype(BF16)
        elif g == 6:
            o_ref[0] = (p * NA_DH ** -0.5).astype(BF16)
        else:
            o_ref[0] = p.astype(BF16)


def _in_proj(x, m, w_in_bf, rope_tabs):
    b, n, d = x.shape
    tm = ROW_TILE
    tab_spec = pl.BlockSpec((tm, RET_DK), lambda i, j: (j, 0))
    out_spec = pl.BlockSpec((1, tm, HEAD_W), lambda i, j: (i, j, 0))
    return pl.pallas_call(
        _in_proj_kernel,
        out_shape=[jax.ShapeDtypeStruct((b, n, HEAD_W), BF16)] * N_IN_GROUPS,
        grid=(b, n // tm),
        in_specs=[
            pl.BlockSpec((1, tm, d), lambda i, j: (i, j, 0)),
            pl.BlockSpec((1, N_MOD, d), lambda i, j: (i, 0, 0)),
            pl.BlockSpec((d, N_IN_GROUPS * HEAD_W), lambda i, j: (0, 0)),
            tab_spec, tab_spec, tab_spec, tab_spec,
        ],
        out_specs=[out_spec] * N_IN_GROUPS,
        compiler_params=_cparams("arbitrary", "arbitrary"),
        name="in_proj",
    )(x, m, w_in_bf, *rope_tabs)


def _ctx_proj_kernel(x_ref, m_ref, w_ref, rk_ref, rv_ref, nk_ref, nv_ref):
    shift = m_ref[0, 0:1, :]
    scale = m_ref[0, 1:2, :]
    hm = (x_ref[0] * (1.0 + scale) + shift).astype(BF16)
    for g, o_ref in enumerate((rk_ref, rv_ref, nk_ref, nv_ref)):
        o_ref[0] = _dot(hm, w_ref[:, g * HEAD_W:(g + 1) * HEAD_W]).astype(BF16)


def _ctx_proj(ctx, m_ctx, w_in_bf):
    b, l, d = ctx.shape
    out_spec = pl.BlockSpec((1, l, HEAD_W), lambda i: (i, 0, 0))
    return pl.pallas_call(
        _ctx_proj_kernel,
        out_shape=[jax.ShapeDtypeStruct((b, l, HEAD_W), BF16)] * N_KV_GROUPS,
        grid=(b,),
        in_specs=[
            pl.BlockSpec((1, l, d), lambda i: (i, 0, 0)),
            pl.BlockSpec((1, N_MOD, d), lambda i: (0, 0, 0)),
            pl.BlockSpec((d, N_KV_GROUPS * HEAD_W), lambda i: (0, 0)),
        ],
        out_specs=[out_spec] * N_KV_GROUPS,
        compiler_params=_cparams("arbitrary"),
        name="ctx_proj",
    )(ctx, m_ctx, w_in_bf)


def _retention_kernel(lg_ref, q_ref, k_ref, v_ref, g_ref, kc_ref, vc_ref, o_ref, u_ref, s_ref):
    hd = pl.program_id(1)
    n = q_ref.shape[1]
    c = RET_CHUNK
    dk = RET_DK
    nc = n // c
    l = kc_ref.shape[1]
    lgf = lg_ref[0, hd]
    lgb = lg_ref[1, hd]

    ii = lax.broadcasted_iota(jnp.int32, (c, c), 0).astype(F32)
    jj = lax.broadcasted_iota(jnp.int32, (c, c), 1).astype(F32)
    diff = ii - jj
    decay = (jnp.where(diff >= 0, jnp.exp(lgf * jnp.maximum(diff, 0.0)), 0.0)
             + jnp.where(diff <= 0, jnp.exp(lgb * jnp.maximum(-diff, 0.0)), 0.0))
    idx = lax.broadcasted_iota(jnp.int32, (c, 1), 0).astype(F32)
    q_dec_f = jnp.exp(lgf * (idx + 1.0))
    k_dec_f = jnp.exp(lgf * (c - 1.0 - idx))
    q_dec_b = jnp.exp(lgb * (c - idx))
    k_dec_b = jnp.exp(lgb * idx)
    ones = jnp.ones((1, dk), F32)
    chunk_dec_f = jnp.exp(ones * (lgf * c))
    chunk_dec_b = jnp.exp(ones * (lgb * c))

    pos = lax.broadcasted_iota(jnp.int32, (l, 1), 0).astype(F32)
    kc = kc_ref[0].astype(F32)
    vc = vc_ref[0]
    s_f0 = _dot_tn((kc * jnp.exp(lgf * (l - 1.0 - pos))).astype(BF16), vc)
    s_b0 = _dot_tn((kc * jnp.exp(lgb * pos)).astype(BF16), vc)

    def chunk_rows(i):
        return pl.ds(pl.multiple_of(i * c, c), c)

    def kv_step(i, carry):
        rows = chunk_rows(i)
        k_i = k_ref[0, rows, :].astype(F32)
        kk = jnp.concatenate([(k_i * k_dec_f).astype(BF16), (k_i * k_dec_b).astype(BF16)], axis=1)
        u_ref[i] = _dot_tn(kk, v_ref[0, rows, :])
        return carry

    lax.fori_loop(0, nc, kv_step, 0, unroll=True)

    def scan_f(i, s):
        s_ref[i, :, 0:dk] = s.astype(BF16)
        return s * chunk_dec_f + u_ref[i, 0:dk, :]

    lax.fori_loop(0, nc, scan_f, s_f0, unroll=True)

    def scan_b(t, s):
        i = nc - 1 - t
        s_ref[i, :, dk:2 * dk] = s.astype(BF16)
        return s * chunk_dec_b + u_ref[i, dk:2 * dk, :]

    lax.fori_loop(0, nc, scan_b, s_b0, unroll=True)

    def out_step(i, carry):
        rows = chunk_rows(i)
        q_i = q_ref[0, rows, :]
        v_i = v_ref[0, rows, :]
        att = (_dot_nt(q_i, k_ref[0, rows, :]) * decay).astype(BF16)
        inter = _dot(q_i, s_ref[i])
        o = _dot(att, v_i) + inter[:, 0:dk] * q_dec_f + inter[:, dk:2 * dk] * q_dec_b
        mu = jnp.mean(o, axis=-1, keepdims=True)
        var = jnp.mean(jnp.square(o - mu), axis=-1, keepdims=True)
        o_n = (o - mu) * lax.rsqrt(var + LN_EPS)
        gate = g_ref[0, rows, :].astype(F32)
        o_ref[0, rows, :] = (_silu(gate) * o_n).astype(BF16)
        return carry

    lax.fori_loop(0, nc, out_step, 0, unroll=True)


def _retention(log_gamma2, rq, rk, rv, rg, rk_c, rv_c):
    b, n, _ = rq.shape
    l = rk_c.shape[1]
    nc = n // RET_CHUNK
    seq_spec = pl.BlockSpec((1, n, RET_DK), lambda i, h: (i, 0, h))
    ctx_spec = pl.BlockSpec((1, l, RET_DK), lambda i, h: (i, 0, h))
    return pl.pallas_call(
        _retention_kernel,
        out_shape=jax.ShapeDtypeStruct((b, n, HEAD_W), BF16),
        grid=(b, RET_HEADS),
        in_specs=[pl.BlockSpec(memory_space=pltpu.SMEM),
                  seq_spec, seq_spec, seq_spec, seq_spec, ctx_spec, ctx_spec],
        out_specs=seq_spec,
        scratch_shapes=[pltpu.VMEM((nc, 2 * RET_DK, RET_DK), F32),
                        pltpu.VMEM((nc, RET_DK, 2 * RET_DK), BF16)],
        compiler_params=_cparams("arbitrary", "arbitrary"),
        name="retention",
    )(log_gamma2, rq, rk, rv, rg, rk_c, rv_c)


def _na_geometry(rows):
    rb = NA_ROWS_PER_BLOCK
    wr = min(NA_WIN_R, rows)
    key_rows = min(rows, rb + wr - 1)
    variants, block_variant, block_start = [], [], []
    for kb in range(rows // rb):
        q_rows = kb * rb + np.arange(rb)
        r0 = np.clip(q_rows - wr // 2, 0, rows - wr)
        ks = int(np.clip(r0.min(), 0, rows - key_rows))
        assert r0.max() + wr <= ks + key_rows
        kr = ks + np.arange(key_rows)
        valid = (kr[None, :] >= r0[:, None]) & (kr[None, :] < r0[:, None] + wr)
        ridx = np.where(valid, kr[None, :] - q_rows[:, None] + NA_WIN_R - 1, N_BIAS_ROWS).astype(np.int32)
        for vi, v in enumerate(variants):
            if np.array_equal(v, ridx):
                block_variant.append(vi)
                break
        else:
            variants.append(ridx)
            block_variant.append(len(variants) - 1)
        block_start.append(ks)
    return key_rows, np.stack(variants), block_variant, block_start


def _na_bias_table(rpb, variant_rows):
    n_heads = rpb.shape[0]
    qc = np.arange(GRID_W)[:, None]
    kc = np.arange(GRID_W)[None, :]
    wstart = np.clip(qc - NA_WIN_C // 2, 0, GRID_W - NA_WIN_C)
    col_ok = (kc >= wstart) & (kc < wstart + NA_WIN_C)
    cidx = np.clip(kc - qc + NA_WIN_C - 1, 0, 2 * NA_WIN_C - 2)
    onehot = ((cidx[None] == np.arange(2 * NA_WIN_C - 1)[:, None, None]) & col_ok[None]).astype(np.float32)
    blocks = jnp.einsum("hrd,dqk->hrqk", rpb.astype(F32), jnp.asarray(onehot), precision=HIGHEST)
    blocks = jnp.where(col_ok[None, None], blocks, MASK_VALUE)
    blocks = jnp.concatenate([blocks, jnp.full((n_heads, 1, GRID_W, GRID_W), MASK_VALUE, F32)], axis=1)
    nvar, rb, key_rows = variant_rows.shape
    tab = blocks[:, variant_rows.reshape(-1)].reshape(n_heads, nvar, rb, key_rows, GRID_W, GRID_W)
    return jnp.transpose(tab, (0, 1, 2, 4, 3, 5)).reshape(n_heads, nvar, rb * GRID_W, key_rows * GRID_W)


def _na_kernel(q_ref, k_ref, v_ref, kc_ref, vc_ref, bias_ref, o_ref, *, block_variant, block_start, key_rows):
    qb = NA_ROWS_PER_BLOCK * GRID_W
    nk = key_rows * GRID_W
    lane = lax.broadcasted_iota(jnp.int32, (qb, LANES), 1)
    kc = kc_ref[0]
    vc = vc_ref[0]
    for kb, (var, ks) in enumerate(zip(block_variant, block_start)):
        q = q_ref[0, kb * qb:(kb + 1) * qb, :]
        k_win = k_ref[0, ks * GRID_W:ks * GRID_W + nk, :]
        v_win = v_ref[0, ks * GRID_W:ks * GRID_W + nk, :]
        outs = []
        for j in range(2):
            in_head = (lane >= j * NA_DH) & (lane < (j + 1) * NA_DH)
            qm = jnp.where(in_head, q, jnp.zeros_like(q))
            s_win = _dot_nt(qm, k_win) + bias_ref[j, var]
            s_ctx = _dot_nt(qm, kc)
            m = jnp.maximum(jnp.max(s_win, axis=-1, keepdims=True), jnp.max(s_ctx, axis=-1, keepdims=True))
            p_win = jnp.exp(s_win - m)
            p_ctx = jnp.exp(s_ctx - m)
            denom = jnp.sum(p_win, axis=-1, keepdims=True) + jnp.sum(p_ctx, axis=-1, keepdims=True)
            o = _dot(p_win.astype(BF16), v_win) + _dot(p_ctx.astype(BF16), vc)
            outs.append(o / denom)
        o_ref[0, kb * qb:(kb + 1) * qb, :] = jnp.where(lane < NA_DH, outs[0], outs[1]).astype(BF16)


def _neighbourhood_attention(nq, nk, nv, nk_c, nv_c, rpb):
    b, n, _ = nq.shape
    l = nk_c.shape[1]
    rows = n // GRID_W
    key_rows, variant_rows, block_variant, block_start = _na_geometry(rows)
    bias = _na_bias_table(rpb, variant_rows)
    nvar = variant_rows.shape[0]
    qb = NA_ROWS_PER_BLOCK * GRID_W
    nkeys = key_rows * GRID_W
    seq_spec = pl.BlockSpec((1, n, LANES), lambda h, i: (i, 0, h))
    ctx_spec = pl.BlockSpec((1, l, LANES), lambda h, i: (i, 0, h))
    kern = functools.partial(_na_kernel, block_variant=tuple(block_variant), block_start=tuple(block_start),
                             key_rows=key_rows)
    return pl.pallas_call(
        kern,
        out_shape=jax.ShapeDtypeStruct((b, n, HEAD_W), BF16),
        grid=(NA_HEADS // 2, b),
        in_specs=[seq_spec, seq_spec, seq_spec, ctx_spec, ctx_spec,
                  pl.BlockSpec((2, nvar, qb, nkeys), lambda h, i: (h, 0, 0, 0))],
        out_specs=seq_spec,
        compiler_params=_cparams("arbitrary", "arbitrary"),
        name="neighbourhood_attention",
    )(nq, nk, nv, nk_c, nv_c, bias)


def _layer_norm(z, g, b):
    mu = jnp.mean(z, axis=-1, keepdims=True)
    var = jnp.mean(jnp.square(z - mu), axis=-1, keepdims=True)
    return (z - mu) * lax.rsqrt(var + LN_EPS) * g + b


def _split_bf16(v):
    hi = v.astype(BF16)
    return hi, (v - hi.astype(F32)).astype(BF16)


def _route(t, wr_ref, br_ref):
    t_hi, t_lo = _split_bf16(t)
    w_hi, w_lo = _split_bf16(wr_ref[...])
    hi = _dot(t_hi, jnp.concatenate([w_hi, w_lo], axis=1))
    logits = hi[:, 0:LANES] + (_dot(t_lo, w_hi) + hi[:, LANES:2 * LANES]) + br_ref[...]
    lane = lax.broadcasted_iota(jnp.int32, logits.shape, 1).astype(F32)
    neg = -jnp.inf
    big = float(LANES)

    def first_max(vals):
        vmax = jnp.max(vals, axis=-1, keepdims=True)
        return vmax, jnp.min(jnp.where(vals == vmax, lane, big), axis=-1, keepdims=True)

    is_grp = lane < MOE_GROUPS
    g_max, grp = first_max(jnp.where(is_grp, logits, neg))
    g_sum = jnp.sum(jnp.where(is_grp, jnp.exp(logits - g_max), 0.0), axis=-1, keepdims=True)
    gate_g = 1.0 / g_sum
    lo = MOE_GROUPS + grp * MOE_PER_GROUP
    in_grp = (lane >= lo) & (lane < lo + MOE_PER_GROUP)
    le = jnp.where(in_grp, logits, neg)
    v1, i1 = first_max(le)
    v2, i2 = first_max(jnp.where(lane == i1, neg, le))
    e21 = jnp.exp(v2 - v1)
    w1 = gate_g / (1.0 + e21)
    w2 = gate_g * e21 / (1.0 + e21)
    return jnp.where(lane == 0, w1,
                     jnp.where(lane == 1, w2,
                               jnp.where(lane == 2, i1 - MOE_GROUPS,
                                         jnp.where(lane == 3, i2 - MOE_GROUPS, 0.0))))


def _post_norm_route(h, y, m_ref, ln_ref, wr_ref, br_ref, h_out, t_out, r_out):
    gate = m_ref[0, 2:3, :]
    shift = m_ref[0, 3:4, :]
    scale = m_ref[0, 4:5, :]
    h1 = _layer_norm(DEEPNORM_ALPHA * h + gate * y, ln_ref[0:1, :], ln_ref[1:2, :])
    t = h1 * (1.0 + scale) + shift
    h_out[0] = h1
    t_out[0] = t.astype(BF16)
    r_out[0] = _route(t, wr_ref, br_ref)


def _epilogue_specs(b, n, d, tm):
    in_specs = [
        pl.BlockSpec((1, N_MOD, d), lambda i, j: (i, 0, 0)),
        pl.BlockSpec((2, d), lambda i, j: (0, 0)),
        pl.BlockSpec((d, LANES), lambda i, j: (0, 0)),
        pl.BlockSpec((1, LANES), lambda i, j: (0, 0)),
    ]
    out_shape = [jax.ShapeDtypeStruct((b, n, d), F32), jax.ShapeDtypeStruct((b, n, d), BF16),
                 jax.ShapeDtypeStruct((b, n, LANES), F32)]
    out_specs = [pl.BlockSpec((1, tm, d), lambda i, j: (i, j, 0)),
                 pl.BlockSpec((1, tm, d), lambda i, j: (i, j, 0)),
                 pl.BlockSpec((1, tm, LANES), lambda i, j: (i, j, 0))]
    return in_specs, out_shape, out_specs


def _out_proj_kernel(x_ref, yr_ref, yn_ref, w_ref, m_ref, ln_ref, wr_ref, br_ref, h_out, t_out, r_out):
    y = _dot(yr_ref[0], w_ref[0:HEAD_W, :]) + _dot(yn_ref[0], w_ref[HEAD_W:2 * HEAD_W, :])
    _post_norm_route(x_ref[0], y, m_ref, ln_ref, wr_ref, br_ref, h_out, t_out, r_out)


def _out_proj(x, y_ret, y_na, w_out_bf, m, ln, router):
    b, n, d = x.shape
    tm = ROW_TILE
    ep_in, out_shape, out_specs = _epilogue_specs(b, n, d, tm)
    return pl.pallas_call(
        _out_proj_kernel,
        out_shape=out_shape,
        grid=(b, n // tm),
        in_specs=[
            pl.BlockSpec((1, tm, d), lambda i, j: (i, j, 0)),
            pl.BlockSpec((1, tm, HEAD_W), lambda i, j: (i, j, 0)),
            pl.BlockSpec((1, tm, HEAD_W), lambda i, j: (i, j, 0)),
            pl.BlockSpec((2 * HEAD_W, d), lambda i, j: (0, 0)),
        ] + ep_in,
        out_specs=out_specs,
        compiler_params=_cparams("arbitrary", "arbitrary"),
        name="out_proj",
    )(x, y_ret, y_na, w_out_bf, m, ln, *router)


POOL_HALO = max(POOL_SIZES) // 2
POOL_TILE = 256


def _pool_kernel(x_ref, prev_ref, next_ref, pw_ref, ps_ref, m_ref, ln_ref, wr_ref, br_ref,
                 h_out, t_out, r_out, *, n):
    j = pl.program_id(1)
    nj = pl.num_programs(1)
    tm = x_ref.shape[1]
    halo = POOL_HALO
    shift = m_ref[0, 0:1, :]
    scale = m_ref[0, 1:2, :]
    x = x_ref[0]
    hm = x * (1.0 + scale) + shift
    prev = jnp.where(j > 0, prev_ref[0] * (1.0 + scale) + shift, 0.0)
    nxt = jnp.where(j < nj - 1, next_ref[0] * (1.0 + scale) + shift, 0.0)
    ext = jnp.concatenate([prev, hm, nxt], axis=0)
    pos = (j * tm + lax.broadcasted_iota(jnp.int32, (tm, 1), 0))
    ys = []
    for g, w in enumerate(POOL_SIZES):
        cols = slice(g * POOL_GROUP, (g + 1) * POOL_GROUP)
        s = ext[:, cols]
        span = 1
        while span < w:
            s = s[:s.shape[0] - span] + s[span:]
            span *= 2
        off = halo - w // 2
        win = s[off:off + tm]
        cnt = (jnp.minimum(pos + (w - w // 2), n) - jnp.maximum(pos - w // 2, 0)).astype(F32)
        z = (win / cnt - hm[:, cols]).astype(BF16)
        ys.append(_dot(z, pw_ref[g]))
    y = jnp.concatenate(ys, axis=-1) * ps_ref[...]
    _post_norm_route(x, y, m_ref, ln_ref, wr_ref, br_ref, h_out, t_out, r_out)


def _pool_mixer(h, pool_w_bf, pool_scale, m, ln, router):
    b, n, d = h.shape
    tm = POOL_TILE
    halo = POOL_HALO
    blocks_per_tile = tm // halo
    n_halo_blocks = n // halo
    ep_in, out_shape, out_specs = _epilogue_specs(b, n, d, tm)
    return pl.pallas_call(
        functools.partial(_pool_kernel, n=n),
        out_shape=out_shape,
        grid=(b, n // tm),
        in_specs=[
            pl.BlockSpec((1, tm, d), lambda i, j: (i, j, 0)),
            pl.BlockSpec((1, halo, d), lambda i, j: (i, jnp.maximum(j * blocks_per_tile - 1, 0), 0)),
            pl.BlockSpec((1, halo, d),
                         lambda i, j: (i, jnp.minimum((j + 1) * blocks_per_tile, n_halo_blocks - 1), 0)),
            pl.BlockSpec((len(POOL_SIZES), POOL_GROUP, POOL_GROUP), lambda i, j: (0, 0, 0)),
            pl.BlockSpec((1, d), lambda i, j: (0, 0)),
        ] + ep_in,
        out_specs=out_specs,
        compiler_params=_cparams("arbitrary", "arbitrary"),
        name="pool_mixer",
    )(h, h, h, pool_w_bf, pool_scale.reshape(1, d), m, ln, *router)


def _expert_kernel(te_ref, tv_ref, x_ref, wg_ref, wu_ref, wd_ref, buf_ref, o_ref, wg_s, wu_s, wd_s):
    del buf_ref
    j = pl.program_id(0)
    prev_e = te_ref[jnp.maximum(j - 1, 0)]

    @pl.when((j == 0) | (te_ref[j] != prev_e))
    def _():
        wg_s[...] = wg_ref[0, 0].astype(BF16)
        wu_s[...] = wu_ref[0, 0].astype(BF16)
        wd_s[...] = wd_ref[0, 0].astype(BF16)

    @pl.when(tv_ref[j] == 1)
    def _():
        x = x_ref[...]
        gate = _dot(x, wg_s[...])
        up = _dot(x, wu_s[...])
        act = (_silu(gate) * up).astype(BF16)
        o_ref[...] = _dot(act, wd_s[...]).astype(BF16)

    @pl.when(tv_ref[j] == 0)
    def _():
        o_ref[...] = jnp.zeros_like(o_ref)


def _experts(tile_expert, tile_valid, x_chunk, y_prev, w_gate, w_up, w_down, layer, chunk, n_chunks):
    pc, d = x_chunk.shape
    tm = MOE_TILE
    tiles = pc // tm
    hid = w_gate.shape[-1]
    in_specs = [
        pl.BlockSpec((tm, d), lambda j, te, tv: (j, 0)),
        pl.BlockSpec((1, 1, d, hid), lambda j, te, tv: (layer, te[j], 0, 0)),
        pl.BlockSpec((1, 1, d, hid), lambda j, te, tv: (layer, te[j], 0, 0)),
        pl.BlockSpec((1, 1, hid, d), lambda j, te, tv: (layer, te[j], 0, 0)),
    ]
    in_specs.append(pl.BlockSpec(memory_space=pl.ANY))
    args = [tile_expert, tile_valid, x_chunk, w_gate, w_up, w_down, y_prev]
    aliases = {len(args) - 1: 0}
    grid_spec = pltpu.PrefetchScalarGridSpec(
        num_scalar_prefetch=2,
        grid=(tiles,),
        in_specs=in_specs,
        out_specs=pl.BlockSpec((tm, d), lambda j, te, tv: (chunk * tiles + j, 0)),
        scratch_shapes=[pltpu.VMEM((d, hid), BF16), pltpu.VMEM((d, hid), BF16), pltpu.VMEM((hid, d), BF16)],
    )
    return pl.pallas_call(
        _expert_kernel,
        out_shape=jax.ShapeDtypeStruct((pc * n_chunks, d), BF16),
        grid_spec=grid_spec,
        input_output_aliases=aliases,
        compiler_params=_cparams("arbitrary"),
        name="experts",
    )(*args)


def _dispatch_plan(route, tm):
    tok = route.shape[0]
    n_assign = 2 * tok
    p = n_assign + MOE_EXPERTS * tm
    n_tiles = p // tm
    i32 = jnp.int32
    eid = jnp.transpose(route[:, 2:4]).astype(i32).reshape(n_assign)
    sorted_e, order = lax.sort((eid, jnp.arange(n_assign, dtype=i32)), num_keys=1)
    experts = jnp.arange(MOE_EXPERTS, dtype=i32)
    counts = jnp.sum((eid[:, None] == experts[None, :]).astype(i32), axis=0)
    padded = ((counts + tm - 1) // tm) * tm
    pad_end = jnp.cumsum(padded)
    pad_start = pad_end - padded
    start = jnp.cumsum(counts) - counts
    tile_row = jnp.arange(n_tiles, dtype=i32) * tm
    tile_expert = jnp.minimum(jnp.sum((pad_end[None, :] <= tile_row[:, None]).astype(i32), axis=1), MOE_EXPERTS - 1)
    tile_valid = (tile_row < pad_end[-1]).astype(i32)
    rank = (tile_row - pad_start[tile_expert])[:, None] + jnp.arange(tm, dtype=i32)[None, :]
    row_valid = (rank < counts[tile_expert][:, None]).reshape(p)
    src = jnp.clip(start[tile_expert][:, None] + rank, 0, n_assign - 1).reshape(p)
    src_assign = order.at[src].get(mode="promise_in_bounds")
    tok_of_pos = jnp.where(row_valid, src_assign, jnp.arange(p, dtype=i32)) % tok
    offs = pad_start - start
    dest_sorted = jnp.arange(n_assign, dtype=i32) + jnp.sum(
        jnp.where(sorted_e[:, None] == experts[None, :], offs[None, :], 0), axis=1)
    _, pos_of_assign = lax.sort((order, dest_sorted), num_keys=1)
    return tile_expert, tile_valid, tok_of_pos, pos_of_assign


def _n_chunks(b):
    return 4 if b % 4 == 0 else (2 if b % 2 == 0 else 1)


def _post_norm_kernel(h_ref, y0_ref, y1_ref, r_ref, m_ref, ln_ref, o_ref):
    gate = m_ref[0, 5:6, :]
    r = r_ref[0]
    y = r[:, 0:1] * y0_ref[...].astype(F32) + r[:, 1:2] * y1_ref[...].astype(F32)
    o_ref[0] = _layer_norm(DEEPNORM_ALPHA * h_ref[0] + gate * y, ln_ref[0:1, :], ln_ref[1:2, :])


def _post_norm(h, y2, route, m, ln, chunk, n_chunks):
    b, n, d = h.shape
    tm = ROW_TILE
    bc = b // n_chunks
    b0 = chunk * bc
    tiles_per_seq = n // tm
    tiles_per_k = bc * tiles_per_seq
    row_spec = pl.BlockSpec((1, tm, d), lambda i, j: (b0 + i, j, 0))
    return pl.pallas_call(
        _post_norm_kernel,
        out_shape=jax.ShapeDtypeStruct((b, n, d), F32),
        grid=(bc, tiles_per_seq),
        in_specs=[row_spec,
                  pl.BlockSpec((tm, d), lambda i, j: (i * tiles_per_seq + j, 0)),
                  pl.BlockSpec((tm, d), lambda i, j: (tiles_per_k + i * tiles_per_seq + j, 0)),
                  pl.BlockSpec((1, tm, LANES), lambda i, j: (b0 + i, j, 0)),
                  pl.BlockSpec((1, N_MOD, d), lambda i, j: (b0 + i, 0, 0)),
                  pl.BlockSpec((2, d), lambda i, j: (0, 0))],
        out_specs=row_spec,
        input_output_aliases={0: 0},
        compiler_params=_cparams("arbitrary", "arbitrary"),
        name="post_norm",
    )(h, y2, y2, route, m, ln)


def _moe_rows(tok):
    return 2 * tok + MOE_EXPERTS * MOE_TILE


def _moe_post_norm(h1, t, route, m, ln, w_gate, w_up, w_down, layer, row_buf):
    b, n, d = t.shape
    tok = b * n
    nch = _n_chunks(b)
    tile_expert, tile_valid, tok_of_pos, pos_of_assign = _dispatch_plan(route.reshape(tok, LANES), MOE_TILE)
    t2 = t.reshape(tok, d)
    tiles = tile_expert.shape[0] // nch
    pc = tiles * MOE_TILE
    y_sorted = row_buf
    for c in range(nch):
        x_c = t2.at[tok_of_pos[c * pc:(c + 1) * pc]].get(mode="promise_in_bounds")
        y_sorted = _experts(tile_expert[c * tiles:(c + 1) * tiles], tile_valid[c * tiles:(c + 1) * tiles],
                            x_c, y_sorted, w_gate, w_up, w_down, layer, c, nch)
    bc = b // nch
    pos3 = pos_of_assign.reshape(2, b, n)
    h = h1
    for c in range(nch):
        pos_c = pos3[:, c * bc:(c + 1) * bc].reshape(2 * bc * n)
        y2_c = y_sorted.at[pos_c].get(mode="promise_in_bounds")
        h = _post_norm(h, y2_c, route, m, ln, c, nch)
    return h, y_sorted


def _rope_tables(n):
    t = jnp.arange(n)
    rows = (t // GRID_W).astype(F32)
    cols = (t % GRID_W).astype(F32)
    n_freq = RET_DK // 4
    inv_freq = ROPE_BASE ** (-jnp.arange(n_freq, dtype=F32) / n_freq)
    ang = jnp.concatenate([rows[:, None] * inv_freq, cols[:, None] * inv_freq], axis=-1)
    cos, sin = jnp.cos(ang), jnp.sin(ang)
    cos2 = jnp.concatenate([cos, cos], axis=-1)
    sin2 = jnp.concatenate([-sin, sin], axis=-1)
    q_scale = RET_DK ** -0.5
    return cos2 * q_scale, sin2 * q_scale, cos2, sin2


def _router_params(w_r1, b_r1, w_r2, b_r2):
    d = w_r1.shape[0]
    w2 = jnp.transpose(w_r2, (1, 0, 2)).reshape(d, MOE_EXPERTS)
    pad = LANES - MOE_GROUPS - MOE_EXPERTS
    wr = jnp.concatenate([w_r1, w2, jnp.zeros((d, pad), F32)], axis=-1)
    br = jnp.concatenate([b_r1, b_r2.reshape(MOE_EXPERTS), jnp.zeros((pad,), F32)]).reshape(1, LANES)
    return wr, br


def kernel(x, c, ctx, c_ctx, w_mod, b_mod, ln_g, ln_b, ab_w_in, ab_w_out, ab_log_decay, ab_rpb, pool_w, pool_scale, moe_w_r1, moe_b_r1, moe_w_r2, moe_b_r2, moe_w_gate, moe_w_up, moe_w_down):
    b, n, d = x.shape
    cc = jnp.concatenate([c, c_ctx[None, :], jnp.zeros((MOD_ROWS - b - 1, d), F32)], axis=0)
    mod = _modulation(cc, w_mod, b_mod)
    h = x
    row_buf = jnp.zeros((_moe_rows(b * n), d), BF16)
    for i in range(DEPTH):
        j = i // 2
        m = mod[i, :b].reshape(b, N_MOD, d)
        ln1 = jnp.stack([ln_g[i, 0], ln_b[i, 0]])
        ln2 = jnp.stack([ln_g[i, 1], ln_b[i, 1]])
        router = _router_params(moe_w_r1[i], moe_b_r1[i], moe_w_r2[i], moe_b_r2[i])
        if i % 2 == 0:
            m_ctx = mod[i, b].reshape(1, N_MOD, d)
            w_in_bf = ab_w_in[j].astype(BF16)
            log_gamma2 = jnp.log1p(-jnp.exp(ab_log_decay[j].astype(F32)))
            rk, rv, nk, nv, rq, rg, nq = _in_proj(h, m, w_in_bf, _rope_tables(n))
            rk_c, rv_c, nk_c, nv_c = _ctx_proj(ctx, m_ctx, w_in_bf[:, :N_KV_GROUPS * HEAD_W])
            y_ret = _retention(log_gamma2, rq, rk, rv, rg, rk_c, rv_c)
            y_na = _neighbourhood_attention(nq, nk, nv, nk_c, nv_c, ab_rpb[j])
            h1, t, route = _out_proj(h, y_ret, y_na, ab_w_out[j].astype(BF16), m, ln1, router)
        else:
            h1, t, route = _pool_mixer(h, pool_w[j].astype(BF16), pool_scale[j], m, ln1, router)
        h, row_buf = _moe_post_norm(h1, t, route, m, ln2, moe_w_gate, moe_w_up, moe_w_down, i, row_buf)
    return h
```

```python
import functools

import numpy as np
import jax
import jax.numpy as jnp
from jax import lax
from jax.experimental import pallas as pl
from jax.experimental.pallas import tpu as pltpu

F32 = jnp.float32
BF16 = jnp.bfloat16
HIGHEST = lax.Precision.HIGHEST

D_MODEL = 1024
DEPTH = 2
GRID_W = 64
RET_HEADS = 4
RET_DK = 128
RET_CHUNK = 128
NA_HEADS = 8
NA_DH = 64
NA_WIN_R = 8
NA_WIN_C = 16
N_BIAS_ROWS = 2 * NA_WIN_R - 1
POOL_SIZES = (2, 4, 8, 16)
POOL_GROUP = D_MODEL // len(POOL_SIZES)
MOE_GROUPS = 4
MOE_PER_GROUP = 8
MOE_EXPERTS = MOE_GROUPS * MOE_PER_GROUP
MOE_HIDDEN = D_MODEL // 2
ROPE_BASE = 10000.0
LN_EPS = 1e-5
N_MOD = 6
DEEPNORM_ALPHA = (2 * DEPTH) ** 0.25
HEAD_W = 512
N_IN_GROUPS = 7
N_KV_GROUPS = 4
MASK_VALUE = -1e30

LANES = 128
VMEM_LIMIT = 56 * 1024 * 1024

NA_ROWS_PER_BLOCK = 4
ROW_TILE = 512
EPILOGUE_TILE = 1024
MOE_TILE = 512
MOD_ROWS = 24
MOD_COL_TILE = 1536


def _cparams(*sem):
    return pltpu.CompilerParams(dimension_semantics=sem, vmem_limit_bytes=VMEM_LIMIT)


def _silu(v):
    return v / (1.0 + jnp.exp(-v))


def _dot(a, b):
    return jnp.dot(a, b, preferred_element_type=F32)


def _dot_nt(a, b):
    return lax.dot_general(a, b, (((1,), (1,)), ((), ())), preferred_element_type=F32)


def _dot_tn(a, b):
    return lax.dot_general(a, b, (((0,), (0,)), ((), ())), preferred_element_type=F32)


def _mod_kernel(c_ref, w_ref, b_ref, o_ref):
    s = _silu(c_ref[...])
    o_ref[0] = jnp.dot(s, w_ref[0], precision=HIGHEST, preferred_element_type=F32) + b_ref[0]


def _modulation(cc, w_mod, b_mod):
    depth, d, n = w_mod.shape
    return pl.pallas_call(
        _mod_kernel,
        out_shape=jax.ShapeDtypeStruct((depth, MOD_ROWS, n), F32),
        grid=(depth, n // MOD_COL_TILE),
        in_specs=[
            pl.BlockSpec((MOD_ROWS, d), lambda i, j: (0, 0)),
            pl.BlockSpec((1, d, MOD_COL_TILE), lambda i, j: (i, 0, j)),
            pl.BlockSpec((1, 1, MOD_COL_TILE), lambda i, j: (i, 0, j)),
        ],
        out_specs=pl.BlockSpec((1, MOD_ROWS, MOD_COL_TILE), lambda i, j: (i, 0, j)),
        compiler_params=_cparams("arbitrary", "arbitrary"),
        name="modulation",
    )(cc, w_mod, b_mod.reshape(depth, 1, n))


def _rope(v, cos2, sin2):
    return v * cos2 + pltpu.roll(v, RET_DK // 2, axis=1) * sin2


def _in_proj_kernel(x_ref, m_ref, w_ref, cq_ref, sq_ref, ck_ref, sk_ref,
                    rk_ref, rv_ref, nk_ref, nv_ref, rq_ref, rg_ref, nq_ref, buf_ref):
    buf_ref[...] = jnp.zeros_like(buf_ref)
    shift = m_ref[0, 0:1, :]
    scale = m_ref[0, 1:2, :]
    hm = (x_ref[0] * (1.0 + scale) + shift).astype(BF16)
    outs = (rk_ref, rv_ref, nk_ref, nv_ref, rq_ref, rg_ref, nq_ref)
    for g, o_ref in enumerate(outs):
        p = _dot(hm, w_ref[:, g * HEAD_W:(g + 1) * HEAD_W])
        if g == 0 or g == 4:
            cos2 = (ck_ref if g == 0 else cq_ref)[...]
            sin2 = (sk_ref if g == 0 else sq_ref)[...]
            for hd in range(RET_HEADS):
                sl = slice(hd * RET_DK, (hd + 1) * RET_DK)
                o_ref[0, :, sl] = _rope(p[:, sl], cos2, sin2).astype(BF16)
        elif g == 6:
            o_ref[0] = (p * NA_DH ** -0.5).astype(BF16)
        else:
            o_ref[0] = p.astype(BF16)


def _in_proj(x, m, w_in_bf, rope_tabs, buf_rows):
    b, n, d = x.shape
    tm = ROW_TILE
    tiles_per_seq = n // tm
    buf_tile = buf_rows // (b * tiles_per_seq)
    assert buf_tile * b * tiles_per_seq == buf_rows and buf_tile % 16 == 0
    tab_spec = pl.BlockSpec((tm, RET_DK), lambda i, j: (j, 0))
    out_spec = pl.BlockSpec((1, tm, HEAD_W), lambda i, j: (i, j, 0))
    return pl.pallas_call(
        _in_proj_kernel,
        out_shape=[jax.ShapeDtypeStruct((b, n, HEAD_W), BF16)] * N_IN_GROUPS
        + [jax.ShapeDtypeStruct((buf_rows, d), BF16)],
        grid=(b, tiles_per_seq),
        in_specs=[
            pl.BlockSpec((1, tm, d), lambda i, j: (i, j, 0)),
            pl.BlockSpec((1, N_MOD, d), lambda i, j: (i, 0, 0)),
            pl.BlockSpec((d, N_IN_GROUPS * HEAD_W), lambda i, j: (0, 0)),
            tab_spec, tab_spec, tab_spec, tab_spec,
        ],
        out_specs=[out_spec] * N_IN_GROUPS + [pl.BlockSpec((buf_tile, d), lambda i, j: (i * tiles_per_seq + j, 0))],
        compiler_params=_cparams("arbitrary", "arbitrary"),
        name="in_proj",
    )(x, m, w_in_bf, *rope_tabs)


def _ctx_proj_kernel(x_ref, m_ref, w_ref, rk_ref, rv_ref, nk_ref, nv_ref):
    shift = m_ref[0, 0:1, :]
    scale = m_ref[0, 1:2, :]
    hm = (x_ref[0] * (1.0 + scale) + shift).astype(BF16)
    for g, o_ref in enumerate((rk_ref, rv_ref, nk_ref, nv_ref)):
        o_ref[0] = _dot(hm, w_ref[:, g * HEAD_W:(g + 1) * HEAD_W]).astype(BF16)


def _ctx_proj(ctx, m_ctx, w_in_bf):
    b, l, d = ctx.shape
    out_spec = pl.BlockSpec((1, l, HEAD_W), lambda i: (i, 0, 0))
    return pl.pallas_call(
        _ctx_proj_kernel,
        out_shape=[jax.ShapeDtypeStruct((b, l, HEAD_W), BF16)] * N_KV_GROUPS,
        grid=(b,),
        in_specs=[
            pl.BlockSpec((1, l, d), lambda i: (i, 0, 0)),
            pl.BlockSpec((1, N_MOD, d), lambda i: (0, 0, 0)),
            pl.BlockSpec((d, N_KV_GROUPS * HEAD_W), lambda i: (0, 0)),
        ],
        out_specs=[out_spec] * N_KV_GROUPS,
        compiler_params=_cparams("arbitrary"),
        name="ctx_proj",
    )(ctx, m_ctx, w_in_bf)


def _retention_kernel(lg_ref, q_ref, k_ref, v_ref, g_ref, kc_ref, vc_ref, o_ref, u_ref, s_ref):
    hd = pl.program_id(1)
    n = q_ref.shape[1]
    c = RET_CHUNK
    dk = RET_DK
    nc = n // c
    l = kc_ref.shape[1]
    lgf = lg_ref[0, hd]
    lgb = lg_ref[1, hd]

    ii = lax.broadcasted_iota(jnp.int32, (c, c), 0).astype(F32)
    jj = lax.broadcasted_iota(jnp.int32, (c, c), 1).astype(F32)
    diff = ii - jj
    decay = (jnp.where(diff >= 0, jnp.exp(lgf * jnp.maximum(diff, 0.0)), 0.0)
             + jnp.where(diff <= 0, jnp.exp(lgb * jnp.maximum(-diff, 0.0)), 0.0))
    idx = lax.broadcasted_iota(jnp.int32, (c, 1), 0).astype(F32)
    q_dec_f = jnp.exp(lgf * (idx + 1.0))
    k_dec_f = jnp.exp(lgf * (c - 1.0 - idx))
    q_dec_b = jnp.exp(lgb * (c - idx))
    k_dec_b = jnp.exp(lgb * idx)
    ones = jnp.ones((1, dk), F32)
    chunk_dec_f = jnp.exp(ones * (lgf * c))
    chunk_dec_b = jnp.exp(ones * (lgb * c))

    pos = lax.broadcasted_iota(jnp.int32, (l, 1), 0).astype(F32)
    kc = kc_ref[0].astype(F32)
    vc = vc_ref[0]
    s_f0 = _dot_tn((kc * jnp.exp(lgf * (l - 1.0 - pos))).astype(BF16), vc)
    s_b0 = _dot_tn((kc * jnp.exp(lgb * pos)).astype(BF16), vc)

    def chunk_rows(i):
        return pl.ds(pl.multiple_of(i * c, c), c)

    def kv_step(i, carry):
        rows = chunk_rows(i)
        k_i = k_ref[0, rows, :].astype(F32)
        kk = jnp.concatenate([(k_i * k_dec_f).astype(BF16), (k_i * k_dec_b).astype(BF16)], axis=1)
        u_ref[i] = _dot_tn(kk, v_ref[0, rows, :])
        return carry

    lax.fori_loop(0, nc, kv_step, 0, unroll=True)

    def scan_f(i, s):
        s_ref[i, :, 0:dk] = s.astype(BF16)
        return s * chunk_dec_f + u_ref[i, 0:dk, :]

    lax.fori_loop(0, nc, scan_f, s_f0, unroll=True)

    def scan_b(t, s):
        i = nc - 1 - t
        s_ref[i, :, dk:2 * dk] = s.astype(BF16)
        return s * chunk_dec_b + u_ref[i, dk:2 * dk, :]

    lax.fori_loop(0, nc, scan_b, s_b0, unroll=True)

    def out_step(i, carry):
        rows = chunk_rows(i)
        q_i = q_ref[0, rows, :]
        v_i = v_ref[0, rows, :]
        att = (_dot_nt(q_i, k_ref[0, rows, :]) * decay).astype(BF16)
        inter = _dot(q_i, s_ref[i])
        o = _dot(att, v_i) + inter[:, 0:dk] * q_dec_f + inter[:, dk:2 * dk] * q_dec_b
        mu = jnp.mean(o, axis=-1, keepdims=True)
        var = jnp.mean(jnp.square(o - mu), axis=-1, keepdims=True)
        o_n = (o - mu) * lax.rsqrt(var + LN_EPS)
        gate = g_ref[0, rows, :].astype(F32)
        o_ref[0, rows, :] = (_silu(gate) * o_n).astype(BF16)
        return carry

    lax.fori_loop(0, nc, out_step, 0, unroll=True)


def _retention(log_gamma2, rq, rk, rv, rg, rk_c, rv_c):
    b, n, _ = rq.shape
    l = rk_c.shape[1]
    nc = n // RET_CHUNK
    seq_spec = pl.BlockSpec((1, n, RET_DK), lambda i, h: (i, 0, h))
    ctx_spec = pl.BlockSpec((1, l, RET_DK), lambda i, h: (i, 0, h))
    return pl.pallas_call(
        _retention_kernel,
        out_shape=jax.ShapeDtypeStruct((b, n, HEAD_W), BF16),
        grid=(b, RET_HEADS),
        in_specs=[pl.BlockSpec(memory_space=pltpu.SMEM),
                  seq_spec, seq_spec, seq_spec, seq_spec, ctx_spec, ctx_spec],
        out_specs=seq_spec,
        scratch_shapes=[pltpu.VMEM((nc, 2 * RET_DK, RET_DK), F32),
                        pltpu.VMEM((nc, RET_DK, 2 * RET_DK), BF16)],
        compiler_params=_cparams("arbitrary", "arbitrary"),
        name="retention",
    )(log_gamma2, rq, rk, rv, rg, rk_c, rv_c)


def _na_geometry(rows):
    rb = NA_ROWS_PER_BLOCK
    wr = min(NA_WIN_R, rows)
    key_rows = min(rows, rb + wr - 1)
    variants, block_variant, block_start = [], [], []
    for kb in range(rows // rb):
        q_rows = kb * rb + np.arange(rb)
        r0 = np.clip(q_rows - wr // 2, 0, rows - wr)
        ks = int(np.clip(r0.min(), 0, rows - key_rows))
        assert r0.max() + wr <= ks + key_rows
        kr = ks + np.arange(key_rows)
        valid = (kr[None, :] >= r0[:, None]) & (kr[None, :] < r0[:, None] + wr)
        ridx = np.where(valid, kr[None, :] - q_rows[:, None] + NA_WIN_R - 1, N_BIAS_ROWS).astype(np.int32)
        for vi, v in enumerate(variants):
            if np.array_equal(v, ridx):
                block_variant.append(vi)
                break
        else:
            variants.append(ridx)
            block_variant.append(len(variants) - 1)
        block_start.append(ks)
    return key_rows, np.stack(variants), block_variant, block_start


def _na_bias_table(rpb, variant_rows):
    n_heads = rpb.shape[0]
    qc = np.arange(GRID_W)[:, None]
    kc = np.arange(GRID_W)[None, :]
    wstart = np.clip(qc - NA_WIN_C // 2, 0, GRID_W - NA_WIN_C)
    col_ok = (kc >= wstart) & (kc < wstart + NA_WIN_C)
    cidx = np.clip(kc - qc + NA_WIN_C - 1, 0, 2 * NA_WIN_C - 2)
    onehot = ((cidx[None] == np.arange(2 * NA_WIN_C - 1)[:, None, None]) & col_ok[None]).astype(np.float32)
    blocks = jnp.einsum("hrd,dqk->hrqk", rpb.astype(F32), jnp.asarray(onehot), precision=HIGHEST)
    blocks = jnp.where(col_ok[None, None], blocks, MASK_VALUE)
    blocks = jnp.concatenate([blocks, jnp.full((n_heads, 1, GRID_W, GRID_W), MASK_VALUE, F32)], axis=1)
    nvar, rb, key_rows = variant_rows.shape
    tab = blocks[:, variant_rows.reshape(-1)].reshape(n_heads, nvar, rb, key_rows, GRID_W, GRID_W)
    return jnp.transpose(tab, (0, 1, 2, 4, 3, 5)).reshape(n_heads, nvar, rb * GRID_W, key_rows * GRID_W)


def _na_kernel(q_ref, k_ref, v_ref, kc_ref, vc_ref, bias_ref, o_ref, *, block_variant, block_start, key_rows):
    qb = NA_ROWS_PER_BLOCK * GRID_W
    nk = key_rows * GRID_W
    lane = lax.broadcasted_iota(jnp.int32, (qb, LANES), 1)
    kc = kc_ref[0]
    vc = vc_ref[0]
    for kb, (var, ks) in enumerate(zip(block_variant, block_start)):
        q = q_ref[0, kb * qb:(kb + 1) * qb, :]
        k_win = k_ref[0, ks * GRID_W:ks * GRID_W + nk, :]
        v_win = v_ref[0, ks * GRID_W:ks * GRID_W + nk, :]
        outs = []
        for j in range(2):
            in_head = (lane >= j * NA_DH) & (lane < (j + 1) * NA_DH)
            qm = jnp.where(in_head, q, jnp.zeros_like(q))
            s_win = _dot_nt(qm, k_win) + bias_ref[j, var]
            s_ctx = _dot_nt(qm, kc)
            m = jnp.maximum(jnp.max(s_win, axis=-1, keepdims=True), jnp.max(s_ctx, axis=-1, keepdims=True))
            p_win = jnp.exp(s_win - m)
            p_ctx = jnp.exp(s_ctx - m)
            denom = jnp.sum(p_win, axis=-1, keepdims=True) + jnp.sum(p_ctx, axis=-1, keepdims=True)
            o = _dot(p_win.astype(BF16), v_win) + _dot(p_ctx.astype(BF16), vc)
            outs.append(o / denom)
        o_ref[0, kb * qb:(kb + 1) * qb, :] = jnp.where(lane < NA_DH, outs[0], outs[1]).astype(BF16)


def _neighbourhood_attention(nq, nk, nv, nk_c, nv_c, rpb):
    b, n, _ = nq.shape
    l = nk_c.shape[1]
    rows = n // GRID_W
    key_rows, variant_rows, block_variant, block_start = _na_geometry(rows)
    bias = _na_bias_table(rpb, variant_rows)
    nvar = variant_rows.shape[0]
    qb = NA_ROWS_PER_BLOCK * GRID_W
    nkeys = key_rows * GRID_W
    seq_spec = pl.BlockSpec((1, n, LANES), lambda h, i: (i, 0, h))
    ctx_spec = pl.BlockSpec((1, l, LANES), lambda h, i: (i, 0, h))
    kern = functools.partial(_na_kernel, block_variant=tuple(block_variant), block_start=tuple(block_start),
                             key_rows=key_rows)
    return pl.pallas_call(
        kern,
        out_shape=jax.ShapeDtypeStruct((b, n, HEAD_W), BF16),
        grid=(NA_HEADS // 2, b),
        in_specs=[seq_spec, seq_spec, seq_spec, ctx_spec, ctx_spec,
                  pl.BlockSpec((2, nvar, qb, nkeys), lambda h, i: (h, 0, 0, 0))],
        out_specs=seq_spec,
        compiler_params=_cparams("arbitrary", "arbitrary"),
        name="neighbourhood_attention",
    )(nq, nk, nv, nk_c, nv_c, bias)


def _layer_norm(z, g, b):
    mu = jnp.mean(z, axis=-1, keepdims=True)
    var = jnp.mean(jnp.square(z - mu), axis=-1, keepdims=True)
    return (z - mu) * lax.rsqrt(var + LN_EPS) * g + b


def _split_bf16(v):
    hi = v.astype(BF16)
    return hi, (v - hi.astype(F32)).astype(BF16)


def _route(t, wr_ref, br_ref):
    t_hi, t_lo = _split_bf16(t)
    w_hi, w_lo = _split_bf16(wr_ref[...])
    hi = _dot(t_hi, jnp.concatenate([w_hi, w_lo], axis=1))
    logits = hi[:, 0:LANES] + (_dot(t_lo, w_hi) + hi[:, LANES:2 * LANES]) + br_ref[...]
    lane = lax.broadcasted_iota(jnp.int32, logits.shape, 1).astype(F32)
    neg = -jnp.inf
    big = float(LANES)

    def first_max(vals):
        vmax = jnp.max(vals, axis=-1, keepdims=True)
        return vmax, jnp.min(jnp.where(vals == vmax, lane, big), axis=-1, keepdims=True)

    is_grp = lane < MOE_GROUPS
    g_max, grp = first_max(jnp.where(is_grp, logits, neg))
    g_sum = jnp.sum(jnp.where(is_grp, jnp.exp(logits - g_max), 0.0), axis=-1, keepdims=True)
    gate_g = 1.0 / g_sum
    lo = MOE_GROUPS + grp * MOE_PER_GROUP
    in_grp = (lane >= lo) & (lane < lo + MOE_PER_GROUP)
    le = jnp.where(in_grp, logits, neg)
    v1, i1 = first_max(le)
    v2, i2 = first_max(jnp.where(lane == i1, neg, le))
    e21 = jnp.exp(v2 - v1)
    w1 = gate_g / (1.0 + e21)
    w2 = gate_g * e21 / (1.0 + e21)
    return jnp.where(lane == 0, w1,
                     jnp.where(lane == 1, w2,
                               jnp.where(lane == 2, i1 - MOE_GROUPS,
                                         jnp.where(lane == 3, i2 - MOE_GROUPS, 0.0))))


def _post_norm_route(h, y, m_ref, ln_ref, wr_ref, br_ref, h_out, t_out, r_out):
    gate = m_ref[0, 2:3, :]
    shift = m_ref[0, 3:4, :]
    scale = m_ref[0, 4:5, :]
    h1 = _layer_norm(DEEPNORM_ALPHA * h + gate * y, ln_ref[0:1, :], ln_ref[1:2, :])
    t = h1 * (1.0 + scale) + shift
    h_out[0] = h1
    t_out[0] = t.astype(BF16)
    r_out[0] = _route(t, wr_ref, br_ref)


def _epilogue_specs(b, n, d, tm):
    in_specs = [
        pl.BlockSpec((1, N_MOD, d), lambda i, j: (i, 0, 0)),
        pl.BlockSpec((2, d), lambda i, j: (0, 0)),
        pl.BlockSpec((d, LANES), lambda i, j: (0, 0)),
        pl.BlockSpec((1, LANES), lambda i, j: (0, 0)),
    ]
    out_shape = [jax.ShapeDtypeStruct((b, n, d), F32), jax.ShapeDtypeStruct((b, n, d), BF16),
                 jax.ShapeDtypeStruct((b, n, LANES), F32)]
    out_specs = [pl.BlockSpec((1, tm, d), lambda i, j: (i, j, 0)),
                 pl.BlockSpec((1, tm, d), lambda i, j: (i, j, 0)),
                 pl.BlockSpec((1, tm, LANES), lambda i, j: (i, j, 0))]
    return in_specs, out_shape, out_specs


def _out_proj_kernel(x_ref, yr_ref, yn_ref, w_ref, m_ref, ln_ref, wr_ref, br_ref, h_out, t_out, r_out):
    y = _dot(yr_ref[0], w_ref[0:HEAD_W, :]) + _dot(yn_ref[0], w_ref[HEAD_W:2 * HEAD_W, :])
    _post_norm_route(x_ref[0], y, m_ref, ln_ref, wr_ref, br_ref, h_out, t_out, r_out)


def _out_proj(x, y_ret, y_na, w_out_bf, m, ln, router):
    b, n, d = x.shape
    tm = EPILOGUE_TILE
    ep_in, out_shape, out_specs = _epilogue_specs(b, n, d, tm)
    return pl.pallas_call(
        _out_proj_kernel,
        out_shape=out_shape,
        grid=(b, n // tm),
        in_specs=[
            pl.BlockSpec((1, tm, d), lambda i, j: (i, j, 0)),
            pl.BlockSpec((1, tm, HEAD_W), lambda i, j: (i, j, 0)),
            pl.BlockSpec((1, tm, HEAD_W), lambda i, j: (i, j, 0)),
            pl.BlockSpec((2 * HEAD_W, d), lambda i, j: (0, 0)),
        ] + ep_in,
        out_specs=out_specs,
        compiler_params=_cparams("arbitrary", "arbitrary"),
        name="out_proj",
    )(x, y_ret, y_na, w_out_bf, m, ln, *router)


POOL_HALO = max(POOL_SIZES) // 2
POOL_TILE = EPILOGUE_TILE


def _pool_kernel(x_ref, prev_ref, next_ref, pw_ref, ps_ref, m_ref, ln_ref, wr_ref, br_ref,
                 h_out, t_out, r_out, *, n):
    j = pl.program_id(1)
    nj = pl.num_programs(1)
    tm = x_ref.shape[1]
    halo = POOL_HALO
    shift = m_ref[0, 0:1, :]
    scale = m_ref[0, 1:2, :]
    x = x_ref[0]
    hm = x * (1.0 + scale) + shift
    prev = jnp.where(j > 0, prev_ref[0] * (1.0 + scale) + shift, 0.0)
    nxt = jnp.where(j < nj - 1, next_ref[0] * (1.0 + scale) + shift, 0.0)
    ext = jnp.concatenate([prev, hm, nxt], axis=0)
    pos = (j * tm + lax.broadcasted_iota(jnp.int32, (tm, 1), 0))
    ys = []
    for g, w in enumerate(POOL_SIZES):
        cols = slice(g * POOL_GROUP, (g + 1) * POOL_GROUP)
        s = ext[:, cols]
        span = 1
        while span < w:
            s = s[:s.shape[0] - span] + s[span:]
            span *= 2
        off = halo - w // 2
        win = s[off:off + tm]
        cnt = (jnp.minimum(pos + (w - w // 2), n) - jnp.maximum(pos - w // 2, 0)).astype(F32)
        z = (win / cnt - hm[:, cols]).astype(BF16)
        ys.append(_dot(z, pw_ref[g]))
    y = jnp.concatenate(ys, axis=-1) * ps_ref[...]
    _post_norm_route(x, y, m_ref, ln_ref, wr_ref, br_ref, h_out, t_out, r_out)


def _pool_mixer(h, pool_w_bf, pool_scale, m, ln, router):
    b, n, d = h.shape
    tm = POOL_TILE
    halo = POOL_HALO
    blocks_per_tile = tm // halo
    n_halo_blocks = n // halo
    ep_in, out_shape, out_specs = _epilogue_specs(b, n, d, tm)
    return pl.pallas_call(
        functools.partial(_pool_kernel, n=n),
        out_shape=out_shape,
        grid=(b, n // tm),
        in_specs=[
            pl.BlockSpec((1, tm, d), lambda i, j: (i, j, 0)),
            pl.BlockSpec((1, halo, d), lambda i, j: (i, jnp.maximum(j * blocks_per_tile - 1, 0), 0)),
            pl.BlockSpec((1, halo, d),
                         lambda i, j: (i, jnp.minimum((j + 1) * blocks_per_tile, n_halo_blocks - 1), 0)),
            pl.BlockSpec((len(POOL_SIZES), POOL_GROUP, POOL_GROUP), lambda i, j: (0, 0, 0)),
            pl.BlockSpec((1, d), lambda i, j: (0, 0)),
        ] + ep_in,
        out_specs=out_specs,
        compiler_params=_cparams("arbitrary", "arbitrary"),
        name="pool_mixer",
    )(h, h, h, pool_w_bf, pool_scale.reshape(1, d), m, ln, *router)


def _expert_kernel(te_ref, tv_ref, x_ref, wg_ref, wu_ref, wd_ref, buf_ref, o_ref, wg_s, wu_s, wd_s):
    del buf_ref
    j = pl.program_id(0)
    prev_e = te_ref[jnp.maximum(j - 1, 0)]

    @pl.when((j == 0) | (te_ref[j] != prev_e))
    def _():
        wg_s[...] = wg_ref[0, 0].astype(BF16)
        wu_s[...] = wu_ref[0, 0].astype(BF16)
        wd_s[...] = wd_ref[0, 0].astype(BF16)

    @pl.when(tv_ref[j] == 1)
    def _():
        x = x_ref[...]
        gate = _dot(x, wg_s[...])
        up = _dot(x, wu_s[...])
        act = (_silu(gate) * up).astype(BF16)
        o_ref[...] = _dot(act, wd_s[...]).astype(BF16)

    @pl.when(tv_ref[j] == 0)
    def _():
        o_ref[...] = jnp.zeros_like(o_ref)


def _experts(tile_expert, tile_valid, x_chunk, y_prev, w_gate, w_up, w_down, layer, chunk, n_chunks):
    pc, d = x_chunk.shape
    tm = MOE_TILE
    tiles = pc // tm
    hid = w_gate.shape[-1]
    in_specs = [
        pl.BlockSpec((tm, d), lambda j, te, tv: (j, 0)),
        pl.BlockSpec((1, 1, d, hid), lambda j, te, tv: (layer, te[j], 0, 0)),
        pl.BlockSpec((1, 1, d, hid), lambda j, te, tv: (layer, te[j], 0, 0)),
        pl.BlockSpec((1, 1, hid, d), lambda j, te, tv: (layer, te[j], 0, 0)),
    ]
    in_specs.append(pl.BlockSpec(memory_space=pl.ANY))
    args = [tile_expert, tile_valid, x_chunk, w_gate, w_up, w_down, y_prev]
    aliases = {len(args) - 1: 0}
    grid_spec = pltpu.PrefetchScalarGridSpec(
        num_scalar_prefetch=2,
        grid=(tiles,),
        in_specs=in_specs,
        out_specs=pl.BlockSpec((tm, d), lambda j, te, tv: (chunk * tiles + j, 0)),
        scratch_shapes=[pltpu.VMEM((d, hid), BF16), pltpu.VMEM((d, hid), BF16), pltpu.VMEM((hid, d), BF16)],
    )
    return pl.pallas_call(
        _expert_kernel,
        out_shape=jax.ShapeDtypeStruct((pc * n_chunks, d), BF16),
        grid_spec=grid_spec,
        input_output_aliases=aliases,
        compiler_params=_cparams("arbitrary"),
        name="experts",
    )(*args)


def _dispatch_plan(route, tm):
    tok = route.shape[0]
    n_assign = 2 * tok
    p = n_assign + MOE_EXPERTS * tm
    n_tiles = p // tm
    i32 = jnp.int32
    eid = jnp.transpose(route[:, 2:4]).astype(i32).reshape(n_assign)
    sorted_e, order = lax.sort((eid, jnp.arange(n_assign, dtype=i32)), num_keys=1)
    experts = jnp.arange(MOE_EXPERTS, dtype=i32)
    counts = jnp.sum((eid[:, None] == experts[None, :]).astype(i32), axis=0)
    padded = ((counts + tm - 1) // tm) * tm
    pad_end = jnp.cumsum(padded)
    pad_start = pad_end - padded
    start = jnp.cumsum(counts) - counts
    tile_row = jnp.arange(n_tiles, dtype=i32) * tm
    tile_expert = jnp.minimum(jnp.sum((pad_end[None, :] <= tile_row[:, None]).astype(i32), axis=1), MOE_EXPERTS - 1)
    tile_valid = (tile_row < pad_end[-1]).astype(i32)
    rank = (tile_row - pad_start[tile_expert])[:, None] + jnp.arange(tm, dtype=i32)[None, :]
    row_valid = (rank < counts[tile_expert][:, None]).reshape(p)
    src = jnp.clip(start[tile_expert][:, None] + rank, 0, n_assign - 1).reshape(p)
    src_assign = order.at[src].get(mode="promise_in_bounds")
    tok_of_pos = jnp.where(row_valid, src_assign, jnp.arange(p, dtype=i32)) % tok
    offs = pad_start - start
    dest_sorted = jnp.arange(n_assign, dtype=i32) + jnp.sum(
        jnp.where(sorted_e[:, None] == experts[None, :], offs[None, :], 0), axis=1)
    _, pos_of_assign = lax.sort((order, dest_sorted), num_keys=1)
    return tile_expert, tile_valid, tok_of_pos, pos_of_assign


def _n_chunks(b):
    return 4 if b % 4 == 0 else (2 if b % 2 == 0 else 1)


def _post_norm_kernel(h_ref, y0_ref, y1_ref, r_ref, m_ref, ln_ref, o_ref):
    gate = m_ref[0, 5:6, :]
    r = r_ref[0]
    y = r[:, 0:1] * y0_ref[...].astype(F32) + r[:, 1:2] * y1_ref[...].astype(F32)
    o_ref[0] = _layer_norm(DEEPNORM_ALPHA * h_ref[0] + gate * y, ln_ref[0:1, :], ln_ref[1:2, :])


def _post_norm(h, y2, route, m, ln, chunk, n_chunks):
    b, n, d = h.shape
    tm = ROW_TILE
    bc = b // n_chunks
    b0 = chunk * bc
    tiles_per_seq = n // tm
    tiles_per_k = bc * tiles_per_seq
    row_spec = pl.BlockSpec((1, tm, d), lambda i, j: (b0 + i, j, 0))
    return pl.pallas_call(
        _post_norm_kernel,
        out_shape=jax.ShapeDtypeStruct((b, n, d), F32),
        grid=(bc, tiles_per_seq),
        in_specs=[row_spec,
                  pl.BlockSpec((tm, d), lambda i, j: (i * tiles_per_seq + j, 0)),
                  pl.BlockSpec((tm, d), lambda i, j: (tiles_per_k + i * tiles_per_seq + j, 0)),
                  pl.BlockSpec((1, tm, LANES), lambda i, j: (b0 + i, j, 0)),
                  pl.BlockSpec((1, N_MOD, d), lambda i, j: (b0 + i, 0, 0)),
                  pl.BlockSpec((2, d), lambda i, j: (0, 0))],
        out_specs=row_spec,
        input_output_aliases={0: 0},
        compiler_params=_cparams("arbitrary", "arbitrary"),
        name="post_norm",
    )(h, y2, y2, route, m, ln)


def _moe_rows(tok):
    return 2 * tok + MOE_EXPERTS * MOE_TILE


def _moe_post_norm(h1, t, route, m, ln, w_gate, w_up, w_down, layer, row_buf):
    b, n, d = t.shape
    tok = b * n
    nch = _n_chunks(b)
    tile_expert, tile_valid, tok_of_pos, pos_of_assign = _dispatch_plan(route.reshape(tok, LANES), MOE_TILE)
    t2 = t.reshape(tok, d)
    tiles = tile_expert.shape[0] // nch
    pc = tiles * MOE_TILE
    y_sorted = row_buf
    for c in range(nch):
        x_c = t2.at[tok_of_pos[c * pc:(c + 1) * pc]].get(mode="promise_in_bounds")
        y_sorted = _experts(tile_expert[c * tiles:(c + 1) * tiles], tile_valid[c * tiles:(c + 1) * tiles],
                            x_c, y_sorted, w_gate, w_up, w_down, layer, c, nch)
    bc = b // nch
    pos3 = pos_of_assign.reshape(2, b, n)
    h = h1
    for c in range(nch):
        pos_c = pos3[:, c * bc:(c + 1) * bc].reshape(2 * bc * n)
        y2_c = y_sorted.at[pos_c].get(mode="promise_in_bounds")
        h = _post_norm(h, y2_c, route, m, ln, c, nch)
    return h, y_sorted


def _rope_tables(n):
    t = jnp.arange(n)
    rows = (t // GRID_W).astype(F32)
    cols = (t % GRID_W).astype(F32)
    n_freq = RET_DK // 4
    inv_freq = ROPE_BASE ** (-jnp.arange(n_freq, dtype=F32) / n_freq)
    ang = jnp.concatenate([rows[:, None] * inv_freq, cols[:, None] * inv_freq], axis=-1)
    cos, sin = jnp.cos(ang), jnp.sin(ang)
    cos2 = jnp.concatenate([cos, cos], axis=-1)
    sin2 = jnp.concatenate([-sin, sin], axis=-1)
    q_scale = RET_DK ** -0.5
    return cos2 * q_scale, sin2 * q_scale, cos2, sin2


def _router_params(w_r1, b_r1, w_r2, b_r2):
    d = w_r1.shape[0]
    w2 = jnp.transpose(w_r2, (1, 0, 2)).reshape(d, MOE_EXPERTS)
    pad = LANES - MOE_GROUPS - MOE_EXPERTS
    wr = jnp.concatenate([w_r1, w2, jnp.zeros((d, pad), F32)], axis=-1)
    br = jnp.concatenate([b_r1, b_r2.reshape(MOE_EXPERTS), jnp.zeros((pad,), F32)]).reshape(1, LANES)
    return wr, br


def kernel(x, c, ctx, c_ctx, w_mod, b_mod, ln_g, ln_b, ab_w_in, ab_w_out, ab_log_decay, ab_rpb, pool_w, pool_scale, moe_w_r1, moe_b_r1, moe_w_r2, moe_b_r2, moe_w_gate, moe_w_up, moe_w_down):
    b, n, d = x.shape
    cc = jnp.concatenate([c, c_ctx[None, :], jnp.zeros((MOD_ROWS - b - 1, d), F32)], axis=0)
    mod = _modulation(cc, w_mod, b_mod)
    h = x
    row_buf = None
    for i in range(DEPTH):
        j = i // 2
        m = mod[i, :b].reshape(b, N_MOD, d)
        ln1 = jnp.stack([ln_g[i, 0], ln_b[i, 0]])
        ln2 = jnp.stack([ln_g[i, 1], ln_b[i, 1]])
        router = _router_params(moe_w_r1[i], moe_b_r1[i], moe_w_r2[i], moe_b_r2[i])
        if i % 2 == 0:
            m_ctx = mod[i, b].reshape(1, N_MOD, d)
            w_in_bf = ab_w_in[j].astype(BF16)
            log_gamma2 = jnp.log1p(-jnp.exp(ab_log_decay[j].astype(F32)))
            rk, rv, nk, nv, rq, rg, nq, row_buf = _in_proj(h, m, w_in_bf, _rope_tables(n), _moe_rows(b * n))
            rk_c, rv_c, nk_c, nv_c = _ctx_proj(ctx, m_ctx, w_in_bf[:, :N_KV_GROUPS * HEAD_W])
            y_ret = _retention(log_gamma2, rq, rk, rv, rg, rk_c, rv_c)
            y_na = _neighbourhood_attention(nq, nk, nv, nk_c, nv_c, ab_rpb[j])
            h1, t, route = _out_proj(h, y_ret, y_na, ab_w_out[j].astype(BF16), m, ln1, router)
        else:
            h1, t, route = _pool_mixer(h, pool_w[j].astype(BF16), pool_scale[j], m, ln1, router)
        h, row_buf = _moe_post_norm(h1, t, route, m, ln2, moe_w_gate, moe_w_up, moe_w_down, i, row_buf)
    return h
```

```python
import functools

import numpy as np
import jax
import jax.numpy as jnp
from jax import lax
from jax.experimental import pallas as pl
from jax.experimental.pallas import tpu as pltpu

F32 = jnp.float32
BF16 = jnp.bfloat16
HIGHEST = lax.Precision.HIGHEST

D_MODEL = 1024
DEPTH = 2
GRID_W = 64
RET_HEADS = 4
RET_DK = 128
RET_CHUNK = 128
NA_HEADS = 8
NA_DH = 64
NA_WIN_R = 8
NA_WIN_C = 16
N_BIAS_ROWS = 2 * NA_WIN_R - 1
POOL_SIZES = (2, 4, 8, 16)
POOL_GROUP = D_MODEL // len(POOL_SIZES)
MOE_GROUPS = 4
MOE_PER_GROUP = 8
MOE_EXPERTS = MOE_GROUPS * MOE_PER_GROUP
MOE_HIDDEN = D_MODEL // 2
ROPE_BASE = 10000.0
LN_EPS = 1e-5
N_MOD = 6
DEEPNORM_ALPHA = (2 * DEPTH) ** 0.25
HEAD_W = 512
N_IN_GROUPS = 7
N_KV_GROUPS = 4
LOG2E = float(np.log2(np.e))
MASK_VALUE = -1e30

LANES = 128
VMEM_LIMIT = 56 * 1024 * 1024

NA_ROWS_PER_BLOCK = 4
ROW_TILE = 512
EPILOGUE_TILE = 1024
MOE_TILE = 512
MOD_ROWS = 24
MOD_COL_TILE = 1536


def _cparams(*sem):
    return pltpu.CompilerParams(dimension_semantics=sem, vmem_limit_bytes=VMEM_LIMIT)


def _silu(v):
    return v / (1.0 + jnp.exp(-v))


def _dot(a, b):
    return jnp.dot(a, b, preferred_element_type=F32)


def _dot_nt(a, b):
    return lax.dot_general(a, b, (((1,), (1,)), ((), ())), preferred_element_type=F32)


def _dot_tn(a, b):
    return lax.dot_general(a, b, (((0,), (0,)), ((), ())), preferred_element_type=F32)


def _mod_kernel(c_ref, w_ref, b_ref, o_ref):
    s = _silu(c_ref[...])
    o_ref[0] = jnp.dot(s, w_ref[0], precision=HIGHEST, preferred_element_type=F32) + b_ref[0]


def _modulation(cc, w_mod, b_mod):
    depth, d, n = w_mod.shape
    return pl.pallas_call(
        _mod_kernel,
        out_shape=jax.ShapeDtypeStruct((depth, MOD_ROWS, n), F32),
        grid=(depth, n // MOD_COL_TILE),
        in_specs=[
            pl.BlockSpec((MOD_ROWS, d), lambda i, j: (0, 0)),
            pl.BlockSpec((1, d, MOD_COL_TILE), lambda i, j: (i, 0, j)),
            pl.BlockSpec((1, 1, MOD_COL_TILE), lambda i, j: (i, 0, j)),
        ],
        out_specs=pl.BlockSpec((1, MOD_ROWS, MOD_COL_TILE), lambda i, j: (i, 0, j)),
        compiler_params=_cparams("arbitrary", "arbitrary"),
        name="modulation",
    )(cc, w_mod, b_mod.reshape(depth, 1, n))


def _rope(v, cos2, sin2):
    return v * cos2 + pltpu.roll(v, RET_DK // 2, axis=1) * sin2


def _in_proj_kernel(x_ref, m_ref, w_ref, cq_ref, sq_ref, ck_ref, sk_ref,
                    rk_ref, rv_ref, nk_ref, nv_ref, rq_ref, rg_ref, nq_ref, buf_ref):
    buf_ref[...] = jnp.zeros_like(buf_ref)
    shift = m_ref[0, 0:1, :]
    scale = m_ref[0, 1:2, :]
    hm = (x_ref[0] * (1.0 + scale) + shift).astype(BF16)
    outs = (rk_ref, rv_ref, nk_ref, nv_ref, rq_ref, rg_ref, nq_ref)
    for g, o_ref in enumerate(outs):
        p = _dot(hm, w_ref[:, g * HEAD_W:(g + 1) * HEAD_W])
        if g == 0 or g == 4:
            cos2 = (ck_ref if g == 0 else cq_ref)[...]
            sin2 = (sk_ref if g == 0 else sq_ref)[...]
            for hd in range(RET_HEADS):
                sl = slice(hd * RET_DK, (hd + 1) * RET_DK)
                o_ref[0, :, sl] = _rope(p[:, sl], cos2, sin2).astype(BF16)
        elif g == 6:
            o_ref[0] = (p * (NA_DH ** -0.5 * LOG2E)).astype(BF16)
        else:
            o_ref[0] = p.astype(BF16)


def _in_proj(x, m, w_in_bf, rope_tabs, buf_rows):
    b, n, d = x.shape
    tm = ROW_TILE
    tiles_per_seq = n // tm
    buf_tile = buf_rows // (b * tiles_per_seq)
    assert buf_tile * b * tiles_per_seq == buf_rows and buf_tile % 16 == 0
    tab_spec = pl.BlockSpec((tm, RET_DK), lambda i, j: (j, 0))
    out_spec = pl.BlockSpec((1, tm, HEAD_W), lambda i, j: (i, j, 0))
    return pl.pallas_call(
        _in_proj_kernel,
        out_shape=[jax.ShapeDtypeStruct((b, n, HEAD_W), BF16)] * N_IN_GROUPS
        + [jax.ShapeDtypeStruct((buf_rows, d), BF16)],
        grid=(b, tiles_per_seq),
        in_specs=[
            pl.BlockSpec((1, tm, d), lambda i, j: (i, j, 0)),
            pl.BlockSpec((1, N_MOD, d), lambda i, j: (i, 0, 0)),
            pl.BlockSpec((d, N_IN_GROUPS * HEAD_W), lambda i, j: (0, 0)),
            tab_spec, tab_spec, tab_spec, tab_spec,
        ],
        out_specs=[out_spec] * N_IN_GROUPS + [pl.BlockSpec((buf_tile, d), lambda i, j: (i * tiles_per_seq + j, 0))],
        compiler_params=_cparams("arbitrary", "arbitrary"),
        name="in_proj",
    )(x, m, w_in_bf, *rope_tabs)


def _ctx_proj_kernel(x_ref, m_ref, w_ref, rk_ref, rv_ref, nk_ref, nv_ref):
    shift = m_ref[0, 0:1, :]
    scale = m_ref[0, 1:2, :]
    hm = (x_ref[0] * (1.0 + scale) + shift).astype(BF16)
    for g, o_ref in enumerate((rk_ref, rv_ref, nk_ref, nv_ref)):
        o_ref[0] = _dot(hm, w_ref[:, g * HEAD_W:(g + 1) * HEAD_W]).astype(BF16)


def _ctx_proj(ctx, m_ctx, w_in_bf):
    b, l, d = ctx.shape
    out_spec = pl.BlockSpec((1, l, HEAD_W), lambda i: (i, 0, 0))
    return pl.pallas_call(
        _ctx_proj_kernel,
        out_shape=[jax.ShapeDtypeStruct((b, l, HEAD_W), BF16)] * N_KV_GROUPS,
        grid=(b,),
        in_specs=[
            pl.BlockSpec((1, l, d), lambda i: (i, 0, 0)),
            pl.BlockSpec((1, N_MOD, d), lambda i: (0, 0, 0)),
            pl.BlockSpec((d, N_KV_GROUPS * HEAD_W), lambda i: (0, 0)),
        ],
        out_specs=[out_spec] * N_KV_GROUPS,
        compiler_params=_cparams("arbitrary"),
        name="ctx_proj",
    )(ctx, m_ctx, w_in_bf)


def _retention_kernel(lg_ref, q_ref, k_ref, v_ref, g_ref, kc_ref, vc_ref, o_ref, u_ref, s_ref):
    hd = pl.program_id(1)
    n = q_ref.shape[1]
    c = RET_CHUNK
    dk = RET_DK
    nc = n // c
    l = kc_ref.shape[1]
    lgf = lg_ref[0, hd]
    lgb = lg_ref[1, hd]

    ii = lax.broadcasted_iota(jnp.int32, (c, c), 0).astype(F32)
    jj = lax.broadcasted_iota(jnp.int32, (c, c), 1).astype(F32)
    diff = ii - jj
    decay = (jnp.where(diff >= 0, jnp.exp(lgf * jnp.maximum(diff, 0.0)), 0.0)
             + jnp.where(diff <= 0, jnp.exp(lgb * jnp.maximum(-diff, 0.0)), 0.0))
    idx = lax.broadcasted_iota(jnp.int32, (c, 1), 0).astype(F32)
    q_dec_f = jnp.exp(lgf * (idx + 1.0))
    k_dec_f = jnp.exp(lgf * (c - 1.0 - idx))
    q_dec_b = jnp.exp(lgb * (c - idx))
    k_dec_b = jnp.exp(lgb * idx)
    ones = jnp.ones((1, dk), F32)
    chunk_dec_f = jnp.exp(ones * (lgf * c))
    chunk_dec_b = jnp.exp(ones * (lgb * c))

    pos = lax.broadcasted_iota(jnp.int32, (l, 1), 0).astype(F32)
    kc = kc_ref[0].astype(F32)
    vc = vc_ref[0]
    s_f0 = _dot_tn((kc * jnp.exp(lgf * (l - 1.0 - pos))).astype(BF16), vc)
    s_b0 = _dot_tn((kc * jnp.exp(lgb * pos)).astype(BF16), vc)

    def chunk_rows(i):
        return pl.ds(pl.multiple_of(i * c, c), c)

    def kv_step(i, carry):
        rows = chunk_rows(i)
        k_i = k_ref[0, rows, :].astype(F32)
        kk = jnp.concatenate([(k_i * k_dec_f).astype(BF16), (k_i * k_dec_b).astype(BF16)], axis=1)
        u_ref[i] = _dot_tn(kk, v_ref[0, rows, :])
        return carry

    lax.fori_loop(0, nc, kv_step, 0, unroll=True)

    def scan_f(i, s):
        s_ref[i, :, 0:dk] = s.astype(BF16)
        return s * chunk_dec_f + u_ref[i, 0:dk, :]

    lax.fori_loop(0, nc, scan_f, s_f0, unroll=True)

    def scan_b(t, s):
        i = nc - 1 - t
        s_ref[i, :, dk:2 * dk] = s.astype(BF16)
        return s * chunk_dec_b + u_ref[i, dk:2 * dk, :]

    lax.fori_loop(0, nc, scan_b, s_b0, unroll=True)

    def out_step(i, carry):
        rows = chunk_rows(i)
        q_i = q_ref[0, rows, :]
        v_i = v_ref[0, rows, :]
        att = (_dot_nt(q_i, k_ref[0, rows, :]) * decay).astype(BF16)
        inter = _dot(q_i, s_ref[i])
        o = _dot(att, v_i) + inter[:, 0:dk] * q_dec_f + inter[:, dk:2 * dk] * q_dec_b
        mu = jnp.mean(o, axis=-1, keepdims=True)
        var = jnp.mean(jnp.square(o - mu), axis=-1, keepdims=True)
        o_n = (o - mu) * lax.rsqrt(var + LN_EPS)
        gate = g_ref[0, rows, :].astype(F32)
        o_ref[0, rows, :] = (_silu(gate) * o_n).astype(BF16)
        return carry

    lax.fori_loop(0, nc, out_step, 0, unroll=True)


def _retention(log_gamma2, rq, rk, rv, rg, rk_c, rv_c):
    b, n, _ = rq.shape
    l = rk_c.shape[1]
    nc = n // RET_CHUNK
    seq_spec = pl.BlockSpec((1, n, RET_DK), lambda i, h: (i, 0, h))
    ctx_spec = pl.BlockSpec((1, l, RET_DK), lambda i, h: (i, 0, h))
    return pl.pallas_call(
        _retention_kernel,
        out_shape=jax.ShapeDtypeStruct((b, n, HEAD_W), BF16),
        grid=(b, RET_HEADS),
        in_specs=[pl.BlockSpec(memory_space=pltpu.SMEM),
                  seq_spec, seq_spec, seq_spec, seq_spec, ctx_spec, ctx_spec],
        out_specs=seq_spec,
        scratch_shapes=[pltpu.VMEM((nc, 2 * RET_DK, RET_DK), F32),
                        pltpu.VMEM((nc, RET_DK, 2 * RET_DK), BF16)],
        compiler_params=_cparams("arbitrary", "arbitrary"),
        name="retention",
    )(log_gamma2, rq, rk, rv, rg, rk_c, rv_c)


def _na_geometry(rows):
    rb = NA_ROWS_PER_BLOCK
    wr = min(NA_WIN_R, rows)
    key_rows = min(rows, rb + wr - 1)
    variants, block_variant, block_start = [], [], []
    for kb in range(rows // rb):
        q_rows = kb * rb + np.arange(rb)
        r0 = np.clip(q_rows - wr // 2, 0, rows - wr)
        ks = int(np.clip(r0.min(), 0, rows - key_rows))
        assert r0.max() + wr <= ks + key_rows
        kr = ks + np.arange(key_rows)
        valid = (kr[None, :] >= r0[:, None]) & (kr[None, :] < r0[:, None] + wr)
        ridx = np.where(valid, kr[None, :] - q_rows[:, None] + NA_WIN_R - 1, N_BIAS_ROWS).astype(np.int32)
        for vi, v in enumerate(variants):
            if np.array_equal(v, ridx):
                block_variant.append(vi)
                break
        else:
            variants.append(ridx)
            block_variant.append(len(variants) - 1)
        block_start.append(ks)
    return key_rows, np.stack(variants), block_variant, block_start


def _na_bias_table(rpb, variant_rows):
    n_heads = rpb.shape[0]
    qc = np.arange(GRID_W)[:, None]
    kc = np.arange(GRID_W)[None, :]
    wstart = np.clip(qc - NA_WIN_C // 2, 0, GRID_W - NA_WIN_C)
    col_ok = (kc >= wstart) & (kc < wstart + NA_WIN_C)
    cidx = np.clip(kc - qc + NA_WIN_C - 1, 0, 2 * NA_WIN_C - 2)
    onehot = ((cidx[None] == np.arange(2 * NA_WIN_C - 1)[:, None, None]) & col_ok[None]).astype(np.float32)
    blocks = jnp.einsum("hrd,dqk->hrqk", rpb.astype(F32), jnp.asarray(onehot), precision=HIGHEST)
    blocks = jnp.where(col_ok[None, None], blocks, MASK_VALUE)
    blocks = jnp.concatenate([blocks, jnp.full((n_heads, 1, GRID_W, GRID_W), MASK_VALUE, F32)], axis=1)
    nvar, rb, key_rows = variant_rows.shape
    tab = blocks[:, variant_rows.reshape(-1)].reshape(n_heads, nvar, rb, key_rows, GRID_W, GRID_W)
    return jnp.transpose(tab, (0, 1, 2, 4, 3, 5)).reshape(n_heads, nvar, rb * GRID_W, key_rows * GRID_W)


def _na_kernel(q_ref, k_ref, v_ref, kc_ref, vc_ref, bias_ref, o_ref, *, block_variant, block_start, key_rows):
    qb = NA_ROWS_PER_BLOCK * GRID_W
    nk = key_rows * GRID_W
    first_head = lax.broadcasted_iota(jnp.int32, (qb, LANES), 1) < NA_DH
    kc = kc_ref[0]
    vc = vc_ref[0]
    for kb, (var, ks) in enumerate(zip(block_variant, block_start)):
        q = q_ref[0, kb * qb:(kb + 1) * qb, :]
        k_win = k_ref[0, ks * GRID_W:ks * GRID_W + nk, :]
        v_win = v_ref[0, ks * GRID_W:ks * GRID_W + nk, :]
        zero = jnp.zeros_like(q)
        q2 = jnp.concatenate([jnp.where(first_head, q, zero), jnp.where(first_head, zero, q)], axis=0)
        s_win = _dot_nt(q2, k_win) + jnp.concatenate([bias_ref[0, var], bias_ref[1, var]], axis=0)
        s_ctx = _dot_nt(q2, kc)
        m = jnp.maximum(jnp.max(s_win, axis=-1, keepdims=True), jnp.max(s_ctx, axis=-1, keepdims=True))
        p_win = jnp.exp2(s_win - m)
        p_ctx = jnp.exp2(s_ctx - m)
        denom = jnp.sum(p_win, axis=-1, keepdims=True) + jnp.sum(p_ctx, axis=-1, keepdims=True)
        o = (_dot(p_win.astype(BF16), v_win) + _dot(p_ctx.astype(BF16), vc)) / denom
        o_ref[0, kb * qb:(kb + 1) * qb, :] = jnp.where(first_head, o[0:qb], o[qb:2 * qb]).astype(BF16)


def _neighbourhood_attention(nq, nk, nv, nk_c, nv_c, rpb):
    b, n, _ = nq.shape
    l = nk_c.shape[1]
    rows = n // GRID_W
    key_rows, variant_rows, block_variant, block_start = _na_geometry(rows)
    bias = _na_bias_table(rpb, variant_rows) * LOG2E
    nvar = variant_rows.shape[0]
    qb = NA_ROWS_PER_BLOCK * GRID_W
    nkeys = key_rows * GRID_W
    seq_spec = pl.BlockSpec((1, n, LANES), lambda h, i: (i, 0, h))
    ctx_spec = pl.BlockSpec((1, l, LANES), lambda h, i: (i, 0, h))
    kern = functools.partial(_na_kernel, block_variant=tuple(block_variant), block_start=tuple(block_start),
                             key_rows=key_rows)
    return pl.pallas_call(
        kern,
        out_shape=jax.ShapeDtypeStruct((b, n, HEAD_W), BF16),
        grid=(NA_HEADS // 2, b),
        in_specs=[seq_spec, seq_spec, seq_spec, ctx_spec, ctx_spec,
                  pl.BlockSpec((2, nvar, qb, nkeys), lambda h, i: (h, 0, 0, 0))],
        out_specs=seq_spec,
        compiler_params=_cparams("arbitrary", "arbitrary"),
        name="neighbourhood_attention",
    )(nq, nk, nv, nk_c, nv_c, bias)


def _layer_norm(z, g, b):
    mu = jnp.mean(z, axis=-1, keepdims=True)
    var = jnp.mean(jnp.square(z - mu), axis=-1, keepdims=True)
    return (z - mu) * lax.rsqrt(var + LN_EPS) * g + b


def _split_bf16(v):
    hi = v.astype(BF16)
    return hi, (v - hi.astype(F32)).astype(BF16)


def _route(t, wr_ref, br_ref):
    t_hi, t_lo = _split_bf16(t)
    w_hi, w_lo = _split_bf16(wr_ref[...])
    hi = _dot(t_hi, jnp.concatenate([w_hi, w_lo], axis=1))
    logits = hi[:, 0:LANES] + (_dot(t_lo, w_hi) + hi[:, LANES:2 * LANES]) + br_ref[...]
    lane = lax.broadcasted_iota(jnp.int32, logits.shape, 1).astype(F32)
    neg = -jnp.inf
    big = float(LANES)

    def first_max(vals):
        vmax = jnp.max(vals, axis=-1, keepdims=True)
        return vmax, jnp.min(jnp.where(vals == vmax, lane, big), axis=-1, keepdims=True)

    is_grp = lane < MOE_GROUPS
    g_max, grp = first_max(jnp.where(is_grp, logits, neg))
    g_sum = jnp.sum(jnp.where(is_grp, jnp.exp(logits - g_max), 0.0), axis=-1, keepdims=True)
    gate_g = 1.0 / g_sum
    lo = MOE_GROUPS + grp * MOE_PER_GROUP
    in_grp = (lane >= lo) & (lane < lo + MOE_PER_GROUP)
    le = jnp.where(in_grp, logits, neg)
    v1, i1 = first_max(le)
    v2, i2 = first_max(jnp.where(lane == i1, neg, le))
    e21 = jnp.exp(v2 - v1)
    w1 = gate_g / (1.0 + e21)
    w2 = gate_g * e21 / (1.0 + e21)
    return jnp.where(lane == 0, w1,
                     jnp.where(lane == 1, w2,
                               jnp.where(lane == 2, i1 - MOE_GROUPS,
                                         jnp.where(lane == 3, i2 - MOE_GROUPS, 0.0))))


def _post_norm_route(h, y, m_ref, ln_ref, wr_ref, br_ref, h_out, t_out, r_out):
    gate = m_ref[0, 2:3, :]
    shift = m_ref[0, 3:4, :]
    scale = m_ref[0, 4:5, :]
    h1 = _layer_norm(DEEPNORM_ALPHA * h + gate * y, ln_ref[0:1, :], ln_ref[1:2, :])
    t = h1 * (1.0 + scale) + shift
    h_out[0] = h1
    t_out[0] = t.astype(BF16)
    r_out[0] = _route(t, wr_ref, br_ref)


def _epilogue_specs(b, n, d, tm):
    in_specs = [
        pl.BlockSpec((1, N_MOD, d), lambda i, j: (i, 0, 0)),
        pl.BlockSpec((2, d), lambda i, j: (0, 0)),
        pl.BlockSpec((d, LANES), lambda i, j: (0, 0)),
        pl.BlockSpec((1, LANES), lambda i, j: (0, 0)),
    ]
    out_shape = [jax.ShapeDtypeStruct((b, n, d), F32), jax.ShapeDtypeStruct((b, n, d), BF16),
                 jax.ShapeDtypeStruct((b, n, LANES), F32)]
    out_specs = [pl.BlockSpec((1, tm, d), lambda i, j: (i, j, 0)),
                 pl.BlockSpec((1, tm, d), lambda i, j: (i, j, 0)),
                 pl.BlockSpec((1, tm, LANES), lambda i, j: (i, j, 0))]
    return in_specs, out_shape, out_specs


def _out_proj_kernel(x_ref, yr_ref, yn_ref, w_ref, m_ref, ln_ref, wr_ref, br_ref, h_out, t_out, r_out):
    y = _dot(yr_ref[0], w_ref[0:HEAD_W, :]) + _dot(yn_ref[0], w_ref[HEAD_W:2 * HEAD_W, :])
    _post_norm_route(x_ref[0], y, m_ref, ln_ref, wr_ref, br_ref, h_out, t_out, r_out)


def _out_proj(x, y_ret, y_na, w_out_bf, m, ln, router):
    b, n, d = x.shape
    tm = EPILOGUE_TILE
    ep_in, out_shape, out_specs = _epilogue_specs(b, n, d, tm)
    return pl.pallas_call(
        _out_proj_kernel,
        out_shape=out_shape,
        grid=(b, n // tm),
        in_specs=[
            pl.BlockSpec((1, tm, d), lambda i, j: (i, j, 0)),
            pl.BlockSpec((1, tm, HEAD_W), lambda i, j: (i, j, 0)),
            pl.BlockSpec((1, tm, HEAD_W), lambda i, j: (i, j, 0)),
            pl.BlockSpec((2 * HEAD_W, d), lambda i, j: (0, 0)),
        ] + ep_in,
        out_specs=out_specs,
        compiler_params=_cparams("arbitrary", "arbitrary"),
        name="out_proj",
    )(x, y_ret, y_na, w_out_bf, m, ln, *router)


POOL_HALO = max(POOL_SIZES) // 2
POOL_TILE = EPILOGUE_TILE


def _pool_kernel(x_ref, prev_ref, next_ref, pw_ref, ps_ref, m_ref, ln_ref, wr_ref, br_ref,
                 h_out, t_out, r_out, *, n):
    j = pl.program_id(1)
    nj = pl.num_programs(1)
    tm = x_ref.shape[1]
    halo = POOL_HALO
    shift = m_ref[0, 0:1, :]
    scale = m_ref[0, 1:2, :]
    x = x_ref[0]
    hm = x * (1.0 + scale) + shift
    prev = jnp.where(j > 0, prev_ref[0] * (1.0 + scale) + shift, 0.0)
    nxt = jnp.where(j < nj - 1, next_ref[0] * (1.0 + scale) + shift, 0.0)
    ext = jnp.concatenate([prev, hm, nxt], axis=0)
    pos = (j * tm + lax.broadcasted_iota(jnp.int32, (tm, 1), 0))
    ys = []
    for g, w in enumerate(POOL_SIZES):
        cols = slice(g * POOL_GROUP, (g + 1) * POOL_GROUP)
        s = ext[:, cols]
        span = 1
        while span < w:
            s = s[:s.shape[0] - span] + s[span:]
            span *= 2
        off = halo - w // 2
        win = s[off:off + tm]
        cnt = (jnp.minimum(pos + (w - w // 2), n) - jnp.maximum(pos - w // 2, 0)).astype(F32)
        z = (win / cnt - hm[:, cols]).astype(BF16)
        ys.append(_dot(z, pw_ref[g]))
    y = jnp.concatenate(ys, axis=-1) * ps_ref[...]
    _post_norm_route(x, y, m_ref, ln_ref, wr_ref, br_ref, h_out, t_out, r_out)


def _pool_mixer(h, pool_w_bf, pool_scale, m, ln, router):
    b, n, d = h.shape
    tm = POOL_TILE
    halo = POOL_HALO
    blocks_per_tile = tm // halo
    n_halo_blocks = n // halo
    ep_in, out_shape, out_specs = _epilogue_specs(b, n, d, tm)
    return pl.pallas_call(
        functools.partial(_pool_kernel, n=n),
        out_shape=out_shape,
        grid=(b, n // tm),
        in_specs=[
            pl.BlockSpec((1, tm, d), lambda i, j: (i, j, 0)),
            pl.BlockSpec((1, halo, d), lambda i, j: (i, jnp.maximum(j * blocks_per_tile - 1, 0), 0)),
            pl.BlockSpec((1, halo, d),
                         lambda i, j: (i, jnp.minimum((j + 1) * blocks_per_tile, n_halo_blocks - 1), 0)),
            pl.BlockSpec((len(POOL_SIZES), POOL_GROUP, POOL_GROUP), lambda i, j: (0, 0, 0)),
            pl.BlockSpec((1, d), lambda i, j: (0, 0)),
        ] + ep_in,
        out_specs=out_specs,
        compiler_params=_cparams("arbitrary", "arbitrary"),
        name="pool_mixer",
    )(h, h, h, pool_w_bf, pool_scale.reshape(1, d), m, ln, *router)


def _expert_kernel(te_ref, tv_ref, x_ref, wg_ref, wu_ref, wd_ref, buf_ref, o_ref, wg_s, wu_s, wd_s):
    del buf_ref
    j = pl.program_id(0)
    prev_e = te_ref[jnp.maximum(j - 1, 0)]

    @pl.when((j == 0) | (te_ref[j] != prev_e))
    def _():
        wg_s[...] = wg_ref[0, 0].astype(BF16)
        wu_s[...] = wu_ref[0, 0].astype(BF16)
        wd_s[...] = wd_ref[0, 0].astype(BF16)

    @pl.when(tv_ref[j] == 1)
    def _():
        x = x_ref[...]
        gate = _dot(x, wg_s[...])
        up = _dot(x, wu_s[...])
        act = (_silu(gate) * up).astype(BF16)
        o_ref[...] = _dot(act, wd_s[...]).astype(BF16)

    @pl.when(tv_ref[j] == 0)
    def _():
        o_ref[...] = jnp.zeros_like(o_ref)


def _experts(tile_expert, tile_valid, x_chunk, y_prev, w_gate, w_up, w_down, layer, chunk, n_chunks):
    pc, d = x_chunk.shape
    tm = MOE_TILE
    tiles = pc // tm
    hid = w_gate.shape[-1]
    in_specs = [
        pl.BlockSpec((tm, d), lambda j, te, tv: (j, 0)),
        pl.BlockSpec((1, 1, d, hid), lambda j, te, tv: (layer, te[j], 0, 0)),
        pl.BlockSpec((1, 1, d, hid), lambda j, te, tv: (layer, te[j], 0, 0)),
        pl.BlockSpec((1, 1, hid, d), lambda j, te, tv: (layer, te[j], 0, 0)),
    ]
    in_specs.append(pl.BlockSpec(memory_space=pl.ANY))
    args = [tile_expert, tile_valid, x_chunk, w_gate, w_up, w_down, y_prev]
    aliases = {len(args) - 1: 0}
    grid_spec = pltpu.PrefetchScalarGridSpec(
        num_scalar_prefetch=2,
        grid=(tiles,),
        in_specs=in_specs,
        out_specs=pl.BlockSpec((tm, d), lambda j, te, tv: (chunk * tiles + j, 0)),
        scratch_shapes=[pltpu.VMEM((d, hid), BF16), pltpu.VMEM((d, hid), BF16), pltpu.VMEM((hid, d), BF16)],
    )
    return pl.pallas_call(
        _expert_kernel,
        out_shape=jax.ShapeDtypeStruct((pc * n_chunks, d), BF16),
        grid_spec=grid_spec,
        input_output_aliases=aliases,
        compiler_params=_cparams("arbitrary"),
        name="experts",
    )(*args)


ASSIGN_BITS = 16


def _dispatch_plan(route, tm):
    tok = route.shape[0]
    n_assign = 2 * tok
    n_pad = MOE_EXPERTS * tm
    p = n_assign + n_pad
    n_tiles = p // tm
    assert n_assign <= 1 << ASSIGN_BITS
    i32 = jnp.int32
    low_mask = (1 << ASSIGN_BITS) - 1
    eid = jnp.transpose(route[:, 2:4]).astype(i32).reshape(n_assign)
    experts = jnp.arange(MOE_EXPERTS, dtype=i32)
    counts = jnp.sum((eid[:, None] == experts[None, :]).astype(i32), axis=0)
    padded = ((counts + tm - 1) // tm) * tm
    pad_end = jnp.cumsum(padded)
    tile_row = jnp.arange(n_tiles, dtype=i32) * tm
    tile_expert = jnp.minimum(jnp.sum((pad_end[None, :] <= tile_row[:, None]).astype(i32), axis=1), MOE_EXPERTS - 1)
    tile_valid = (tile_row < pad_end[-1]).astype(i32)
    need_end = jnp.cumsum(padded - counts)
    pad_id = jnp.arange(n_pad, dtype=i32)
    pad_expert = jnp.sum((need_end[None, :] <= pad_id[:, None]).astype(i32), axis=1)
    packed = jnp.concatenate([((2 * eid) << ASSIGN_BITS) | jnp.arange(n_assign, dtype=i32),
                              ((2 * pad_expert + 1) << ASSIGN_BITS) | (pad_id & low_mask)])
    packed = lax.sort(packed)
    entry = packed & low_mask
    is_real = ((packed >> ASSIGN_BITS) & 1) == 0
    tok_of_pos = entry % tok
    _, pos = lax.sort((jnp.where(is_real, entry, n_assign), jnp.arange(p, dtype=i32)), num_keys=1)
    return tile_expert, tile_valid, tok_of_pos, pos[:n_assign]


def _n_chunks(b):
    return 4 if b % 4 == 0 else (2 if b % 2 == 0 else 1)


def _post_norm_kernel(h_ref, y0_ref, y1_ref, r_ref, m_ref, ln_ref, o_ref):
    gate = m_ref[0, 5:6, :]
    r = r_ref[0]
    y = r[:, 0:1] * y0_ref[...].astype(F32) + r[:, 1:2] * y1_ref[...].astype(F32)
    o_ref[0] = _layer_norm(DEEPNORM_ALPHA * h_ref[0] + gate * y, ln_ref[0:1, :], ln_ref[1:2, :])


def _post_norm(h, y2, route, m, ln, chunk, n_chunks):
    b, n, d = h.shape
    tm = ROW_TILE
    bc = b // n_chunks
    b0 = chunk * bc
    tiles_per_seq = n // tm
    tiles_per_k = bc * tiles_per_seq
    row_spec = pl.BlockSpec((1, tm, d), lambda i, j: (b0 + i, j, 0))
    return pl.pallas_call(
        _post_norm_kernel,
        out_shape=jax.ShapeDtypeStruct((b, n, d), F32),
        grid=(bc, tiles_per_seq),
        in_specs=[row_spec,
                  pl.BlockSpec((tm, d), lambda i, j: (i * tiles_per_seq + j, 0)),
                  pl.BlockSpec((tm, d), lambda i, j: (tiles_per_k + i * tiles_per_seq + j, 0)),
                  pl.BlockSpec((1, tm, LANES), lambda i, j: (b0 + i, j, 0)),
                  pl.BlockSpec((1, N_MOD, d), lambda i, j: (b0 + i, 0, 0)),
                  pl.BlockSpec((2, d), lambda i, j: (0, 0))],
        out_specs=row_spec,
        input_output_aliases={0: 0},
        compiler_params=_cparams("arbitrary", "arbitrary"),
        name="post_norm",
    )(h, y2, y2, route, m, ln)


def _moe_rows(tok):
    return 2 * tok + MOE_EXPERTS * MOE_TILE


def _moe_post_norm(h1, t, route, m, ln, w_gate, w_up, w_down, layer, row_buf):
    b, n, d = t.shape
    tok = b * n
    nch = _n_chunks(b)
    tile_expert, tile_valid, tok_of_pos, pos_of_assign = _dispatch_plan(route.reshape(tok, LANES), MOE_TILE)
    t2 = t.reshape(tok, d)
    tiles = tile_expert.shape[0] // nch
    pc = tiles * MOE_TILE
    y_sorted = row_buf
    for c in range(nch):
        x_c = t2.at[tok_of_pos[c * pc:(c + 1) * pc]].get(mode="promise_in_bounds")
        y_sorted = _experts(tile_expert[c * tiles:(c + 1) * tiles], tile_valid[c * tiles:(c + 1) * tiles],
                            x_c, y_sorted, w_gate, w_up, w_down, layer, c, nch)
    bc = b // nch
    pos3 = pos_of_assign.reshape(2, b, n)
    h = h1
    for c in range(nch):
        pos_c = pos3[:, c * bc:(c + 1) * bc].reshape(2 * bc * n)
        y2_c = y_sorted.at[pos_c].get(mode="promise_in_bounds")
        h = _post_norm(h, y2_c, route, m, ln, c, nch)
    return h, y_sorted


def _rope_tables(n):
    t = jnp.arange(n)
    rows = (t // GRID_W).astype(F32)
    cols = (t % GRID_W).astype(F32)
    n_freq = RET_DK // 4
    inv_freq = ROPE_BASE ** (-jnp.arange(n_freq, dtype=F32) / n_freq)
    ang = jnp.concatenate([rows[:, None] * inv_freq, cols[:, None] * inv_freq], axis=-1)
    cos, sin = jnp.cos(ang), jnp.sin(ang)
    cos2 = jnp.concatenate([cos, cos], axis=-1)
    sin2 = jnp.concatenate([-sin, sin], axis=-1)
    q_scale = RET_DK ** -0.5
    return cos2 * q_scale, sin2 * q_scale, cos2, sin2


def _router_params(w_r1, b_r1, w_r2, b_r2):
    d = w_r1.shape[0]
    w2 = jnp.transpose(w_r2, (1, 0, 2)).reshape(d, MOE_EXPERTS)
    pad = LANES - MOE_GROUPS - MOE_EXPERTS
    wr = jnp.concatenate([w_r1, w2, jnp.zeros((d, pad), F32)], axis=-1)
    br = jnp.concatenate([b_r1, b_r2.reshape(MOE_EXPERTS), jnp.zeros((pad,), F32)]).reshape(1, LANES)
    return wr, br


def kernel(x, c, ctx, c_ctx, w_mod, b_mod, ln_g, ln_b, ab_w_in, ab_w_out, ab_log_decay, ab_rpb, pool_w, pool_scale, moe_w_r1, moe_b_r1, moe_w_r2, moe_b_r2, moe_w_gate, moe_w_up, moe_w_down):
    b, n, d = x.shape
    cc = jnp.concatenate([c, c_ctx[None, :], jnp.zeros((MOD_ROWS - b - 1, d), F32)], axis=0)
    mod = _modulation(cc, w_mod, b_mod)
    h = x
    row_buf = None
    for i in range(DEPTH):
        j = i // 2
        m = mod[i, :b].reshape(b, N_MOD, d)
        ln1 = jnp.stack([ln_g[i, 0], ln_b[i, 0]])
        ln2 = jnp.stack([ln_g[i, 1], ln_b[i, 1]])
        router = _router_params(moe_w_r1[i], moe_b_r1[i], moe_w_r2[i], moe_b_r2[i])
        if i % 2 == 0:
            m_ctx = mod[i, b].reshape(1, N_MOD, d)
            w_in_bf = ab_w_in[j].astype(BF16)
            log_gamma2 = jnp.log1p(-jnp.exp(ab_log_decay[j].astype(F32)))
            rk, rv, nk, nv, rq, rg, nq, row_buf = _in_proj(h, m, w_in_bf, _rope_tables(n), _moe_rows(b * n))
            rk_c, rv_c, nk_c, nv_c = _ctx_proj(ctx, m_ctx, w_in_bf[:, :N_KV_GROUPS * HEAD_W])
            y_ret = _retention(log_gamma2, rq, rk, rv, rg, rk_c, rv_c)
            y_na = _neighbourhood_attention(nq, nk, nv, nk_c, nv_c, ab_rpb[j])
            h1, t, route = _out_proj(h, y_ret, y_na, ab_w_out[j].astype(BF16), m, ln1, router)
        else:
            h1, t, route = _pool_mixer(h, pool_w[j].astype(BF16), pool_scale[j], m, ln1, router)
        h, row_buf = _moe_post_norm(h1, t, route, m, ln2, moe_w_gate, moe_w_up, moe_w_down, i, row_buf)
    return h
```

```python
import functools

import numpy as np
import jax
import jax.numpy as jnp
from jax import lax
from jax.experimental import pallas as pl
from jax.experimental.pallas import tpu as pltpu

F32 = jnp.float32
BF16 = jnp.bfloat16
HIGHEST = lax.Precision.HIGHEST

D_MODEL = 1024
DEPTH = 2
GRID_W = 64
RET_HEADS = 4
RET_DK = 128
RET_CHUNK = 128
NA_HEADS = 8
NA_DH = 64
NA_WIN_R = 8
NA_WIN_C = 16
N_BIAS_ROWS = 2 * NA_WIN_R - 1
POOL_SIZES = (2, 4, 8, 16)
POOL_GROUP = D_MODEL // len(POOL_SIZES)
MOE_GROUPS = 4
MOE_PER_GROUP = 8
MOE_EXPERTS = MOE_GROUPS * MOE_PER_GROUP
MOE_HIDDEN = D_MODEL // 2
ROPE_BASE = 10000.0
LN_EPS = 1e-5
N_MOD = 6
DEEPNORM_ALPHA = (2 * DEPTH) ** 0.25
HEAD_W = 512
N_IN_GROUPS = 7
N_KV_GROUPS = 4
LOG2E = float(np.log2(np.e))
MASK_VALUE = -1e30

LANES = 128
VMEM_LIMIT = 56 * 1024 * 1024

NA_ROWS_PER_BLOCK = 4
ROW_TILE = 512
EPILOGUE_TILE = 1024
MOE_TILE = 512
MOD_ROWS = 24
MOD_COL_TILE = 1536


def _cparams(*sem):
    return pltpu.CompilerParams(dimension_semantics=sem, vmem_limit_bytes=VMEM_LIMIT)


def _silu(v):
    return v / (1.0 + jnp.exp(-v))


def _dot(a, b):
    return jnp.dot(a, b, preferred_element_type=F32)


def _dot_nt(a, b):
    return lax.dot_general(a, b, (((1,), (1,)), ((), ())), preferred_element_type=F32)


def _dot_tn(a, b):
    return lax.dot_general(a, b, (((0,), (0,)), ((), ())), preferred_element_type=F32)


def _mod_kernel(c_ref, w_ref, b_ref, o_ref):
    s = _silu(c_ref[...])
    o_ref[0] = jnp.dot(s, w_ref[0], precision=HIGHEST, preferred_element_type=F32) + b_ref[0]


def _modulation(cc, w_mod, b_mod):
    depth, d, n = w_mod.shape
    return pl.pallas_call(
        _mod_kernel,
        out_shape=jax.ShapeDtypeStruct((depth, MOD_ROWS, n), F32),
        grid=(depth, n // MOD_COL_TILE),
        in_specs=[
            pl.BlockSpec((MOD_ROWS, d), lambda i, j: (0, 0)),
            pl.BlockSpec((1, d, MOD_COL_TILE), lambda i, j: (i, 0, j)),
            pl.BlockSpec((1, 1, MOD_COL_TILE), lambda i, j: (i, 0, j)),
        ],
        out_specs=pl.BlockSpec((1, MOD_ROWS, MOD_COL_TILE), lambda i, j: (i, 0, j)),
        compiler_params=_cparams("arbitrary", "arbitrary"),
        name="modulation",
    )(cc, w_mod, b_mod.reshape(depth, 1, n))


def _rope(v, cos2, sin2):
    return v * cos2 + pltpu.roll(v, RET_DK // 2, axis=1) * sin2


def _in_proj_kernel(x_ref, m_ref, w_ref, cq_ref, sq_ref, ck_ref, sk_ref,
                    rk_ref, rv_ref, nk_ref, nv_ref, rq_ref, rg_ref, nq_ref, buf_ref):
    buf_ref[...] = jnp.zeros_like(buf_ref)
    shift = m_ref[0, 0:1, :]
    scale = m_ref[0, 1:2, :]
    hm = (x_ref[0] * (1.0 + scale) + shift).astype(BF16)
    outs = (rk_ref, rv_ref, nk_ref, nv_ref, rq_ref, rg_ref, nq_ref)
    for g, o_ref in enumerate(outs):
        p = _dot(hm, w_ref[:, g * HEAD_W:(g + 1) * HEAD_W])
        if g == 0 or g == 4:
            cos2 = (ck_ref if g == 0 else cq_ref)[...]
            sin2 = (sk_ref if g == 0 else sq_ref)[...]
            for hd in range(RET_HEADS):
                sl = slice(hd * RET_DK, (hd + 1) * RET_DK)
                o_ref[0, :, sl] = _rope(p[:, sl], cos2, sin2).astype(BF16)
        elif g == 6:
            o_ref[0] = (p * (NA_DH ** -0.5 * LOG2E)).astype(BF16)
        else:
            o_ref[0] = p.astype(BF16)


def _in_proj(x, m, w_in_bf, rope_tabs, buf_rows):
    b, n, d = x.shape
    tm = ROW_TILE
    tiles_per_seq = n // tm
    buf_tile = buf_rows // (b * tiles_per_seq)
    assert buf_tile * b * tiles_per_seq == buf_rows and buf_tile % 16 == 0
    tab_spec = pl.BlockSpec((tm, RET_DK), lambda i, j: (j, 0))
    out_spec = pl.BlockSpec((1, tm, HEAD_W), lambda i, j: (i, j, 0))
    return pl.pallas_call(
        _in_proj_kernel,
        out_shape=[jax.ShapeDtypeStruct((b, n, HEAD_W), BF16)] * N_IN_GROUPS
        + [jax.ShapeDtypeStruct((buf_rows, d), BF16)],
        grid=(b, tiles_per_seq),
        in_specs=[
            pl.BlockSpec((1, tm, d), lambda i, j: (i, j, 0)),
            pl.BlockSpec((1, N_MOD, d), lambda i, j: (i, 0, 0)),
            pl.BlockSpec((d, N_IN_GROUPS * HEAD_W), lambda i, j: (0, 0)),
            tab_spec, tab_spec, tab_spec, tab_spec,
        ],
        out_specs=[out_spec] * N_IN_GROUPS + [pl.BlockSpec((buf_tile, d), lambda i, j: (i * tiles_per_seq + j, 0))],
        compiler_params=_cparams("arbitrary", "arbitrary"),
        name="in_proj",
    )(x, m, w_in_bf, *rope_tabs)


def _ctx_proj_kernel(x_ref, m_ref, w_ref, rk_ref, rv_ref, nk_ref, nv_ref):
    shift = m_ref[0, 0:1, :]
    scale = m_ref[0, 1:2, :]
    hm = (x_ref[0] * (1.0 + scale) + shift).astype(BF16)
    for g, o_ref in enumerate((rk_ref, rv_ref, nk_ref, nv_ref)):
        o_ref[0] = _dot(hm, w_ref[:, g * HEAD_W:(g + 1) * HEAD_W]).astype(BF16)


def _ctx_proj(ctx, m_ctx, w_in_bf):
    b, l, d = ctx.shape
    out_spec = pl.BlockSpec((1, l, HEAD_W), lambda i: (i, 0, 0))
    return pl.pallas_call(
        _ctx_proj_kernel,
        out_shape=[jax.ShapeDtypeStruct((b, l, HEAD_W), BF16)] * N_KV_GROUPS,
        grid=(b,),
        in_specs=[
            pl.BlockSpec((1, l, d), lambda i: (i, 0, 0)),
            pl.BlockSpec((1, N_MOD, d), lambda i: (0, 0, 0)),
            pl.BlockSpec((d, N_KV_GROUPS * HEAD_W), lambda i: (0, 0)),
        ],
        out_specs=[out_spec] * N_KV_GROUPS,
        compiler_params=_cparams("arbitrary"),
        name="ctx_proj",
    )(ctx, m_ctx, w_in_bf)


def _retention_kernel(lg_ref, q_ref, k_ref, v_ref, g_ref, kc_ref, vc_ref, o_ref, u_ref, s_ref):
    hd = pl.program_id(1)
    n = q_ref.shape[1]
    c = RET_CHUNK
    dk = RET_DK
    nc = n // c
    l = kc_ref.shape[1]
    lgf = lg_ref[0, hd]
    lgb = lg_ref[1, hd]

    ii = lax.broadcasted_iota(jnp.int32, (c, c), 0).astype(F32)
    jj = lax.broadcasted_iota(jnp.int32, (c, c), 1).astype(F32)
    diff = ii - jj
    decay = (jnp.where(diff >= 0, jnp.exp(lgf * jnp.maximum(diff, 0.0)), 0.0)
             + jnp.where(diff <= 0, jnp.exp(lgb * jnp.maximum(-diff, 0.0)), 0.0))
    idx = lax.broadcasted_iota(jnp.int32, (c, 1), 0).astype(F32)
    q_dec_f = jnp.exp(lgf * (idx + 1.0))
    k_dec_f = jnp.exp(lgf * (c - 1.0 - idx))
    q_dec_b = jnp.exp(lgb * (c - idx))
    k_dec_b = jnp.exp(lgb * idx)
    ones = jnp.ones((1, dk), F32)
    chunk_dec_f = jnp.exp(ones * (lgf * c))
    chunk_dec_b = jnp.exp(ones * (lgb * c))

    pos = lax.broadcasted_iota(jnp.int32, (l, 1), 0).astype(F32)
    kc = kc_ref[0].astype(F32)
    vc = vc_ref[0]
    s_f0 = _dot_tn((kc * jnp.exp(lgf * (l - 1.0 - pos))).astype(BF16), vc)
    s_b0 = _dot_tn((kc * jnp.exp(lgb * pos)).astype(BF16), vc)

    def chunk_rows(i):
        return pl.ds(pl.multiple_of(i * c, c), c)

    def kv_step(i, carry):
        rows = chunk_rows(i)
        k_i = k_ref[0, rows, :].astype(F32)
        kk = jnp.concatenate([(k_i * k_dec_f).astype(BF16), (k_i * k_dec_b).astype(BF16)], axis=1)
        u_ref[i] = _dot_tn(kk, v_ref[0, rows, :])
        return carry

    lax.fori_loop(0, nc, kv_step, 0, unroll=True)

    def scan_f(i, s):
        s_ref[i, :, 0:dk] = s.astype(BF16)
        return s * chunk_dec_f + u_ref[i, 0:dk, :]

    lax.fori_loop(0, nc, scan_f, s_f0, unroll=True)

    def scan_b(t, s):
        i = nc - 1 - t
        s_ref[i, :, dk:2 * dk] = s.astype(BF16)
        return s * chunk_dec_b + u_ref[i, dk:2 * dk, :]

    lax.fori_loop(0, nc, scan_b, s_b0, unroll=True)

    def out_step(i, carry):
        rows = chunk_rows(i)
        q_i = q_ref[0, rows, :]
        v_i = v_ref[0, rows, :]
        att = (_dot_nt(q_i, k_ref[0, rows, :]) * decay).astype(BF16)
        inter = _dot(q_i, s_ref[i])
        o = _dot(att, v_i) + inter[:, 0:dk] * q_dec_f + inter[:, dk:2 * dk] * q_dec_b
        mu = jnp.mean(o, axis=-1, keepdims=True)
        var = jnp.mean(jnp.square(o - mu), axis=-1, keepdims=True)
        o_n = (o - mu) * lax.rsqrt(var + LN_EPS)
        gate = g_ref[0, rows, :].astype(F32)
        o_ref[0, rows, :] = (_silu(gate) * o_n).astype(BF16)
        return carry

    lax.fori_loop(0, nc, out_step, 0, unroll=True)


def _retention(log_gamma2, rq, rk, rv, rg, rk_c, rv_c):
    b, n, _ = rq.shape
    l = rk_c.shape[1]
    nc = n // RET_CHUNK
    seq_spec = pl.BlockSpec((1, n, RET_DK), lambda i, h: (i, 0, h))
    ctx_spec = pl.BlockSpec((1, l, RET_DK), lambda i, h: (i, 0, h))
    return pl.pallas_call(
        _retention_kernel,
        out_shape=jax.ShapeDtypeStruct((b, n, HEAD_W), BF16),
        grid=(b, RET_HEADS),
        in_specs=[pl.BlockSpec(memory_space=pltpu.SMEM),
                  seq_spec, seq_spec, seq_spec, seq_spec, ctx_spec, ctx_spec],
        out_specs=seq_spec,
        scratch_shapes=[pltpu.VMEM((nc, 2 * RET_DK, RET_DK), F32),
                        pltpu.VMEM((nc, RET_DK, 2 * RET_DK), BF16)],
        compiler_params=_cparams("arbitrary", "arbitrary"),
        name="retention",
    )(log_gamma2, rq, rk, rv, rg, rk_c, rv_c)


def _na_geometry(rows):
    rb = NA_ROWS_PER_BLOCK
    wr = min(NA_WIN_R, rows)
    key_rows = min(rows, rb + wr - 1)
    variants, block_variant, block_start = [], [], []
    for kb in range(rows // rb):
        q_rows = kb * rb + np.arange(rb)
        r0 = np.clip(q_rows - wr // 2, 0, rows - wr)
        ks = int(np.clip(r0.min(), 0, rows - key_rows))
        assert r0.max() + wr <= ks + key_rows
        kr = ks + np.arange(key_rows)
        valid = (kr[None, :] >= r0[:, None]) & (kr[None, :] < r0[:, None] + wr)
        ridx = np.where(valid, kr[None, :] - q_rows[:, None] + NA_WIN_R - 1, N_BIAS_ROWS).astype(np.int32)
        for vi, v in enumerate(variants):
            if np.array_equal(v, ridx):
                block_variant.append(vi)
                break
        else:
            variants.append(ridx)
            block_variant.append(len(variants) - 1)
        block_start.append(ks)
    return key_rows, np.stack(variants), block_variant, block_start


def _na_bias_table(rpb, variant_rows):
    n_heads = rpb.shape[0]
    qc = np.arange(GRID_W)[:, None]
    kc = np.arange(GRID_W)[None, :]
    wstart = np.clip(qc - NA_WIN_C // 2, 0, GRID_W - NA_WIN_C)
    col_ok = (kc >= wstart) & (kc < wstart + NA_WIN_C)
    cidx = np.clip(kc - qc + NA_WIN_C - 1, 0, 2 * NA_WIN_C - 2)
    onehot = ((cidx[None] == np.arange(2 * NA_WIN_C - 1)[:, None, None]) & col_ok[None]).astype(np.float32)
    blocks = jnp.einsum("hrd,dqk->hrqk", rpb.astype(F32), jnp.asarray(onehot), precision=HIGHEST)
    blocks = jnp.where(col_ok[None, None], blocks, MASK_VALUE)
    blocks = jnp.concatenate([blocks, jnp.full((n_heads, 1, GRID_W, GRID_W), MASK_VALUE, F32)], axis=1)
    nvar, rb, key_rows = variant_rows.shape
    tab = blocks[:, variant_rows.reshape(-1)].reshape(n_heads, nvar, rb, key_rows, GRID_W, GRID_W)
    return jnp.transpose(tab, (0, 1, 2, 4, 3, 5)).reshape(n_heads, nvar, rb * GRID_W, key_rows * GRID_W)


def _na_kernel(q_ref, k_ref, v_ref, kc_ref, vc_ref, bias_ref, o_ref, *, block_variant, block_start, key_rows):
    qb = NA_ROWS_PER_BLOCK * GRID_W
    nk = key_rows * GRID_W
    first_head = lax.broadcasted_iota(jnp.int32, (qb, LANES), 1) < NA_DH
    kc = kc_ref[0]
    vc = vc_ref[0]
    for kb, (var, ks) in enumerate(zip(block_variant, block_start)):
        q = q_ref[0, kb * qb:(kb + 1) * qb, :]
        k_win = k_ref[0, ks * GRID_W:ks * GRID_W + nk, :]
        v_win = v_ref[0, ks * GRID_W:ks * GRID_W + nk, :]
        zero = jnp.zeros_like(q)
        q2 = jnp.concatenate([jnp.where(first_head, q, zero), jnp.where(first_head, zero, q)], axis=0)
        s_win = _dot_nt(q2, k_win) + jnp.concatenate([bias_ref[0, var], bias_ref[1, var]], axis=0)
        s_ctx = _dot_nt(q2, kc)
        m = jnp.maximum(jnp.max(s_win, axis=-1, keepdims=True), jnp.max(s_ctx, axis=-1, keepdims=True))
        p_win = jnp.exp2(s_win - m)
        p_ctx = jnp.exp2(s_ctx - m)
        denom = jnp.sum(p_win, axis=-1, keepdims=True) + jnp.sum(p_ctx, axis=-1, keepdims=True)
        o = (_dot(p_win.astype(BF16), v_win) + _dot(p_ctx.astype(BF16), vc)) / denom
        o_ref[0, kb * qb:(kb + 1) * qb, :] = jnp.where(first_head, o[0:qb], o[qb:2 * qb]).astype(BF16)


def _neighbourhood_attention(nq, nk, nv, nk_c, nv_c, rpb):
    b, n, _ = nq.shape
    l = nk_c.shape[1]
    rows = n // GRID_W
    key_rows, variant_rows, block_variant, block_start = _na_geometry(rows)
    bias = _na_bias_table(rpb, variant_rows) * LOG2E
    nvar = variant_rows.shape[0]
    qb = NA_ROWS_PER_BLOCK * GRID_W
    nkeys = key_rows * GRID_W
    seq_spec = pl.BlockSpec((1, n, LANES), lambda h, i: (i, 0, h))
    ctx_spec = pl.BlockSpec((1, l, LANES), lambda h, i: (i, 0, h))
    kern = functools.partial(_na_kernel, block_variant=tuple(block_variant), block_start=tuple(block_start),
                             key_rows=key_rows)
    return pl.pallas_call(
        kern,
        out_shape=jax.ShapeDtypeStruct((b, n, HEAD_W), BF16),
        grid=(NA_HEADS // 2, b),
        in_specs=[seq_spec, seq_spec, seq_spec, ctx_spec, ctx_spec,
                  pl.BlockSpec((2, nvar, qb, nkeys), lambda h, i: (h, 0, 0, 0))],
        out_specs=seq_spec,
        compiler_params=_cparams("arbitrary", "arbitrary"),
        name="neighbourhood_attention",
    )(nq, nk, nv, nk_c, nv_c, bias)


def _layer_norm(z, g, b):
    mu = jnp.mean(z, axis=-1, keepdims=True)
    var = jnp.mean(jnp.square(z - mu), axis=-1, keepdims=True)
    return (z - mu) * lax.rsqrt(var + LN_EPS) * g + b


def _split_bf16(v):
    hi = v.astype(BF16)
    return hi, (v - hi.astype(F32)).astype(BF16)


RANK_BLOCK = 128


def _expert_prefix(onehot, running):
    rows = onehot.shape[0]
    ii = lax.broadcasted_iota(jnp.int32, (RANK_BLOCK, RANK_BLOCK), 0)
    jj = lax.broadcasted_iota(jnp.int32, (RANK_BLOCK, RANK_BLOCK), 1)
    lower = (jj < ii).astype(BF16)
    outs = []
    for r in range(rows // RANK_BLOCK):
        oh = onehot[r * RANK_BLOCK:(r + 1) * RANK_BLOCK]
        before = _dot(lower, oh.astype(BF16)) + running
        outs.append(jnp.sum(oh * before, axis=-1, keepdims=True))
        running = running + jnp.sum(oh, axis=0, keepdims=True)
    return jnp.concatenate(outs, axis=0), running


def _route(t, wr_ref, br_ref, cnt_ref):
    t_hi, t_lo = _split_bf16(t)
    w_hi, w_lo = _split_bf16(wr_ref[...])
    hi = _dot(t_hi, jnp.concatenate([w_hi, w_lo], axis=1))
    logits = hi[:, 0:LANES] + (_dot(t_lo, w_hi) + hi[:, LANES:2 * LANES]) + br_ref[...]
    lane = lax.broadcasted_iota(jnp.int32, logits.shape, 1).astype(F32)
    neg = -jnp.inf
    big = float(LANES)

    def first_max(vals):
        vmax = jnp.max(vals, axis=-1, keepdims=True)
        return vmax, jnp.min(jnp.where(vals == vmax, lane, big), axis=-1, keepdims=True)

    is_grp = lane < MOE_GROUPS
    g_max, grp = first_max(jnp.where(is_grp, logits, neg))
    g_sum = jnp.sum(jnp.where(is_grp, jnp.exp(logits - g_max), 0.0), axis=-1, keepdims=True)
    gate_g = 1.0 / g_sum
    lo = MOE_GROUPS + grp * MOE_PER_GROUP
    in_grp = (lane >= lo) & (lane < lo + MOE_PER_GROUP)
    le = jnp.where(in_grp, logits, neg)
    v1, i1 = first_max(le)
    v2, i2 = first_max(jnp.where(lane == i1, neg, le))
    e21 = jnp.exp(v2 - v1)
    w1 = gate_g / (1.0 + e21)
    w2 = gate_g * e21 / (1.0 + e21)
    @pl.when((pl.program_id(0) == 0) & (pl.program_id(1) == 0))
    def _():
        cnt_ref[...] = jnp.zeros_like(cnt_ref)

    before1, count1 = _expert_prefix((lane == i1).astype(F32), cnt_ref[0:1, :])
    before2, count2 = _expert_prefix((lane == i2).astype(F32), cnt_ref[1:2, :])
    cnt_ref[0:1, :] = count1
    cnt_ref[1:2, :] = count2
    fields = (w1, w2, i1 - MOE_GROUPS, i2 - MOE_GROUPS, before1, before2)
    out = jnp.zeros_like(logits)
    for k, v in enumerate(fields):
        out = jnp.where(lane == k, v, out)
    return out


def _post_norm_route(h, y, m_ref, ln_ref, wr_ref, br_ref, h_out, t_out, r_out, cnt_out):
    gate = m_ref[0, 2:3, :]
    shift = m_ref[0, 3:4, :]
    scale = m_ref[0, 4:5, :]
    h1 = _layer_norm(DEEPNORM_ALPHA * h + gate * y, ln_ref[0:1, :], ln_ref[1:2, :])
    t = h1 * (1.0 + scale) + shift
    h_out[0] = h1
    t_out[0] = t.astype(BF16)
    r_out[0] = _route(t, wr_ref, br_ref, cnt_out)


def _epilogue_specs(b, n, d, tm):
    in_specs = [
        pl.BlockSpec((1, N_MOD, d), lambda i, j: (i, 0, 0)),
        pl.BlockSpec((2, d), lambda i, j: (0, 0)),
        pl.BlockSpec((d, LANES), lambda i, j: (0, 0)),
        pl.BlockSpec((1, LANES), lambda i, j: (0, 0)),
    ]
    out_shape = [jax.ShapeDtypeStruct((b, n, d), F32), jax.ShapeDtypeStruct((b, n, d), BF16),
                 jax.ShapeDtypeStruct((b, n, LANES), F32), jax.ShapeDtypeStruct((8, LANES), F32)]
    out_specs = [pl.BlockSpec((1, tm, d), lambda i, j: (i, j, 0)),
                 pl.BlockSpec((1, tm, d), lambda i, j: (i, j, 0)),
                 pl.BlockSpec((1, tm, LANES), lambda i, j: (i, j, 0)),
                 pl.BlockSpec((8, LANES), lambda i, j: (0, 0))]
    return in_specs, out_shape, out_specs


def _out_proj_kernel(x_ref, yr_ref, yn_ref, w_ref, m_ref, ln_ref, wr_ref, br_ref, h_out, t_out, r_out, cnt_out):
    y = _dot(yr_ref[0], w_ref[0:HEAD_W, :]) + _dot(yn_ref[0], w_ref[HEAD_W:2 * HEAD_W, :])
    _post_norm_route(x_ref[0], y, m_ref, ln_ref, wr_ref, br_ref, h_out, t_out, r_out, cnt_out)


def _out_proj(x, y_ret, y_na, w_out_bf, m, ln, router):
    b, n, d = x.shape
    tm = EPILOGUE_TILE
    ep_in, out_shape, out_specs = _epilogue_specs(b, n, d, tm)
    return pl.pallas_call(
        _out_proj_kernel,
        out_shape=out_shape,
        grid=(b, n // tm),
        in_specs=[
            pl.BlockSpec((1, tm, d), lambda i, j: (i, j, 0)),
            pl.BlockSpec((1, tm, HEAD_W), lambda i, j: (i, j, 0)),
            pl.BlockSpec((1, tm, HEAD_W), lambda i, j: (i, j, 0)),
            pl.BlockSpec((2 * HEAD_W, d), lambda i, j: (0, 0)),
        ] + ep_in,
        out_specs=out_specs,
        compiler_params=_cparams("arbitrary", "arbitrary"),
        name="out_proj",
    )(x, y_ret, y_na, w_out_bf, m, ln, *router)


POOL_HALO = max(POOL_SIZES) // 2
POOL_TILE = EPILOGUE_TILE


def _pool_kernel(x_ref, prev_ref, next_ref, pw_ref, ps_ref, m_ref, ln_ref, wr_ref, br_ref,
                 h_out, t_out, r_out, cnt_out, *, n):
    j = pl.program_id(1)
    nj = pl.num_programs(1)
    tm = x_ref.shape[1]
    halo = POOL_HALO
    shift = m_ref[0, 0:1, :]
    scale = m_ref[0, 1:2, :]
    x = x_ref[0]
    hm = x * (1.0 + scale) + shift
    prev = jnp.where(j > 0, prev_ref[0] * (1.0 + scale) + shift, 0.0)
    nxt = jnp.where(j < nj - 1, next_ref[0] * (1.0 + scale) + shift, 0.0)
    ext = jnp.concatenate([prev, hm, nxt], axis=0)
    pos = (j * tm + lax.broadcasted_iota(jnp.int32, (tm, 1), 0))
    ys = []
    for g, w in enumerate(POOL_SIZES):
        cols = slice(g * POOL_GROUP, (g + 1) * POOL_GROUP)
        s = ext[:, cols]
        span = 1
        while span < w:
            s = s[:s.shape[0] - span] + s[span:]
            span *= 2
        off = halo - w // 2
        win = s[off:off + tm]
        cnt = (jnp.minimum(pos + (w - w // 2), n) - jnp.maximum(pos - w // 2, 0)).astype(F32)
        z = (win / cnt - hm[:, cols]).astype(BF16)
        ys.append(_dot(z, pw_ref[g]))
    y = jnp.concatenate(ys, axis=-1) * ps_ref[...]
    _post_norm_route(x, y, m_ref, ln_ref, wr_ref, br_ref, h_out, t_out, r_out, cnt_out)


def _pool_mixer(h, pool_w_bf, pool_scale, m, ln, router):
    b, n, d = h.shape
    tm = POOL_TILE
    halo = POOL_HALO
    blocks_per_tile = tm // halo
    n_halo_blocks = n // halo
    ep_in, out_shape, out_specs = _epilogue_specs(b, n, d, tm)
    return pl.pallas_call(
        functools.partial(_pool_kernel, n=n),
        out_shape=out_shape,
        grid=(b, n // tm),
        in_specs=[
            pl.BlockSpec((1, tm, d), lambda i, j: (i, j, 0)),
            pl.BlockSpec((1, halo, d), lambda i, j: (i, jnp.maximum(j * blocks_per_tile - 1, 0), 0)),
            pl.BlockSpec((1, halo, d),
                         lambda i, j: (i, jnp.minimum((j + 1) * blocks_per_tile, n_halo_blocks - 1), 0)),
            pl.BlockSpec((len(POOL_SIZES), POOL_GROUP, POOL_GROUP), lambda i, j: (0, 0, 0)),
            pl.BlockSpec((1, d), lambda i, j: (0, 0)),
        ] + ep_in,
        out_specs=out_specs,
        compiler_params=_cparams("arbitrary", "arbitrary"),
        name="pool_mixer",
    )(h, h, h, pool_w_bf, pool_scale.reshape(1, d), m, ln, *router)


def _expert_kernel(te_ref, tv_ref, x_ref, wg_ref, wu_ref, wd_ref, buf_ref, o_ref, wg_s, wu_s, wd_s):
    del buf_ref
    j = pl.program_id(0)
    prev_e = te_ref[jnp.maximum(j - 1, 0)]

    @pl.when((j == 0) | (te_ref[j] != prev_e))
    def _():
        wg_s[...] = wg_ref[0, 0].astype(BF16)
        wu_s[...] = wu_ref[0, 0].astype(BF16)
        wd_s[...] = wd_ref[0, 0].astype(BF16)

    @pl.when(tv_ref[j] == 1)
    def _():
        x = x_ref[...]
        gate = _dot(x, wg_s[...])
        up = _dot(x, wu_s[...])
        act = (_silu(gate) * up).astype(BF16)
        o_ref[...] = _dot(act, wd_s[...]).astype(BF16)

    @pl.when(tv_ref[j] == 0)
    def _():
        o_ref[...] = jnp.zeros_like(o_ref)


def _experts(tile_expert, tile_valid, x_chunk, y_prev, w_gate, w_up, w_down, layer, chunk, n_chunks):
    pc, d = x_chunk.shape
    tm = MOE_TILE
    tiles = pc // tm
    hid = w_gate.shape[-1]
    in_specs = [
        pl.BlockSpec((tm, d), lambda j, te, tv: (j, 0)),
        pl.BlockSpec((1, 1, d, hid), lambda j, te, tv: (layer, te[j], 0, 0)),
        pl.BlockSpec((1, 1, d, hid), lambda j, te, tv: (layer, te[j], 0, 0)),
        pl.BlockSpec((1, 1, hid, d), lambda j, te, tv: (layer, te[j], 0, 0)),
    ]
    in_specs.append(pl.BlockSpec(memory_space=pl.ANY))
    args = [tile_expert, tile_valid, x_chunk, w_gate, w_up, w_down, y_prev]
    aliases = {len(args) - 1: 0}
    grid_spec = pltpu.PrefetchScalarGridSpec(
        num_scalar_prefetch=2,
        grid=(tiles,),
        in_specs=in_specs,
        out_specs=pl.BlockSpec((tm, d), lambda j, te, tv: (chunk * tiles + j, 0)),
        scratch_shapes=[pltpu.VMEM((d, hid), BF16), pltpu.VMEM((d, hid), BF16), pltpu.VMEM((hid, d), BF16)],
    )
    return pl.pallas_call(
        _expert_kernel,
        out_shape=jax.ShapeDtypeStruct((pc * n_chunks, d), BF16),
        grid_spec=grid_spec,
        input_output_aliases=aliases,
        compiler_params=_cparams("arbitrary"),
        name="experts",
    )(*args)


ASSIGN_BITS = 16


def _dispatch_plan(route, cnt, tm):
    tok = route.shape[0]
    n_assign = 2 * tok
    n_pad = MOE_EXPERTS * tm
    p = n_assign + n_pad
    n_tiles = p // tm
    assert n_assign <= 1 << ASSIGN_BITS
    i32 = jnp.int32
    low_mask = (1 << ASSIGN_BITS) - 1
    fields = jnp.transpose(route[:, 2:6]).astype(i32)
    eid = fields[0:2].reshape(n_assign)
    before = fields[2:4].reshape(n_assign)
    first_counts = cnt[0, MOE_GROUPS:MOE_GROUPS + MOE_EXPERTS].astype(i32)
    counts = first_counts + cnt[1, MOE_GROUPS:MOE_GROUPS + MOE_EXPERTS].astype(i32)
    padded = ((counts + tm - 1) // tm) * tm
    pad_end = jnp.cumsum(padded)
    pad_start = pad_end - padded
    tile_row = jnp.arange(n_tiles, dtype=i32) * tm
    tile_expert = jnp.minimum(jnp.sum((pad_end[None, :] <= tile_row[:, None]).astype(i32), axis=1), MOE_EXPERTS - 1)
    tile_valid = (tile_row < pad_end[-1]).astype(i32)
    base = jnp.concatenate([pad_start, pad_start + first_counts])
    code = eid + MOE_EXPERTS * (jnp.arange(n_assign, dtype=i32) >= tok).astype(i32)
    codes = jnp.arange(2 * MOE_EXPERTS, dtype=i32)
    pos_of_assign = before + jnp.sum(jnp.where(code[:, None] == codes[None, :], base[None, :], 0), axis=1)
    need_end = jnp.cumsum(padded - counts)
    pad_id = jnp.arange(n_pad, dtype=i32)
    pad_expert = jnp.sum((need_end[None, :] <= pad_id[:, None]).astype(i32), axis=1)
    packed = jnp.concatenate([((2 * eid) << ASSIGN_BITS) | jnp.arange(n_assign, dtype=i32),
                              ((2 * pad_expert + 1) << ASSIGN_BITS) | (pad_id & low_mask)])
    tok_of_pos = (lax.sort(packed) & low_mask) % tok
    return tile_expert, tile_valid, tok_of_pos, pos_of_assign


def _n_chunks(b):
    return 4 if b % 4 == 0 else (2 if b % 2 == 0 else 1)


def _post_norm_kernel(h_ref, y0_ref, y1_ref, r_ref, m_ref, ln_ref, o_ref):
    gate = m_ref[0, 5:6, :]
    r = r_ref[0]
    y = r[:, 0:1] * y0_ref[...].astype(F32) + r[:, 1:2] * y1_ref[...].astype(F32)
    o_ref[0] = _layer_norm(DEEPNORM_ALPHA * h_ref[0] + gate * y, ln_ref[0:1, :], ln_ref[1:2, :])


def _post_norm(h, y2, route, m, ln, chunk, n_chunks):
    b, n, d = h.shape
    tm = ROW_TILE
    bc = b // n_chunks
    b0 = chunk * bc
    tiles_per_seq = n // tm
    tiles_per_k = bc * tiles_per_seq
    row_spec = pl.BlockSpec((1, tm, d), lambda i, j: (b0 + i, j, 0))
    return pl.pallas_call(
        _post_norm_kernel,
        out_shape=jax.ShapeDtypeStruct((b, n, d), F32),
        grid=(bc, tiles_per_seq),
        in_specs=[row_spec,
                  pl.BlockSpec((tm, d), lambda i, j: (i * tiles_per_seq + j, 0)),
                  pl.BlockSpec((tm, d), lambda i, j: (tiles_per_k + i * tiles_per_seq + j, 0)),
                  pl.BlockSpec((1, tm, LANES), lambda i, j: (b0 + i, j, 0)),
                  pl.BlockSpec((1, N_MOD, d), lambda i, j: (b0 + i, 0, 0)),
                  pl.BlockSpec((2, d), lambda i, j: (0, 0))],
        out_specs=row_spec,
        input_output_aliases={0: 0},
        compiler_params=_cparams("arbitrary", "arbitrary"),
        name="post_norm",
    )(h, y2, y2, route, m, ln)


def _moe_rows(tok):
    return 2 * tok + MOE_EXPERTS * MOE_TILE


def _moe_post_norm(h1, t, route, cnt, m, ln, w_gate, w_up, w_down, layer, row_buf):
    b, n, d = t.shape
    tok = b * n
    nch = _n_chunks(b)
    tile_expert, tile_valid, tok_of_pos, pos_of_assign = _dispatch_plan(route.reshape(tok, LANES), cnt, MOE_TILE)
    t2 = t.reshape(tok, d)
    tiles = tile_expert.shape[0] // nch
    pc = tiles * MOE_TILE
    y_sorted = row_buf
    for c in range(nch):
        x_c = t2.at[tok_of_pos[c * pc:(c + 1) * pc]].get(mode="promise_in_bounds")
        y_sorted = _experts(tile_expert[c * tiles:(c + 1) * tiles], tile_valid[c * tiles:(c + 1) * tiles],
                            x_c, y_sorted, w_gate, w_up, w_down, layer, c, nch)
    bc = b // nch
    pos3 = pos_of_assign.reshape(2, b, n)
    h = h1
    for c in range(nch):
        pos_c = pos3[:, c * bc:(c + 1) * bc].reshape(2 * bc * n)
        y2_c = y_sorted.at[pos_c].get(mode="promise_in_bounds")
        h = _post_norm(h, y2_c, route, m, ln, c, nch)
    return h, y_sorted


def _rope_tables(n):
    t = jnp.arange(n)
    rows = (t // GRID_W).astype(F32)
    cols = (t % GRID_W).astype(F32)
    n_freq = RET_DK // 4
    inv_freq = ROPE_BASE ** (-jnp.arange(n_freq, dtype=F32) / n_freq)
    ang = jnp.concatenate([rows[:, None] * inv_freq, cols[:, None] * inv_freq], axis=-1)
    cos, sin = jnp.cos(ang), jnp.sin(ang)
    cos2 = jnp.concatenate([cos, cos], axis=-1)
    sin2 = jnp.concatenate([-sin, sin], axis=-1)
    q_scale = RET_DK ** -0.5
    return cos2 * q_scale, sin2 * q_scale, cos2, sin2


def _router_params(w_r1, b_r1, w_r2, b_r2):
    d = w_r1.shape[0]
    w2 = jnp.transpose(w_r2, (1, 0, 2)).reshape(d, MOE_EXPERTS)
    pad = LANES - MOE_GROUPS - MOE_EXPERTS
    wr = jnp.concatenate([w_r1, w2, jnp.zeros((d, pad), F32)], axis=-1)
    br = jnp.concatenate([b_r1, b_r2.reshape(MOE_EXPERTS), jnp.zeros((pad,), F32)]).reshape(1, LANES)
    return wr, br


def kernel(x, c, ctx, c_ctx, w_mod, b_mod, ln_g, ln_b, ab_w_in, ab_w_out, ab_log_decay, ab_rpb, pool_w, pool_scale, moe_w_r1, moe_b_r1, moe_w_r2, moe_b_r2, moe_w_gate, moe_w_up, moe_w_down):
    b, n, d = x.shape
    cc = jnp.concatenate([c, c_ctx[None, :], jnp.zeros((MOD_ROWS - b - 1, d), F32)], axis=0)
    mod = _modulation(cc, w_mod, b_mod)
    h = x
    row_buf = None
    for i in range(DEPTH):
        j = i // 2
        m = mod[i, :b].reshape(b, N_MOD, d)
        ln1 = jnp.stack([ln_g[i, 0], ln_b[i, 0]])
        ln2 = jnp.stack([ln_g[i, 1], ln_b[i, 1]])
        router = _router_params(moe_w_r1[i], moe_b_r1[i], moe_w_r2[i], moe_b_r2[i])
        if i % 2 == 0:
            m_ctx = mod[i, b].reshape(1, N_MOD, d)
            w_in_bf = ab_w_in[j].astype(BF16)
            log_gamma2 = jnp.log1p(-jnp.exp(ab_log_decay[j].astype(F32)))
            rk, rv, nk, nv, rq, rg, nq, row_buf = _in_proj(h, m, w_in_bf, _rope_tables(n), _moe_rows(b * n))
            rk_c, rv_c, nk_c, nv_c = _ctx_proj(ctx, m_ctx, w_in_bf[:, :N_KV_GROUPS * HEAD_W])
            y_ret = _retention(log_gamma2, rq, rk, rv, rg, rk_c, rv_c)
            y_na = _neighbourhood_attention(nq, nk, nv, nk_c, nv_c, ab_rpb[j])
            h1, t, route, cnt = _out_proj(h, y_ret, y_na, ab_w_out[j].astype(BF16), m, ln1, router)
        else:
            h1, t, route, cnt = _pool_mixer(h, pool_w[j].astype(BF16), pool_scale[j], m, ln1, router)
        h, row_buf = _moe_post_norm(h1, t, route, cnt, m, ln2, moe_w_gate, moe_w_up, moe_w_down, i, row_buf)
    return h
```

```python
import functools

import numpy as np
import jax
import jax.numpy as jnp
from jax import lax
from jax.experimental import pallas as pl
from jax.experimental.pallas import tpu as pltpu

F32 = jnp.float32
BF16 = jnp.bfloat16
HIGHEST = lax.Precision.HIGHEST

D_MODEL = 1024
DEPTH = 2
GRID_W = 64
RET_HEADS = 4
RET_DK = 128
RET_CHUNK = 128
NA_HEADS = 8
NA_DH = 64
NA_WIN_R = 8
NA_WIN_C = 16
N_BIAS_ROWS = 2 * NA_WIN_R - 1
POOL_SIZES = (2, 4, 8, 16)
POOL_GROUP = D_MODEL // len(POOL_SIZES)
MOE_GROUPS = 4
MOE_PER_GROUP = 8
MOE_EXPERTS = MOE_GROUPS * MOE_PER_GROUP
MOE_HIDDEN = D_MODEL // 2
ROPE_BASE = 10000.0
LN_EPS = 1e-5
N_MOD = 6
DEEPNORM_ALPHA = (2 * DEPTH) ** 0.25
HEAD_W = 512
N_IN_GROUPS = 7
N_KV_GROUPS = 4
LOG2E = float(np.log2(np.e))
MASK_VALUE = -1e30

LANES = 128
VMEM_LIMIT = 56 * 1024 * 1024

NA_ROWS_PER_BLOCK = 4
ROW_TILE = 512
EPILOGUE_TILE = 1024
MOE_TILE = 512
MOD_ROWS = 24
MOD_COL_TILE = 1536


def _cparams(*sem):
    return pltpu.CompilerParams(dimension_semantics=sem, vmem_limit_bytes=VMEM_LIMIT)


def _silu(v):
    return v / (1.0 + jnp.exp(-v))


def _dot(a, b):
    return jnp.dot(a, b, preferred_element_type=F32)


def _dot_nt(a, b):
    return lax.dot_general(a, b, (((1,), (1,)), ((), ())), preferred_element_type=F32)


def _dot_tn(a, b):
    return lax.dot_general(a, b, (((0,), (0,)), ((), ())), preferred_element_type=F32)


def _mod_kernel(c_ref, w_ref, b_ref, o_ref):
    s = _silu(c_ref[...])
    o_ref[0] = jnp.dot(s, w_ref[0], precision=HIGHEST, preferred_element_type=F32) + b_ref[0]


def _modulation(cc, w_mod, b_mod):
    depth, d, n = w_mod.shape
    return pl.pallas_call(
        _mod_kernel,
        out_shape=jax.ShapeDtypeStruct((depth, MOD_ROWS, n), F32),
        grid=(depth, n // MOD_COL_TILE),
        in_specs=[
            pl.BlockSpec((MOD_ROWS, d), lambda i, j: (0, 0)),
            pl.BlockSpec((1, d, MOD_COL_TILE), lambda i, j: (i, 0, j)),
            pl.BlockSpec((1, 1, MOD_COL_TILE), lambda i, j: (i, 0, j)),
        ],
        out_specs=pl.BlockSpec((1, MOD_ROWS, MOD_COL_TILE), lambda i, j: (i, 0, j)),
        compiler_params=_cparams("arbitrary", "arbitrary"),
        name="modulation",
    )(cc, w_mod, b_mod.reshape(depth, 1, n))


def _rope(v, cos2, sin2):
    return v * cos2 + pltpu.roll(v, RET_DK // 2, axis=1) * sin2


def _in_proj_kernel(x_ref, m_ref, w_ref, cq_ref, sq_ref, ck_ref, sk_ref,
                    rk_ref, rv_ref, nk_ref, nv_ref, rq_ref, rg_ref, nq_ref, buf_ref):
    buf_ref[...] = jnp.zeros_like(buf_ref)
    shift = m_ref[0, 0:1, :]
    scale = m_ref[0, 1:2, :]
    hm = (x_ref[0] * (1.0 + scale) + shift).astype(BF16)
    outs = (rk_ref, rv_ref, nk_ref, nv_ref, rq_ref, rg_ref, nq_ref)
    for g, o_ref in enumerate(outs):
        p = _dot(hm, w_ref[:, g * HEAD_W:(g + 1) * HEAD_W])
        if g == 0 or g == 4:
            cos2 = (ck_ref if g == 0 else cq_ref)[...]
            sin2 = (sk_ref if g == 0 else sq_ref)[...]
            for hd in range(RET_HEADS):
                sl = slice(hd * RET_DK, (hd + 1) * RET_DK)
                o_ref[0, :, sl] = _rope(p[:, sl], cos2, sin2).astype(BF16)
        elif g == 6:
            o_ref[0] = (p * (NA_DH ** -0.5 * LOG2E)).astype(BF16)
        else:
            o_ref[0] = p.astype(BF16)


def _in_proj(x, m, w_in_bf, rope_tabs, buf_rows):
    b, n, d = x.shape
    tm = ROW_TILE
    tiles_per_seq = n // tm
    buf_tile = buf_rows // (b * tiles_per_seq)
    assert buf_tile * b * tiles_per_seq == buf_rows and buf_tile % 16 == 0
    tab_spec = pl.BlockSpec((tm, RET_DK), lambda i, j: (j, 0))
    out_spec = pl.BlockSpec((1, tm, HEAD_W), lambda i, j: (i, j, 0))
    return pl.pallas_call(
        _in_proj_kernel,
        out_shape=[jax.ShapeDtypeStruct((b, n, HEAD_W), BF16)] * N_IN_GROUPS
        + [jax.ShapeDtypeStruct((buf_rows, d), BF16)],
        grid=(b, tiles_per_seq),
        in_specs=[
            pl.BlockSpec((1, tm, d), lambda i, j: (i, j, 0)),
            pl.BlockSpec((1, N_MOD, d), lambda i, j: (i, 0, 0)),
            pl.BlockSpec((d, N_IN_GROUPS * HEAD_W), lambda i, j: (0, 0)),
            tab_spec, tab_spec, tab_spec, tab_spec,
        ],
        out_specs=[out_spec] * N_IN_GROUPS + [pl.BlockSpec((buf_tile, d), lambda i, j: (i * tiles_per_seq + j, 0))],
        compiler_params=_cparams("arbitrary", "arbitrary"),
        name="in_proj",
    )(x, m, w_in_bf, *rope_tabs)


def _ctx_proj_kernel(x_ref, m_ref, w_ref, rk_ref, rv_ref, nk_ref, nv_ref):
    shift = m_ref[0, 0:1, :]
    scale = m_ref[0, 1:2, :]
    hm = (x_ref[0] * (1.0 + scale) + shift).astype(BF16)
    for g, o_ref in enumerate((rk_ref, rv_ref, nk_ref, nv_ref)):
        o_ref[0] = _dot(hm, w_ref[:, g * HEAD_W:(g + 1) * HEAD_W]).astype(BF16)


def _ctx_proj(ctx, m_ctx, w_in_bf):
    b, l, d = ctx.shape
    out_spec = pl.BlockSpec((1, l, HEAD_W), lambda i: (i, 0, 0))
    return pl.pallas_call(
        _ctx_proj_kernel,
        out_shape=[jax.ShapeDtypeStruct((b, l, HEAD_W), BF16)] * N_KV_GROUPS,
        grid=(b,),
        in_specs=[
            pl.BlockSpec((1, l, d), lambda i: (i, 0, 0)),
            pl.BlockSpec((1, N_MOD, d), lambda i: (0, 0, 0)),
            pl.BlockSpec((d, N_KV_GROUPS * HEAD_W), lambda i: (0, 0)),
        ],
        out_specs=[out_spec] * N_KV_GROUPS,
        compiler_params=_cparams("arbitrary"),
        name="ctx_proj",
    )(ctx, m_ctx, w_in_bf)


def _retention_kernel(lg_ref, q_ref, k_ref, v_ref, g_ref, kc_ref, vc_ref, o_ref, u_ref, s_ref):
    hd = pl.program_id(1)
    n = q_ref.shape[1]
    c = RET_CHUNK
    dk = RET_DK
    nc = n // c
    l = kc_ref.shape[1]
    lgf = lg_ref[0, hd]
    lgb = lg_ref[1, hd]

    ii = lax.broadcasted_iota(jnp.int32, (c, c), 0).astype(F32)
    jj = lax.broadcasted_iota(jnp.int32, (c, c), 1).astype(F32)
    diff = ii - jj
    decay = (jnp.where(diff >= 0, jnp.exp(lgf * jnp.maximum(diff, 0.0)), 0.0)
             + jnp.where(diff <= 0, jnp.exp(lgb * jnp.maximum(-diff, 0.0)), 0.0))
    idx = lax.broadcasted_iota(jnp.int32, (c, 1), 0).astype(F32)
    q_dec_f = jnp.exp(lgf * (idx + 1.0))
    k_dec_f = jnp.exp(lgf * (c - 1.0 - idx))
    q_dec_b = jnp.exp(lgb * (c - idx))
    k_dec_b = jnp.exp(lgb * idx)
    ones = jnp.ones((1, dk), F32)
    chunk_dec_f = jnp.exp(ones * (lgf * c))
    chunk_dec_b = jnp.exp(ones * (lgb * c))

    pos = lax.broadcasted_iota(jnp.int32, (l, 1), 0).astype(F32)
    kc = kc_ref[0].astype(F32)
    vc = vc_ref[0]
    s_f0 = _dot_tn((kc * jnp.exp(lgf * (l - 1.0 - pos))).astype(BF16), vc)
    s_b0 = _dot_tn((kc * jnp.exp(lgb * pos)).astype(BF16), vc)

    def chunk_rows(i):
        return pl.ds(pl.multiple_of(i * c, c), c)

    def kv_step(i, carry):
        rows = chunk_rows(i)
        k_i = k_ref[0, rows, :].astype(F32)
        kk = jnp.concatenate([(k_i * k_dec_f).astype(BF16), (k_i * k_dec_b).astype(BF16)], axis=1)
        u_ref[i] = _dot_tn(kk, v_ref[0, rows, :])
        return carry

    lax.fori_loop(0, nc, kv_step, 0, unroll=True)

    def scan_f(i, s):
        s_ref[i, :, 0:dk] = s.astype(BF16)
        return s * chunk_dec_f + u_ref[i, 0:dk, :]

    lax.fori_loop(0, nc, scan_f, s_f0, unroll=True)

    def scan_b(t, s):
        i = nc - 1 - t
        s_ref[i, :, dk:2 * dk] = s.astype(BF16)
        return s * chunk_dec_b + u_ref[i, dk:2 * dk, :]

    lax.fori_loop(0, nc, scan_b, s_b0, unroll=True)

    def out_step(i, carry):
        rows = chunk_rows(i)
        q_i = q_ref[0, rows, :]
        v_i = v_ref[0, rows, :]
        att = (_dot_nt(q_i, k_ref[0, rows, :]) * decay).astype(BF16)
        inter = _dot(q_i, s_ref[i])
        o = _dot(att, v_i) + inter[:, 0:dk] * q_dec_f + inter[:, dk:2 * dk] * q_dec_b
        mu = jnp.mean(o, axis=-1, keepdims=True)
        var = jnp.mean(jnp.square(o - mu), axis=-1, keepdims=True)
        o_n = (o - mu) * lax.rsqrt(var + LN_EPS)
        gate = g_ref[0, rows, :].astype(F32)
        o_ref[0, rows, :] = (_silu(gate) * o_n).astype(BF16)
        return carry

    lax.fori_loop(0, nc, out_step, 0, unroll=True)


def _retention(log_gamma2, rq, rk, rv, rg, rk_c, rv_c):
    b, n, _ = rq.shape
    l = rk_c.shape[1]
    nc = n // RET_CHUNK
    seq_spec = pl.BlockSpec((1, n, RET_DK), lambda i, h: (i, 0, h))
    ctx_spec = pl.BlockSpec((1, l, RET_DK), lambda i, h: (i, 0, h))
    return pl.pallas_call(
        _retention_kernel,
        out_shape=jax.ShapeDtypeStruct((b, n, HEAD_W), BF16),
        grid=(b, RET_HEADS),
        in_specs=[pl.BlockSpec(memory_space=pltpu.SMEM),
                  seq_spec, seq_spec, seq_spec, seq_spec, ctx_spec, ctx_spec],
        out_specs=seq_spec,
        scratch_shapes=[pltpu.VMEM((nc, 2 * RET_DK, RET_DK), F32),
                        pltpu.VMEM((nc, RET_DK, 2 * RET_DK), BF16)],
        compiler_params=_cparams("arbitrary", "arbitrary"),
        name="retention",
    )(log_gamma2, rq, rk, rv, rg, rk_c, rv_c)


def _na_geometry(rows):
    rb = NA_ROWS_PER_BLOCK
    wr = min(NA_WIN_R, rows)
    key_rows = min(rows, rb + wr - 1)
    variants, block_variant, block_start = [], [], []
    for kb in range(rows // rb):
        q_rows = kb * rb + np.arange(rb)
        r0 = np.clip(q_rows - wr // 2, 0, rows - wr)
        ks = int(np.clip(r0.min(), 0, rows - key_rows))
        assert r0.max() + wr <= ks + key_rows
        kr = ks + np.arange(key_rows)
        valid = (kr[None, :] >= r0[:, None]) & (kr[None, :] < r0[:, None] + wr)
        ridx = np.where(valid, kr[None, :] - q_rows[:, None] + NA_WIN_R - 1, N_BIAS_ROWS).astype(np.int32)
        for vi, v in enumerate(variants):
            if np.array_equal(v, ridx):
                block_variant.append(vi)
                break
        else:
            variants.append(ridx)
            block_variant.append(len(variants) - 1)
        block_start.append(ks)
    return key_rows, np.stack(variants), block_variant, block_start


def _na_bias_table(rpb, variant_rows):
    n_heads = rpb.shape[0]
    qc = np.arange(GRID_W)[:, None]
    kc = np.arange(GRID_W)[None, :]
    wstart = np.clip(qc - NA_WIN_C // 2, 0, GRID_W - NA_WIN_C)
    col_ok = (kc >= wstart) & (kc < wstart + NA_WIN_C)
    cidx = np.clip(kc - qc + NA_WIN_C - 1, 0, 2 * NA_WIN_C - 2)
    onehot = ((cidx[None] == np.arange(2 * NA_WIN_C - 1)[:, None, None]) & col_ok[None]).astype(np.float32)
    blocks = jnp.einsum("hrd,dqk->hqrk", rpb.astype(F32) * LOG2E, jnp.asarray(onehot), precision=HIGHEST)
    blocks = jnp.where(col_ok[None, :, None, :], blocks, MASK_VALUE)
    masked = jnp.full((n_heads, GRID_W, GRID_W), MASK_VALUE, F32)
    nvar, rb, key_rows = variant_rows.shape
    rows = [jnp.concatenate([masked if r == N_BIAS_ROWS else blocks[:, :, r, :] for r in variant_rows[v, a]], axis=-1)
            for v in range(nvar) for a in range(rb)]
    return jnp.stack(rows, axis=1).reshape(n_heads, nvar, rb * GRID_W, key_rows * GRID_W)


def _na_kernel(q_ref, k_ref, v_ref, kc_ref, vc_ref, bias_ref, o_ref, *, block_variant, block_start, key_rows):
    qb = NA_ROWS_PER_BLOCK * GRID_W
    nk = key_rows * GRID_W
    first_head = lax.broadcasted_iota(jnp.int32, (qb, LANES), 1) < NA_DH
    kc = kc_ref[0]
    vc = vc_ref[0]
    for kb, (var, ks) in enumerate(zip(block_variant, block_start)):
        q = q_ref[0, kb * qb:(kb + 1) * qb, :]
        k_win = k_ref[0, ks * GRID_W:ks * GRID_W + nk, :]
        v_win = v_ref[0, ks * GRID_W:ks * GRID_W + nk, :]
        zero = jnp.zeros_like(q)
        q2 = jnp.concatenate([jnp.where(first_head, q, zero), jnp.where(first_head, zero, q)], axis=0)
        s_win = _dot_nt(q2, k_win) + jnp.concatenate([bias_ref[0, var], bias_ref[1, var]], axis=0)
        s_ctx = _dot_nt(q2, kc)
        m = jnp.maximum(jnp.max(s_win, axis=-1, keepdims=True), jnp.max(s_ctx, axis=-1, keepdims=True))
        p_win = jnp.exp2(s_win - m)
        p_ctx = jnp.exp2(s_ctx - m)
        denom = jnp.sum(p_win, axis=-1, keepdims=True) + jnp.sum(p_ctx, axis=-1, keepdims=True)
        o = (_dot(p_win.astype(BF16), v_win) + _dot(p_ctx.astype(BF16), vc)) / denom
        o_ref[0, kb * qb:(kb + 1) * qb, :] = jnp.where(first_head, o[0:qb], o[qb:2 * qb]).astype(BF16)


def _neighbourhood_attention(nq, nk, nv, nk_c, nv_c, rpb):
    b, n, _ = nq.shape
    l = nk_c.shape[1]
    rows = n // GRID_W
    key_rows, variant_rows, block_variant, block_start = _na_geometry(rows)
    bias = _na_bias_table(rpb, variant_rows)
    nvar = variant_rows.shape[0]
    qb = NA_ROWS_PER_BLOCK * GRID_W
    nkeys = key_rows * GRID_W
    seq_spec = pl.BlockSpec((1, n, LANES), lambda h, i: (i, 0, h))
    ctx_spec = pl.BlockSpec((1, l, LANES), lambda h, i: (i, 0, h))
    kern = functools.partial(_na_kernel, block_variant=tuple(block_variant), block_start=tuple(block_start),
                             key_rows=key_rows)
    return pl.pallas_call(
        kern,
        out_shape=jax.ShapeDtypeStruct((b, n, HEAD_W), BF16),
        grid=(NA_HEADS // 2, b),
        in_specs=[seq_spec, seq_spec, seq_spec, ctx_spec, ctx_spec,
                  pl.BlockSpec((2, nvar, qb, nkeys), lambda h, i: (h, 0, 0, 0))],
        out_specs=seq_spec,
        compiler_params=_cparams("arbitrary", "arbitrary"),
        name="neighbourhood_attention",
    )(nq, nk, nv, nk_c, nv_c, bias)


def _layer_norm(z, g, b):
    mu = jnp.mean(z, axis=-1, keepdims=True)
    var = jnp.mean(jnp.square(z - mu), axis=-1, keepdims=True)
    return (z - mu) * lax.rsqrt(var + LN_EPS) * g + b


def _split_bf16(v):
    hi = v.astype(BF16)
    return hi, (v - hi.astype(F32)).astype(BF16)


def _route(t, wr_ref, br_ref):
    t_hi, t_lo = _split_bf16(t)
    w_hi, w_lo = _split_bf16(wr_ref[...])
    hi = _dot(t_hi, jnp.concatenate([w_hi, w_lo], axis=1))
    logits = hi[:, 0:LANES] + (_dot(t_lo, w_hi) + hi[:, LANES:2 * LANES]) + br_ref[...]
    lane = lax.broadcasted_iota(jnp.int32, logits.shape, 1).astype(F32)
    neg = -jnp.inf
    big = float(LANES)

    def first_max(vals):
        vmax = jnp.max(vals, axis=-1, keepdims=True)
        return vmax, jnp.min(jnp.where(vals == vmax, lane, big), axis=-1, keepdims=True)

    is_grp = lane < MOE_GROUPS
    g_max, grp = first_max(jnp.where(is_grp, logits, neg))
    g_sum = jnp.sum(jnp.where(is_grp, jnp.exp(logits - g_max), 0.0), axis=-1, keepdims=True)
    gate_g = 1.0 / g_sum
    lo = MOE_GROUPS + grp * MOE_PER_GROUP
    in_grp = (lane >= lo) & (lane < lo + MOE_PER_GROUP)
    le = jnp.where(in_grp, logits, neg)
    v1, i1 = first_max(le)
    v2, i2 = first_max(jnp.where(lane == i1, neg, le))
    e21 = jnp.exp(v2 - v1)
    w1 = gate_g / (1.0 + e21)
    w2 = gate_g * e21 / (1.0 + e21)
    return jnp.where(lane == 0, w1,
                     jnp.where(lane == 1, w2,
                               jnp.where(lane == 2, i1 - MOE_GROUPS,
                                         jnp.where(lane == 3, i2 - MOE_GROUPS, 0.0))))


def _post_norm_route(h, y, m_ref, ln_ref, wr_ref, br_ref, h_out, t_out, r_out):
    gate = m_ref[0, 2:3, :]
    shift = m_ref[0, 3:4, :]
    scale = m_ref[0, 4:5, :]
    h1 = _layer_norm(DEEPNORM_ALPHA * h + gate * y, ln_ref[0:1, :], ln_ref[1:2, :])
    t = h1 * (1.0 + scale) + shift
    h_out[0] = h1
    t_out[0] = t.astype(BF16)
    r_out[0] = _route(t, wr_ref, br_ref)


def _epilogue_specs(b, n, d, tm):
    in_specs = [
        pl.BlockSpec((1, N_MOD, d), lambda i, j: (i, 0, 0)),
        pl.BlockSpec((2, d), lambda i, j: (0, 0)),
        pl.BlockSpec((d, LANES), lambda i, j: (0, 0)),
        pl.BlockSpec((1, LANES), lambda i, j: (0, 0)),
    ]
    out_shape = [jax.ShapeDtypeStruct((b, n, d), F32), jax.ShapeDtypeStruct((b, n, d), BF16),
                 jax.ShapeDtypeStruct((b, n, LANES), F32)]
    out_specs = [pl.BlockSpec((1, tm, d), lambda i, j: (i, j, 0)),
                 pl.BlockSpec((1, tm, d), lambda i, j: (i, j, 0)),
                 pl.BlockSpec((1, tm, LANES), lambda i, j: (i, j, 0))]
    return in_specs, out_shape, out_specs


def _out_proj_kernel(x_ref, yr_ref, yn_ref, w_ref, m_ref, ln_ref, wr_ref, br_ref, h_out, t_out, r_out):
    y = _dot(yr_ref[0], w_ref[0:HEAD_W, :]) + _dot(yn_ref[0], w_ref[HEAD_W:2 * HEAD_W, :])
    _post_norm_route(x_ref[0], y, m_ref, ln_ref, wr_ref, br_ref, h_out, t_out, r_out)


def _out_proj(x, y_ret, y_na, w_out_bf, m, ln, router):
    b, n, d = x.shape
    tm = EPILOGUE_TILE
    ep_in, out_shape, out_specs = _epilogue_specs(b, n, d, tm)
    return pl.pallas_call(
        _out_proj_kernel,
        out_shape=out_shape,
        grid=(b, n // tm),
        in_specs=[
            pl.BlockSpec((1, tm, d), lambda i, j: (i, j, 0)),
            pl.BlockSpec((1, tm, HEAD_W), lambda i, j: (i, j, 0)),
            pl.BlockSpec((1, tm, HEAD_W), lambda i, j: (i, j, 0)),
            pl.BlockSpec((2 * HEAD_W, d), lambda i, j: (0, 0)),
        ] + ep_in,
        out_specs=out_specs,
        compiler_params=_cparams("arbitrary", "arbitrary"),
        name="out_proj",
    )(x, y_ret, y_na, w_out_bf, m, ln, *router)


POOL_HALO = max(POOL_SIZES) // 2
POOL_TILE = EPILOGUE_TILE


def _pool_kernel(x_ref, prev_ref, next_ref, pw_ref, ps_ref, m_ref, ln_ref, wr_ref, br_ref,
                 h_out, t_out, r_out, *, n):
    j = pl.program_id(1)
    nj = pl.num_programs(1)
    tm = x_ref.shape[1]
    halo = POOL_HALO
    shift = m_ref[0, 0:1, :]
    scale = m_ref[0, 1:2, :]
    x = x_ref[0]
    hm = x * (1.0 + scale) + shift
    prev = jnp.where(j > 0, prev_ref[0] * (1.0 + scale) + shift, 0.0)
    nxt = jnp.where(j < nj - 1, next_ref[0] * (1.0 + scale) + shift, 0.0)
    ext = jnp.concatenate([prev, hm, nxt], axis=0)
    pos = (j * tm + lax.broadcasted_iota(jnp.int32, (tm, 1), 0))
    ys = []
    for g, w in enumerate(POOL_SIZES):
        cols = slice(g * POOL_GROUP, (g + 1) * POOL_GROUP)
        s = ext[:, cols]
        span = 1
        while span < w:
            s = s[:s.shape[0] - span] + s[span:]
            span *= 2
        off = halo - w // 2
        win = s[off:off + tm]
        cnt = (jnp.minimum(pos + (w - w // 2), n) - jnp.maximum(pos - w // 2, 0)).astype(F32)
        z = (win / cnt - hm[:, cols]).astype(BF16)
        ys.append(_dot(z, pw_ref[g]))
    y = jnp.concatenate(ys, axis=-1) * ps_ref[...]
    _post_norm_route(x, y, m_ref, ln_ref, wr_ref, br_ref, h_out, t_out, r_out)


def _pool_mixer(h, pool_w_bf, pool_scale, m, ln, router):
    b, n, d = h.shape
    tm = POOL_TILE
    halo = POOL_HALO
    blocks_per_tile = tm // halo
    n_halo_blocks = n // halo
    ep_in, out_shape, out_specs = _epilogue_specs(b, n, d, tm)
    return pl.pallas_call(
        functools.partial(_pool_kernel, n=n),
        out_shape=out_shape,
        grid=(b, n // tm),
        in_specs=[
            pl.BlockSpec((1, tm, d), lambda i, j: (i, j, 0)),
            pl.BlockSpec((1, halo, d), lambda i, j: (i, jnp.maximum(j * blocks_per_tile - 1, 0), 0)),
            pl.BlockSpec((1, halo, d),
                         lambda i, j: (i, jnp.minimum((j + 1) * blocks_per_tile, n_halo_blocks - 1), 0)),
            pl.BlockSpec((len(POOL_SIZES), POOL_GROUP, POOL_GROUP), lambda i, j: (0, 0, 0)),
            pl.BlockSpec((1, d), lambda i, j: (0, 0)),
        ] + ep_in,
        out_specs=out_specs,
        compiler_params=_cparams("arbitrary", "arbitrary"),
        name="pool_mixer",
    )(h, h, h, pool_w_bf, pool_scale.reshape(1, d), m, ln, *router)


def _expert_kernel(te_ref, tv_ref, x_ref, wg_ref, wu_ref, wd_ref, buf_ref, o_ref, wg_s, wu_s, wd_s):
    del buf_ref
    j = pl.program_id(0)
    prev_e = te_ref[jnp.maximum(j - 1, 0)]

    @pl.when((j == 0) | (te_ref[j] != prev_e))
    def _():
        wg_s[...] = wg_ref[0, 0].astype(BF16)
        wu_s[...] = wu_ref[0, 0].astype(BF16)
        wd_s[...] = wd_ref[0, 0].astype(BF16)

    @pl.when(tv_ref[j] == 1)
    def _():
        x = x_ref[...]
        gate = _dot(x, wg_s[...])
        up = _dot(x, wu_s[...])
        act = (_silu(gate) * up).astype(BF16)
        o_ref[...] = _dot(act, wd_s[...]).astype(BF16)

    @pl.when(tv_ref[j] == 0)
    def _():
        o_ref[...] = jnp.zeros_like(o_ref)


def _experts(tile_expert, tile_valid, x_chunk, y_prev, w_gate, w_up, w_down, layer, chunk, n_chunks):
    pc, d = x_chunk.shape
    tm = MOE_TILE
    tiles = pc // tm
    hid = w_gate.shape[-1]
    in_specs = [
        pl.BlockSpec((tm, d), lambda j, te, tv: (j, 0)),
        pl.BlockSpec((1, 1, d, hid), lambda j, te, tv: (layer, te[j], 0, 0)),
        pl.BlockSpec((1, 1, d, hid), lambda j, te, tv: (layer, te[j], 0, 0)),
        pl.BlockSpec((1, 1, hid, d), lambda j, te, tv: (layer, te[j], 0, 0)),
    ]
    in_specs.append(pl.BlockSpec(memory_space=pl.ANY))
    args = [tile_expert, tile_valid, x_chunk, w_gate, w_up, w_down, y_prev]
    aliases = {len(args) - 1: 0}
    grid_spec = pltpu.PrefetchScalarGridSpec(
        num_scalar_prefetch=2,
        grid=(tiles,),
        in_specs=in_specs,
        out_specs=pl.BlockSpec((tm, d), lambda j, te, tv: (chunk * tiles + j, 0)),
        scratch_shapes=[pltpu.VMEM((d, hid), BF16), pltpu.VMEM((d, hid), BF16), pltpu.VMEM((hid, d), BF16)],
    )
    return pl.pallas_call(
        _expert_kernel,
        out_shape=jax.ShapeDtypeStruct((pc * n_chunks, d), BF16),
        grid_spec=grid_spec,
        input_output_aliases=aliases,
        compiler_params=_cparams("arbitrary"),
        name="experts",
    )(*args)


ASSIGN_BITS = 16


def _dispatch_plan(route, tm):
    tok = route.shape[0]
    n_assign = 2 * tok
    n_pad = MOE_EXPERTS * tm
    p = n_assign + n_pad
    n_tiles = p // tm
    assert n_assign <= 1 << ASSIGN_BITS
    i32 = jnp.int32
    low_mask = (1 << ASSIGN_BITS) - 1
    eid = jnp.transpose(route[:, 2:4]).astype(i32).reshape(n_assign)
    experts = jnp.arange(MOE_EXPERTS, dtype=i32)
    counts = jnp.sum((eid[:, None] == experts[None, :]).astype(i32), axis=0)
    padded = ((counts + tm - 1) // tm) * tm
    pad_end = jnp.cumsum(padded)
    tile_row = jnp.arange(n_tiles, dtype=i32) * tm
    tile_expert = jnp.minimum(jnp.sum((pad_end[None, :] <= tile_row[:, None]).astype(i32), axis=1), MOE_EXPERTS - 1)
    tile_valid = (tile_row < pad_end[-1]).astype(i32)
    need_end = jnp.cumsum(padded - counts)
    pad_id = jnp.arange(n_pad, dtype=i32)
    pad_expert = jnp.sum((need_end[None, :] <= pad_id[:, None]).astype(i32), axis=1)
    packed = jnp.concatenate([((2 * eid) << ASSIGN_BITS) | jnp.arange(n_assign, dtype=i32),
                              ((2 * pad_expert + 1) << ASSIGN_BITS) | (pad_id & low_mask)])
    packed = lax.sort(packed)
    entry = packed & low_mask
    is_real = ((packed >> ASSIGN_BITS) & 1) == 0
    tok_of_pos = entry % tok
    _, pos = lax.sort((jnp.where(is_real, entry, n_assign), jnp.arange(p, dtype=i32)), num_keys=1)
    return tile_expert, tile_valid, tok_of_pos, pos[:n_assign]


def _n_chunks(b):
    return 4 if b % 4 == 0 else (2 if b % 2 == 0 else 1)


def _post_norm_kernel(h_ref, y0_ref, y1_ref, r_ref, m_ref, ln_ref, o_ref):
    gate = m_ref[0, 5:6, :]
    r = r_ref[0]
    y = r[:, 0:1] * y0_ref[...].astype(F32) + r[:, 1:2] * y1_ref[...].astype(F32)
    o_ref[0] = _layer_norm(DEEPNORM_ALPHA * h_ref[0] + gate * y, ln_ref[0:1, :], ln_ref[1:2, :])


def _post_norm(h, y2, route, m, ln, chunk, n_chunks):
    b, n, d = h.shape
    tm = ROW_TILE
    bc = b // n_chunks
    b0 = chunk * bc
    tiles_per_seq = n // tm
    tiles_per_k = bc * tiles_per_seq
    row_spec = pl.BlockSpec((1, tm, d), lambda i, j: (b0 + i, j, 0))
    return pl.pallas_call(
        _post_norm_kernel,
        out_shape=jax.ShapeDtypeStruct((b, n, d), F32),
        grid=(bc, tiles_per_seq),
        in_specs=[row_spec,
                  pl.BlockSpec((tm, d), lambda i, j: (i * tiles_per_seq + j, 0)),
                  pl.BlockSpec((tm, d), lambda i, j: (tiles_per_k + i * tiles_per_seq + j, 0)),
                  pl.BlockSpec((1, tm, LANES), lambda i, j: (b0 + i, j, 0)),
                  pl.BlockSpec((1, N_MOD, d), lambda i, j: (b0 + i, 0, 0)),
                  pl.BlockSpec((2, d), lambda i, j: (0, 0))],
        out_specs=row_spec,
        input_output_aliases={0: 0},
        compiler_params=_cparams("arbitrary", "arbitrary"),
        name="post_norm",
    )(h, y2, y2, route, m, ln)


def _moe_rows(tok):
    return 2 * tok + MOE_EXPERTS * MOE_TILE


def _moe_post_norm(h1, t, route, m, ln, w_gate, w_up, w_down, layer, row_buf):
    b, n, d = t.shape
    tok = b * n
    nch = _n_chunks(b)
    tile_expert, tile_valid, tok_of_pos, pos_of_assign = _dispatch_plan(route.reshape(tok, LANES), MOE_TILE)
    t2 = t.reshape(tok, d)
    tiles = tile_expert.shape[0] // nch
    pc = tiles * MOE_TILE
    y_sorted = row_buf
    for c in range(nch):
        x_c = t2.at[tok_of_pos[c * pc:(c + 1) * pc]].get(mode="promise_in_bounds")
        y_sorted = _experts(tile_expert[c * tiles:(c + 1) * tiles], tile_valid[c * tiles:(c + 1) * tiles],
                            x_c, y_sorted, w_gate, w_up, w_down, layer, c, nch)
    bc = b // nch
    pos3 = pos_of_assign.reshape(2, b, n)
    h = h1
    for c in range(nch):
        pos_c = pos3[:, c * bc:(c + 1) * bc].reshape(2 * bc * n)
        y2_c = y_sorted.at[pos_c].get(mode="promise_in_bounds")
        h = _post_norm(h, y2_c, route, m, ln, c, nch)
    return h, y_sorted


def _rope_tables(n):
    t = jnp.arange(n)
    rows = (t // GRID_W).astype(F32)
    cols = (t % GRID_W).astype(F32)
    n_freq = RET_DK // 4
    inv_freq = ROPE_BASE ** (-jnp.arange(n_freq, dtype=F32) / n_freq)
    ang = jnp.concatenate([rows[:, None] * inv_freq, cols[:, None] * inv_freq], axis=-1)
    cos, sin = jnp.cos(ang), jnp.sin(ang)
    cos2 = jnp.concatenate([cos, cos], axis=-1)
    sin2 = jnp.concatenate([-sin, sin], axis=-1)
    q_scale = RET_DK ** -0.5
    return cos2 * q_scale, sin2 * q_scale, cos2, sin2


def _router_params(w_r1, b_r1, w_r2, b_r2):
    d = w_r1.shape[0]
    w2 = jnp.transpose(w_r2, (1, 0, 2)).reshape(d, MOE_EXPERTS)
    pad = LANES - MOE_GROUPS - MOE_EXPERTS
    wr = jnp.concatenate([w_r1, w2, jnp.zeros((d, pad), F32)], axis=-1)
    br = jnp.concatenate([b_r1, b_r2.reshape(MOE_EXPERTS), jnp.zeros((pad,), F32)]).reshape(1, LANES)
    return wr, br


def kernel(x, c, ctx, c_ctx, w_mod, b_mod, ln_g, ln_b, ab_w_in, ab_w_out, ab_log_decay, ab_rpb, pool_w, pool_scale, moe_w_r1, moe_b_r1, moe_w_r2, moe_b_r2, moe_w_gate, moe_w_up, moe_w_down):
    b, n, d = x.shape
    cc = jnp.concatenate([c, c_ctx[None, :], jnp.zeros((MOD_ROWS - b - 1, d), F32)], axis=0)
    mod = _modulation(cc, w_mod, b_mod)
    h = x
    row_buf = None
    for i in range(DEPTH):
        j = i // 2
        m = mod[i, :b].reshape(b, N_MOD, d)
        ln1 = jnp.stack([ln_g[i, 0], ln_b[i, 0]])
        ln2 = jnp.stack([ln_g[i, 1], ln_b[i, 1]])
        router = _router_params(moe_w_r1[i], moe_b_r1[i], moe_w_r2[i], moe_b_r2[i])
        if i % 2 == 0:
            m_ctx = mod[i, b].reshape(1, N_MOD, d)
            w_in_bf = ab_w_in[j].astype(BF16)
            log_gamma2 = jnp.log1p(-jnp.exp(ab_log_decay[j].astype(F32)))
            rk, rv, nk, nv, rq, rg, nq, row_buf = _in_proj(h, m, w_in_bf, _rope_tables(n), _moe_rows(b * n))
            rk_c, rv_c, nk_c, nv_c = _ctx_proj(ctx, m_ctx, w_in_bf[:, :N_KV_GROUPS * HEAD_W])
            y_ret = _retention(log_gamma2, rq, rk, rv, rg, rk_c, rv_c)
            y_na = _neighbourhood_attention(nq, nk, nv, nk_c, nv_c, ab_rpb[j])
            h1, t, route = _out_proj(h, y_ret, y_na, ab_w_out[j].astype(BF16), m, ln1, router)
        else:
            h1, t, route = _pool_mixer(h, pool_w[j].astype(BF16), pool_scale[j], m, ln1, router)
        h, row_buf = _moe_post_norm(h1, t, route, m, ln2, moe_w_gate, moe_w_up, moe_w_down, i, row_buf)
    return h
```

```python
import functools

import numpy as np
import jax
import jax.numpy as jnp
from jax import lax
from jax.experimental import pallas as pl
from jax.experimental.pallas import tpu as pltpu

F32 = jnp.float32
BF16 = jnp.bfloat16
HIGHEST = lax.Precision.HIGHEST

D_MODEL = 1024
DEPTH = 2
GRID_W = 64
RET_HEADS = 4
RET_DK = 128
RET_CHUNK = 128
NA_HEADS = 8
NA_DH = 64
NA_WIN_R = 8
NA_WIN_C = 16
N_BIAS_ROWS = 2 * NA_WIN_R - 1
POOL_SIZES = (2, 4, 8, 16)
POOL_GROUP = D_MODEL // len(POOL_SIZES)
MOE_GROUPS = 4
MOE_PER_GROUP = 8
MOE_EXPERTS = MOE_GROUPS * MOE_PER_GROUP
MOE_HIDDEN = D_MODEL // 2
ROPE_BASE = 10000.0
LN_EPS = 1e-5
N_MOD = 6
DEEPNORM_ALPHA = (2 * DEPTH) ** 0.25
HEAD_W = 512
N_IN_GROUPS = 7
N_KV_GROUPS = 4
LOG2E = float(np.log2(np.e))
MASK_VALUE = -1e30

LANES = 128
VMEM_LIMIT = 56 * 1024 * 1024

NA_ROWS_PER_BLOCK = 4
ROW_TILE = 512
EPILOGUE_TILE = 1024
MOE_TILE = 512
MOD_ROWS = 24
MOD_COL_TILE = 1536


def _cparams(*sem):
    return pltpu.CompilerParams(dimension_semantics=sem, vmem_limit_bytes=VMEM_LIMIT)


def _silu(v):
    return v / (1.0 + jnp.exp(-v))


def _dot(a, b):
    return jnp.dot(a, b, preferred_element_type=F32)


def _dot_nt(a, b):
    return lax.dot_general(a, b, (((1,), (1,)), ((), ())), preferred_element_type=F32)


def _dot_tn(a, b):
    return lax.dot_general(a, b, (((0,), (0,)), ((), ())), preferred_element_type=F32)


def _mod_kernel(c_ref, w_ref, b_ref, o_ref):
    s = _silu(c_ref[...])
    o_ref[0] = jnp.dot(s, w_ref[0], precision=HIGHEST, preferred_element_type=F32) + b_ref[0]


def _modulation(cc, w_mod, b_mod):
    depth, d, n = w_mod.shape
    return pl.pallas_call(
        _mod_kernel,
        out_shape=jax.ShapeDtypeStruct((depth, MOD_ROWS, n), F32),
        grid=(depth, n // MOD_COL_TILE),
        in_specs=[
            pl.BlockSpec((MOD_ROWS, d), lambda i, j: (0, 0)),
            pl.BlockSpec((1, d, MOD_COL_TILE), lambda i, j: (i, 0, j)),
            pl.BlockSpec((1, 1, MOD_COL_TILE), lambda i, j: (i, 0, j)),
        ],
        out_specs=pl.BlockSpec((1, MOD_ROWS, MOD_COL_TILE), lambda i, j: (i, 0, j)),
        compiler_params=_cparams("arbitrary", "arbitrary"),
        name="modulation",
    )(cc, w_mod, b_mod.reshape(depth, 1, n))


def _rope(v, cos2, sin2):
    return v * cos2 + pltpu.roll(v, RET_DK // 2, axis=1) * sin2


def _in_proj_kernel(x_ref, m_ref, w_ref, cq_ref, sq_ref, ck_ref, sk_ref,
                    rk_ref, rv_ref, nk_ref, nv_ref, rq_ref, rg_ref, nq_ref, buf_ref):
    buf_ref[...] = jnp.zeros_like(buf_ref)
    shift = m_ref[0, 0:1, :]
    scale = m_ref[0, 1:2, :]
    hm = (x_ref[0] * (1.0 + scale) + shift).astype(BF16)
    outs = (rk_ref, rv_ref, nk_ref, nv_ref, rq_ref, rg_ref, nq_ref)
    for g, o_ref in enumerate(outs):
        p = _dot(hm, w_ref[:, g * HEAD_W:(g + 1) * HEAD_W])
        if g == 0 or g == 4:
            cos2 = (ck_ref if g == 0 else cq_ref)[...]
            sin2 = (sk_ref if g == 0 else sq_ref)[...]
            for hd in range(RET_HEADS):
                sl = slice(hd * RET_DK, (hd + 1) * RET_DK)
                o_ref[0, :, sl] = _rope(p[:, sl], cos2, sin2).astype(BF16)
        elif g == 6:
            o_ref[0] = (p * (NA_DH ** -0.5 * LOG2E)).astype(BF16)
        else:
            o_ref[0] = p.astype(BF16)


def _in_proj(x, m, w_in_bf, rope_tabs, buf_rows):
    b, n, d = x.shape
    tm = ROW_TILE
    tiles_per_seq = n // tm
    buf_tile = buf_rows // (b * tiles_per_seq)
    assert buf_tile * b * tiles_per_seq == buf_rows and buf_tile % 16 == 0
    tab_spec = pl.BlockSpec((tm, RET_DK), lambda i, j: (j, 0))
    out_spec = pl.BlockSpec((1, tm, HEAD_W), lambda i, j: (i, j, 0))
    return pl.pallas_call(
        _in_proj_kernel,
        out_shape=[jax.ShapeDtypeStruct((b, n, HEAD_W), BF16)] * N_IN_GROUPS
        + [jax.ShapeDtypeStruct((buf_rows, d), BF16)],
        grid=(b, tiles_per_seq),
        in_specs=[
            pl.BlockSpec((1, tm, d), lambda i, j: (i, j, 0)),
            pl.BlockSpec((1, N_MOD, d), lambda i, j: (i, 0, 0)),
            pl.BlockSpec((d, N_IN_GROUPS * HEAD_W), lambda i, j: (0, 0)),
            tab_spec, tab_spec, tab_spec, tab_spec,
        ],
        out_specs=[out_spec] * N_IN_GROUPS + [pl.BlockSpec((buf_tile, d), lambda i, j: (i * tiles_per_seq + j, 0))],
        compiler_params=_cparams("arbitrary", "arbitrary"),
        name="in_proj",
    )(x, m, w_in_bf, *rope_tabs)


def _ctx_proj_kernel(x_ref, m_ref, w_ref, rk_ref, rv_ref, nk_ref, nv_ref):
    shift = m_ref[0, 0:1, :]
    scale = m_ref[0, 1:2, :]
    hm = (x_ref[0] * (1.0 + scale) + shift).astype(BF16)
    for g, o_ref in enumerate((rk_ref, rv_ref, nk_ref, nv_ref)):
        o_ref[0] = _dot(hm, w_ref[:, g * HEAD_W:(g + 1) * HEAD_W]).astype(BF16)


def _ctx_proj(ctx, m_ctx, w_in_bf):
    b, l, d = ctx.shape
    out_spec = pl.BlockSpec((1, l, HEAD_W), lambda i: (i, 0, 0))
    return pl.pallas_call(
        _ctx_proj_kernel,
        out_shape=[jax.ShapeDtypeStruct((b, l, HEAD_W), BF16)] * N_KV_GROUPS,
        grid=(b,),
        in_specs=[
            pl.BlockSpec((1, l, d), lambda i: (i, 0, 0)),
            pl.BlockSpec((1, N_MOD, d), lambda i: (0, 0, 0)),
            pl.BlockSpec((d, N_KV_GROUPS * HEAD_W), lambda i: (0, 0)),
        ],
        out_specs=[out_spec] * N_KV_GROUPS,
        compiler_params=_cparams("arbitrary"),
        name="ctx_proj",
    )(ctx, m_ctx, w_in_bf)


def _retention_kernel(lg_ref, q_ref, k_ref, v_ref, g_ref, kc_ref, vc_ref, o_ref, u_ref, s_ref):
    hd = pl.program_id(1)
    n = q_ref.shape[1]
    c = RET_CHUNK
    dk = RET_DK
    nc = n // c
    l = kc_ref.shape[1]
    lgf = lg_ref[0, hd]
    lgb = lg_ref[1, hd]

    ii = lax.broadcasted_iota(jnp.int32, (c, c), 0).astype(F32)
    jj = lax.broadcasted_iota(jnp.int32, (c, c), 1).astype(F32)
    diff = ii - jj
    decay = (jnp.where(diff >= 0, jnp.exp(lgf * jnp.maximum(diff, 0.0)), 0.0)
             + jnp.where(diff <= 0, jnp.exp(lgb * jnp.maximum(-diff, 0.0)), 0.0))
    idx = lax.broadcasted_iota(jnp.int32, (c, 1), 0).astype(F32)
    q_dec_f = jnp.exp(lgf * (idx + 1.0))
    k_dec_f = jnp.exp(lgf * (c - 1.0 - idx))
    q_dec_b = jnp.exp(lgb * (c - idx))
    k_dec_b = jnp.exp(lgb * idx)
    ones = jnp.ones((1, dk), F32)
    chunk_dec_f = jnp.exp(ones * (lgf * c))
    chunk_dec_b = jnp.exp(ones * (lgb * c))

    pos = lax.broadcasted_iota(jnp.int32, (l, 1), 0).astype(F32)
    kc = kc_ref[0].astype(F32)
    vc = vc_ref[0]
    s_f0 = _dot_tn((kc * jnp.exp(lgf * (l - 1.0 - pos))).astype(BF16), vc)
    s_b0 = _dot_tn((kc * jnp.exp(lgb * pos)).astype(BF16), vc)

    def chunk_rows(i):
        return pl.ds(pl.multiple_of(i * c, c), c)

    def kv_step(i, carry):
        rows = chunk_rows(i)
        k_i = k_ref[0, rows, :].astype(F32)
        kk = jnp.concatenate([(k_i * k_dec_f).astype(BF16), (k_i * k_dec_b).astype(BF16)], axis=1)
        u_ref[i] = _dot_tn(kk, v_ref[0, rows, :])
        return carry

    lax.fori_loop(0, nc, kv_step, 0, unroll=True)

    def scan_f(i, s):
        s_ref[i, :, 0:dk] = s.astype(BF16)
        return s * chunk_dec_f + u_ref[i, 0:dk, :]

    lax.fori_loop(0, nc, scan_f, s_f0, unroll=True)

    def scan_b(t, s):
        i = nc - 1 - t
        s_ref[i, :, dk:2 * dk] = s.astype(BF16)
        return s * chunk_dec_b + u_ref[i, dk:2 * dk, :]

    lax.fori_loop(0, nc, scan_b, s_b0, unroll=True)

    def out_step(i, carry):
        rows = chunk_rows(i)
        q_i = q_ref[0, rows, :]
        v_i = v_ref[0, rows, :]
        att = (_dot_nt(q_i, k_ref[0, rows, :]) * decay).astype(BF16)
        inter = _dot(q_i, s_ref[i])
        o = _dot(att, v_i) + inter[:, 0:dk] * q_dec_f + inter[:, dk:2 * dk] * q_dec_b
        mu = jnp.mean(o, axis=-1, keepdims=True)
        var = jnp.mean(jnp.square(o - mu), axis=-1, keepdims=True)
        o_n = (o - mu) * lax.rsqrt(var + LN_EPS)
        gate = g_ref[0, rows, :].astype(F32)
        o_ref[0, rows, :] = (_silu(gate) * o_n).astype(BF16)
        return carry

    lax.fori_loop(0, nc, out_step, 0, unroll=True)


def _retention(log_gamma2, rq, rk, rv, rg, rk_c, rv_c):
    b, n, _ = rq.shape
    l = rk_c.shape[1]
    nc = n // RET_CHUNK
    seq_spec = pl.BlockSpec((1, n, RET_DK), lambda i, h: (i, 0, h))
    ctx_spec = pl.BlockSpec((1, l, RET_DK), lambda i, h: (i, 0, h))
    return pl.pallas_call(
        _retention_kernel,
        out_shape=jax.ShapeDtypeStruct((b, n, HEAD_W), BF16),
        grid=(b, RET_HEADS),
        in_specs=[pl.BlockSpec(memory_space=pltpu.SMEM),
                  seq_spec, seq_spec, seq_spec, seq_spec, ctx_spec, ctx_spec],
        out_specs=seq_spec,
        scratch_shapes=[pltpu.VMEM((nc, 2 * RET_DK, RET_DK), F32),
                        pltpu.VMEM((nc, RET_DK, 2 * RET_DK), BF16)],
        compiler_params=_cparams("arbitrary", "arbitrary"),
        name="retention",
    )(log_gamma2, rq, rk, rv, rg, rk_c, rv_c)


def _na_geometry(rows):
    rb = NA_ROWS_PER_BLOCK
    wr = min(NA_WIN_R, rows)
    key_rows = min(rows, rb + wr - 1)
    variants, block_variant, block_start = [], [], []
    for kb in range(rows // rb):
        q_rows = kb * rb + np.arange(rb)
        r0 = np.clip(q_rows - wr // 2, 0, rows - wr)
        ks = int(np.clip(r0.min(), 0, rows - key_rows))
        assert r0.max() + wr <= ks + key_rows
        kr = ks + np.arange(key_rows)
        valid = (kr[None, :] >= r0[:, None]) & (kr[None, :] < r0[:, None] + wr)
        ridx = np.where(valid, kr[None, :] - q_rows[:, None] + NA_WIN_R - 1, N_BIAS_ROWS).astype(np.int32)
        for vi, v in enumerate(variants):
            if np.array_equal(v, ridx):
                block_variant.append(vi)
                break
        else:
            variants.append(ridx)
            block_variant.append(len(variants) - 1)
        block_start.append(ks)
    return key_rows, np.stack(variants), block_variant, block_start


def _na_bias_table(rpb, variant_rows):
    n_heads = rpb.shape[0]
    qc = np.arange(GRID_W)[:, None]
    kc = np.arange(GRID_W)[None, :]
    wstart = np.clip(qc - NA_WIN_C // 2, 0, GRID_W - NA_WIN_C)
    col_ok = (kc >= wstart) & (kc < wstart + NA_WIN_C)
    cidx = np.clip(kc - qc + NA_WIN_C - 1, 0, 2 * NA_WIN_C - 2)
    onehot = ((cidx[None] == np.arange(2 * NA_WIN_C - 1)[:, None, None]) & col_ok[None]).astype(np.float32)
    blocks = jnp.einsum("hrd,dqk->hqrk", rpb.astype(F32) * LOG2E, jnp.asarray(onehot), precision=HIGHEST)
    blocks = jnp.where(col_ok[None, :, None, :], blocks, MASK_VALUE)
    masked = jnp.full((n_heads, GRID_W, GRID_W), MASK_VALUE, F32)
    nvar, rb, key_rows = variant_rows.shape
    rows = [jnp.concatenate([masked if r == N_BIAS_ROWS else blocks[:, :, r, :] for r in variant_rows[v, a]], axis=-1)
            for v in range(nvar) for a in range(rb)]
    return jnp.stack(rows, axis=1).reshape(n_heads, nvar, rb * GRID_W, key_rows * GRID_W)


def _na_kernel(q_ref, k_ref, v_ref, kc_ref, vc_ref, bias_ref, o_ref, *, block_variant, block_start, key_rows):
    qb = NA_ROWS_PER_BLOCK * GRID_W
    nk = key_rows * GRID_W
    first_head = lax.broadcasted_iota(jnp.int32, (qb, LANES), 1) < NA_DH
    kc = kc_ref[0]
    vc = vc_ref[0]
    for kb, (var, ks) in enumerate(zip(block_variant, block_start)):
        q = q_ref[0, kb * qb:(kb + 1) * qb, :]
        k_win = k_ref[0, ks * GRID_W:ks * GRID_W + nk, :]
        v_win = v_ref[0, ks * GRID_W:ks * GRID_W + nk, :]
        zero = jnp.zeros_like(q)
        q2 = jnp.concatenate([jnp.where(first_head, q, zero), jnp.where(first_head, zero, q)], axis=0)
        s_win = _dot_nt(q2, k_win) + jnp.concatenate([bias_ref[0, var], bias_ref[1, var]], axis=0)
        s_ctx = _dot_nt(q2, kc)
        m = jnp.maximum(jnp.max(s_win, axis=-1, keepdims=True), jnp.max(s_ctx, axis=-1, keepdims=True))
        p_win = jnp.exp2(s_win - m)
        p_ctx = jnp.exp2(s_ctx - m)
        denom = jnp.sum(p_win, axis=-1, keepdims=True) + jnp.sum(p_ctx, axis=-1, keepdims=True)
        o = (_dot(p_win.astype(BF16), v_win) + _dot(p_ctx.astype(BF16), vc)) / denom
        o_ref[0, kb * qb:(kb + 1) * qb, :] = jnp.where(first_head, o[0:qb], o[qb:2 * qb]).astype(BF16)


def _neighbourhood_attention(nq, nk, nv, nk_c, nv_c, rpb):
    b, n, _ = nq.shape
    l = nk_c.shape[1]
    rows = n // GRID_W
    key_rows, variant_rows, block_variant, block_start = _na_geometry(rows)
    bias = _na_bias_table(rpb, variant_rows)
    nvar = variant_rows.shape[0]
    qb = NA_ROWS_PER_BLOCK * GRID_W
    nkeys = key_rows * GRID_W
    seq_spec = pl.BlockSpec((1, n, LANES), lambda h, i: (i, 0, h))
    ctx_spec = pl.BlockSpec((1, l, LANES), lambda h, i: (i, 0, h))
    kern = functools.partial(_na_kernel, block_variant=tuple(block_variant), block_start=tuple(block_start),
                             key_rows=key_rows)
    return pl.pallas_call(
        kern,
        out_shape=jax.ShapeDtypeStruct((b, n, HEAD_W), BF16),
        grid=(NA_HEADS // 2, b),
        in_specs=[seq_spec, seq_spec, seq_spec, ctx_spec, ctx_spec,
                  pl.BlockSpec((2, nvar, qb, nkeys), lambda h, i: (h, 0, 0, 0))],
        out_specs=seq_spec,
        compiler_params=_cparams("arbitrary", "arbitrary"),
        name="neighbourhood_attention",
    )(nq, nk, nv, nk_c, nv_c, bias)


def _layer_norm(z, g, b):
    mu = jnp.mean(z, axis=-1, keepdims=True)
    var = jnp.mean(jnp.square(z - mu), axis=-1, keepdims=True)
    return (z - mu) * lax.rsqrt(var + LN_EPS) * g + b


def _split_bf16(v):
    hi = v.astype(BF16)
    return hi, (v - hi.astype(F32)).astype(BF16)


def _route(t, wr_ref, br_ref):
    t_hi, t_lo = _split_bf16(t)
    w_hi, w_lo = _split_bf16(wr_ref[...])
    hi = _dot(t_hi, jnp.concatenate([w_hi, w_lo], axis=1))
    logits = hi[:, 0:LANES] + (_dot(t_lo, w_hi) + hi[:, LANES:2 * LANES]) + br_ref[...]
    lane = lax.broadcasted_iota(jnp.int32, logits.shape, 1).astype(F32)
    neg = -jnp.inf
    big = float(LANES)

    def first_max(vals):
        vmax = jnp.max(vals, axis=-1, keepdims=True)
        return vmax, jnp.min(jnp.where(vals == vmax, lane, big), axis=-1, keepdims=True)

    is_grp = lane < MOE_GROUPS
    g_max, grp = first_max(jnp.where(is_grp, logits, neg))
    g_sum = jnp.sum(jnp.where(is_grp, jnp.exp(logits - g_max), 0.0), axis=-1, keepdims=True)
    gate_g = 1.0 / g_sum
    lo = MOE_GROUPS + grp * MOE_PER_GROUP
    in_grp = (lane >= lo) & (lane < lo + MOE_PER_GROUP)
    le = jnp.where(in_grp, logits, neg)
    v1, i1 = first_max(le)
    v2, i2 = first_max(jnp.where(lane == i1, neg, le))
    e21 = jnp.exp(v2 - v1)
    w1 = gate_g / (1.0 + e21)
    w2 = gate_g * e21 / (1.0 + e21)
    return jnp.where(lane == 0, w1,
                     jnp.where(lane == 1, w2,
                               jnp.where(lane == 2, i1 - MOE_GROUPS,
                                         jnp.where(lane == 3, i2 - MOE_GROUPS, 0.0))))


def _post_norm_route(h, y, m_ref, ln_ref, wr_ref, br_ref, h_out, t_out, r_out):
    gate = m_ref[0, 2:3, :]
    shift = m_ref[0, 3:4, :]
    scale = m_ref[0, 4:5, :]
    h1 = _layer_norm(DEEPNORM_ALPHA * h + gate * y, ln_ref[0:1, :], ln_ref[1:2, :])
    t = h1 * (1.0 + scale) + shift
    h_out[0] = h1
    t_out[0] = t.astype(BF16)
    r_out[0] = _route(t, wr_ref, br_ref)


def _epilogue_specs(b, n, d, tm):
    in_specs = [
        pl.BlockSpec((1, N_MOD, d), lambda i, j: (i, 0, 0)),
        pl.BlockSpec((2, d), lambda i, j: (0, 0)),
        pl.BlockSpec((d, LANES), lambda i, j: (0, 0)),
        pl.BlockSpec((1, LANES), lambda i, j: (0, 0)),
    ]
    out_shape = [jax.ShapeDtypeStruct((b, n, d), F32), jax.ShapeDtypeStruct((b, n, d), BF16),
                 jax.ShapeDtypeStruct((b, n, LANES), F32)]
    out_specs = [pl.BlockSpec((1, tm, d), lambda i, j: (i, j, 0)),
                 pl.BlockSpec((1, tm, d), lambda i, j: (i, j, 0)),
                 pl.BlockSpec((1, tm, LANES), lambda i, j: (i, j, 0))]
    return in_specs, out_shape, out_specs


def _out_proj_kernel(x_ref, yr_ref, yn_ref, w_ref, m_ref, ln_ref, wr_ref, br_ref, h_out, t_out, r_out):
    y = _dot(yr_ref[0], w_ref[0:HEAD_W, :]) + _dot(yn_ref[0], w_ref[HEAD_W:2 * HEAD_W, :])
    _post_norm_route(x_ref[0], y, m_ref, ln_ref, wr_ref, br_ref, h_out, t_out, r_out)


def _out_proj(x, y_ret, y_na, w_out_bf, m, ln, router):
    b, n, d = x.shape
    tm = EPILOGUE_TILE
    ep_in, out_shape, out_specs = _epilogue_specs(b, n, d, tm)
    return pl.pallas_call(
        _out_proj_kernel,
        out_shape=out_shape,
        grid=(b, n // tm),
        in_specs=[
            pl.BlockSpec((1, tm, d), lambda i, j: (i, j, 0)),
            pl.BlockSpec((1, tm, HEAD_W), lambda i, j: (i, j, 0)),
            pl.BlockSpec((1, tm, HEAD_W), lambda i, j: (i, j, 0)),
            pl.BlockSpec((2 * HEAD_W, d), lambda i, j: (0, 0)),
        ] + ep_in,
        out_specs=out_specs,
        compiler_params=_cparams("arbitrary", "arbitrary"),
        name="out_proj",
    )(x, y_ret, y_na, w_out_bf, m, ln, *router)


POOL_HALO = max(POOL_SIZES) // 2
POOL_TILE = EPILOGUE_TILE


def _pool_kernel(x_ref, prev_ref, next_ref, pw_ref, ps_ref, m_ref, ln_ref, wr_ref, br_ref,
                 h_out, t_out, r_out, *, n):
    j = pl.program_id(1)
    nj = pl.num_programs(1)
    tm = x_ref.shape[1]
    halo = POOL_HALO
    shift = m_ref[0, 0:1, :]
    scale = m_ref[0, 1:2, :]
    x = x_ref[0]
    hm = x * (1.0 + scale) + shift
    prev = jnp.where(j > 0, prev_ref[0] * (1.0 + scale) + shift, 0.0)
    nxt = jnp.where(j < nj - 1, next_ref[0] * (1.0 + scale) + shift, 0.0)
    ext = jnp.concatenate([prev, hm, nxt], axis=0)
    pos = (j * tm + lax.broadcasted_iota(jnp.int32, (tm, 1), 0))
    ys = []
    for g, w in enumerate(POOL_SIZES):
        cols = slice(g * POOL_GROUP, (g + 1) * POOL_GROUP)
        s = ext[:, cols]
        span = 1
        while span < w:
            s = s[:s.shape[0] - span] + s[span:]
            span *= 2
        off = halo - w // 2
        win = s[off:off + tm]
        cnt = (jnp.minimum(pos + (w - w // 2), n) - jnp.maximum(pos - w // 2, 0)).astype(F32)
        z = (win / cnt - hm[:, cols]).astype(BF16)
        ys.append(_dot(z, pw_ref[g]))
    y = jnp.concatenate(ys, axis=-1) * ps_ref[...]
    _post_norm_route(x, y, m_ref, ln_ref, wr_ref, br_ref, h_out, t_out, r_out)


def _pool_mixer(h, pool_w_bf, pool_scale, m, ln, router):
    b, n, d = h.shape
    tm = POOL_TILE
    halo = POOL_HALO
    blocks_per_tile = tm // halo
    n_halo_blocks = n // halo
    ep_in, out_shape, out_specs = _epilogue_specs(b, n, d, tm)
    return pl.pallas_call(
        functools.partial(_pool_kernel, n=n),
        out_shape=out_shape,
        grid=(b, n // tm),
        in_specs=[
            pl.BlockSpec((1, tm, d), lambda i, j: (i, j, 0)),
            pl.BlockSpec((1, halo, d), lambda i, j: (i, jnp.maximum(j * blocks_per_tile - 1, 0), 0)),
            pl.BlockSpec((1, halo, d),
                         lambda i, j: (i, jnp.minimum((j + 1) * blocks_per_tile, n_halo_blocks - 1), 0)),
            pl.BlockSpec((len(POOL_SIZES), POOL_GROUP, POOL_GROUP), lambda i, j: (0, 0, 0)),
            pl.BlockSpec((1, d), lambda i, j: (0, 0)),
        ] + ep_in,
        out_specs=out_specs,
        compiler_params=_cparams("arbitrary", "arbitrary"),
        name="pool_mixer",
    )(h, h, h, pool_w_bf, pool_scale.reshape(1, d), m, ln, *router)


def _expert_kernel(tile_ref, exp_ref, lo_ref, hi_ref, x_ref, wg_ref, wu_ref, wd_ref, buf_ref, o_ref,
                   wg_s, wu_s, wd_s):
    del buf_ref
    g = pl.program_id(0)
    prev_e = exp_ref[jnp.maximum(g - 1, 0)]

    @pl.when((g == 0) | (exp_ref[g] != prev_e))
    def _():
        wg_s[...] = wg_ref[0, 0].astype(BF16)
        wu_s[...] = wu_ref[0, 0].astype(BF16)
        wd_s[...] = wd_ref[0, 0].astype(BF16)

    lo = lo_ref[g]
    hi = hi_ref[g]

    def expert_rows():
        x = x_ref[...]
        gate = _dot(x, wg_s[...])
        up = _dot(x, wu_s[...])
        act = (_silu(gate) * up).astype(BF16)
        return _dot(act, wd_s[...]).astype(BF16)

    @pl.when((hi > lo) & (lo == 0))
    def _():
        row = lax.broadcasted_iota(jnp.int32, (o_ref.shape[0], 1), 0)
        o_ref[...] = jnp.where(row < hi, expert_rows(), jnp.zeros((), BF16))

    @pl.when((hi > lo) & (lo > 0))
    def _():
        row = lax.broadcasted_iota(jnp.int32, (o_ref.shape[0], 1), 0)
        o_ref[...] = jnp.where((row >= lo) & (row < hi), expert_rows(), o_ref[...])


def _experts(steps, x_chunk, y_prev, w_gate, w_up, w_down, layer, chunk):
    pc, d = x_chunk.shape
    tm = MOE_TILE
    tiles = pc // tm
    hid = w_gate.shape[-1]
    n_steps = steps[0].shape[0]
    w_spec = lambda shape: pl.BlockSpec(shape, lambda g, tile, exp, lo, hi: (layer, exp[g], 0, 0))
    in_specs = [
        pl.BlockSpec((tm, d), lambda g, tile, exp, lo, hi: (tile[g], 0)),
        w_spec((1, 1, d, hid)), w_spec((1, 1, d, hid)), w_spec((1, 1, hid, d)),
        pl.BlockSpec(memory_space=pl.ANY),
    ]
    args = [*steps, x_chunk, w_gate, w_up, w_down, y_prev]
    grid_spec = pltpu.PrefetchScalarGridSpec(
        num_scalar_prefetch=4,
        grid=(n_steps,),
        in_specs=in_specs,
        out_specs=pl.BlockSpec((tm, d), lambda g, tile, exp, lo, hi: (chunk * tiles + tile[g], 0)),
        scratch_shapes=[pltpu.VMEM((d, hid), BF16), pltpu.VMEM((d, hid), BF16), pltpu.VMEM((hid, d), BF16)],
    )
    return pl.pallas_call(
        _expert_kernel,
        out_shape=jax.ShapeDtypeStruct(y_prev.shape, BF16),
        grid_spec=grid_spec,
        input_output_aliases={len(args) - 1: 0},
        compiler_params=_cparams("arbitrary"),
        name="experts",
    )(*args)


ASSIGN_BITS = 16


def _dispatch_plan(route, tm, n_chunks):
    tok = route.shape[0]
    n_assign = 2 * tok
    assert n_assign <= 1 << ASSIGN_BITS
    i32 = jnp.int32
    low_mask = (1 << ASSIGN_BITS) - 1
    eid = jnp.transpose(route[:, 2:4]).astype(i32).reshape(n_assign)
    experts = jnp.arange(MOE_EXPERTS, dtype=i32)
    counts = jnp.sum((eid[:, None] == experts[None, :]).astype(i32), axis=0)
    end = jnp.cumsum(counts)
    start = end - counts
    order = lax.sort((eid << ASSIGN_BITS) | jnp.arange(n_assign, dtype=i32)) & low_mask
    tok_of_pos = order % tok
    _, pos_of_assign = lax.sort((order, jnp.arange(n_assign, dtype=i32)), num_keys=1)
    rows_per_chunk = n_assign // n_chunks
    tiles_per_chunk = rows_per_chunk // tm
    chunk_lo = (jnp.arange(n_chunks, dtype=i32) * rows_per_chunk)[:, None]
    tile_starts = chunk_lo + jnp.arange(tiles_per_chunk, dtype=i32)[None, :] * tm
    cuts = jnp.concatenate([tile_starts, jnp.clip(start[None, :], chunk_lo, chunk_lo + rows_per_chunk)], axis=1)
    cuts = jnp.sort(cuts, axis=1)
    nxt = jnp.concatenate([cuts[:, 1:], chunk_lo + rows_per_chunk], axis=1)
    tile = jnp.minimum((cuts - chunk_lo) // tm, tiles_per_chunk - 1)
    tile_row = chunk_lo + tile * tm
    lo = cuts - tile_row
    hi = nxt - tile_row
    first_row = jnp.minimum(cuts, chunk_lo + rows_per_chunk - 1)
    expert = jnp.sum((end[None, None, :] <= first_row[:, :, None]).astype(i32), axis=2)
    return (tile, expert, lo, hi), tok_of_pos, pos_of_assign


def _n_chunks(b):
    return 4 if b % 4 == 0 else (2 if b % 2 == 0 else 1)


def _post_norm_kernel(h_ref, y0_ref, y1_ref, r_ref, m_ref, ln_ref, o_ref):
    gate = m_ref[0, 5:6, :]
    r = r_ref[0]
    y = r[:, 0:1] * y0_ref[...].astype(F32) + r[:, 1:2] * y1_ref[...].astype(F32)
    o_ref[0] = _layer_norm(DEEPNORM_ALPHA * h_ref[0] + gate * y, ln_ref[0:1, :], ln_ref[1:2, :])


def _post_norm(h, y2, route, m, ln, chunk, n_chunks):
    b, n, d = h.shape
    tm = ROW_TILE
    bc = b // n_chunks
    b0 = chunk * bc
    tiles_per_seq = n // tm
    tiles_per_k = bc * tiles_per_seq
    row_spec = pl.BlockSpec((1, tm, d), lambda i, j: (b0 + i, j, 0))
    return pl.pallas_call(
        _post_norm_kernel,
        out_shape=jax.ShapeDtypeStruct((b, n, d), F32),
        grid=(bc, tiles_per_seq),
        in_specs=[row_spec,
                  pl.BlockSpec((tm, d), lambda i, j: (i * tiles_per_seq + j, 0)),
                  pl.BlockSpec((tm, d), lambda i, j: (tiles_per_k + i * tiles_per_seq + j, 0)),
                  pl.BlockSpec((1, tm, LANES), lambda i, j: (b0 + i, j, 0)),
                  pl.BlockSpec((1, N_MOD, d), lambda i, j: (b0 + i, 0, 0)),
                  pl.BlockSpec((2, d), lambda i, j: (0, 0))],
        out_specs=row_spec,
        input_output_aliases={0: 0},
        compiler_params=_cparams("arbitrary", "arbitrary"),
        name="post_norm",
    )(h, y2, y2, route, m, ln)


def _moe_rows(tok):
    return 2 * tok


def _moe_post_norm(h1, t, route, m, ln, w_gate, w_up, w_down, layer, row_buf):
    b, n, d = t.shape
    tok = b * n
    nch = _n_chunks(b)
    steps, tok_of_pos, pos_of_assign = _dispatch_plan(route.reshape(tok, LANES), MOE_TILE, nch)
    t2 = t.reshape(tok, d)
    pc = 2 * tok // nch
    y_sorted = row_buf
    for c in range(nch):
        x_c = t2.at[tok_of_pos[c * pc:(c + 1) * pc]].get(mode="promise_in_bounds")
        y_sorted = _experts(tuple(v[c] for v in steps), x_c, y_sorted, w_gate, w_up, w_down, layer, c)
    bc = b // nch
    pos3 = pos_of_assign.reshape(2, b, n)
    h = h1
    for c in range(nch):
        pos_c = pos3[:, c * bc:(c + 1) * bc].reshape(2 * bc * n)
        y2_c = y_sorted.at[pos_c].get(mode="promise_in_bounds")
        h = _post_norm(h, y2_c, route, m, ln, c, nch)
    return h, y_sorted


def _rope_tables(n):
    t = jnp.arange(n)
    rows = (t // GRID_W).astype(F32)
    cols = (t % GRID_W).astype(F32)
    n_freq = RET_DK // 4
    inv_freq = ROPE_BASE ** (-jnp.arange(n_freq, dtype=F32) / n_freq)
    ang = jnp.concatenate([rows[:, None] * inv_freq, cols[:, None] * inv_freq], axis=-1)
    cos, sin = jnp.cos(ang), jnp.sin(ang)
    cos2 = jnp.concatenate([cos, cos], axis=-1)
    sin2 = jnp.concatenate([-sin, sin], axis=-1)
    q_scale = RET_DK ** -0.5
    return cos2 * q_scale, sin2 * q_scale, cos2, sin2


def _router_params(w_r1, b_r1, w_r2, b_r2):
    d = w_r1.shape[0]
    w2 = jnp.transpose(w_r2, (1, 0, 2)).reshape(d, MOE_EXPERTS)
    pad = LANES - MOE_GROUPS - MOE_EXPERTS
    wr = jnp.concatenate([w_r1, w2, jnp.zeros((d, pad), F32)], axis=-1)
    br = jnp.concatenate([b_r1, b_r2.reshape(MOE_EXPERTS), jnp.zeros((pad,), F32)]).reshape(1, LANES)
    return wr, br


def kernel(x, c, ctx, c_ctx, w_mod, b_mod, ln_g, ln_b, ab_w_in, ab_w_out, ab_log_decay, ab_rpb, pool_w, pool_scale, moe_w_r1, moe_b_r1, moe_w_r2, moe_b_r2, moe_w_gate, moe_w_up, moe_w_down):
    b, n, d = x.shape
    cc = jnp.concatenate([c, c_ctx[None, :], jnp.zeros((MOD_ROWS - b - 1, d), F32)], axis=0)
    mod = _modulation(cc, w_mod, b_mod)
    h = x
    row_buf = None
    for i in range(DEPTH):
        j = i // 2
        m = mod[i, :b].reshape(b, N_MOD, d)
        ln1 = jnp.stack([ln_g[i, 0], ln_b[i, 0]])
        ln2 = jnp.stack([ln_g[i, 1], ln_b[i, 1]])
        router = _router_params(moe_w_r1[i], moe_b_r1[i], moe_w_r2[i], moe_b_r2[i])
        if i % 2 == 0:
            m_ctx = mod[i, b].reshape(1, N_MOD, d)
            w_in_bf = ab_w_in[j].astype(BF16)
            log_gamma2 = jnp.log1p(-jnp.exp(ab_log_decay[j].astype(F32)))
            rk, rv, nk, nv, rq, rg, nq, row_buf = _in_proj(h, m, w_in_bf, _rope_tables(n), _moe_rows(b * n))
            rk_c, rv_c, nk_c, nv_c = _ctx_proj(ctx, m_ctx, w_in_bf[:, :N_KV_GROUPS * HEAD_W])
            y_ret = _retention(log_gamma2, rq, rk, rv, rg, rk_c, rv_c)
            y_na = _neighbourhood_attention(nq, nk, nv, nk_c, nv_c, ab_rpb[j])
            h1, t, route = _out_proj(h, y_ret, y_na, ab_w_out[j].astype(BF16), m, ln1, router)
        else:
            h1, t, route = _pool_mixer(h, pool_w[j].astype(BF16), pool_scale[j], m, ln1, router)
        h, row_buf = _moe_post_norm(h1, t, route, m, ln2, moe_w_gate, moe_w_up, moe_w_down, i, row_buf)
    return h
```

```python
import functools

import numpy as np
import jax
import jax.numpy as jnp
from jax import lax
from jax.experimental import pallas as pl
from jax.experimental.pallas import tpu as pltpu

F32 = jnp.float32
BF16 = jnp.bfloat16
HIGHEST = lax.Precision.HIGHEST

D_MODEL = 1024
DEPTH = 2
GRID_W = 64
RET_HEADS = 4
RET_DK = 128
RET_CHUNK = 128
NA_HEADS = 8
NA_DH = 64
NA_WIN_R = 8
NA_WIN_C = 16
N_BIAS_ROWS = 2 * NA_WIN_R - 1
POOL_SIZES = (2, 4, 8, 16)
POOL_GROUP = D_MODEL // len(POOL_SIZES)
MOE_GROUPS = 4
MOE_PER_GROUP = 8
MOE_EXPERTS = MOE_GROUPS * MOE_PER_GROUP
MOE_HIDDEN = D_MODEL // 2
ROPE_BASE = 10000.0
LN_EPS = 1e-5
N_MOD = 6
DEEPNORM_ALPHA = (2 * DEPTH) ** 0.25
HEAD_W = 512
N_IN_GROUPS = 7
N_KV_GROUPS = 4
LOG2E = float(np.log2(np.e))
MASK_VALUE = -1e30

LANES = 128
VMEM_LIMIT = 56 * 1024 * 1024

NA_ROWS_PER_BLOCK = 4
ROW_TILE = 512
EPILOGUE_TILE = 1024
MOE_TILE = 512
MOE_SUB_TILE = 256
MOD_ROWS = 24
MOD_COL_TILE = 1536


def _cparams(*sem):
    return pltpu.CompilerParams(dimension_semantics=sem, vmem_limit_bytes=VMEM_LIMIT)


def _silu(v):
    return v / (1.0 + jnp.exp(-v))


def _dot(a, b):
    return jnp.dot(a, b, preferred_element_type=F32)


def _dot_nt(a, b):
    return lax.dot_general(a, b, (((1,), (1,)), ((), ())), preferred_element_type=F32)


def _dot_tn(a, b):
    return lax.dot_general(a, b, (((0,), (0,)), ((), ())), preferred_element_type=F32)


def _mod_kernel(c_ref, w_ref, b_ref, o_ref):
    s = _silu(c_ref[...])
    o_ref[0] = jnp.dot(s, w_ref[0], precision=HIGHEST, preferred_element_type=F32) + b_ref[0]


def _modulation(cc, w_mod, b_mod):
    depth, d, n = w_mod.shape
    return pl.pallas_call(
        _mod_kernel,
        out_shape=jax.ShapeDtypeStruct((depth, MOD_ROWS, n), F32),
        grid=(depth, n // MOD_COL_TILE),
        in_specs=[
            pl.BlockSpec((MOD_ROWS, d), lambda i, j: (0, 0)),
            pl.BlockSpec((1, d, MOD_COL_TILE), lambda i, j: (i, 0, j)),
            pl.BlockSpec((1, 1, MOD_COL_TILE), lambda i, j: (i, 0, j)),
        ],
        out_specs=pl.BlockSpec((1, MOD_ROWS, MOD_COL_TILE), lambda i, j: (i, 0, j)),
        compiler_params=_cparams("arbitrary", "arbitrary"),
        name="modulation",
    )(cc, w_mod, b_mod.reshape(depth, 1, n))


def _rope(v, cos2, sin2):
    return v * cos2 + pltpu.roll(v, RET_DK // 2, axis=1) * sin2


def _in_proj_kernel(x_ref, m_ref, w_ref, cq_ref, sq_ref, ck_ref, sk_ref,
                    rk_ref, rv_ref, nk_ref, nv_ref, rq_ref, rg_ref, nq_ref, buf_ref):
    buf_ref[...] = jnp.zeros_like(buf_ref)
    shift = m_ref[0, 0:1, :]
    scale = m_ref[0, 1:2, :]
    hm = (x_ref[0] * (1.0 + scale) + shift).astype(BF16)
    outs = (rk_ref, rv_ref, nk_ref, nv_ref, rq_ref, rg_ref, nq_ref)
    for g, o_ref in enumerate(outs):
        p = _dot(hm, w_ref[:, g * HEAD_W:(g + 1) * HEAD_W])
        if g == 0 or g == 4:
            cos2 = (ck_ref if g == 0 else cq_ref)[...]
            sin2 = (sk_ref if g == 0 else sq_ref)[...]
            for hd in range(RET_HEADS):
                sl = slice(hd * RET_DK, (hd + 1) * RET_DK)
                o_ref[0, :, sl] = _rope(p[:, sl], cos2, sin2).astype(BF16)
        elif g == 6:
            o_ref[0] = (p * (NA_DH ** -0.5 * LOG2E)).astype(BF16)
        else:
            o_ref[0] = p.astype(BF16)


def _in_proj(x, m, w_in_bf, rope_tabs, buf_rows):
    b, n, d = x.shape
    tm = ROW_TILE
    tiles_per_seq = n // tm
    buf_tile = buf_rows // (b * tiles_per_seq)
    assert buf_tile * b * tiles_per_seq == buf_rows and buf_tile % 16 == 0
    tab_spec = pl.BlockSpec((tm, RET_DK), lambda i, j: (j, 0))
    out_spec = pl.BlockSpec((1, tm, HEAD_W), lambda i, j: (i, j, 0))
    return pl.pallas_call(
        _in_proj_kernel,
        out_shape=[jax.ShapeDtypeStruct((b, n, HEAD_W), BF16)] * N_IN_GROUPS
        + [jax.ShapeDtypeStruct((buf_rows, d), BF16)],
        grid=(b, tiles_per_seq),
        in_specs=[
            pl.BlockSpec((1, tm, d), lambda i, j: (i, j, 0)),
            pl.BlockSpec((1, N_MOD, d), lambda i, j: (i, 0, 0)),
            pl.BlockSpec((d, N_IN_GROUPS * HEAD_W), lambda i, j: (0, 0)),
            tab_spec, tab_spec, tab_spec, tab_spec,
        ],
        out_specs=[out_spec] * N_IN_GROUPS + [pl.BlockSpec((buf_tile, d), lambda i, j: (i * tiles_per_seq + j, 0))],
        compiler_params=_cparams("arbitrary", "arbitrary"),
        name="in_proj",
    )(x, m, w_in_bf, *rope_tabs)


def _ctx_proj_kernel(x_ref, m_ref, w_ref, rk_ref, rv_ref, nk_ref, nv_ref):
    shift = m_ref[0, 0:1, :]
    scale = m_ref[0, 1:2, :]
    hm = (x_ref[0] * (1.0 + scale) + shift).astype(BF16)
    for g, o_ref in enumerate((rk_ref, rv_ref, nk_ref, nv_ref)):
        o_ref[0] = _dot(hm, w_ref[:, g * HEAD_W:(g + 1) * HEAD_W]).astype(BF16)


def _ctx_proj(ctx, m_ctx, w_in_bf):
    b, l, d = ctx.shape
    out_spec = pl.BlockSpec((1, l, HEAD_W), lambda i: (i, 0, 0))
    return pl.pallas_call(
        _ctx_proj_kernel,
        out_shape=[jax.ShapeDtypeStruct((b, l, HEAD_W), BF16)] * N_KV_GROUPS,
        grid=(b,),
        in_specs=[
            pl.BlockSpec((1, l, d), lambda i: (i, 0, 0)),
            pl.BlockSpec((1, N_MOD, d), lambda i: (0, 0, 0)),
            pl.BlockSpec((d, N_KV_GROUPS * HEAD_W), lambda i: (0, 0)),
        ],
        out_specs=[out_spec] * N_KV_GROUPS,
        compiler_params=_cparams("arbitrary"),
        name="ctx_proj",
    )(ctx, m_ctx, w_in_bf)


def _retention_kernel(lg_ref, q_ref, k_ref, v_ref, g_ref, kc_ref, vc_ref, o_ref, u_ref, s_ref):
    hd = pl.program_id(1)
    n = q_ref.shape[1]
    c = RET_CHUNK
    dk = RET_DK
    nc = n // c
    l = kc_ref.shape[1]
    lgf = lg_ref[0, hd]
    lgb = lg_ref[1, hd]

    ii = lax.broadcasted_iota(jnp.int32, (c, c), 0).astype(F32)
    jj = lax.broadcasted_iota(jnp.int32, (c, c), 1).astype(F32)
    diff = ii - jj
    decay = (jnp.where(diff >= 0, jnp.exp(lgf * jnp.maximum(diff, 0.0)), 0.0)
             + jnp.where(diff <= 0, jnp.exp(lgb * jnp.maximum(-diff, 0.0)), 0.0))
    idx = lax.broadcasted_iota(jnp.int32, (c, 1), 0).astype(F32)
    q_dec_f = jnp.exp(lgf * (idx + 1.0))
    k_dec_f = jnp.exp(lgf * (c - 1.0 - idx))
    q_dec_b = jnp.exp(lgb * (c - idx))
    k_dec_b = jnp.exp(lgb * idx)
    ones = jnp.ones((1, dk), F32)
    chunk_dec_f = jnp.exp(ones * (lgf * c))
    chunk_dec_b = jnp.exp(ones * (lgb * c))

    pos = lax.broadcasted_iota(jnp.int32, (l, 1), 0).astype(F32)
    kc = kc_ref[0].astype(F32)
    vc = vc_ref[0]
    s_f0 = _dot_tn((kc * jnp.exp(lgf * (l - 1.0 - pos))).astype(BF16), vc)
    s_b0 = _dot_tn((kc * jnp.exp(lgb * pos)).astype(BF16), vc)

    def chunk_rows(i):
        return pl.ds(pl.multiple_of(i * c, c), c)

    def kv_step(i, carry):
        rows = chunk_rows(i)
        k_i = k_ref[0, rows, :].astype(F32)
        kk = jnp.concatenate([(k_i * k_dec_f).astype(BF16), (k_i * k_dec_b).astype(BF16)], axis=1)
        u_ref[i] = _dot_tn(kk, v_ref[0, rows, :])
        return carry

    lax.fori_loop(0, nc, kv_step, 0, unroll=True)

    def scan_f(i, s):
        s_ref[i, :, 0:dk] = s.astype(BF16)
        return s * chunk_dec_f + u_ref[i, 0:dk, :]

    lax.fori_loop(0, nc, scan_f, s_f0, unroll=True)

    def scan_b(t, s):
        i = nc - 1 - t
        s_ref[i, :, dk:2 * dk] = s.astype(BF16)
        return s * chunk_dec_b + u_ref[i, dk:2 * dk, :]

    lax.fori_loop(0, nc, scan_b, s_b0, unroll=True)

    def out_step(i, carry):
        rows = chunk_rows(i)
        q_i = q_ref[0, rows, :]
        v_i = v_ref[0, rows, :]
        att = (_dot_nt(q_i, k_ref[0, rows, :]) * decay).astype(BF16)
        inter = _dot(q_i, s_ref[i])
        o = _dot(att, v_i) + inter[:, 0:dk] * q_dec_f + inter[:, dk:2 * dk] * q_dec_b
        mu = jnp.mean(o, axis=-1, keepdims=True)
        var = jnp.mean(jnp.square(o - mu), axis=-1, keepdims=True)
        o_n = (o - mu) * lax.rsqrt(var + LN_EPS)
        gate = g_ref[0, rows, :].astype(F32)
        o_ref[0, rows, :] = (_silu(gate) * o_n).astype(BF16)
        return carry

    lax.fori_loop(0, nc, out_step, 0, unroll=True)


def _retention(log_gamma2, rq, rk, rv, rg, rk_c, rv_c):
    b, n, _ = rq.shape
    l = rk_c.shape[1]
    nc = n // RET_CHUNK
    seq_spec = pl.BlockSpec((1, n, RET_DK), lambda i, h: (i, 0, h))
    ctx_spec = pl.BlockSpec((1, l, RET_DK), lambda i, h: (i, 0, h))
    return pl.pallas_call(
        _retention_kernel,
        out_shape=jax.ShapeDtypeStruct((b, n, HEAD_W), BF16),
        grid=(b, RET_HEADS),
        in_specs=[pl.BlockSpec(memory_space=pltpu.SMEM),
                  seq_spec, seq_spec, seq_spec, seq_spec, ctx_spec, ctx_spec],
        out_specs=seq_spec,
        scratch_shapes=[pltpu.VMEM((nc, 2 * RET_DK, RET_DK), F32),
                        pltpu.VMEM((nc, RET_DK, 2 * RET_DK), BF16)],
        compiler_params=_cparams("arbitrary", "arbitrary"),
        name="retention",
    )(log_gamma2, rq, rk, rv, rg, rk_c, rv_c)


def _na_geometry(rows):
    rb = NA_ROWS_PER_BLOCK
    wr = min(NA_WIN_R, rows)
    key_rows = min(rows, rb + wr - 1)
    variants, block_variant, block_start = [], [], []
    for kb in range(rows // rb):
        q_rows = kb * rb + np.arange(rb)
        r0 = np.clip(q_rows - wr // 2, 0, rows - wr)
        ks = int(np.clip(r0.min(), 0, rows - key_rows))
        assert r0.max() + wr <= ks + key_rows
        kr = ks + np.arange(key_rows)
        valid = (kr[None, :] >= r0[:, None]) & (kr[None, :] < r0[:, None] + wr)
        ridx = np.where(valid, kr[None, :] - q_rows[:, None] + NA_WIN_R - 1, N_BIAS_ROWS).astype(np.int32)
        for vi, v in enumerate(variants):
            if np.array_equal(v, ridx):
                block_variant.append(vi)
                break
        else:
            variants.append(ridx)
            block_variant.append(len(variants) - 1)
        block_start.append(ks)
    return key_rows, np.stack(variants), block_variant, block_start


def _na_bias_table(rpb, variant_rows):
    n_heads = rpb.shape[0]
    qc = np.arange(GRID_W)[:, None]
    kc = np.arange(GRID_W)[None, :]
    wstart = np.clip(qc - NA_WIN_C // 2, 0, GRID_W - NA_WIN_C)
    col_ok = (kc >= wstart) & (kc < wstart + NA_WIN_C)
    cidx = np.clip(kc - qc + NA_WIN_C - 1, 0, 2 * NA_WIN_C - 2)
    onehot = ((cidx[None] == np.arange(2 * NA_WIN_C - 1)[:, None, None]) & col_ok[None]).astype(np.float32)
    blocks = jnp.einsum("hrd,dqk->hqrk", rpb.astype(F32) * LOG2E, jnp.asarray(onehot), precision=HIGHEST)
    blocks = jnp.where(col_ok[None, :, None, :], blocks, MASK_VALUE)
    masked = jnp.full((n_heads, GRID_W, GRID_W), MASK_VALUE, F32)
    nvar, rb, key_rows = variant_rows.shape
    rows = [jnp.concatenate([masked if r == N_BIAS_ROWS else blocks[:, :, r, :] for r in variant_rows[v, a]], axis=-1)
            for v in range(nvar) for a in range(rb)]
    return jnp.stack(rows, axis=1).reshape(n_heads, nvar, rb * GRID_W, key_rows * GRID_W)


def _na_kernel(q_ref, k_ref, v_ref, kc_ref, vc_ref, bias_ref, o_ref, *, block_variant, block_start, key_rows):
    qb = NA_ROWS_PER_BLOCK * GRID_W
    nk = key_rows * GRID_W
    first_head = lax.broadcasted_iota(jnp.int32, (qb, LANES), 1) < NA_DH
    kc = kc_ref[0]
    vc = vc_ref[0]
    for kb, (var, ks) in enumerate(zip(block_variant, block_start)):
        q = q_ref[0, kb * qb:(kb + 1) * qb, :]
        k_win = k_ref[0, ks * GRID_W:ks * GRID_W + nk, :]
        v_win = v_ref[0, ks * GRID_W:ks * GRID_W + nk, :]
        zero = jnp.zeros_like(q)
        q2 = jnp.concatenate([jnp.where(first_head, q, zero), jnp.where(first_head, zero, q)], axis=0)
        s_win = _dot_nt(q2, k_win) + jnp.concatenate([bias_ref[0, var], bias_ref[1, var]], axis=0)
        s_ctx = _dot_nt(q2, kc)
        m = jnp.maximum(jnp.max(s_win, axis=-1, keepdims=True), jnp.max(s_ctx, axis=-1, keepdims=True))
        p_win = jnp.exp2(s_win - m)
        p_ctx = jnp.exp2(s_ctx - m)
        denom = jnp.sum(p_win, axis=-1, keepdims=True) + jnp.sum(p_ctx, axis=-1, keepdims=True)
        o = (_dot(p_win.astype(BF16), v_win) + _dot(p_ctx.astype(BF16), vc)) / denom
        o_ref[0, kb * qb:(kb + 1) * qb, :] = jnp.where(first_head, o[0:qb], o[qb:2 * qb]).astype(BF16)


def _neighbourhood_attention(nq, nk, nv, nk_c, nv_c, rpb):
    b, n, _ = nq.shape
    l = nk_c.shape[1]
    rows = n // GRID_W
    key_rows, variant_rows, block_variant, block_start = _na_geometry(rows)
    bias = _na_bias_table(rpb, variant_rows)
    nvar = variant_rows.shape[0]
    qb = NA_ROWS_PER_BLOCK * GRID_W
    nkeys = key_rows * GRID_W
    seq_spec = pl.BlockSpec((1, n, LANES), lambda h, i: (i, 0, h))
    ctx_spec = pl.BlockSpec((1, l, LANES), lambda h, i: (i, 0, h))
    kern = functools.partial(_na_kernel, block_variant=tuple(block_variant), block_start=tuple(block_start),
                             key_rows=key_rows)
    return pl.pallas_call(
        kern,
        out_shape=jax.ShapeDtypeStruct((b, n, HEAD_W), BF16),
        grid=(NA_HEADS // 2, b),
        in_specs=[seq_spec, seq_spec, seq_spec, ctx_spec, ctx_spec,
                  pl.BlockSpec((2, nvar, qb, nkeys), lambda h, i: (h, 0, 0, 0))],
        out_specs=seq_spec,
        compiler_params=_cparams("arbitrary", "arbitrary"),
        name="neighbourhood_attention",
    )(nq, nk, nv, nk_c, nv_c, bias)


def _layer_norm(z, g, b):
    mu = jnp.mean(z, axis=-1, keepdims=True)
    var = jnp.mean(jnp.square(z - mu), axis=-1, keepdims=True)
    return (z - mu) * lax.rsqrt(var + LN_EPS) * g + b


def _split_bf16(v):
    hi = v.astype(BF16)
    return hi, (v - hi.astype(F32)).astype(BF16)


def _route(t, wr_ref, br_ref):
    t_hi, t_lo = _split_bf16(t)
    w_hi, w_lo = _split_bf16(wr_ref[...])
    hi = _dot(t_hi, jnp.concatenate([w_hi, w_lo], axis=1))
    logits = hi[:, 0:LANES] + (_dot(t_lo, w_hi) + hi[:, LANES:2 * LANES]) + br_ref[...]
    lane = lax.broadcasted_iota(jnp.int32, logits.shape, 1).astype(F32)
    neg = -jnp.inf
    big = float(LANES)

    def first_max(vals):
        vmax = jnp.max(vals, axis=-1, keepdims=True)
        return vmax, jnp.min(jnp.where(vals == vmax, lane, big), axis=-1, keepdims=True)

    is_grp = lane < MOE_GROUPS
    g_max, grp = first_max(jnp.where(is_grp, logits, neg))
    g_sum = jnp.sum(jnp.where(is_grp, jnp.exp(logits - g_max), 0.0), axis=-1, keepdims=True)
    gate_g = 1.0 / g_sum
    lo = MOE_GROUPS + grp * MOE_PER_GROUP
    in_grp = (lane >= lo) & (lane < lo + MOE_PER_GROUP)
    le = jnp.where(in_grp, logits, neg)
    v1, i1 = first_max(le)
    v2, i2 = first_max(jnp.where(lane == i1, neg, le))
    e21 = jnp.exp(v2 - v1)
    w1 = gate_g / (1.0 + e21)
    w2 = gate_g * e21 / (1.0 + e21)
    return jnp.where(lane == 0, w1,
                     jnp.where(lane == 1, w2,
                               jnp.where(lane == 2, i1 - MOE_GROUPS,
                                         jnp.where(lane == 3, i2 - MOE_GROUPS, 0.0))))


def _post_norm_route(h, y, m_ref, ln_ref, wr_ref, br_ref, h_out, t_out, r_out):
    gate = m_ref[0, 2:3, :]
    shift = m_ref[0, 3:4, :]
    scale = m_ref[0, 4:5, :]
    h1 = _layer_norm(DEEPNORM_ALPHA * h + gate * y, ln_ref[0:1, :], ln_ref[1:2, :])
    t = h1 * (1.0 + scale) + shift
    h_out[0] = h1
    t_out[0] = t.astype(BF16)
    r_out[0] = _route(t, wr_ref, br_ref)


def _epilogue_specs(b, n, d, tm):
    in_specs = [
        pl.BlockSpec((1, N_MOD, d), lambda i, j: (i, 0, 0)),
        pl.BlockSpec((2, d), lambda i, j: (0, 0)),
        pl.BlockSpec((d, LANES), lambda i, j: (0, 0)),
        pl.BlockSpec((1, LANES), lambda i, j: (0, 0)),
    ]
    out_shape = [jax.ShapeDtypeStruct((b, n, d), F32), jax.ShapeDtypeStruct((b, n, d), BF16),
                 jax.ShapeDtypeStruct((b, n, LANES), F32)]
    out_specs = [pl.BlockSpec((1, tm, d), lambda i, j: (i, j, 0)),
                 pl.BlockSpec((1, tm, d), lambda i, j: (i, j, 0)),
                 pl.BlockSpec((1, tm, LANES), lambda i, j: (i, j, 0))]
    return in_specs, out_shape, out_specs


def _out_proj_kernel(x_ref, yr_ref, yn_ref, w_ref, m_ref, ln_ref, wr_ref, br_ref, h_out, t_out, r_out):
    y = _dot(yr_ref[0], w_ref[0:HEAD_W, :]) + _dot(yn_ref[0], w_ref[HEAD_W:2 * HEAD_W, :])
    _post_norm_route(x_ref[0], y, m_ref, ln_ref, wr_ref, br_ref, h_out, t_out, r_out)


def _out_proj(x, y_ret, y_na, w_out_bf, m, ln, router):
    b, n, d = x.shape
    tm = EPILOGUE_TILE
    ep_in, out_shape, out_specs = _epilogue_specs(b, n, d, tm)
    return pl.pallas_call(
        _out_proj_kernel,
        out_shape=out_shape,
        grid=(b, n // tm),
        in_specs=[
            pl.BlockSpec((1, tm, d), lambda i, j: (i, j, 0)),
            pl.BlockSpec((1, tm, HEAD_W), lambda i, j: (i, j, 0)),
            pl.BlockSpec((1, tm, HEAD_W), lambda i, j: (i, j, 0)),
            pl.BlockSpec((2 * HEAD_W, d), lambda i, j: (0, 0)),
        ] + ep_in,
        out_specs=out_specs,
        compiler_params=_cparams("arbitrary", "arbitrary"),
        name="out_proj",
    )(x, y_ret, y_na, w_out_bf, m, ln, *router)


POOL_HALO = max(POOL_SIZES) // 2
POOL_TILE = EPILOGUE_TILE


def _pool_kernel(x_ref, prev_ref, next_ref, pw_ref, ps_ref, m_ref, ln_ref, wr_ref, br_ref,
                 h_out, t_out, r_out, *, n):
    j = pl.program_id(1)
    nj = pl.num_programs(1)
    tm = x_ref.shape[1]
    halo = POOL_HALO
    shift = m_ref[0, 0:1, :]
    scale = m_ref[0, 1:2, :]
    x = x_ref[0]
    hm = x * (1.0 + scale) + shift
    prev = jnp.where(j > 0, prev_ref[0] * (1.0 + scale) + shift, 0.0)
    nxt = jnp.where(j < nj - 1, next_ref[0] * (1.0 + scale) + shift, 0.0)
    ext = jnp.concatenate([prev, hm, nxt], axis=0)
    pos = (j * tm + lax.broadcasted_iota(jnp.int32, (tm, 1), 0))
    ys = []
    for g, w in enumerate(POOL_SIZES):
        cols = slice(g * POOL_GROUP, (g + 1) * POOL_GROUP)
        s = ext[:, cols]
        span = 1
        while span < w:
            s = s[:s.shape[0] - span] + s[span:]
            span *= 2
        off = halo - w // 2
        win = s[off:off + tm]
        cnt = (jnp.minimum(pos + (w - w // 2), n) - jnp.maximum(pos - w // 2, 0)).astype(F32)
        z = (win / cnt - hm[:, cols]).astype(BF16)
        ys.append(_dot(z, pw_ref[g]))
    y = jnp.concatenate(ys, axis=-1) * ps_ref[...]
    _post_norm_route(x, y, m_ref, ln_ref, wr_ref, br_ref, h_out, t_out, r_out)


def _pool_mixer(h, pool_w_bf, pool_scale, m, ln, router):
    b, n, d = h.shape
    tm = POOL_TILE
    halo = POOL_HALO
    blocks_per_tile = tm // halo
    n_halo_blocks = n // halo
    ep_in, out_shape, out_specs = _epilogue_specs(b, n, d, tm)
    return pl.pallas_call(
        functools.partial(_pool_kernel, n=n),
        out_shape=out_shape,
        grid=(b, n // tm),
        in_specs=[
            pl.BlockSpec((1, tm, d), lambda i, j: (i, j, 0)),
            pl.BlockSpec((1, halo, d), lambda i, j: (i, jnp.maximum(j * blocks_per_tile - 1, 0), 0)),
            pl.BlockSpec((1, halo, d),
                         lambda i, j: (i, jnp.minimum((j + 1) * blocks_per_tile, n_halo_blocks - 1), 0)),
            pl.BlockSpec((len(POOL_SIZES), POOL_GROUP, POOL_GROUP), lambda i, j: (0, 0, 0)),
            pl.BlockSpec((1, d), lambda i, j: (0, 0)),
        ] + ep_in,
        out_specs=out_specs,
        compiler_params=_cparams("arbitrary", "arbitrary"),
        name="pool_mixer",
    )(h, h, h, pool_w_bf, pool_scale.reshape(1, d), m, ln, *router)


def _expert_kernel(tile_ref, exp_ref, lo_ref, hi_ref, x_ref, wg_ref, wu_ref, wd_ref, buf_ref, o_ref,
                   wg_s, wu_s, wd_s):
    del buf_ref
    g = pl.program_id(0)
    prev_e = exp_ref[jnp.maximum(g - 1, 0)]

    @pl.when((g == 0) | (exp_ref[g] != prev_e))
    def _():
        wg_s[...] = wg_ref[0, 0].astype(BF16)
        wu_s[...] = wu_ref[0, 0].astype(BF16)
        wd_s[...] = wd_ref[0, 0].astype(BF16)

    lo = lo_ref[g]
    hi = hi_ref[g]
    tm = o_ref.shape[0]

    def mlp(x):
        gate = _dot(x, wg_s[...])
        up = _dot(x, wu_s[...])
        act = (_silu(gate) * up).astype(BF16)
        return _dot(act, wd_s[...]).astype(BF16)

    whole = (lo == 0) & (hi == tm)

    @pl.when(whole)
    def _():
        o_ref[...] = mlp(x_ref[...])

    for r0 in range(0, tm, MOE_SUB_TILE):
        r1 = r0 + MOE_SUB_TILE
        hit = (hi > lo) & jnp.logical_not(whole) & (hi > r0) & (lo < r1)
        row = r0 + lax.broadcasted_iota(jnp.int32, (MOE_SUB_TILE, 1), 0)

        @pl.when(hit & (lo <= r0))
        def _():
            o_ref[r0:r1, :] = jnp.where(row < hi, mlp(x_ref[r0:r1, :]), jnp.zeros((), BF16))

        @pl.when(hit & (lo > r0))
        def _():
            o_ref[r0:r1, :] = jnp.where((row >= lo) & (row < hi), mlp(x_ref[r0:r1, :]), o_ref[r0:r1, :])


def _experts(steps, x_chunk, y_prev, w_gate, w_up, w_down, layer, chunk):
    pc, d = x_chunk.shape
    tm = MOE_TILE
    tiles = pc // tm
    hid = w_gate.shape[-1]
    n_steps = steps[0].shape[0]
    w_spec = lambda shape: pl.BlockSpec(shape, lambda g, tile, exp, lo, hi: (layer, exp[g], 0, 0))
    in_specs = [
        pl.BlockSpec((tm, d), lambda g, tile, exp, lo, hi: (tile[g], 0)),
        w_spec((1, 1, d, hid)), w_spec((1, 1, d, hid)), w_spec((1, 1, hid, d)),
        pl.BlockSpec(memory_space=pl.ANY),
    ]
    args = [*steps, x_chunk, w_gate, w_up, w_down, y_prev]
    grid_spec = pltpu.PrefetchScalarGridSpec(
        num_scalar_prefetch=4,
        grid=(n_steps,),
        in_specs=in_specs,
        out_specs=pl.BlockSpec((tm, d), lambda g, tile, exp, lo, hi: (chunk * tiles + tile[g], 0)),
        scratch_shapes=[pltpu.VMEM((d, hid), BF16), pltpu.VMEM((d, hid), BF16), pltpu.VMEM((hid, d), BF16)],
    )
    return pl.pallas_call(
        _expert_kernel,
        out_shape=jax.ShapeDtypeStruct(y_prev.shape, BF16),
        grid_spec=grid_spec,
        input_output_aliases={len(args) - 1: 0},
        compiler_params=_cparams("arbitrary"),
        name="experts",
    )(*args)


ASSIGN_BITS = 16


def _dispatch_plan(route, tm, n_chunks):
    tok = route.shape[0]
    n_assign = 2 * tok
    assert n_assign <= 1 << ASSIGN_BITS
    i32 = jnp.int32
    low_mask = (1 << ASSIGN_BITS) - 1
    eid = jnp.transpose(route[:, 2:4]).astype(i32).reshape(n_assign)
    experts = jnp.arange(MOE_EXPERTS, dtype=i32)
    counts = jnp.sum((eid[:, None] == experts[None, :]).astype(i32), axis=0)
    end = jnp.cumsum(counts)
    start = end - counts
    order = lax.sort((eid << ASSIGN_BITS) | jnp.arange(n_assign, dtype=i32)) & low_mask
    tok_of_pos = order % tok
    _, pos_of_assign = lax.sort((order, jnp.arange(n_assign, dtype=i32)), num_keys=1)
    rows_per_chunk = n_assign // n_chunks
    tiles_per_chunk = rows_per_chunk // tm
    chunk_lo = (jnp.arange(n_chunks, dtype=i32) * rows_per_chunk)[:, None]
    tile_starts = chunk_lo + jnp.arange(tiles_per_chunk, dtype=i32)[None, :] * tm
    cuts = jnp.concatenate([tile_starts, jnp.clip(start[None, :], chunk_lo, chunk_lo + rows_per_chunk)], axis=1)
    cuts = jnp.sort(cuts, axis=1)
    nxt = jnp.concatenate([cuts[:, 1:], chunk_lo + rows_per_chunk], axis=1)
    tile = jnp.minimum((cuts - chunk_lo) // tm, tiles_per_chunk - 1)
    tile_row = chunk_lo + tile * tm
    lo = cuts - tile_row
    hi = nxt - tile_row
    first_row = jnp.minimum(cuts, chunk_lo + rows_per_chunk - 1)
    expert = jnp.sum((end[None, None, :] <= first_row[:, :, None]).astype(i32), axis=2)
    return (tile, expert, lo, hi), tok_of_pos, pos_of_assign


def _n_chunks(b):
    return 4 if b % 4 == 0 else (2 if b % 2 == 0 else 1)


def _post_norm_kernel(h_ref, y0_ref, y1_ref, r_ref, m_ref, ln_ref, o_ref):
    gate = m_ref[0, 5:6, :]
    r = r_ref[0]
    y = r[:, 0:1] * y0_ref[...].astype(F32) + r[:, 1:2] * y1_ref[...].astype(F32)
    o_ref[0] = _layer_norm(DEEPNORM_ALPHA * h_ref[0] + gate * y, ln_ref[0:1, :], ln_ref[1:2, :])


def _post_norm(h, y2, route, m, ln, chunk, n_chunks):
    b, n, d = h.shape
    tm = ROW_TILE
    bc = b // n_chunks
    b0 = chunk * bc
    tiles_per_seq = n // tm
    tiles_per_k = bc * tiles_per_seq
    row_spec = pl.BlockSpec((1, tm, d), lambda i, j: (b0 + i, j, 0))
    return pl.pallas_call(
        _post_norm_kernel,
        out_shape=jax.ShapeDtypeStruct((b, n, d), F32),
        grid=(bc, tiles_per_seq),
        in_specs=[row_spec,
                  pl.BlockSpec((tm, d), lambda i, j: (i * tiles_per_seq + j, 0)),
                  pl.BlockSpec((tm, d), lambda i, j: (tiles_per_k + i * tiles_per_seq + j, 0)),
                  pl.BlockSpec((1, tm, LANES), lambda i, j: (b0 + i, j, 0)),
                  pl.BlockSpec((1, N_MOD, d), lambda i, j: (b0 + i, 0, 0)),
                  pl.BlockSpec((2, d), lambda i, j: (0, 0))],
        out_specs=row_spec,
        input_output_aliases={0: 0},
        compiler_params=_cparams("arbitrary", "arbitrary"),
        name="post_norm",
    )(h, y2, y2, route, m, ln)


def _moe_rows(tok):
    return 2 * tok


def _moe_post_norm(h1, t, route, m, ln, w_gate, w_up, w_down, layer, row_buf):
    b, n, d = t.shape
    tok = b * n
    nch = _n_chunks(b)
    steps, tok_of_pos, pos_of_assign = _dispatch_plan(route.reshape(tok, LANES), MOE_TILE, nch)
    t2 = t.reshape(tok, d)
    pc = 2 * tok // nch
    y_sorted = row_buf
    for c in range(nch):
        x_c = t2.at[tok_of_pos[c * pc:(c + 1) * pc]].get(mode="promise_in_bounds")
        y_sorted = _experts(tuple(v[c] for v in steps), x_c, y_sorted, w_gate, w_up, w_down, layer, c)
    bc = b // nch
    pos3 = pos_of_assign.reshape(2, b, n)
    h = h1
    for c in range(nch):
        pos_c = pos3[:, c * bc:(c + 1) * bc].reshape(2 * bc * n)
        y2_c = y_sorted.at[pos_c].get(mode="promise_in_bounds")
        h = _post_norm(h, y2_c, route, m, ln, c, nch)
    return h, y_sorted


def _rope_tables(n):
    t = jnp.arange(n)
    rows = (t // GRID_W).astype(F32)
    cols = (t % GRID_W).astype(F32)
    n_freq = RET_DK // 4
    inv_freq = ROPE_BASE ** (-jnp.arange(n_freq, dtype=F32) / n_freq)
    ang = jnp.concatenate([rows[:, None] * inv_freq, cols[:, None] * inv_freq], axis=-1)
    cos, sin = jnp.cos(ang), jnp.sin(ang)
    cos2 = jnp.concatenate([cos, cos], axis=-1)
    sin2 = jnp.concatenate([-sin, sin], axis=-1)
    q_scale = RET_DK ** -0.5
    return cos2 * q_scale, sin2 * q_scale, cos2, sin2


def _router_params(w_r1, b_r1, w_r2, b_r2):
    d = w_r1.shape[0]
    w2 = jnp.transpose(w_r2, (1, 0, 2)).reshape(d, MOE_EXPERTS)
    pad = LANES - MOE_GROUPS - MOE_EXPERTS
    wr = jnp.concatenate([w_r1, w2, jnp.zeros((d, pad), F32)], axis=-1)
    br = jnp.concatenate([b_r1, b_r2.reshape(MOE_EXPERTS), jnp.zeros((pad,), F32)]).reshape(1, LANES)
    return wr, br


def kernel(x, c, ctx, c_ctx, w_mod, b_mod, ln_g, ln_b, ab_w_in, ab_w_out, ab_log_decay, ab_rpb, pool_w, pool_scale, moe_w_r1, moe_b_r1, moe_w_r2, moe_b_r2, moe_w_gate, moe_w_up, moe_w_down):
    b, n, d = x.shape
    cc = jnp.concatenate([c, c_ctx[None, :], jnp.zeros((MOD_ROWS - b - 1, d), F32)], axis=0)
    mod = _modulation(cc, w_mod, b_mod)
    h = x
    row_buf = None
    for i in range(DEPTH):
        j = i // 2
        m = mod[i, :b].reshape(b, N_MOD, d)
        ln1 = jnp.stack([ln_g[i, 0], ln_b[i, 0]])
        ln2 = jnp.stack([ln_g[i, 1], ln_b[i, 1]])
        router = _router_params(moe_w_r1[i], moe_b_r1[i], moe_w_r2[i], moe_b_r2[i])
        if i % 2 == 0:
            m_ctx = mod[i, b].reshape(1, N_MOD, d)
            w_in_bf = ab_w_in[j].astype(BF16)
            log_gamma2 = jnp.log1p(-jnp.exp(ab_log_decay[j].astype(F32)))
            rk, rv, nk, nv, rq, rg, nq, row_buf = _in_proj(h, m, w_in_bf, _rope_tables(n), _moe_rows(b * n))
            rk_c, rv_c, nk_c, nv_c = _ctx_proj(ctx, m_ctx, w_in_bf[:, :N_KV_GROUPS * HEAD_W])
            y_ret = _retention(log_gamma2, rq, rk, rv, rg, rk_c, rv_c)
            y_na = _neighbourhood_attention(nq, nk, nv, nk_c, nv_c, ab_rpb[j])
            h1, t, route = _out_proj(h, y_ret, y_na, ab_w_out[j].astype(BF16), m, ln1, router)
        else:
            h1, t, route = _pool_mixer(h, pool_w[j].astype(BF16), pool_scale[j], m, ln1, router)
        h, row_buf = _moe_post_norm(h1, t, route, m, ln2, moe_w_gate, moe_w_up, moe_w_down, i, row_buf)
    return h
```

```python
import functools

import numpy as np
import jax
import jax.numpy as jnp
from jax import lax
from jax.experimental import pallas as pl
from jax.experimental.pallas import tpu as pltpu

F32 = jnp.float32
BF16 = jnp.bfloat16
HIGHEST = lax.Precision.HIGHEST

D_MODEL = 1024
DEPTH = 2
GRID_W = 64
RET_HEADS = 4
RET_DK = 128
RET_CHUNK = 128
NA_HEADS = 8
NA_DH = 64
NA_WIN_R = 8
NA_WIN_C = 16
N_BIAS_ROWS = 2 * NA_WIN_R - 1
POOL_SIZES = (2, 4, 8, 16)
POOL_GROUP = D_MODEL // len(POOL_SIZES)
MOE_GROUPS = 4
MOE_PER_GROUP = 8
MOE_EXPERTS = MOE_GROUPS * MOE_PER_GROUP
MOE_HIDDEN = D_MODEL // 2
ROPE_BASE = 10000.0
LN_EPS = 1e-5
N_MOD = 6
DEEPNORM_ALPHA = (2 * DEPTH) ** 0.25
HEAD_W = 512
N_IN_GROUPS = 7
N_KV_GROUPS = 4
LOG2E = float(np.log2(np.e))
MASK_VALUE = -1e30

LANES = 128
VMEM_LIMIT = 56 * 1024 * 1024

NA_ROWS_PER_BLOCK = 4
ROW_TILE = 512
EPILOGUE_TILE = 1024
MOE_TILE = 512
MOD_ROWS = 24
MOD_COL_TILE = 1536


def _cparams(*sem):
    return pltpu.CompilerParams(dimension_semantics=sem, vmem_limit_bytes=VMEM_LIMIT)


def _silu(v):
    return v / (1.0 + jnp.exp(-v))


def _dot(a, b):
    return jnp.dot(a, b, preferred_element_type=F32)


def _dot_nt(a, b):
    return lax.dot_general(a, b, (((1,), (1,)), ((), ())), preferred_element_type=F32)


def _dot_tn(a, b):
    return lax.dot_general(a, b, (((0,), (0,)), ((), ())), preferred_element_type=F32)


def _mod_kernel(c_ref, w_ref, b_ref, o_ref):
    s = _silu(c_ref[...])
    o_ref[0] = jnp.dot(s, w_ref[0], precision=HIGHEST, preferred_element_type=F32) + b_ref[0]


def _modulation(cc, w_mod, b_mod):
    depth, d, n = w_mod.shape
    return pl.pallas_call(
        _mod_kernel,
        out_shape=jax.ShapeDtypeStruct((depth, MOD_ROWS, n), F32),
        grid=(depth, n // MOD_COL_TILE),
        in_specs=[
            pl.BlockSpec((MOD_ROWS, d), lambda i, j: (0, 0)),
            pl.BlockSpec((1, d, MOD_COL_TILE), lambda i, j: (i, 0, j)),
            pl.BlockSpec((1, 1, MOD_COL_TILE), lambda i, j: (i, 0, j)),
        ],
        out_specs=pl.BlockSpec((1, MOD_ROWS, MOD_COL_TILE), lambda i, j: (i, 0, j)),
        compiler_params=_cparams("arbitrary", "arbitrary"),
        name="modulation",
    )(cc, w_mod, b_mod.reshape(depth, 1, n))


def _rope(v, cos2, sin2):
    return v * cos2 + pltpu.roll(v, RET_DK // 2, axis=1) * sin2


def _in_proj_kernel(x_ref, m_ref, w_ref, cq_ref, sq_ref, ck_ref, sk_ref,
                    rk_ref, rv_ref, nk_ref, nv_ref, rq_ref, rg_ref, nq_ref, buf_ref):
    buf_ref[...] = jnp.zeros_like(buf_ref)
    shift = m_ref[0, 0:1, :]
    scale = m_ref[0, 1:2, :]
    hm = (x_ref[0] * (1.0 + scale) + shift).astype(BF16)
    outs = (rk_ref, rv_ref, nk_ref, nv_ref, rq_ref, rg_ref, nq_ref)
    for g, o_ref in enumerate(outs):
        p = _dot(hm, w_ref[:, g * HEAD_W:(g + 1) * HEAD_W])
        if g == 0 or g == 4:
            cos2 = (ck_ref if g == 0 else cq_ref)[...]
            sin2 = (sk_ref if g == 0 else sq_ref)[...]
            for hd in range(RET_HEADS):
                sl = slice(hd * RET_DK, (hd + 1) * RET_DK)
                o_ref[0, :, sl] = _rope(p[:, sl], cos2, sin2).astype(BF16)
        elif g == 6:
            o_ref[0] = (p * (NA_DH ** -0.5 * LOG2E)).astype(BF16)
        else:
            o_ref[0] = p.astype(BF16)


def _in_proj(x, m, w_in_bf, rope_tabs, buf_rows):
    b, n, d = x.shape
    tm = ROW_TILE
    tiles_per_seq = n // tm
    buf_tile = buf_rows // (b * tiles_per_seq)
    assert buf_tile * b * tiles_per_seq == buf_rows and buf_tile % 16 == 0
    tab_spec = pl.BlockSpec((tm, RET_DK), lambda i, j: (j, 0))
    out_spec = pl.BlockSpec((1, tm, HEAD_W), lambda i, j: (i, j, 0))
    return pl.pallas_call(
        _in_proj_kernel,
        out_shape=[jax.ShapeDtypeStruct((b, n, HEAD_W), BF16)] * N_IN_GROUPS
        + [jax.ShapeDtypeStruct((buf_rows, d), BF16)],
        grid=(b, tiles_per_seq),
        in_specs=[
            pl.BlockSpec((1, tm, d), lambda i, j: (i, j, 0)),
            pl.BlockSpec((1, N_MOD, d), lambda i, j: (i, 0, 0)),
            pl.BlockSpec((d, N_IN_GROUPS * HEAD_W), lambda i, j: (0, 0)),
            tab_spec, tab_spec, tab_spec, tab_spec,
        ],
        out_specs=[out_spec] * N_IN_GROUPS + [pl.BlockSpec((buf_tile, d), lambda i, j: (i * tiles_per_seq + j, 0))],
        compiler_params=_cparams("arbitrary", "arbitrary"),
        name="in_proj",
    )(x, m, w_in_bf, *rope_tabs)


def _ctx_proj_kernel(x_ref, m_ref, w_ref, rk_ref, rv_ref, nk_ref, nv_ref):
    shift = m_ref[0, 0:1, :]
    scale = m_ref[0, 1:2, :]
    hm = (x_ref[0] * (1.0 + scale) + shift).astype(BF16)
    for g, o_ref in enumerate((rk_ref, rv_ref, nk_ref, nv_ref)):
        o_ref[0] = _dot(hm, w_ref[:, g * HEAD_W:(g + 1) * HEAD_W]).astype(BF16)


def _ctx_proj(ctx, m_ctx, w_in_bf):
    b, l, d = ctx.shape
    out_spec = pl.BlockSpec((1, l, HEAD_W), lambda i: (i, 0, 0))
    return pl.pallas_call(
        _ctx_proj_kernel,
        out_shape=[jax.ShapeDtypeStruct((b, l, HEAD_W), BF16)] * N_KV_GROUPS,
        grid=(b,),
        in_specs=[
            pl.BlockSpec((1, l, d), lambda i: (i, 0, 0)),
            pl.BlockSpec((1, N_MOD, d), lambda i: (0, 0, 0)),
            pl.BlockSpec((d, N_KV_GROUPS * HEAD_W), lambda i: (0, 0)),
        ],
        out_specs=[out_spec] * N_KV_GROUPS,
        compiler_params=_cparams("arbitrary"),
        name="ctx_proj",
    )(ctx, m_ctx, w_in_bf)


def _retention_kernel(lg_ref, q_ref, k_ref, v_ref, g_ref, kc_ref, vc_ref, o_ref, u_ref, s_ref):
    hd = pl.program_id(1)
    n = q_ref.shape[1]
    c = RET_CHUNK
    dk = RET_DK
    nc = n // c
    l = kc_ref.shape[1]
    lgf = lg_ref[0, hd]
    lgb = lg_ref[1, hd]

    ii = lax.broadcasted_iota(jnp.int32, (c, c), 0).astype(F32)
    jj = lax.broadcasted_iota(jnp.int32, (c, c), 1).astype(F32)
    diff = ii - jj
    decay = (jnp.where(diff >= 0, jnp.exp(lgf * jnp.maximum(diff, 0.0)), 0.0)
             + jnp.where(diff <= 0, jnp.exp(lgb * jnp.maximum(-diff, 0.0)), 0.0))
    idx = lax.broadcasted_iota(jnp.int32, (c, 1), 0).astype(F32)
    q_dec_f = jnp.exp(lgf * (idx + 1.0))
    k_dec_f = jnp.exp(lgf * (c - 1.0 - idx))
    q_dec_b = jnp.exp(lgb * (c - idx))
    k_dec_b = jnp.exp(lgb * idx)
    ones = jnp.ones((1, dk), F32)
    chunk_dec_f = jnp.exp(ones * (lgf * c))
    chunk_dec_b = jnp.exp(ones * (lgb * c))

    pos = lax.broadcasted_iota(jnp.int32, (l, 1), 0).astype(F32)
    kc = kc_ref[0].astype(F32)
    vc = vc_ref[0]
    s_f0 = _dot_tn((kc * jnp.exp(lgf * (l - 1.0 - pos))).astype(BF16), vc)
    s_b0 = _dot_tn((kc * jnp.exp(lgb * pos)).astype(BF16), vc)

    def chunk_rows(i):
        return pl.ds(pl.multiple_of(i * c, c), c)

    def kv_step(i, carry):
        rows = chunk_rows(i)
        k_i = k_ref[0, rows, :].astype(F32)
        kk = jnp.concatenate([(k_i * k_dec_f).astype(BF16), (k_i * k_dec_b).astype(BF16)], axis=1)
        u_ref[i] = _dot_tn(kk, v_ref[0, rows, :])
        return carry

    lax.fori_loop(0, nc, kv_step, 0, unroll=True)

    def scan_f(i, s):
        s_ref[i, :, 0:dk] = s.astype(BF16)
        return s * chunk_dec_f + u_ref[i, 0:dk, :]

    lax.fori_loop(0, nc, scan_f, s_f0, unroll=True)

    def scan_b(t, s):
        i = nc - 1 - t
        s_ref[i, :, dk:2 * dk] = s.astype(BF16)
        return s * chunk_dec_b + u_ref[i, dk:2 * dk, :]

    lax.fori_loop(0, nc, scan_b, s_b0, unroll=True)

    def out_step(i, carry):
        rows = chunk_rows(i)
        q_i = q_ref[0, rows, :]
        v_i = v_ref[0, rows, :]
        att = (_dot_nt(q_i, k_ref[0, rows, :]) * decay).astype(BF16)
        inter = _dot(q_i, s_ref[i])
        o = _dot(att, v_i) + inter[:, 0:dk] * q_dec_f + inter[:, dk:2 * dk] * q_dec_b
        mu = jnp.mean(o, axis=-1, keepdims=True)
        var = jnp.mean(jnp.square(o - mu), axis=-1, keepdims=True)
        o_n = (o - mu) * lax.rsqrt(var + LN_EPS)
        gate = g_ref[0, rows, :].astype(F32)
        o_ref[0, rows, :] = (_silu(gate) * o_n).astype(BF16)
        return carry

    lax.fori_loop(0, nc, out_step, 0, unroll=True)


def _retention(log_gamma2, rq, rk, rv, rg, rk_c, rv_c):
    b, n, _ = rq.shape
    l = rk_c.shape[1]
    nc = n // RET_CHUNK
    seq_spec = pl.BlockSpec((1, n, RET_DK), lambda i, h: (i, 0, h))
    ctx_spec = pl.BlockSpec((1, l, RET_DK), lambda i, h: (i, 0, h))
    return pl.pallas_call(
        _retention_kernel,
        out_shape=jax.ShapeDtypeStruct((b, n, HEAD_W), BF16),
        grid=(b, RET_HEADS),
        in_specs=[pl.BlockSpec(memory_space=pltpu.SMEM),
                  seq_spec, seq_spec, seq_spec, seq_spec, ctx_spec, ctx_spec],
        out_specs=seq_spec,
        scratch_shapes=[pltpu.VMEM((nc, 2 * RET_DK, RET_DK), F32),
                        pltpu.VMEM((nc, RET_DK, 2 * RET_DK), BF16)],
        compiler_params=_cparams("arbitrary", "arbitrary"),
        name="retention",
    )(log_gamma2, rq, rk, rv, rg, rk_c, rv_c)


def _na_geometry(rows):
    rb = NA_ROWS_PER_BLOCK
    wr = min(NA_WIN_R, rows)
    key_rows = min(rows, rb + wr - 1)
    variants, block_variant, block_start = [], [], []
    for kb in range(rows // rb):
        q_rows = kb * rb + np.arange(rb)
        r0 = np.clip(q_rows - wr // 2, 0, rows - wr)
        ks = int(np.clip(r0.min(), 0, rows - key_rows))
        assert r0.max() + wr <= ks + key_rows
        kr = ks + np.arange(key_rows)
        valid = (kr[None, :] >= r0[:, None]) & (kr[None, :] < r0[:, None] + wr)
        ridx = np.where(valid, kr[None, :] - q_rows[:, None] + NA_WIN_R - 1, N_BIAS_ROWS).astype(np.int32)
        for vi, v in enumerate(variants):
            if np.array_equal(v, ridx):
                block_variant.append(vi)
                break
        else:
            variants.append(ridx)
            block_variant.append(len(variants) - 1)
        block_start.append(ks)
    return key_rows, np.stack(variants), block_variant, block_start


def _na_bias_table(rpb, variant_rows):
    n_heads = rpb.shape[0]
    qc = np.arange(GRID_W)[:, None]
    kc = np.arange(GRID_W)[None, :]
    wstart = np.clip(qc - NA_WIN_C // 2, 0, GRID_W - NA_WIN_C)
    col_ok = (kc >= wstart) & (kc < wstart + NA_WIN_C)
    cidx = np.clip(kc - qc + NA_WIN_C - 1, 0, 2 * NA_WIN_C - 2)
    onehot = ((cidx[None] == np.arange(2 * NA_WIN_C - 1)[:, None, None]) & col_ok[None]).astype(np.float32)
    blocks = jnp.einsum("hrd,dqk->hqrk", rpb.astype(F32) * LOG2E, jnp.asarray(onehot), precision=HIGHEST)
    blocks = jnp.where(col_ok[None, :, None, :], blocks, MASK_VALUE)
    masked = jnp.full((n_heads, GRID_W, GRID_W), MASK_VALUE, F32)
    nvar, rb, key_rows = variant_rows.shape
    rows = [jnp.concatenate([masked if r == N_BIAS_ROWS else blocks[:, :, r, :] for r in variant_rows[v, a]], axis=-1)
            for v in range(nvar) for a in range(rb)]
    return jnp.stack(rows, axis=1).reshape(n_heads, nvar, rb * GRID_W, key_rows * GRID_W)


def _na_kernel(q_ref, k_ref, v_ref, kc_ref, vc_ref, bias_ref, o_ref, *, block_variant, block_start, key_rows):
    qb = NA_ROWS_PER_BLOCK * GRID_W
    nk = key_rows * GRID_W
    first_head = lax.broadcasted_iota(jnp.int32, (qb, LANES), 1) < NA_DH
    kc = kc_ref[0]
    vc = vc_ref[0]
    for kb, (var, ks) in enumerate(zip(block_variant, block_start)):
        q = q_ref[0, kb * qb:(kb + 1) * qb, :]
        k_win = k_ref[0, ks * GRID_W:ks * GRID_W + nk, :]
        v_win = v_ref[0, ks * GRID_W:ks * GRID_W + nk, :]
        zero = jnp.zeros_like(q)
        q2 = jnp.concatenate([jnp.where(first_head, q, zero), jnp.where(first_head, zero, q)], axis=0)
        s_win = _dot_nt(q2, k_win) + jnp.concatenate([bias_ref[0, var], bias_ref[1, var]], axis=0)
        s_ctx = _dot_nt(q2, kc)
        m = jnp.maximum(jnp.max(s_win, axis=-1, keepdims=True), jnp.max(s_ctx, axis=-1, keepdims=True))
        p_win = jnp.exp2(s_win - m)
        p_ctx = jnp.exp2(s_ctx - m)
        denom = jnp.sum(p_win, axis=-1, keepdims=True) + jnp.sum(p_ctx, axis=-1, keepdims=True)
        o = (_dot(p_win.astype(BF16), v_win) + _dot(p_ctx.astype(BF16), vc)) / denom
        o_ref[0, kb * qb:(kb + 1) * qb, :] = jnp.where(first_head, o[0:qb], o[qb:2 * qb]).astype(BF16)


def _neighbourhood_attention(nq, nk, nv, nk_c, nv_c, rpb):
    b, n, _ = nq.shape
    l = nk_c.shape[1]
    rows = n // GRID_W
    key_rows, variant_rows, block_variant, block_start = _na_geometry(rows)
    bias = _na_bias_table(rpb, variant_rows)
    nvar = variant_rows.shape[0]
    qb = NA_ROWS_PER_BLOCK * GRID_W
    nkeys = key_rows * GRID_W
    seq_spec = pl.BlockSpec((1, n, LANES), lambda h, i: (i, 0, h))
    ctx_spec = pl.BlockSpec((1, l, LANES), lambda h, i: (i, 0, h))
    kern = functools.partial(_na_kernel, block_variant=tuple(block_variant), block_start=tuple(block_start),
                             key_rows=key_rows)
    return pl.pallas_call(
        kern,
        out_shape=jax.ShapeDtypeStruct((b, n, HEAD_W), BF16),
        grid=(NA_HEADS // 2, b),
        in_specs=[seq_spec, seq_spec, seq_spec, ctx_spec, ctx_spec,
                  pl.BlockSpec((2, nvar, qb, nkeys), lambda h, i: (h, 0, 0, 0))],
        out_specs=seq_spec,
        compiler_params=_cparams("arbitrary", "arbitrary"),
        name="neighbourhood_attention",
    )(nq, nk, nv, nk_c, nv_c, bias)


def _layer_norm(z, g, b):
    mu = jnp.mean(z, axis=-1, keepdims=True)
    var = jnp.mean(jnp.square(z - mu), axis=-1, keepdims=True)
    return (z - mu) * lax.rsqrt(var + LN_EPS) * g + b


def _split_bf16(v):
    hi = v.astype(BF16)
    return hi, (v - hi.astype(F32)).astype(BF16)


def _route(t, wr_ref, br_ref):
    t_hi, t_lo = _split_bf16(t)
    w_hi, w_lo = _split_bf16(wr_ref[...])
    hi = _dot(t_hi, jnp.concatenate([w_hi, w_lo], axis=1))
    logits = hi[:, 0:LANES] + (_dot(t_lo, w_hi) + hi[:, LANES:2 * LANES]) + br_ref[...]
    lane = lax.broadcasted_iota(jnp.int32, logits.shape, 1).astype(F32)
    neg = -jnp.inf
    big = float(LANES)

    def first_max(vals):
        vmax = jnp.max(vals, axis=-1, keepdims=True)
        return vmax, jnp.min(jnp.where(vals == vmax, lane, big), axis=-1, keepdims=True)

    is_grp = lane < MOE_GROUPS
    g_max, grp = first_max(jnp.where(is_grp, logits, neg))
    g_sum = jnp.sum(jnp.where(is_grp, jnp.exp(logits - g_max), 0.0), axis=-1, keepdims=True)
    gate_g = 1.0 / g_sum
    lo = MOE_GROUPS + grp * MOE_PER_GROUP
    in_grp = (lane >= lo) & (lane < lo + MOE_PER_GROUP)
    le = jnp.where(in_grp, logits, neg)
    v1, i1 = first_max(le)
    v2, i2 = first_max(jnp.where(lane == i1, neg, le))
    e21 = jnp.exp(v2 - v1)
    w1 = gate_g / (1.0 + e21)
    w2 = gate_g * e21 / (1.0 + e21)
    return jnp.where(lane == 0, w1,
                     jnp.where(lane == 1, w2,
                               jnp.where(lane == 2, i1 - MOE_GROUPS,
                                         jnp.where(lane == 3, i2 - MOE_GROUPS, 0.0))))


def _post_norm_route(h, y, m_ref, ln_ref, wr_ref, br_ref, h_out, t_out, r_out):
    gate = m_ref[0, 2:3, :]
    shift = m_ref[0, 3:4, :]
    scale = m_ref[0, 4:5, :]
    h1 = _layer_norm(DEEPNORM_ALPHA * h + gate * y, ln_ref[0:1, :], ln_ref[1:2, :])
    t = h1 * (1.0 + scale) + shift
    h_out[0] = h1
    t_out[0] = t.astype(BF16)
    r_out[0] = _route(t, wr_ref, br_ref)


def _epilogue_specs(b, n, d, tm):
    in_specs = [
        pl.BlockSpec((1, N_MOD, d), lambda i, j: (i, 0, 0)),
        pl.BlockSpec((2, d), lambda i, j: (0, 0)),
        pl.BlockSpec((d, LANES), lambda i, j: (0, 0)),
        pl.BlockSpec((1, LANES), lambda i, j: (0, 0)),
    ]
    out_shape = [jax.ShapeDtypeStruct((b, n, d), F32), jax.ShapeDtypeStruct((b, n, d), BF16),
                 jax.ShapeDtypeStruct((b, n, LANES), F32)]
    out_specs = [pl.BlockSpec((1, tm, d), lambda i, j: (i, j, 0)),
                 pl.BlockSpec((1, tm, d), lambda i, j: (i, j, 0)),
                 pl.BlockSpec((1, tm, LANES), lambda i, j: (i, j, 0))]
    return in_specs, out_shape, out_specs


def _out_proj_kernel(x_ref, yr_ref, yn_ref, w_ref, m_ref, ln_ref, wr_ref, br_ref, h_out, t_out, r_out):
    y = _dot(yr_ref[0], w_ref[0:HEAD_W, :]) + _dot(yn_ref[0], w_ref[HEAD_W:2 * HEAD_W, :])
    _post_norm_route(x_ref[0], y, m_ref, ln_ref, wr_ref, br_ref, h_out, t_out, r_out)


def _out_proj(x, y_ret, y_na, w_out_bf, m, ln, router):
    b, n, d = x.shape
    tm = EPILOGUE_TILE
    ep_in, out_shape, out_specs = _epilogue_specs(b, n, d, tm)
    return pl.pallas_call(
        _out_proj_kernel,
        out_shape=out_shape,
        grid=(b, n // tm),
        in_specs=[
            pl.BlockSpec((1, tm, d), lambda i, j: (i, j, 0)),
            pl.BlockSpec((1, tm, HEAD_W), lambda i, j: (i, j, 0)),
            pl.BlockSpec((1, tm, HEAD_W), lambda i, j: (i, j, 0)),
            pl.BlockSpec((2 * HEAD_W, d), lambda i, j: (0, 0)),
        ] + ep_in,
        out_specs=out_specs,
        compiler_params=_cparams("arbitrary", "arbitrary"),
        name="out_proj",
    )(x, y_ret, y_na, w_out_bf, m, ln, *router)


POOL_HALO = max(POOL_SIZES) // 2
POOL_TILE = EPILOGUE_TILE


def _pool_kernel(x_ref, prev_ref, next_ref, pw_ref, ps_ref, m_ref, ln_ref, wr_ref, br_ref,
                 h_out, t_out, r_out, *, n):
    j = pl.program_id(1)
    nj = pl.num_programs(1)
    tm = x_ref.shape[1]
    halo = POOL_HALO
    shift = m_ref[0, 0:1, :]
    scale = m_ref[0, 1:2, :]
    x = x_ref[0]
    hm = x * (1.0 + scale) + shift
    prev = jnp.where(j > 0, prev_ref[0] * (1.0 + scale) + shift, 0.0)
    nxt = jnp.where(j < nj - 1, next_ref[0] * (1.0 + scale) + shift, 0.0)
    ext = jnp.concatenate([prev, hm, nxt], axis=0)
    pos = (j * tm + lax.broadcasted_iota(jnp.int32, (tm, 1), 0))
    ys = []
    for g, w in enumerate(POOL_SIZES):
        cols = slice(g * POOL_GROUP, (g + 1) * POOL_GROUP)
        s = ext[:, cols]
        span = 1
        while span < w:
            s = s[:s.shape[0] - span] + s[span:]
            span *= 2
        off = halo - w // 2
        win = s[off:off + tm]
        cnt = (jnp.minimum(pos + (w - w // 2), n) - jnp.maximum(pos - w // 2, 0)).astype(F32)
        z = (win / cnt - hm[:, cols]).astype(BF16)
        ys.append(_dot(z, pw_ref[g]))
    y = jnp.concatenate(ys, axis=-1) * ps_ref[...]
    _post_norm_route(x, y, m_ref, ln_ref, wr_ref, br_ref, h_out, t_out, r_out)


def _pool_mixer(h, pool_w_bf, pool_scale, m, ln, router):
    b, n, d = h.shape
    tm = POOL_TILE
    halo = POOL_HALO
    blocks_per_tile = tm // halo
    n_halo_blocks = n // halo
    ep_in, out_shape, out_specs = _epilogue_specs(b, n, d, tm)
    return pl.pallas_call(
        functools.partial(_pool_kernel, n=n),
        out_shape=out_shape,
        grid=(b, n // tm),
        in_specs=[
            pl.BlockSpec((1, tm, d), lambda i, j: (i, j, 0)),
            pl.BlockSpec((1, halo, d), lambda i, j: (i, jnp.maximum(j * blocks_per_tile - 1, 0), 0)),
            pl.BlockSpec((1, halo, d),
                         lambda i, j: (i, jnp.minimum((j + 1) * blocks_per_tile, n_halo_blocks - 1), 0)),
            pl.BlockSpec((len(POOL_SIZES), POOL_GROUP, POOL_GROUP), lambda i, j: (0, 0, 0)),
            pl.BlockSpec((1, d), lambda i, j: (0, 0)),
        ] + ep_in,
        out_specs=out_specs,
        compiler_params=_cparams("arbitrary", "arbitrary"),
        name="pool_mixer",
    )(h, h, h, pool_w_bf, pool_scale.reshape(1, d), m, ln, *router)


def _expert_kernel(n_ref, tile_ref, exp_ref, lo_ref, hi_ref, x_ref, wg_ref, wu_ref, wd_ref, buf_ref, o_ref,
                   wg_s, wu_s, wd_s):
    del n_ref, buf_ref
    g = pl.program_id(0)
    prev_e = exp_ref[jnp.maximum(g - 1, 0)]

    @pl.when((g == 0) | (exp_ref[g] != prev_e))
    def _():
        wg_s[...] = wg_ref[0, 0].astype(BF16)
        wu_s[...] = wu_ref[0, 0].astype(BF16)
        wd_s[...] = wd_ref[0, 0].astype(BF16)

    lo = lo_ref[g]
    hi = hi_ref[g]

    def expert_rows():
        x = x_ref[...]
        gate = _dot(x, wg_s[...])
        up = _dot(x, wu_s[...])
        act = (_silu(gate) * up).astype(BF16)
        return _dot(act, wd_s[...]).astype(BF16)

    @pl.when((hi > lo) & (lo == 0))
    def _():
        row = lax.broadcasted_iota(jnp.int32, (o_ref.shape[0], 1), 0)
        o_ref[...] = jnp.where(row < hi, expert_rows(), jnp.zeros((), BF16))

    @pl.when((hi > lo) & (lo > 0))
    def _():
        row = lax.broadcasted_iota(jnp.int32, (o_ref.shape[0], 1), 0)
        o_ref[...] = jnp.where((row >= lo) & (row < hi), expert_rows(), o_ref[...])


def _experts(steps, x_chunk, y_prev, w_gate, w_up, w_down, layer, chunk):
    pc, d = x_chunk.shape
    tm = MOE_TILE
    tiles = pc // tm
    hid = w_gate.shape[-1]
    w_spec = lambda shape: pl.BlockSpec(shape, lambda g, n, tile, exp, lo, hi: (layer, exp[g], 0, 0))
    in_specs = [
        pl.BlockSpec((tm, d), lambda g, n, tile, exp, lo, hi: (tile[g], 0)),
        w_spec((1, 1, d, hid)), w_spec((1, 1, d, hid)), w_spec((1, 1, hid, d)),
        pl.BlockSpec(memory_space=pl.ANY),
    ]
    args = [*steps, x_chunk, w_gate, w_up, w_down, y_prev]
    grid_spec = pltpu.PrefetchScalarGridSpec(
        num_scalar_prefetch=5,
        grid=(steps[0][0],),
        in_specs=in_specs,
        out_specs=pl.BlockSpec((tm, d), lambda g, n, tile, exp, lo, hi: (chunk * tiles + tile[g], 0)),
        scratch_shapes=[pltpu.VMEM((d, hid), BF16), pltpu.VMEM((d, hid), BF16), pltpu.VMEM((hid, d), BF16)],
    )
    return pl.pallas_call(
        _expert_kernel,
        out_shape=jax.ShapeDtypeStruct(y_prev.shape, BF16),
        grid_spec=grid_spec,
        input_output_aliases={len(args) - 1: 0},
        compiler_params=_cparams("arbitrary"),
        name="experts",
    )(*args)


ASSIGN_BITS = 16


def _dispatch_plan(route, tm, n_chunks):
    tok = route.shape[0]
    n_assign = 2 * tok
    assert n_assign <= 1 << ASSIGN_BITS
    i32 = jnp.int32
    low_mask = (1 << ASSIGN_BITS) - 1
    eid = jnp.transpose(route[:, 2:4]).astype(i32).reshape(n_assign)
    experts = jnp.arange(MOE_EXPERTS, dtype=i32)
    counts = jnp.sum((eid[:, None] == experts[None, :]).astype(i32), axis=0)
    end = jnp.cumsum(counts)
    start = end - counts
    order = lax.sort((eid << ASSIGN_BITS) | jnp.arange(n_assign, dtype=i32)) & low_mask
    tok_of_pos = order % tok
    _, pos_of_assign = lax.sort((order, jnp.arange(n_assign, dtype=i32)), num_keys=1)
    rows_per_chunk = n_assign // n_chunks
    tiles_per_chunk = rows_per_chunk // tm
    chunk_lo = (jnp.arange(n_chunks, dtype=i32) * rows_per_chunk)[:, None]
    tile_starts = chunk_lo + jnp.arange(tiles_per_chunk, dtype=i32)[None, :] * tm
    chunk_hi = chunk_lo + rows_per_chunk
    inside = (start[None, :] > chunk_lo) & (start[None, :] < chunk_hi)
    n_steps = tiles_per_chunk + jnp.sum(inside.astype(i32), axis=1, keepdims=True)
    cuts = jnp.sort(jnp.concatenate([tile_starts, jnp.where(inside, start[None, :], chunk_hi)], axis=1), axis=1)
    nxt = jnp.concatenate([cuts[:, 1:], chunk_lo + rows_per_chunk], axis=1)
    tile = jnp.minimum((cuts - chunk_lo) // tm, tiles_per_chunk - 1)
    tile_row = chunk_lo + tile * tm
    lo = cuts - tile_row
    hi = nxt - tile_row
    first_row = jnp.minimum(cuts, chunk_lo + rows_per_chunk - 1)
    expert = jnp.sum((end[None, None, :] <= first_row[:, :, None]).astype(i32), axis=2)
    return (n_steps, tile, expert, lo, hi), tok_of_pos, pos_of_assign


def _n_chunks(b):
    return 4 if b % 4 == 0 else (2 if b % 2 == 0 else 1)


def _post_norm_kernel(h_ref, y0_ref, y1_ref, r_ref, m_ref, ln_ref, o_ref):
    gate = m_ref[0, 5:6, :]
    r = r_ref[0]
    y = r[:, 0:1] * y0_ref[...].astype(F32) + r[:, 1:2] * y1_ref[...].astype(F32)
    o_ref[0] = _layer_norm(DEEPNORM_ALPHA * h_ref[0] + gate * y, ln_ref[0:1, :], ln_ref[1:2, :])


def _post_norm(h, y2, route, m, ln, chunk, n_chunks):
    b, n, d = h.shape
    tm = ROW_TILE
    bc = b // n_chunks
    b0 = chunk * bc
    tiles_per_seq = n // tm
    tiles_per_k = bc * tiles_per_seq
    row_spec = pl.BlockSpec((1, tm, d), lambda i, j: (b0 + i, j, 0))
    return pl.pallas_call(
        _post_norm_kernel,
        out_shape=jax.ShapeDtypeStruct((b, n, d), F32),
        grid=(bc, tiles_per_seq),
        in_specs=[row_spec,
                  pl.BlockSpec((tm, d), lambda i, j: (i * tiles_per_seq + j, 0)),
                  pl.BlockSpec((tm, d), lambda i, j: (tiles_per_k + i * tiles_per_seq + j, 0)),
                  pl.BlockSpec((1, tm, LANES), lambda i, j: (b0 + i, j, 0)),
                  pl.BlockSpec((1, N_MOD, d), lambda i, j: (b0 + i, 0, 0)),
                  pl.BlockSpec((2, d), lambda i, j: (0, 0))],
        out_specs=row_spec,
        input_output_aliases={0: 0},
        compiler_params=_cparams("arbitrary", "arbitrary"),
        name="post_norm",
    )(h, y2, y2, route, m, ln)


def _moe_rows(tok):
    return 2 * tok


def _moe_post_norm(h1, t, route, m, ln, w_gate, w_up, w_down, layer, row_buf):
    b, n, d = t.shape
    tok = b * n
    nch = _n_chunks(b)
    steps, tok_of_pos, pos_of_assign = _dispatch_plan(route.reshape(tok, LANES), MOE_TILE, nch)
    t2 = t.reshape(tok, d)
    pc = 2 * tok // nch
    y_sorted = row_buf
    for c in range(nch):
        x_c = t2.at[tok_of_pos[c * pc:(c + 1) * pc]].get(mode="promise_in_bounds")
        y_sorted = _experts(tuple(v[c] for v in steps), x_c, y_sorted, w_gate, w_up, w_down, layer, c)
    bc = b // nch
    pos3 = pos_of_assign.reshape(2, b, n)
    h = h1
    for c in range(nch):
        pos_c = pos3[:, c * bc:(c + 1) * bc].reshape(2 * bc * n)
        y2_c = y_sorted.at[pos_c].get(mode="promise_in_bounds")
        h = _post_norm(h, y2_c, route, m, ln, c, nch)
    return h, y_sorted


def _rope_tables(n):
    t = jnp.arange(n)
    rows = (t // GRID_W).astype(F32)
    cols = (t % GRID_W).astype(F32)
    n_freq = RET_DK // 4
    inv_freq = ROPE_BASE ** (-jnp.arange(n_freq, dtype=F32) / n_freq)
    ang = jnp.concatenate([rows[:, None] * inv_freq, cols[:, None] * inv_freq], axis=-1)
    cos, sin = jnp.cos(ang), jnp.sin(ang)
    cos2 = jnp.concatenate([cos, cos], axis=-1)
    sin2 = jnp.concatenate([-sin, sin], axis=-1)
    q_scale = RET_DK ** -0.5
    return cos2 * q_scale, sin2 * q_scale, cos2, sin2


def _router_params(w_r1, b_r1, w_r2, b_r2):
    d = w_r1.shape[0]
    w2 = jnp.transpose(w_r2, (1, 0, 2)).reshape(d, MOE_EXPERTS)
    pad = LANES - MOE_GROUPS - MOE_EXPERTS
    wr = jnp.concatenate([w_r1, w2, jnp.zeros((d, pad), F32)], axis=-1)
    br = jnp.concatenate([b_r1, b_r2.reshape(MOE_EXPERTS), jnp.zeros((pad,), F32)]).reshape(1, LANES)
    return wr, br


def kernel(x, c, ctx, c_ctx, w_mod, b_mod, ln_g, ln_b, ab_w_in, ab_w_out, ab_log_decay, ab_rpb, pool_w, pool_scale, moe_w_r1, moe_b_r1, moe_w_r2, moe_b_r2, moe_w_gate, moe_w_up, moe_w_down):
    b, n, d = x.shape
    cc = jnp.concatenate([c, c_ctx[None, :], jnp.zeros((MOD_ROWS - b - 1, d), F32)], axis=0)
    mod = _modulation(cc, w_mod, b_mod)
    h = x
    row_buf = None
    for i in range(DEPTH):
        j = i // 2
        m = mod[i, :b].reshape(b, N_MOD, d)
        ln1 = jnp.stack([ln_g[i, 0], ln_b[i, 0]])
        ln2 = jnp.stack([ln_g[i, 1], ln_b[i, 1]])
        router = _router_params(moe_w_r1[i], moe_b_r1[i], moe_w_r2[i], moe_b_r2[i])
        if i % 2 == 0:
            m_ctx = mod[i, b].reshape(1, N_MOD, d)
            w_in_bf = ab_w_in[j].astype(BF16)
            log_gamma2 = jnp.log1p(-jnp.exp(ab_log_decay[j].astype(F32)))
            rk, rv, nk, nv, rq, rg, nq, row_buf = _in_proj(h, m, w_in_bf, _rope_tables(n), _moe_rows(b * n))
            rk_c, rv_c, nk_c, nv_c = _ctx_proj(ctx, m_ctx, w_in_bf[:, :N_KV_GROUPS * HEAD_W])
            y_ret = _retention(log_gamma2, rq, rk, rv, rg, rk_c, rv_c)
            y_na = _neighbourhood_attention(nq, nk, nv, nk_c, nv_c, ab_rpb[j])
            h1, t, route = _out_proj(h, y_ret, y_na, ab_w_out[j].astype(BF16), m, ln1, router)
        else:
            h1, t, route = _pool_mixer(h, pool_w[j].astype(BF16), pool_scale[j], m, ln1, router)
        h, row_buf = _moe_post_norm(h1, t, route, m, ln2, moe_w_gate, moe_w_up, moe_w_down, i, row_buf)
    return h
```

```python
import functools

import numpy as np
import jax
import jax.numpy as jnp
from jax import lax
from jax.experimental import pallas as pl
from jax.experimental.pallas import tpu as pltpu

F32 = jnp.float32
BF16 = jnp.bfloat16
HIGHEST = lax.Precision.HIGHEST

D_MODEL = 1024
DEPTH = 2
GRID_W = 64
RET_HEADS = 4
RET_DK = 128
RET_CHUNK = 128
NA_HEADS = 8
NA_DH = 64
NA_WIN_R = 8
NA_WIN_C = 16
N_BIAS_ROWS = 2 * NA_WIN_R - 1
POOL_SIZES = (2, 4, 8, 16)
POOL_GROUP = D_MODEL // len(POOL_SIZES)
MOE_GROUPS = 4
MOE_PER_GROUP = 8
MOE_EXPERTS = MOE_GROUPS * MOE_PER_GROUP
MOE_HIDDEN = D_MODEL // 2
ROPE_BASE = 10000.0
LN_EPS = 1e-5
N_MOD = 6
DEEPNORM_ALPHA = (2 * DEPTH) ** 0.25
HEAD_W = 512
N_IN_GROUPS = 7
N_KV_GROUPS = 4
LOG2E = float(np.log2(np.e))
MASK_VALUE = -1e30

LANES = 128
VMEM_LIMIT = 56 * 1024 * 1024

NA_ROWS_PER_BLOCK = 4
ROW_TILE = 512
EPILOGUE_TILE = 1024
MOE_TILE = 512
MOD_ROWS = 24
MOD_COL_TILE = 1536


def _cparams(*sem):
    return pltpu.CompilerParams(dimension_semantics=sem, vmem_limit_bytes=VMEM_LIMIT)


def _silu(v):
    return v / (1.0 + jnp.exp(-v))


def _dot(a, b):
    return jnp.dot(a, b, preferred_element_type=F32)


def _dot_nt(a, b):
    return lax.dot_general(a, b, (((1,), (1,)), ((), ())), preferred_element_type=F32)


def _dot_tn(a, b):
    return lax.dot_general(a, b, (((0,), (0,)), ((), ())), preferred_element_type=F32)


def _mod_kernel(c_ref, w_ref, b_ref, o_ref):
    s = _silu(c_ref[...])
    o_ref[0] = jnp.dot(s, w_ref[0], precision=HIGHEST, preferred_element_type=F32) + b_ref[0]


def _modulation(cc, w_mod, b_mod):
    depth, d, n = w_mod.shape
    return pl.pallas_call(
        _mod_kernel,
        out_shape=jax.ShapeDtypeStruct((depth, MOD_ROWS, n), F32),
        grid=(depth, n // MOD_COL_TILE),
        in_specs=[
            pl.BlockSpec((MOD_ROWS, d), lambda i, j: (0, 0)),
            pl.BlockSpec((1, d, MOD_COL_TILE), lambda i, j: (i, 0, j)),
            pl.BlockSpec((1, 1, MOD_COL_TILE), lambda i, j: (i, 0, j)),
        ],
        out_specs=pl.BlockSpec((1, MOD_ROWS, MOD_COL_TILE), lambda i, j: (i, 0, j)),
        compiler_params=_cparams("arbitrary", "arbitrary"),
        name="modulation",
    )(cc, w_mod, b_mod.reshape(depth, 1, n))


def _rope(v, cos2, sin2):
    return v * cos2 + pltpu.roll(v, RET_DK // 2, axis=1) * sin2


def _in_proj_kernel(x_ref, m_ref, w_ref, cq_ref, sq_ref, ck_ref, sk_ref,
                    rk_ref, rv_ref, nk_ref, nv_ref, rq_ref, rg_ref, nq_ref, buf_ref):
    buf_ref[...] = jnp.zeros_like(buf_ref)
    shift = m_ref[0, 0:1, :]
    scale = m_ref[0, 1:2, :]
    hm = (x_ref[0] * (1.0 + scale) + shift).astype(BF16)
    outs = (rk_ref, rv_ref, nk_ref, nv_ref, rq_ref, rg_ref, nq_ref)
    for g, o_ref in enumerate(outs):
        p = _dot(hm, w_ref[:, g * HEAD_W:(g + 1) * HEAD_W])
        if g == 0 or g == 4:
            cos2 = (ck_ref if g == 0 else cq_ref)[...]
            sin2 = (sk_ref if g == 0 else sq_ref)[...]
            for hd in range(RET_HEADS):
                sl = slice(hd * RET_DK, (hd + 1) * RET_DK)
                o_ref[0, :, sl] = _rope(p[:, sl], cos2, sin2).astype(BF16)
        elif g == 6:
            o_ref[0] = (p * (NA_DH ** -0.5 * LOG2E)).astype(BF16)
        else:
            o_ref[0] = p.astype(BF16)


def _in_proj(x, m, w_in_bf, rope_tabs, buf_rows):
    b, n, d = x.shape
    tm = ROW_TILE
    tiles_per_seq = n // tm
    buf_tile = buf_rows // (b * tiles_per_seq)
    assert buf_tile * b * tiles_per_seq == buf_rows and buf_tile % 16 == 0
    tab_spec = pl.BlockSpec((tm, RET_DK), lambda i, j: (j, 0))
    out_spec = pl.BlockSpec((1, tm, HEAD_W), lambda i, j: (i, j, 0))
    return pl.pallas_call(
        _in_proj_kernel,
        out_shape=[jax.ShapeDtypeStruct((b, n, HEAD_W), BF16)] * N_IN_GROUPS
        + [jax.ShapeDtypeStruct((buf_rows, d), BF16)],
        grid=(b, tiles_per_seq),
        in_specs=[
            pl.BlockSpec((1, tm, d), lambda i, j: (i, j, 0)),
            pl.BlockSpec((1, N_MOD, d), lambda i, j: (i, 0, 0)),
            pl.BlockSpec((d, N_IN_GROUPS * HEAD_W), lambda i, j: (0, 0)),
            tab_spec, tab_spec, tab_spec, tab_spec,
        ],
        out_specs=[out_spec] * N_IN_GROUPS + [pl.BlockSpec((buf_tile, d), lambda i, j: (i * tiles_per_seq + j, 0))],
        compiler_params=_cparams("arbitrary", "arbitrary"),
        name="in_proj",
    )(x, m, w_in_bf, *rope_tabs)


def _ctx_proj_kernel(x_ref, m_ref, w_ref, rk_ref, rv_ref, nk_ref, nv_ref):
    shift = m_ref[0, 0:1, :]
    scale = m_ref[0, 1:2, :]
    hm = (x_ref[0] * (1.0 + scale) + shift).astype(BF16)
    for g, o_ref in enumerate((rk_ref, rv_ref, nk_ref, nv_ref)):
        o_ref[0] = _dot(hm, w_ref[:, g * HEAD_W:(g + 1) * HEAD_W]).astype(BF16)


def _ctx_proj(ctx, m_ctx, w_in_bf):
    b, l, d = ctx.shape
    out_spec = pl.BlockSpec((1, l, HEAD_W), lambda i: (i, 0, 0))
    return pl.pallas_call(
        _ctx_proj_kernel,
        out_shape=[jax.ShapeDtypeStruct((b, l, HEAD_W), BF16)] * N_KV_GROUPS,
        grid=(b,),
        in_specs=[
            pl.BlockSpec((1, l, d), lambda i: (i, 0, 0)),
            pl.BlockSpec((1, N_MOD, d), lambda i: (0, 0, 0)),
            pl.BlockSpec((d, N_KV_GROUPS * HEAD_W), lambda i: (0, 0)),
        ],
        out_specs=[out_spec] * N_KV_GROUPS,
        compiler_params=_cparams("arbitrary"),
        name="ctx_proj",
    )(ctx, m_ctx, w_in_bf)


def _retention_kernel(lg_ref, q_ref, k_ref, v_ref, g_ref, kc_ref, vc_ref, o_ref, u_ref, s_ref):
    hd = pl.program_id(1)
    n = q_ref.shape[1]
    c = RET_CHUNK
    dk = RET_DK
    nc = n // c
    l = kc_ref.shape[1]
    lgf = lg_ref[0, hd]
    lgb = lg_ref[1, hd]

    ii = lax.broadcasted_iota(jnp.int32, (c, c), 0).astype(F32)
    jj = lax.broadcasted_iota(jnp.int32, (c, c), 1).astype(F32)
    diff = ii - jj
    decay = (jnp.where(diff >= 0, jnp.exp(lgf * jnp.maximum(diff, 0.0)), 0.0)
             + jnp.where(diff <= 0, jnp.exp(lgb * jnp.maximum(-diff, 0.0)), 0.0))
    idx = lax.broadcasted_iota(jnp.int32, (c, 1), 0).astype(F32)
    q_dec_f = jnp.exp(lgf * (idx + 1.0))
    k_dec_f = jnp.exp(lgf * (c - 1.0 - idx))
    q_dec_b = jnp.exp(lgb * (c - idx))
    k_dec_b = jnp.exp(lgb * idx)
    ones = jnp.ones((1, dk), F32)
    chunk_dec_f = jnp.exp(ones * (lgf * c))
    chunk_dec_b = jnp.exp(ones * (lgb * c))

    pos = lax.broadcasted_iota(jnp.int32, (l, 1), 0).astype(F32)
    kc = kc_ref[0].astype(F32)
    vc = vc_ref[0]
    s_f0 = _dot_tn((kc * jnp.exp(lgf * (l - 1.0 - pos))).astype(BF16), vc)
    s_b0 = _dot_tn((kc * jnp.exp(lgb * pos)).astype(BF16), vc)

    def chunk_rows(i):
        return pl.ds(pl.multiple_of(i * c, c), c)

    def kv_step(i, carry):
        rows = chunk_rows(i)
        k_i = k_ref[0, rows, :].astype(F32)
        kk = jnp.concatenate([(k_i * k_dec_f).astype(BF16), (k_i * k_dec_b).astype(BF16)], axis=1)
        u_ref[i] = _dot_tn(kk, v_ref[0, rows, :])
        return carry

    lax.fori_loop(0, nc, kv_step, 0, unroll=True)

    def scan_f(i, s):
        s_ref[i, :, 0:dk] = s.astype(BF16)
        return s * chunk_dec_f + u_ref[i, 0:dk, :]

    lax.fori_loop(0, nc, scan_f, s_f0, unroll=True)

    def scan_b(t, s):
        i = nc - 1 - t
        s_ref[i, :, dk:2 * dk] = s.astype(BF16)
        return s * chunk_dec_b + u_ref[i, dk:2 * dk, :]

    lax.fori_loop(0, nc, scan_b, s_b0, unroll=True)

    def out_step(i, carry):
        rows = chunk_rows(i)
        q_i = q_ref[0, rows, :]
        v_i = v_ref[0, rows, :]
        att = (_dot_nt(q_i, k_ref[0, rows, :]) * decay).astype(BF16)
        inter = _dot(q_i, s_ref[i])
        o = _dot(att, v_i) + inter[:, 0:dk] * q_dec_f + inter[:, dk:2 * dk] * q_dec_b
        mu = jnp.mean(o, axis=-1, keepdims=True)
        var = jnp.mean(jnp.square(o - mu), axis=-1, keepdims=True)
        o_n = (o - mu) * lax.rsqrt(var + LN_EPS)
        gate = g_ref[0, rows, :].astype(F32)
        o_ref[0, rows, :] = (_silu(gate) * o_n).astype(BF16)
        return carry

    lax.fori_loop(0, nc, out_step, 0, unroll=True)


def _retention(log_gamma2, rq, rk, rv, rg, rk_c, rv_c):
    b, n, _ = rq.shape
    l = rk_c.shape[1]
    nc = n // RET_CHUNK
    seq_spec = pl.BlockSpec((1, n, RET_DK), lambda i, h: (i, 0, h))
    ctx_spec = pl.BlockSpec((1, l, RET_DK), lambda i, h: (i, 0, h))
    return pl.pallas_call(
        _retention_kernel,
        out_shape=jax.ShapeDtypeStruct((b, n, HEAD_W), BF16),
        grid=(b, RET_HEADS),
        in_specs=[pl.BlockSpec(memory_space=pltpu.SMEM),
                  seq_spec, seq_spec, seq_spec, seq_spec, ctx_spec, ctx_spec],
        out_specs=seq_spec,
        scratch_shapes=[pltpu.VMEM((nc, 2 * RET_DK, RET_DK), F32),
                        pltpu.VMEM((nc, RET_DK, 2 * RET_DK), BF16)],
        compiler_params=_cparams("arbitrary", "arbitrary"),
        name="retention",
    )(log_gamma2, rq, rk, rv, rg, rk_c, rv_c)


def _na_geometry(rows):
    rb = NA_ROWS_PER_BLOCK
    wr = min(NA_WIN_R, rows)
    key_rows = min(rows, rb + wr - 1)
    variants, block_variant, block_start = [], [], []
    for kb in range(rows // rb):
        q_rows = kb * rb + np.arange(rb)
        r0 = np.clip(q_rows - wr // 2, 0, rows - wr)
        ks = int(np.clip(r0.min(), 0, rows - key_rows))
        assert r0.max() + wr <= ks + key_rows
        kr = ks + np.arange(key_rows)
        valid = (kr[None, :] >= r0[:, None]) & (kr[None, :] < r0[:, None] + wr)
        ridx = np.where(valid, kr[None, :] - q_rows[:, None] + NA_WIN_R - 1, N_BIAS_ROWS).astype(np.int32)
        for vi, v in enumerate(variants):
            if np.array_equal(v, ridx):
                block_variant.append(vi)
                break
        else:
            variants.append(ridx)
            block_variant.append(len(variants) - 1)
        block_start.append(ks)
    return key_rows, np.stack(variants), block_variant, block_start


def _na_bias_table(rpb, variant_rows):
    n_heads = rpb.shape[0]
    qc = np.arange(GRID_W)[:, None]
    kc = np.arange(GRID_W)[None, :]
    wstart = np.clip(qc - NA_WIN_C // 2, 0, GRID_W - NA_WIN_C)
    col_ok = (kc >= wstart) & (kc < wstart + NA_WIN_C)
    cidx = np.clip(kc - qc + NA_WIN_C - 1, 0, 2 * NA_WIN_C - 2)
    onehot = ((cidx[None] == np.arange(2 * NA_WIN_C - 1)[:, None, None]) & col_ok[None]).astype(np.float32)
    blocks = jnp.einsum("hrd,dqk->hqrk", rpb.astype(F32) * LOG2E, jnp.asarray(onehot), precision=HIGHEST)
    blocks = jnp.where(col_ok[None, :, None, :], blocks, MASK_VALUE)
    masked = jnp.full((n_heads, GRID_W, GRID_W), MASK_VALUE, F32)
    nvar, rb, key_rows = variant_rows.shape
    rows = [jnp.concatenate([masked if r == N_BIAS_ROWS else blocks[:, :, r, :] for r in variant_rows[v, a]], axis=-1)
            for v in range(nvar) for a in range(rb)]
    return jnp.stack(rows, axis=1).reshape(n_heads, nvar, rb * GRID_W, key_rows * GRID_W)


def _na_kernel(q_ref, k_ref, v_ref, kc_ref, vc_ref, bias_ref, o_ref, *, block_variant, block_start, key_rows):
    qb = NA_ROWS_PER_BLOCK * GRID_W
    nk = key_rows * GRID_W
    first_head = lax.broadcasted_iota(jnp.int32, (qb, LANES), 1) < NA_DH
    kc = kc_ref[0]
    vc = vc_ref[0]
    for kb, (var, ks) in enumerate(zip(block_variant, block_start)):
        q = q_ref[0, kb * qb:(kb + 1) * qb, :]
        k_win = k_ref[0, ks * GRID_W:ks * GRID_W + nk, :]
        v_win = v_ref[0, ks * GRID_W:ks * GRID_W + nk, :]
        zero = jnp.zeros_like(q)
        q2 = jnp.concatenate([jnp.where(first_head, q, zero), jnp.where(first_head, zero, q)], axis=0)
        s_win = _dot_nt(q2, k_win) + jnp.concatenate([bias_ref[0, var], bias_ref[1, var]], axis=0)
        s_ctx = _dot_nt(q2, kc)
        m = jnp.maximum(jnp.max(s_win, axis=-1, keepdims=True), jnp.max(s_ctx, axis=-1, keepdims=True))
        p_win = jnp.exp2(s_win - m)
        p_ctx = jnp.exp2(s_ctx - m)
        denom = jnp.sum(p_win, axis=-1, keepdims=True) + jnp.sum(p_ctx, axis=-1, keepdims=True)
        o = (_dot(p_win.astype(BF16), v_win) + _dot(p_ctx.astype(BF16), vc)) / denom
        o_ref[0, kb * qb:(kb + 1) * qb, :] = jnp.where(first_head, o[0:qb], o[qb:2 * qb]).astype(BF16)


def _neighbourhood_attention(nq, nk, nv, nk_c, nv_c, rpb):
    b, n, _ = nq.shape
    l = nk_c.shape[1]
    rows = n // GRID_W
    key_rows, variant_rows, block_variant, block_start = _na_geometry(rows)
    bias = _na_bias_table(rpb, variant_rows)
    nvar = variant_rows.shape[0]
    qb = NA_ROWS_PER_BLOCK * GRID_W
    nkeys = key_rows * GRID_W
    seq_spec = pl.BlockSpec((1, n, LANES), lambda h, i: (i, 0, h))
    ctx_spec = pl.BlockSpec((1, l, LANES), lambda h, i: (i, 0, h))
    kern = functools.partial(_na_kernel, block_variant=tuple(block_variant), block_start=tuple(block_start),
                             key_rows=key_rows)
    return pl.pallas_call(
        kern,
        out_shape=jax.ShapeDtypeStruct((b, n, HEAD_W), BF16),
        grid=(NA_HEADS // 2, b),
        in_specs=[seq_spec, seq_spec, seq_spec, ctx_spec, ctx_spec,
                  pl.BlockSpec((2, nvar, qb, nkeys), lambda h, i: (h, 0, 0, 0))],
        out_specs=seq_spec,
        compiler_params=_cparams("arbitrary", "arbitrary"),
        name="neighbourhood_attention",
    )(nq, nk, nv, nk_c, nv_c, bias)


def _layer_norm(z, g, b):
    mu = jnp.mean(z, axis=-1, keepdims=True)
    var = jnp.mean(jnp.square(z - mu), axis=-1, keepdims=True)
    return (z - mu) * lax.rsqrt(var + LN_EPS) * g + b


def _split_bf16(v):
    hi = v.astype(BF16)
    return hi, (v - hi.astype(F32)).astype(BF16)


def _route(t, wr_ref, br_ref):
    t_hi, t_lo = _split_bf16(t)
    w_hi, w_lo = _split_bf16(wr_ref[...])
    hi = _dot(t_hi, jnp.concatenate([w_hi, w_lo], axis=1))
    logits = hi[:, 0:LANES] + (_dot(t_lo, w_hi) + hi[:, LANES:2 * LANES]) + br_ref[...]
    lane = lax.broadcasted_iota(jnp.int32, logits.shape, 1).astype(F32)
    neg = -jnp.inf
    big = float(LANES)

    def first_max(vals):
        vmax = jnp.max(vals, axis=-1, keepdims=True)
        return vmax, jnp.min(jnp.where(vals == vmax, lane, big), axis=-1, keepdims=True)

    is_grp = lane < MOE_GROUPS
    g_max, grp = first_max(jnp.where(is_grp, logits, neg))
    g_sum = jnp.sum(jnp.where(is_grp, jnp.exp(logits - g_max), 0.0), axis=-1, keepdims=True)
    gate_g = 1.0 / g_sum
    lo = MOE_GROUPS + grp * MOE_PER_GROUP
    in_grp = (lane >= lo) & (lane < lo + MOE_PER_GROUP)
    le = jnp.where(in_grp, logits, neg)
    v1, i1 = first_max(le)
    v2, i2 = first_max(jnp.where(lane == i1, neg, le))
    e21 = jnp.exp(v2 - v1)
    w1 = gate_g / (1.0 + e21)
    w2 = gate_g * e21 / (1.0 + e21)
    return jnp.where(lane == 0, w1,
                     jnp.where(lane == 1, w2,
                               jnp.where(lane == 2, i1 - MOE_GROUPS,
                                         jnp.where(lane == 3, i2 - MOE_GROUPS, 0.0))))


def _post_norm_route(h, y, m_ref, ln_ref, wr_ref, br_ref, h_out, t_out, r_out):
    gate = m_ref[0, 2:3, :]
    shift = m_ref[0, 3:4, :]
    scale = m_ref[0, 4:5, :]
    h1 = _layer_norm(DEEPNORM_ALPHA * h + gate * y, ln_ref[0:1, :], ln_ref[1:2, :])
    t = h1 * (1.0 + scale) + shift
    h_out[0] = h1
    t_out[0] = t.astype(BF16)
    r_out[0] = _route(t, wr_ref, br_ref)


def _epilogue_specs(b, n, d, tm):
    in_specs = [
        pl.BlockSpec((1, N_MOD, d), lambda i, j: (i, 0, 0)),
        pl.BlockSpec((2, d), lambda i, j: (0, 0)),
        pl.BlockSpec((d, LANES), lambda i, j: (0, 0)),
        pl.BlockSpec((1, LANES), lambda i, j: (0, 0)),
    ]
    out_shape = [jax.ShapeDtypeStruct((b, n, d), F32), jax.ShapeDtypeStruct((b, n, d), BF16),
                 jax.ShapeDtypeStruct((b, n, LANES), F32)]
    out_specs = [pl.BlockSpec((1, tm, d), lambda i, j: (i, j, 0)),
                 pl.BlockSpec((1, tm, d), lambda i, j: (i, j, 0)),
                 pl.BlockSpec((1, tm, LANES), lambda i, j: (i, j, 0))]
    return in_specs, out_shape, out_specs


def _out_proj_kernel(x_ref, yr_ref, yn_ref, w_ref, m_ref, ln_ref, wr_ref, br_ref, h_out, t_out, r_out):
    y = _dot(yr_ref[0], w_ref[0:HEAD_W, :]) + _dot(yn_ref[0], w_ref[HEAD_W:2 * HEAD_W, :])
    _post_norm_route(x_ref[0], y, m_ref, ln_ref, wr_ref, br_ref, h_out, t_out, r_out)


def _out_proj(x, y_ret, y_na, w_out_bf, m, ln, router):
    b, n, d = x.shape
    tm = EPILOGUE_TILE
    ep_in, out_shape, out_specs = _epilogue_specs(b, n, d, tm)
    return pl.pallas_call(
        _out_proj_kernel,
        out_shape=out_shape,
        grid=(b, n // tm),
        in_specs=[
            pl.BlockSpec((1, tm, d), lambda i, j: (i, j, 0)),
            pl.BlockSpec((1, tm, HEAD_W), lambda i, j: (i, j, 0)),
            pl.BlockSpec((1, tm, HEAD_W), lambda i, j: (i, j, 0)),
            pl.BlockSpec((2 * HEAD_W, d), lambda i, j: (0, 0)),
        ] + ep_in,
        out_specs=out_specs,
        compiler_params=_cparams("arbitrary", "arbitrary"),
        name="out_proj",
    )(x, y_ret, y_na, w_out_bf, m, ln, *router)


POOL_HALO = max(POOL_SIZES) // 2
POOL_TILE = EPILOGUE_TILE


def _pool_kernel(x_ref, prev_ref, next_ref, pw_ref, ps_ref, m_ref, ln_ref, wr_ref, br_ref,
                 h_out, t_out, r_out, *, n):
    j = pl.program_id(1)
    nj = pl.num_programs(1)
    tm = x_ref.shape[1]
    halo = POOL_HALO
    shift = m_ref[0, 0:1, :]
    scale = m_ref[0, 1:2, :]
    x = x_ref[0]
    hm = x * (1.0 + scale) + shift
    prev = jnp.where(j > 0, prev_ref[0] * (1.0 + scale) + shift, 0.0)
    nxt = jnp.where(j < nj - 1, next_ref[0] * (1.0 + scale) + shift, 0.0)
    ext = jnp.concatenate([prev, hm, nxt], axis=0)
    pos = (j * tm + lax.broadcasted_iota(jnp.int32, (tm, 1), 0))
    ys = []
    for g, w in enumerate(POOL_SIZES):
        cols = slice(g * POOL_GROUP, (g + 1) * POOL_GROUP)
        s = ext[:, cols]
        span = 1
        while span < w:
            s = s[:s.shape[0] - span] + s[span:]
            span *= 2
        off = halo - w // 2
        win = s[off:off + tm]
        cnt = (jnp.minimum(pos + (w - w // 2), n) - jnp.maximum(pos - w // 2, 0)).astype(F32)
        z = (win / cnt - hm[:, cols]).astype(BF16)
        ys.append(_dot(z, pw_ref[g]))
    y = jnp.concatenate(ys, axis=-1) * ps_ref[...]
    _post_norm_route(x, y, m_ref, ln_ref, wr_ref, br_ref, h_out, t_out, r_out)


def _pool_mixer(h, pool_w_bf, pool_scale, m, ln, router):
    b, n, d = h.shape
    tm = POOL_TILE
    halo = POOL_HALO
    blocks_per_tile = tm // halo
    n_halo_blocks = n // halo
    ep_in, out_shape, out_specs = _epilogue_specs(b, n, d, tm)
    return pl.pallas_call(
        functools.partial(_pool_kernel, n=n),
        out_shape=out_shape,
        grid=(b, n // tm),
        in_specs=[
            pl.BlockSpec((1, tm, d), lambda i, j: (i, j, 0)),
            pl.BlockSpec((1, halo, d), lambda i, j: (i, jnp.maximum(j * blocks_per_tile - 1, 0), 0)),
            pl.BlockSpec((1, halo, d),
                         lambda i, j: (i, jnp.minimum((j + 1) * blocks_per_tile, n_halo_blocks - 1), 0)),
            pl.BlockSpec((len(POOL_SIZES), POOL_GROUP, POOL_GROUP), lambda i, j: (0, 0, 0)),
            pl.BlockSpec((1, d), lambda i, j: (0, 0)),
        ] + ep_in,
        out_specs=out_specs,
        compiler_params=_cparams("arbitrary", "arbitrary"),
        name="pool_mixer",
    )(h, h, h, pool_w_bf, pool_scale.reshape(1, d), m, ln, *router)


def _expert_kernel(tile_ref, exp_ref, lo_ref, hi_ref, x_ref, wg_ref, wu_ref, wd_ref, buf_ref, o_ref,
                   wg_s, wu_s, wd_s):
    del buf_ref
    g = pl.program_id(0)
    prev_e = exp_ref[jnp.maximum(g - 1, 0)]

    @pl.when((g == 0) | (exp_ref[g] != prev_e))
    def _():
        wg_s[...] = wg_ref[0, 0].astype(BF16)
        wu_s[...] = wu_ref[0, 0].astype(BF16)
        wd_s[...] = wd_ref[0, 0].astype(BF16)

    lo = lo_ref[g]
    hi = hi_ref[g]

    def expert_rows():
        x = x_ref[...]
        gate = _dot(x, wg_s[...])
        up = _dot(x, wu_s[...])
        act = (_silu(gate) * up).astype(BF16)
        return _dot(act, wd_s[...]).astype(BF16)

    @pl.when((hi > lo) & (lo == 0))
    def _():
        row = lax.broadcasted_iota(jnp.int32, (o_ref.shape[0], 1), 0)
        o_ref[...] = jnp.where(row < hi, expert_rows(), jnp.zeros((), BF16))

    @pl.when((hi > lo) & (lo > 0))
    def _():
        row = lax.broadcasted_iota(jnp.int32, (o_ref.shape[0], 1), 0)
        o_ref[...] = jnp.where((row >= lo) & (row < hi), expert_rows(), o_ref[...])


def _experts(steps, x_chunk, y_prev, w_gate, w_up, w_down, layer, chunk):
    pc, d = x_chunk.shape
    tm = MOE_TILE
    tiles = pc // tm
    hid = w_gate.shape[-1]
    n_steps = steps[0].shape[0]
    w_spec = lambda shape: pl.BlockSpec(shape, lambda g, tile, exp, lo, hi: (layer, exp[g], 0, 0))
    in_specs = [
        pl.BlockSpec((tm, d), lambda g, tile, exp, lo, hi: (tile[g], 0)),
        w_spec((1, 1, d, hid)), w_spec((1, 1, d, hid)), w_spec((1, 1, hid, d)),
        pl.BlockSpec(memory_space=pl.ANY),
    ]
    args = [*steps, x_chunk, w_gate, w_up, w_down, y_prev]
    grid_spec = pltpu.PrefetchScalarGridSpec(
        num_scalar_prefetch=4,
        grid=(n_steps,),
        in_specs=in_specs,
        out_specs=pl.BlockSpec((tm, d), lambda g, tile, exp, lo, hi: (chunk * tiles + tile[g], 0)),
        scratch_shapes=[pltpu.VMEM((d, hid), BF16), pltpu.VMEM((d, hid), BF16), pltpu.VMEM((hid, d), BF16)],
    )
    return pl.pallas_call(
        _expert_kernel,
        out_shape=jax.ShapeDtypeStruct(y_prev.shape, BF16),
        grid_spec=grid_spec,
        input_output_aliases={len(args) - 1: 0},
        compiler_params=_cparams("arbitrary"),
        name="experts",
    )(*args)


ASSIGN_BITS = 16


def _dispatch_plan(route, tm, n_chunks):
    tok = route.shape[0]
    n_assign = 2 * tok
    assert n_assign <= 1 << ASSIGN_BITS
    i32 = jnp.int32
    low_mask = (1 << ASSIGN_BITS) - 1
    eid = jnp.transpose(route[:, 2:4]).astype(i32).reshape(n_assign)
    experts = jnp.arange(MOE_EXPERTS, dtype=i32)
    counts = jnp.sum((eid[:, None] == experts[None, :]).astype(i32), axis=0)
    end = jnp.cumsum(counts)
    start = end - counts
    order = lax.sort((eid << ASSIGN_BITS) | jnp.arange(n_assign, dtype=i32)) & low_mask
    tok_of_pos = order % tok
    _, pos_of_assign = lax.sort((order, jnp.arange(n_assign, dtype=i32)), num_keys=1)
    rows_per_chunk = n_assign // n_chunks
    tiles_per_chunk = rows_per_chunk // tm
    chunk_lo = (jnp.arange(n_chunks, dtype=i32) * rows_per_chunk)[:, None]
    tile_starts = chunk_lo + jnp.arange(tiles_per_chunk, dtype=i32)[None, :] * tm
    cuts = jnp.concatenate([tile_starts, jnp.clip(start[None, :], chunk_lo, chunk_lo + rows_per_chunk)], axis=1)
    cuts = jnp.sort(cuts, axis=1)
    nxt = jnp.concatenate([cuts[:, 1:], chunk_lo + rows_per_chunk], axis=1)
    tile = jnp.minimum((cuts - chunk_lo) // tm, tiles_per_chunk - 1)
    tile_row = chunk_lo + tile * tm
    lo = cuts - tile_row
    hi = nxt - tile_row
    first_row = jnp.minimum(cuts, chunk_lo + rows_per_chunk - 1)
    expert = jnp.sum((end[None, None, :] <= first_row[:, :, None]).astype(i32), axis=2)
    return (tile, expert, lo, hi), tok_of_pos, pos_of_assign


def _n_chunks(b):
    return next(c for c in (8, 4, 2, 1) if b % c == 0)


def _post_norm_kernel(h_ref, y0_ref, y1_ref, r_ref, m_ref, ln_ref, o_ref):
    gate = m_ref[0, 5:6, :]
    r = r_ref[0]
    y = r[:, 0:1] * y0_ref[...].astype(F32) + r[:, 1:2] * y1_ref[...].astype(F32)
    o_ref[0] = _layer_norm(DEEPNORM_ALPHA * h_ref[0] + gate * y, ln_ref[0:1, :], ln_ref[1:2, :])


def _post_norm(h, y2, route, m, ln, chunk, n_chunks):
    b, n, d = h.shape
    tm = ROW_TILE
    bc = b // n_chunks
    b0 = chunk * bc
    tiles_per_seq = n // tm
    tiles_per_k = bc * tiles_per_seq
    row_spec = pl.BlockSpec((1, tm, d), lambda i, j: (b0 + i, j, 0))
    return pl.pallas_call(
        _post_norm_kernel,
        out_shape=jax.ShapeDtypeStruct((b, n, d), F32),
        grid=(bc, tiles_per_seq),
        in_specs=[row_spec,
                  pl.BlockSpec((tm, d), lambda i, j: (i * tiles_per_seq + j, 0)),
                  pl.BlockSpec((tm, d), lambda i, j: (tiles_per_k + i * tiles_per_seq + j, 0)),
                  pl.BlockSpec((1, tm, LANES), lambda i, j: (b0 + i, j, 0)),
                  pl.BlockSpec((1, N_MOD, d), lambda i, j: (b0 + i, 0, 0)),
                  pl.BlockSpec((2, d), lambda i, j: (0, 0))],
        out_specs=row_spec,
        input_output_aliases={0: 0},
        compiler_params=_cparams("arbitrary", "arbitrary"),
        name="post_norm",
    )(h, y2, y2, route, m, ln)


def _moe_rows(tok):
    return 2 * tok


def _moe_post_norm(h1, t, route, m, ln, w_gate, w_up, w_down, layer, row_buf):
    b, n, d = t.shape
    tok = b * n
    nch = _n_chunks(b)
    steps, tok_of_pos, pos_of_assign = _dispatch_plan(route.reshape(tok, LANES), MOE_TILE, nch)
    t2 = t.reshape(tok, d)
    pc = 2 * tok // nch
    y_sorted = row_buf
    for c in range(nch):
        x_c = t2.at[tok_of_pos[c * pc:(c + 1) * pc]].get(mode="promise_in_bounds")
        y_sorted = _experts(tuple(v[c] for v in steps), x_c, y_sorted, w_gate, w_up, w_down, layer, c)
    bc = b // nch
    pos3 = pos_of_assign.reshape(2, b, n)
    h = h1
    for c in range(nch):
        pos_c = pos3[:, c * bc:(c + 1) * bc].reshape(2 * bc * n)
        y2_c = y_sorted.at[pos_c].get(mode="promise_in_bounds")
        h = _post_norm(h, y2_c, route, m, ln, c, nch)
    return h, y_sorted


def _rope_tables(n):
    t = jnp.arange(n)
    rows = (t // GRID_W).astype(F32)
    cols = (t % GRID_W).astype(F32)
    n_freq = RET_DK // 4
    inv_freq = ROPE_BASE ** (-jnp.arange(n_freq, dtype=F32) / n_freq)
    ang = jnp.concatenate([rows[:, None] * inv_freq, cols[:, None] * inv_freq], axis=-1)
    cos, sin = jnp.cos(ang), jnp.sin(ang)
    cos2 = jnp.concatenate([cos, cos], axis=-1)
    sin2 = jnp.concatenate([-sin, sin], axis=-1)
    q_scale = RET_DK ** -0.5
    return cos2 * q_scale, sin2 * q_scale, cos2, sin2


def _router_params(w_r1, b_r1, w_r2, b_r2):
    d = w_r1.shape[0]
    w2 = jnp.transpose(w_r2, (1, 0, 2)).reshape(d, MOE_EXPERTS)
    pad = LANES - MOE_GROUPS - MOE_EXPERTS
    wr = jnp.concatenate([w_r1, w2, jnp.zeros((d, pad), F32)], axis=-1)
    br = jnp.concatenate([b_r1, b_r2.reshape(MOE_EXPERTS), jnp.zeros((pad,), F32)]).reshape(1, LANES)
    return wr, br


def kernel(x, c, ctx, c_ctx, w_mod, b_mod, ln_g, ln_b, ab_w_in, ab_w_out, ab_log_decay, ab_rpb, pool_w, pool_scale, moe_w_r1, moe_b_r1, moe_w_r2, moe_b_r2, moe_w_gate, moe_w_up, moe_w_down):
    b, n, d = x.shape
    cc = jnp.concatenate([c, c_ctx[None, :], jnp.zeros((MOD_ROWS - b - 1, d), F32)], axis=0)
    mod = _modulation(cc, w_mod, b_mod)
    h = x
    row_buf = None
    for i in range(DEPTH):
        j = i // 2
        m = mod[i, :b].reshape(b, N_MOD, d)
        ln1 = jnp.stack([ln_g[i, 0], ln_b[i, 0]])
        ln2 = jnp.stack([ln_g[i, 1], ln_b[i, 1]])
        router = _router_params(moe_w_r1[i], moe_b_r1[i], moe_w_r2[i], moe_b_r2[i])
        if i % 2 == 0:
            m_ctx = mod[i, b].reshape(1, N_MOD, d)
            w_in_bf = ab_w_in[j].astype(BF16)
            log_gamma2 = jnp.log1p(-jnp.exp(ab_log_decay[j].astype(F32)))
            rk, rv, nk, nv, rq, rg, nq, row_buf = _in_proj(h, m, w_in_bf, _rope_tables(n), _moe_rows(b * n))
            rk_c, rv_c, nk_c, nv_c = _ctx_proj(ctx, m_ctx, w_in_bf[:, :N_KV_GROUPS * HEAD_W])
            y_ret = _retention(log_gamma2, rq, rk, rv, rg, rk_c, rv_c)
            y_na = _neighbourhood_attention(nq, nk, nv, nk_c, nv_c, ab_rpb[j])
            h1, t, route = _out_proj(h, y_ret, y_na, ab_w_out[j].astype(BF16), m, ln1, router)
        else:
            h1, t, route = _pool_mixer(h, pool_w[j].astype(BF16), pool_scale[j], m, ln1, router)
        h, row_buf = _moe_post_norm(h1, t, route, m, ln2, moe_w_gate, moe_w_up, moe_w_down, i, row_buf)
    return h
```

```python
import functools

import numpy as np
import jax
import jax.numpy as jnp
from jax import lax
from jax.experimental import pallas as pl
from jax.experimental.pallas import tpu as pltpu

F32 = jnp.float32
BF16 = jnp.bfloat16
HIGHEST = lax.Precision.HIGHEST

D_MODEL = 1024
DEPTH = 2
GRID_W = 64
RET_HEADS = 4
RET_DK = 128
RET_CHUNK = 128
NA_HEADS = 8
NA_DH = 64
NA_WIN_R = 8
NA_WIN_C = 16
N_BIAS_ROWS = 2 * NA_WIN_R - 1
POOL_SIZES = (2, 4, 8, 16)
POOL_GROUP = D_MODEL // len(POOL_SIZES)
MOE_GROUPS = 4
MOE_PER_GROUP = 8
MOE_EXPERTS = MOE_GROUPS * MOE_PER_GROUP
MOE_HIDDEN = D_MODEL // 2
ROPE_BASE = 10000.0
LN_EPS = 1e-5
N_MOD = 6
DEEPNORM_ALPHA = (2 * DEPTH) ** 0.25
HEAD_W = 512
N_IN_GROUPS = 7
N_KV_GROUPS = 4
LOG2E = float(np.log2(np.e))
MASK_VALUE = -1e30

LANES = 128
VMEM_LIMIT = 56 * 1024 * 1024

NA_ROWS_PER_BLOCK = 4
ROW_TILE = 512
EPILOGUE_TILE = 1024
MOE_TILE = 1024
MOD_ROWS = 24
MOD_COL_TILE = 1536


def _cparams(*sem):
    return pltpu.CompilerParams(dimension_semantics=sem, vmem_limit_bytes=VMEM_LIMIT)


def _silu(v):
    return v / (1.0 + jnp.exp(-v))


def _dot(a, b):
    return jnp.dot(a, b, preferred_element_type=F32)


def _dot_nt(a, b):
    return lax.dot_general(a, b, (((1,), (1,)), ((), ())), preferred_element_type=F32)


def _dot_tn(a, b):
    return lax.dot_general(a, b, (((0,), (0,)), ((), ())), preferred_element_type=F32)


def _mod_kernel(c_ref, w_ref, b_ref, o_ref):
    s = _silu(c_ref[...])
    o_ref[0] = jnp.dot(s, w_ref[0], precision=HIGHEST, preferred_element_type=F32) + b_ref[0]


def _modulation(cc, w_mod, b_mod):
    depth, d, n = w_mod.shape
    return pl.pallas_call(
        _mod_kernel,
        out_shape=jax.ShapeDtypeStruct((depth, MOD_ROWS, n), F32),
        grid=(depth, n // MOD_COL_TILE),
        in_specs=[
            pl.BlockSpec((MOD_ROWS, d), lambda i, j: (0, 0)),
            pl.BlockSpec((1, d, MOD_COL_TILE), lambda i, j: (i, 0, j)),
            pl.BlockSpec((1, 1, MOD_COL_TILE), lambda i, j: (i, 0, j)),
        ],
        out_specs=pl.BlockSpec((1, MOD_ROWS, MOD_COL_TILE), lambda i, j: (i, 0, j)),
        compiler_params=_cparams("arbitrary", "arbitrary"),
        name="modulation",
    )(cc, w_mod, b_mod.reshape(depth, 1, n))


def _rope(v, cos2, sin2):
    return v * cos2 + pltpu.roll(v, RET_DK // 2, axis=1) * sin2


def _in_proj_kernel(x_ref, m_ref, w_ref, cq_ref, sq_ref, ck_ref, sk_ref,
                    rk_ref, rv_ref, nk_ref, nv_ref, rq_ref, rg_ref, nq_ref, buf_ref):
    buf_ref[...] = jnp.zeros_like(buf_ref)
    shift = m_ref[0, 0:1, :]
    scale = m_ref[0, 1:2, :]
    hm = (x_ref[0] * (1.0 + scale) + shift).astype(BF16)
    outs = (rk_ref, rv_ref, nk_ref, nv_ref, rq_ref, rg_ref, nq_ref)
    for g, o_ref in enumerate(outs):
        p = _dot(hm, w_ref[:, g * HEAD_W:(g + 1) * HEAD_W])
        if g == 0 or g == 4:
            cos2 = (ck_ref if g == 0 else cq_ref)[...]
            sin2 = (sk_ref if g == 0 else sq_ref)[...]
            for hd in range(RET_HEADS):
                sl = slice(hd * RET_DK, (hd + 1) * RET_DK)
                o_ref[0, :, sl] = _rope(p[:, sl], cos2, sin2).astype(BF16)
        elif g == 6:
            o_ref[0] = (p * (NA_DH ** -0.5 * LOG2E)).astype(BF16)
        else:
            o_ref[0] = p.astype(BF16)


def _in_proj(x, m, w_in_bf, rope_tabs, buf_rows):
    b, n, d = x.shape
    tm = ROW_TILE
    tiles_per_seq = n // tm
    buf_tile = buf_rows // (b * tiles_per_seq)
    assert buf_tile * b * tiles_per_seq == buf_rows and buf_tile % 16 == 0
    tab_spec = pl.BlockSpec((tm, RET_DK), lambda i, j: (j, 0))
    out_spec = pl.BlockSpec((1, tm, HEAD_W), lambda i, j: (i, j, 0))
    return pl.pallas_call(
        _in_proj_kernel,
        out_shape=[jax.ShapeDtypeStruct((b, n, HEAD_W), BF16)] * N_IN_GROUPS
        + [jax.ShapeDtypeStruct((buf_rows, d), BF16)],
        grid=(b, tiles_per_seq),
        in_specs=[
            pl.BlockSpec((1, tm, d), lambda i, j: (i, j, 0)),
            pl.BlockSpec((1, N_MOD, d), lambda i, j: (i, 0, 0)),
            pl.BlockSpec((d, N_IN_GROUPS * HEAD_W), lambda i, j: (0, 0)),
            tab_spec, tab_spec, tab_spec, tab_spec,
        ],
        out_specs=[out_spec] * N_IN_GROUPS + [pl.BlockSpec((buf_tile, d), lambda i, j: (i * tiles_per_seq + j, 0))],
        compiler_params=_cparams("arbitrary", "arbitrary"),
        name="in_proj",
    )(x, m, w_in_bf, *rope_tabs)


def _ctx_proj_kernel(x_ref, m_ref, w_ref, rk_ref, rv_ref, nk_ref, nv_ref):
    shift = m_ref[0, 0:1, :]
    scale = m_ref[0, 1:2, :]
    hm = (x_ref[0] * (1.0 + scale) + shift).astype(BF16)
    for g, o_ref in enumerate((rk_ref, rv_ref, nk_ref, nv_ref)):
        o_ref[0] = _dot(hm, w_ref[:, g * HEAD_W:(g + 1) * HEAD_W]).astype(BF16)


def _ctx_proj(ctx, m_ctx, w_in_bf):
    b, l, d = ctx.shape
    out_spec = pl.BlockSpec((1, l, HEAD_W), lambda i: (i, 0, 0))
    return pl.pallas_call(
        _ctx_proj_kernel,
        out_shape=[jax.ShapeDtypeStruct((b, l, HEAD_W), BF16)] * N_KV_GROUPS,
        grid=(b,),
        in_specs=[
            pl.BlockSpec((1, l, d), lambda i: (i, 0, 0)),
            pl.BlockSpec((1, N_MOD, d), lambda i: (0, 0, 0)),
            pl.BlockSpec((d, N_KV_GROUPS * HEAD_W), lambda i: (0, 0)),
        ],
        out_specs=[out_spec] * N_KV_GROUPS,
        compiler_params=_cparams("arbitrary"),
        name="ctx_proj",
    )(ctx, m_ctx, w_in_bf)


def _retention_kernel(lg_ref, q_ref, k_ref, v_ref, g_ref, kc_ref, vc_ref, o_ref, u_ref, s_ref):
    hd = pl.program_id(1)
    n = q_ref.shape[1]
    c = RET_CHUNK
    dk = RET_DK
    nc = n // c
    l = kc_ref.shape[1]
    lgf = lg_ref[0, hd]
    lgb = lg_ref[1, hd]

    ii = lax.broadcasted_iota(jnp.int32, (c, c), 0).astype(F32)
    jj = lax.broadcasted_iota(jnp.int32, (c, c), 1).astype(F32)
    diff = ii - jj
    decay = (jnp.where(diff >= 0, jnp.exp(lgf * jnp.maximum(diff, 0.0)), 0.0)
             + jnp.where(diff <= 0, jnp.exp(lgb * jnp.maximum(-diff, 0.0)), 0.0))
    idx = lax.broadcasted_iota(jnp.int32, (c, 1), 0).astype(F32)
    q_dec_f = jnp.exp(lgf * (idx + 1.0))
    k_dec_f = jnp.exp(lgf * (c - 1.0 - idx))
    q_dec_b = jnp.exp(lgb * (c - idx))
    k_dec_b = jnp.exp(lgb * idx)
    ones = jnp.ones((1, dk), F32)
    chunk_dec_f = jnp.exp(ones * (lgf * c))
    chunk_dec_b = jnp.exp(ones * (lgb * c))

    pos = lax.broadcasted_iota(jnp.int32, (l, 1), 0).astype(F32)
    kc = kc_ref[0].astype(F32)
    vc = vc_ref[0]
    s_f0 = _dot_tn((kc * jnp.exp(lgf * (l - 1.0 - pos))).astype(BF16), vc)
    s_b0 = _dot_tn((kc * jnp.exp(lgb * pos)).astype(BF16), vc)

    def chunk_rows(i):
        return pl.ds(pl.multiple_of(i * c, c), c)

    def kv_step(i, carry):
        rows = chunk_rows(i)
        k_i = k_ref[0, rows, :].astype(F32)
        kk = jnp.concatenate([(k_i * k_dec_f).astype(BF16), (k_i * k_dec_b).astype(BF16)], axis=1)
        u_ref[i] = _dot_tn(kk, v_ref[0, rows, :])
        return carry

    lax.fori_loop(0, nc, kv_step, 0, unroll=True)

    def scan_f(i, s):
        s_ref[i, :, 0:dk] = s.astype(BF16)
        return s * chunk_dec_f + u_ref[i, 0:dk, :]

    lax.fori_loop(0, nc, scan_f, s_f0, unroll=True)

    def scan_b(t, s):
        i = nc - 1 - t
        s_ref[i, :, dk:2 * dk] = s.astype(BF16)
        return s * chunk_dec_b + u_ref[i, dk:2 * dk, :]

    lax.fori_loop(0, nc, scan_b, s_b0, unroll=True)

    def out_step(i, carry):
        rows = chunk_rows(i)
        q_i = q_ref[0, rows, :]
        v_i = v_ref[0, rows, :]
        att = (_dot_nt(q_i, k_ref[0, rows, :]) * decay).astype(BF16)
        inter = _dot(q_i, s_ref[i])
        o = _dot(att, v_i) + inter[:, 0:dk] * q_dec_f + inter[:, dk:2 * dk] * q_dec_b
        mu = jnp.mean(o, axis=-1, keepdims=True)
        var = jnp.mean(jnp.square(o - mu), axis=-1, keepdims=True)
        o_n = (o - mu) * lax.rsqrt(var + LN_EPS)
        gate = g_ref[0, rows, :].astype(F32)
        o_ref[0, rows, :] = (_silu(gate) * o_n).astype(BF16)
        return carry

    lax.fori_loop(0, nc, out_step, 0, unroll=True)


def _retention(log_gamma2, rq, rk, rv, rg, rk_c, rv_c):
    b, n, _ = rq.shape
    l = rk_c.shape[1]
    nc = n // RET_CHUNK
    seq_spec = pl.BlockSpec((1, n, RET_DK), lambda i, h: (i, 0, h))
    ctx_spec = pl.BlockSpec((1, l, RET_DK), lambda i, h: (i, 0, h))
    return pl.pallas_call(
        _retention_kernel,
        out_shape=jax.ShapeDtypeStruct((b, n, HEAD_W), BF16),
        grid=(b, RET_HEADS),
        in_specs=[pl.BlockSpec(memory_space=pltpu.SMEM),
                  seq_spec, seq_spec, seq_spec, seq_spec, ctx_spec, ctx_spec],
        out_specs=seq_spec,
        scratch_shapes=[pltpu.VMEM((nc, 2 * RET_DK, RET_DK), F32),
                        pltpu.VMEM((nc, RET_DK, 2 * RET_DK), BF16)],
        compiler_params=_cparams("arbitrary", "arbitrary"),
        name="retention",
    )(log_gamma2, rq, rk, rv, rg, rk_c, rv_c)


def _na_geometry(rows):
    rb = NA_ROWS_PER_BLOCK
    wr = min(NA_WIN_R, rows)
    key_rows = min(rows, rb + wr - 1)
    variants, block_variant, block_start = [], [], []
    for kb in range(rows // rb):
        q_rows = kb * rb + np.arange(rb)
        r0 = np.clip(q_rows - wr // 2, 0, rows - wr)
        ks = int(np.clip(r0.min(), 0, rows - key_rows))
        assert r0.max() + wr <= ks + key_rows
        kr = ks + np.arange(key_rows)
        valid = (kr[None, :] >= r0[:, None]) & (kr[None, :] < r0[:, None] + wr)
        ridx = np.where(valid, kr[None, :] - q_rows[:, None] + NA_WIN_R - 1, N_BIAS_ROWS).astype(np.int32)
        for vi, v in enumerate(variants):
            if np.array_equal(v, ridx):
                block_variant.append(vi)
                break
        else:
            variants.append(ridx)
            block_variant.append(len(variants) - 1)
        block_start.append(ks)
    return key_rows, np.stack(variants), block_variant, block_start


def _na_bias_table(rpb, variant_rows):
    n_heads = rpb.shape[0]
    qc = np.arange(GRID_W)[:, None]
    kc = np.arange(GRID_W)[None, :]
    wstart = np.clip(qc - NA_WIN_C // 2, 0, GRID_W - NA_WIN_C)
    col_ok = (kc >= wstart) & (kc < wstart + NA_WIN_C)
    cidx = np.clip(kc - qc + NA_WIN_C - 1, 0, 2 * NA_WIN_C - 2)
    onehot = ((cidx[None] == np.arange(2 * NA_WIN_C - 1)[:, None, None]) & col_ok[None]).astype(np.float32)
    blocks = jnp.einsum("hrd,dqk->hqrk", rpb.astype(F32) * LOG2E, jnp.asarray(onehot), precision=HIGHEST)
    blocks = jnp.where(col_ok[None, :, None, :], blocks, MASK_VALUE)
    masked = jnp.full((n_heads, GRID_W, GRID_W), MASK_VALUE, F32)
    nvar, rb, key_rows = variant_rows.shape
    rows = [jnp.concatenate([masked if r == N_BIAS_ROWS else blocks[:, :, r, :] for r in variant_rows[v, a]], axis=-1)
            for v in range(nvar) for a in range(rb)]
    return jnp.stack(rows, axis=1).reshape(n_heads, nvar, rb * GRID_W, key_rows * GRID_W)


def _na_kernel(q_ref, k_ref, v_ref, kc_ref, vc_ref, bias_ref, o_ref, *, block_variant, block_start, key_rows):
    qb = NA_ROWS_PER_BLOCK * GRID_W
    nk = key_rows * GRID_W
    first_head = lax.broadcasted_iota(jnp.int32, (qb, LANES), 1) < NA_DH
    kc = kc_ref[0]
    vc = vc_ref[0]
    for kb, (var, ks) in enumerate(zip(block_variant, block_start)):
        q = q_ref[0, kb * qb:(kb + 1) * qb, :]
        k_win = k_ref[0, ks * GRID_W:ks * GRID_W + nk, :]
        v_win = v_ref[0, ks * GRID_W:ks * GRID_W + nk, :]
        zero = jnp.zeros_like(q)
        q2 = jnp.concatenate([jnp.where(first_head, q, zero), jnp.where(first_head, zero, q)], axis=0)
        s_win = _dot_nt(q2, k_win) + jnp.concatenate([bias_ref[0, var], bias_ref[1, var]], axis=0)
        s_ctx = _dot_nt(q2, kc)
        m = jnp.maximum(jnp.max(s_win, axis=-1, keepdims=True), jnp.max(s_ctx, axis=-1, keepdims=True))
        p_win = jnp.exp2(s_win - m)
        p_ctx = jnp.exp2(s_ctx - m)
        denom = jnp.sum(p_win, axis=-1, keepdims=True) + jnp.sum(p_ctx, axis=-1, keepdims=True)
        o = (_dot(p_win.astype(BF16), v_win) + _dot(p_ctx.astype(BF16), vc)) / denom
        o_ref[0, kb * qb:(kb + 1) * qb, :] = jnp.where(first_head, o[0:qb], o[qb:2 * qb]).astype(BF16)


def _neighbourhood_attention(nq, nk, nv, nk_c, nv_c, rpb):
    b, n, _ = nq.shape
    l = nk_c.shape[1]
    rows = n // GRID_W
    key_rows, variant_rows, block_variant, block_start = _na_geometry(rows)
    bias = _na_bias_table(rpb, variant_rows)
    nvar = variant_rows.shape[0]
    qb = NA_ROWS_PER_BLOCK * GRID_W
    nkeys = key_rows * GRID_W
    seq_spec = pl.BlockSpec((1, n, LANES), lambda h, i: (i, 0, h))
    ctx_spec = pl.BlockSpec((1, l, LANES), lambda h, i: (i, 0, h))
    kern = functools.partial(_na_kernel, block_variant=tuple(block_variant), block_start=tuple(block_start),
                             key_rows=key_rows)
    return pl.pallas_call(
        kern,
        out_shape=jax.ShapeDtypeStruct((b, n, HEAD_W), BF16),
        grid=(NA_HEADS // 2, b),
        in_specs=[seq_spec, seq_spec, seq_spec, ctx_spec, ctx_spec,
                  pl.BlockSpec((2, nvar, qb, nkeys), lambda h, i: (h, 0, 0, 0))],
        out_specs=seq_spec,
        compiler_params=_cparams("arbitrary", "arbitrary"),
        name="neighbourhood_attention",
    )(nq, nk, nv, nk_c, nv_c, bias)


def _layer_norm(z, g, b):
    mu = jnp.mean(z, axis=-1, keepdims=True)
    var = jnp.mean(jnp.square(z - mu), axis=-1, keepdims=True)
    return (z - mu) * lax.rsqrt(var + LN_EPS) * g + b


def _split_bf16(v):
    hi = v.astype(BF16)
    return hi, (v - hi.astype(F32)).astype(BF16)


def _route(t, wr_ref, br_ref):
    t_hi, t_lo = _split_bf16(t)
    w_hi, w_lo = _split_bf16(wr_ref[...])
    hi = _dot(t_hi, jnp.concatenate([w_hi, w_lo], axis=1))
    logits = hi[:, 0:LANES] + (_dot(t_lo, w_hi) + hi[:, LANES:2 * LANES]) + br_ref[...]
    lane = lax.broadcasted_iota(jnp.int32, logits.shape, 1).astype(F32)
    neg = -jnp.inf
    big = float(LANES)

    def first_max(vals):
        vmax = jnp.max(vals, axis=-1, keepdims=True)
        return vmax, jnp.min(jnp.where(vals == vmax, lane, big), axis=-1, keepdims=True)

    is_grp = lane < MOE_GROUPS
    g_max, grp = first_max(jnp.where(is_grp, logits, neg))
    g_sum = jnp.sum(jnp.where(is_grp, jnp.exp(logits - g_max), 0.0), axis=-1, keepdims=True)
    gate_g = 1.0 / g_sum
    lo = MOE_GROUPS + grp * MOE_PER_GROUP
    in_grp = (lane >= lo) & (lane < lo + MOE_PER_GROUP)
    le = jnp.where(in_grp, logits, neg)
    v1, i1 = first_max(le)
    v2, i2 = first_max(jnp.where(lane == i1, neg, le))
    e21 = jnp.exp(v2 - v1)
    w1 = gate_g / (1.0 + e21)
    w2 = gate_g * e21 / (1.0 + e21)
    return jnp.where(lane == 0, w1,
                     jnp.where(lane == 1, w2,
                               jnp.where(lane == 2, i1 - MOE_GROUPS,
                                         jnp.where(lane == 3, i2 - MOE_GROUPS, 0.0))))


def _post_norm_route(h, y, m_ref, ln_ref, wr_ref, br_ref, h_out, t_out, r_out):
    gate = m_ref[0, 2:3, :]
    shift = m_ref[0, 3:4, :]
    scale = m_ref[0, 4:5, :]
    h1 = _layer_norm(DEEPNORM_ALPHA * h + gate * y, ln_ref[0:1, :], ln_ref[1:2, :])
    t = h1 * (1.0 + scale) + shift
    h_out[0] = h1
    t_out[0] = t.astype(BF16)
    r_out[0] = _route(t, wr_ref, br_ref)


def _epilogue_specs(b, n, d, tm):
    in_specs = [
        pl.BlockSpec((1, N_MOD, d), lambda i, j: (i, 0, 0)),
        pl.BlockSpec((2, d), lambda i, j: (0, 0)),
        pl.BlockSpec((d, LANES), lambda i, j: (0, 0)),
        pl.BlockSpec((1, LANES), lambda i, j: (0, 0)),
    ]
    out_shape = [jax.ShapeDtypeStruct((b, n, d), F32), jax.ShapeDtypeStruct((b, n, d), BF16),
                 jax.ShapeDtypeStruct((b, n, LANES), F32)]
    out_specs = [pl.BlockSpec((1, tm, d), lambda i, j: (i, j, 0)),
                 pl.BlockSpec((1, tm, d), lambda i, j: (i, j, 0)),
                 pl.BlockSpec((1, tm, LANES), lambda i, j: (i, j, 0))]
    return in_specs, out_shape, out_specs


def _out_proj_kernel(x_ref, yr_ref, yn_ref, w_ref, m_ref, ln_ref, wr_ref, br_ref, h_out, t_out, r_out):
    y = _dot(yr_ref[0], w_ref[0:HEAD_W, :]) + _dot(yn_ref[0], w_ref[HEAD_W:2 * HEAD_W, :])
    _post_norm_route(x_ref[0], y, m_ref, ln_ref, wr_ref, br_ref, h_out, t_out, r_out)


def _out_proj(x, y_ret, y_na, w_out_bf, m, ln, router):
    b, n, d = x.shape
    tm = EPILOGUE_TILE
    ep_in, out_shape, out_specs = _epilogue_specs(b, n, d, tm)
    return pl.pallas_call(
        _out_proj_kernel,
        out_shape=out_shape,
        grid=(b, n // tm),
        in_specs=[
            pl.BlockSpec((1, tm, d), lambda i, j: (i, j, 0)),
            pl.BlockSpec((1, tm, HEAD_W), lambda i, j: (i, j, 0)),
            pl.BlockSpec((1, tm, HEAD_W), lambda i, j: (i, j, 0)),
            pl.BlockSpec((2 * HEAD_W, d), lambda i, j: (0, 0)),
        ] + ep_in,
        out_specs=out_specs,
        compiler_params=_cparams("arbitrary", "arbitrary"),
        name="out_proj",
    )(x, y_ret, y_na, w_out_bf, m, ln, *router)


POOL_HALO = max(POOL_SIZES) // 2
POOL_TILE = EPILOGUE_TILE


def _pool_kernel(x_ref, prev_ref, next_ref, pw_ref, ps_ref, m_ref, ln_ref, wr_ref, br_ref,
                 h_out, t_out, r_out, *, n):
    j = pl.program_id(1)
    nj = pl.num_programs(1)
    tm = x_ref.shape[1]
    halo = POOL_HALO
    shift = m_ref[0, 0:1, :]
    scale = m_ref[0, 1:2, :]
    x = x_ref[0]
    hm = x * (1.0 + scale) + shift
    prev = jnp.where(j > 0, prev_ref[0] * (1.0 + scale) + shift, 0.0)
    nxt = jnp.where(j < nj - 1, next_ref[0] * (1.0 + scale) + shift, 0.0)
    ext = jnp.concatenate([prev, hm, nxt], axis=0)
    pos = (j * tm + lax.broadcasted_iota(jnp.int32, (tm, 1), 0))
    ys = []
    for g, w in enumerate(POOL_SIZES):
        cols = slice(g * POOL_GROUP, (g + 1) * POOL_GROUP)
        s = ext[:, cols]
        span = 1
        while span < w:
            s = s[:s.shape[0] - span] + s[span:]
            span *= 2
        off = halo - w // 2
        win = s[off:off + tm]
        cnt = (jnp.minimum(pos + (w - w // 2), n) - jnp.maximum(pos - w // 2, 0)).astype(F32)
        z = (win / cnt - hm[:, cols]).astype(BF16)
        ys.append(_dot(z, pw_ref[g]))
    y = jnp.concatenate(ys, axis=-1) * ps_ref[...]
    _post_norm_route(x, y, m_ref, ln_ref, wr_ref, br_ref, h_out, t_out, r_out)


def _pool_mixer(h, pool_w_bf, pool_scale, m, ln, router):
    b, n, d = h.shape
    tm = POOL_TILE
    halo = POOL_HALO
    blocks_per_tile = tm // halo
    n_halo_blocks = n // halo
    ep_in, out_shape, out_specs = _epilogue_specs(b, n, d, tm)
    return pl.pallas_call(
        functools.partial(_pool_kernel, n=n),
        out_shape=out_shape,
        grid=(b, n // tm),
        in_specs=[
            pl.BlockSpec((1, tm, d), lambda i, j: (i, j, 0)),
            pl.BlockSpec((1, halo, d), lambda i, j: (i, jnp.maximum(j * blocks_per_tile - 1, 0), 0)),
            pl.BlockSpec((1, halo, d),
                         lambda i, j: (i, jnp.minimum((j + 1) * blocks_per_tile, n_halo_blocks - 1), 0)),
            pl.BlockSpec((len(POOL_SIZES), POOL_GROUP, POOL_GROUP), lambda i, j: (0, 0, 0)),
            pl.BlockSpec((1, d), lambda i, j: (0, 0)),
        ] + ep_in,
        out_specs=out_specs,
        compiler_params=_cparams("arbitrary", "arbitrary"),
        name="pool_mixer",
    )(h, h, h, pool_w_bf, pool_scale.reshape(1, d), m, ln, *router)


def _expert_kernel(tile_ref, exp_ref, lo_ref, hi_ref, x_ref, wg_ref, wu_ref, wd_ref, buf_ref, o_ref,
                   wg_s, wu_s, wd_s):
    del buf_ref
    g = pl.program_id(0)
    prev_e = exp_ref[jnp.maximum(g - 1, 0)]

    @pl.when((g == 0) | (exp_ref[g] != prev_e))
    def _():
        wg_s[...] = wg_ref[0, 0].astype(BF16)
        wu_s[...] = wu_ref[0, 0].astype(BF16)
        wd_s[...] = wd_ref[0, 0].astype(BF16)

    lo = lo_ref[g]
    hi = hi_ref[g]

    def expert_rows():
        x = x_ref[...]
        gate = _dot(x, wg_s[...])
        up = _dot(x, wu_s[...])
        act = (_silu(gate) * up).astype(BF16)
        return _dot(act, wd_s[...]).astype(BF16)

    @pl.when((hi > lo) & (lo == 0))
    def _():
        row = lax.broadcasted_iota(jnp.int32, (o_ref.shape[0], 1), 0)
        o_ref[...] = jnp.where(row < hi, expert_rows(), jnp.zeros((), BF16))

    @pl.when((hi > lo) & (lo > 0))
    def _():
        row = lax.broadcasted_iota(jnp.int32, (o_ref.shape[0], 1), 0)
        o_ref[...] = jnp.where((row >= lo) & (row < hi), expert_rows(), o_ref[...])


def _experts(steps, x_chunk, y_prev, w_gate, w_up, w_down, layer, chunk):
    pc, d = x_chunk.shape
    tm = MOE_TILE
    tiles = pc // tm
    hid = w_gate.shape[-1]
    n_steps = steps[0].shape[0]
    w_spec = lambda shape: pl.BlockSpec(shape, lambda g, tile, exp, lo, hi: (layer, exp[g], 0, 0))
    in_specs = [
        pl.BlockSpec((tm, d), lambda g, tile, exp, lo, hi: (tile[g], 0)),
        w_spec((1, 1, d, hid)), w_spec((1, 1, d, hid)), w_spec((1, 1, hid, d)),
        pl.BlockSpec(memory_space=pl.ANY),
    ]
    args = [*steps, x_chunk, w_gate, w_up, w_down, y_prev]
    grid_spec = pltpu.PrefetchScalarGridSpec(
        num_scalar_prefetch=4,
        grid=(n_steps,),
        in_specs=in_specs,
        out_specs=pl.BlockSpec((tm, d), lambda g, tile, exp, lo, hi: (chunk * tiles + tile[g], 0)),
        scratch_shapes=[pltpu.VMEM((d, hid), BF16), pltpu.VMEM((d, hid), BF16), pltpu.VMEM((hid, d), BF16)],
    )
    return pl.pallas_call(
        _expert_kernel,
        out_shape=jax.ShapeDtypeStruct(y_prev.shape, BF16),
        grid_spec=grid_spec,
        input_output_aliases={len(args) - 1: 0},
        compiler_params=_cparams("arbitrary"),
        name="experts",
    )(*args)


ASSIGN_BITS = 16


def _dispatch_plan(route, tm, n_chunks):
    tok = route.shape[0]
    n_assign = 2 * tok
    assert n_assign <= 1 << ASSIGN_BITS
    i32 = jnp.int32
    low_mask = (1 << ASSIGN_BITS) - 1
    eid = jnp.transpose(route[:, 2:4]).astype(i32).reshape(n_assign)
    experts = jnp.arange(MOE_EXPERTS, dtype=i32)
    counts = jnp.sum((eid[:, None] == experts[None, :]).astype(i32), axis=0)
    end = jnp.cumsum(counts)
    start = end - counts
    order = lax.sort((eid << ASSIGN_BITS) | jnp.arange(n_assign, dtype=i32)) & low_mask
    tok_of_pos = order % tok
    _, pos_of_assign = lax.sort((order, jnp.arange(n_assign, dtype=i32)), num_keys=1)
    rows_per_chunk = n_assign // n_chunks
    tiles_per_chunk = rows_per_chunk // tm
    chunk_lo = (jnp.arange(n_chunks, dtype=i32) * rows_per_chunk)[:, None]
    tile_starts = chunk_lo + jnp.arange(tiles_per_chunk, dtype=i32)[None, :] * tm
    cuts = jnp.concatenate([tile_starts, jnp.clip(start[None, :], chunk_lo, chunk_lo + rows_per_chunk)], axis=1)
    cuts = jnp.sort(cuts, axis=1)
    nxt = jnp.concatenate([cuts[:, 1:], chunk_lo + rows_per_chunk], axis=1)
    tile = jnp.minimum((cuts - chunk_lo) // tm, tiles_per_chunk - 1)
    tile_row = chunk_lo + tile * tm
    lo = cuts - tile_row
    hi = nxt - tile_row
    first_row = jnp.minimum(cuts, chunk_lo + rows_per_chunk - 1)
    expert = jnp.sum((end[None, None, :] <= first_row[:, :, None]).astype(i32), axis=2)
    return (tile, expert, lo, hi), tok_of_pos, pos_of_assign


def _n_chunks(b):
    return 4 if b % 4 == 0 else (2 if b % 2 == 0 else 1)


def _post_norm_kernel(h_ref, y0_ref, y1_ref, r_ref, m_ref, ln_ref, o_ref):
    gate = m_ref[0, 5:6, :]
    r = r_ref[0]
    y = r[:, 0:1] * y0_ref[...].astype(F32) + r[:, 1:2] * y1_ref[...].astype(F32)
    o_ref[0] = _layer_norm(DEEPNORM_ALPHA * h_ref[0] + gate * y, ln_ref[0:1, :], ln_ref[1:2, :])


def _post_norm(h, y2, route, m, ln, chunk, n_chunks):
    b, n, d = h.shape
    tm = ROW_TILE
    bc = b // n_chunks
    b0 = chunk * bc
    tiles_per_seq = n // tm
    tiles_per_k = bc * tiles_per_seq
    row_spec = pl.BlockSpec((1, tm, d), lambda i, j: (b0 + i, j, 0))
    return pl.pallas_call(
        _post_norm_kernel,
        out_shape=jax.ShapeDtypeStruct((b, n, d), F32),
        grid=(bc, tiles_per_seq),
        in_specs=[row_spec,
                  pl.BlockSpec((tm, d), lambda i, j: (i * tiles_per_seq + j, 0)),
                  pl.BlockSpec((tm, d), lambda i, j: (tiles_per_k + i * tiles_per_seq + j, 0)),
                  pl.BlockSpec((1, tm, LANES), lambda i, j: (b0 + i, j, 0)),
                  pl.BlockSpec((1, N_MOD, d), lambda i, j: (b0 + i, 0, 0)),
                  pl.BlockSpec((2, d), lambda i, j: (0, 0))],
        out_specs=row_spec,
        input_output_aliases={0: 0},
        compiler_params=_cparams("arbitrary", "arbitrary"),
        name="post_norm",
    )(h, y2, y2, route, m, ln)


def _moe_rows(tok):
    return 2 * tok


def _moe_post_norm(h1, t, route, m, ln, w_gate, w_up, w_down, layer, row_buf):
    b, n, d = t.shape
    tok = b * n
    nch = _n_chunks(b)
    steps, tok_of_pos, pos_of_assign = _dispatch_plan(route.reshape(tok, LANES), MOE_TILE, nch)
    t2 = t.reshape(tok, d)
    pc = 2 * tok // nch
    y_sorted = row_buf
    for c in range(nch):
        x_c = t2.at[tok_of_pos[c * pc:(c + 1) * pc]].get(mode="promise_in_bounds")
        y_sorted = _experts(tuple(v[c] for v in steps), x_c, y_sorted, w_gate, w_up, w_down, layer, c)
    bc = b // nch
    pos3 = pos_of_assign.reshape(2, b, n)
    h = h1
    for c in range(nch):
        pos_c = pos3[:, c * bc:(c + 1) * bc].reshape(2 * bc * n)
        y2_c = y_sorted.at[pos_c].get(mode="promise_in_bounds")
        h = _post_norm(h, y2_c, route, m, ln, c, nch)
    return h, y_sorted


def _rope_tables(n):
    t = jnp.arange(n)
    rows = (t // GRID_W).astype(F32)
    cols = (t % GRID_W).astype(F32)
    n_freq = RET_DK // 4
    inv_freq = ROPE_BASE ** (-jnp.arange(n_freq, dtype=F32) / n_freq)
    ang = jnp.concatenate([rows[:, None] * inv_freq, cols[:, None] * inv_freq], axis=-1)
    cos, sin = jnp.cos(ang), jnp.sin(ang)
    cos2 = jnp.concatenate([cos, cos], axis=-1)
    sin2 = jnp.concatenate([-sin, sin], axis=-1)
    q_scale = RET_DK ** -0.5
    return cos2 * q_scale, sin2 * q_scale, cos2, sin2


def _router_params(w_r1, b_r1, w_r2, b_r2):
    d = w_r1.shape[0]
    w2 = jnp.transpose(w_r2, (1, 0, 2)).reshape(d, MOE_EXPERTS)
    pad = LANES - MOE_GROUPS - MOE_EXPERTS
    wr = jnp.concatenate([w_r1, w2, jnp.zeros((d, pad), F32)], axis=-1)
    br = jnp.concatenate([b_r1, b_r2.reshape(MOE_EXPERTS), jnp.zeros((pad,), F32)]).reshape(1, LANES)
    return wr, br


def kernel(x, c, ctx, c_ctx, w_mod, b_mod, ln_g, ln_b, ab_w_in, ab_w_out, ab_log_decay, ab_rpb, pool_w, pool_scale, moe_w_r1, moe_b_r1, moe_w_r2, moe_b_r2, moe_w_gate, moe_w_up, moe_w_down):
    b, n, d = x.shape
    cc = jnp.concatenate([c, c_ctx[None, :], jnp.zeros((MOD_ROWS - b - 1, d), F32)], axis=0)
    mod = _modulation(cc, w_mod, b_mod)
    h = x
    row_buf = None
    for i in range(DEPTH):
        j = i // 2
        m = mod[i, :b].reshape(b, N_MOD, d)
        ln1 = jnp.stack([ln_g[i, 0], ln_b[i, 0]])
        ln2 = jnp.stack([ln_g[i, 1], ln_b[i, 1]])
        router = _router_params(moe_w_r1[i], moe_b_r1[i], moe_w_r2[i], moe_b_r2[i])
        if i % 2 == 0:
            m_ctx = mod[i, b].reshape(1, N_MOD, d)
            w_in_bf = ab_w_in[j].astype(BF16)
            log_gamma2 = jnp.log1p(-jnp.exp(ab_log_decay[j].astype(F32)))
            rk, rv, nk, nv, rq, rg, nq, row_buf = _in_proj(h, m, w_in_bf, _rope_tables(n), _moe_rows(b * n))
            rk_c, rv_c, nk_c, nv_c = _ctx_proj(ctx, m_ctx, w_in_bf[:, :N_KV_GROUPS * HEAD_W])
            y_ret = _retention(log_gamma2, rq, rk, rv, rg, rk_c, rv_c)
            y_na = _neighbourhood_attention(nq, nk, nv, nk_c, nv_c, ab_rpb[j])
            h1, t, route = _out_proj(h, y_ret, y_na, ab_w_out[j].astype(BF16), m, ln1, router)
        else:
            h1, t, route = _pool_mixer(h, pool_w[j].astype(BF16), pool_scale[j], m, ln1, router)
        h, row_buf = _moe_post_norm(h1, t, route, m, ln2, moe_w_gate, moe_w_up, moe_w_down, i, row_buf)
    return h
```

```python
import functools

import numpy as np
import jax
import jax.numpy as jnp
from jax import lax
from jax.experimental import pallas as pl
from jax.experimental.pallas import tpu as pltpu

F32 = jnp.float32
BF16 = jnp.bfloat16
HIGHEST = lax.Precision.HIGHEST

D_MODEL = 1024
DEPTH = 2
GRID_W = 64
RET_HEADS = 4
RET_DK = 128
RET_CHUNK = 128
NA_HEADS = 8
NA_DH = 64
NA_WIN_R = 8
NA_WIN_C = 16
N_BIAS_ROWS = 2 * NA_WIN_R - 1
POOL_SIZES = (2, 4, 8, 16)
POOL_GROUP = D_MODEL // len(POOL_SIZES)
MOE_GROUPS = 4
MOE_PER_GROUP = 8
MOE_EXPERTS = MOE_GROUPS * MOE_PER_GROUP
MOE_HIDDEN = D_MODEL // 2
ROPE_BASE = 10000.0
LN_EPS = 1e-5
N_MOD = 6
DEEPNORM_ALPHA = (2 * DEPTH) ** 0.25
HEAD_W = 512
N_IN_GROUPS = 7
N_KV_GROUPS = 4
LOG2E = float(np.log2(np.e))
MASK_VALUE = -1e30

LANES = 128
VMEM_LIMIT = 56 * 1024 * 1024

NA_ROWS_PER_BLOCK = 4
ROW_TILE = 1024
EPILOGUE_TILE = 1024
MOE_TILE = 1024
MOD_ROWS = 24
MOD_COL_TILE = 1536


def _cparams(*sem):
    return pltpu.CompilerParams(dimension_semantics=sem, vmem_limit_bytes=VMEM_LIMIT)


def _silu(v):
    return v / (1.0 + jnp.exp(-v))


def _dot(a, b):
    return jnp.dot(a, b, preferred_element_type=F32)


def _dot_nt(a, b):
    return lax.dot_general(a, b, (((1,), (1,)), ((), ())), preferred_element_type=F32)


def _dot_tn(a, b):
    return lax.dot_general(a, b, (((0,), (0,)), ((), ())), preferred_element_type=F32)


def _mod_kernel(c_ref, w_ref, b_ref, o_ref):
    s = _silu(c_ref[...])
    o_ref[0] = jnp.dot(s, w_ref[0], precision=HIGHEST, preferred_element_type=F32) + b_ref[0]


def _modulation(cc, w_mod, b_mod):
    depth, d, n = w_mod.shape
    return pl.pallas_call(
        _mod_kernel,
        out_shape=jax.ShapeDtypeStruct((depth, MOD_ROWS, n), F32),
        grid=(depth, n // MOD_COL_TILE),
        in_specs=[
            pl.BlockSpec((MOD_ROWS, d), lambda i, j: (0, 0)),
            pl.BlockSpec((1, d, MOD_COL_TILE), lambda i, j: (i, 0, j)),
            pl.BlockSpec((1, 1, MOD_COL_TILE), lambda i, j: (i, 0, j)),
        ],
        out_specs=pl.BlockSpec((1, MOD_ROWS, MOD_COL_TILE), lambda i, j: (i, 0, j)),
        compiler_params=_cparams("arbitrary", "arbitrary"),
        name="modulation",
    )(cc, w_mod, b_mod.reshape(depth, 1, n))


def _rope(v, cos2, sin2):
    return v * cos2 + pltpu.roll(v, RET_DK // 2, axis=1) * sin2


def _in_proj_kernel(x_ref, m_ref, w_ref, cq_ref, sq_ref, ck_ref, sk_ref,
                    rk_ref, rv_ref, nk_ref, nv_ref, rq_ref, rg_ref, nq_ref, buf_ref):
    buf_ref[...] = jnp.zeros_like(buf_ref)
    shift = m_ref[0, 0:1, :]
    scale = m_ref[0, 1:2, :]
    hm = (x_ref[0] * (1.0 + scale) + shift).astype(BF16)
    outs = (rk_ref, rv_ref, nk_ref, nv_ref, rq_ref, rg_ref, nq_ref)
    for g, o_ref in enumerate(outs):
        p = _dot(hm, w_ref[:, g * HEAD_W:(g + 1) * HEAD_W])
        if g == 0 or g == 4:
            cos2 = (ck_ref if g == 0 else cq_ref)[...]
            sin2 = (sk_ref if g == 0 else sq_ref)[...]
            for hd in range(RET_HEADS):
                sl = slice(hd * RET_DK, (hd + 1) * RET_DK)
                o_ref[0, :, sl] = _rope(p[:, sl], cos2, sin2).astype(BF16)
        elif g == 6:
            o_ref[0] = (p * (NA_DH ** -0.5 * LOG2E)).astype(BF16)
        else:
            o_ref[0] = p.astype(BF16)


def _in_proj(x, m, w_in_bf, rope_tabs, buf_rows):
    b, n, d = x.shape
    tm = ROW_TILE
    tiles_per_seq = n // tm
    buf_tile = buf_rows // (b * tiles_per_seq)
    assert buf_tile * b * tiles_per_seq == buf_rows and buf_tile % 16 == 0
    tab_spec = pl.BlockSpec((tm, RET_DK), lambda i, j: (j, 0))
    out_spec = pl.BlockSpec((1, tm, HEAD_W), lambda i, j: (i, j, 0))
    return pl.pallas_call(
        _in_proj_kernel,
        out_shape=[jax.ShapeDtypeStruct((b, n, HEAD_W), BF16)] * N_IN_GROUPS
        + [jax.ShapeDtypeStruct((buf_rows, d), BF16)],
        grid=(b, tiles_per_seq),
        in_specs=[
            pl.BlockSpec((1, tm, d), lambda i, j: (i, j, 0)),
            pl.BlockSpec((1, N_MOD, d), lambda i, j: (i, 0, 0)),
            pl.BlockSpec((d, N_IN_GROUPS * HEAD_W), lambda i, j: (0, 0)),
            tab_spec, tab_spec, tab_spec, tab_spec,
        ],
        out_specs=[out_spec] * N_IN_GROUPS + [pl.BlockSpec((buf_tile, d), lambda i, j: (i * tiles_per_seq + j, 0))],
        compiler_params=_cparams("arbitrary", "arbitrary"),
        name="in_proj",
    )(x, m, w_in_bf, *rope_tabs)


def _ctx_proj_kernel(x_ref, m_ref, w_ref, rk_ref, rv_ref, nk_ref, nv_ref):
    shift = m_ref[0, 0:1, :]
    scale = m_ref[0, 1:2, :]
    hm = (x_ref[0] * (1.0 + scale) + shift).astype(BF16)
    for g, o_ref in enumerate((rk_ref, rv_ref, nk_ref, nv_ref)):
        o_ref[0] = _dot(hm, w_ref[:, g * HEAD_W:(g + 1) * HEAD_W]).astype(BF16)


def _ctx_proj(ctx, m_ctx, w_in_bf):
    b, l, d = ctx.shape
    out_spec = pl.BlockSpec((1, l, HEAD_W), lambda i: (i, 0, 0))
    return pl.pallas_call(
        _ctx_proj_kernel,
        out_shape=[jax.ShapeDtypeStruct((b, l, HEAD_W), BF16)] * N_KV_GROUPS,
        grid=(b,),
        in_specs=[
            pl.BlockSpec((1, l, d), lambda i: (i, 0, 0)),
            pl.BlockSpec((1, N_MOD, d), lambda i: (0, 0, 0)),
            pl.BlockSpec((d, N_KV_GROUPS * HEAD_W), lambda i: (0, 0)),
        ],
        out_specs=[out_spec] * N_KV_GROUPS,
        compiler_params=_cparams("arbitrary"),
        name="ctx_proj",
    )(ctx, m_ctx, w_in_bf)


def _retention_kernel(lg_ref, q_ref, k_ref, v_ref, g_ref, kc_ref, vc_ref, o_ref, u_ref, s_ref):
    hd = pl.program_id(1)
    n = q_ref.shape[1]
    c = RET_CHUNK
    dk = RET_DK
    nc = n // c
    l = kc_ref.shape[1]
    lgf = lg_ref[0, hd]
    lgb = lg_ref[1, hd]

    ii = lax.broadcasted_iota(jnp.int32, (c, c), 0).astype(F32)
    jj = lax.broadcasted_iota(jnp.int32, (c, c), 1).astype(F32)
    diff = ii - jj
    decay = (jnp.where(diff >= 0, jnp.exp(lgf * jnp.maximum(diff, 0.0)), 0.0)
             + jnp.where(diff <= 0, jnp.exp(lgb * jnp.maximum(-diff, 0.0)), 0.0))
    idx = lax.broadcasted_iota(jnp.int32, (c, 1), 0).astype(F32)
    q_dec_f = jnp.exp(lgf * (idx + 1.0))
    k_dec_f = jnp.exp(lgf * (c - 1.0 - idx))
    q_dec_b = jnp.exp(lgb * (c - idx))
    k_dec_b = jnp.exp(lgb * idx)
    ones = jnp.ones((1, dk), F32)
    chunk_dec_f = jnp.exp(ones * (lgf * c))
    chunk_dec_b = jnp.exp(ones * (lgb * c))

    pos = lax.broadcasted_iota(jnp.int32, (l, 1), 0).astype(F32)
    kc = kc_ref[0].astype(F32)
    vc = vc_ref[0]
    s_f0 = _dot_tn((kc * jnp.exp(lgf * (l - 1.0 - pos))).astype(BF16), vc)
    s_b0 = _dot_tn((kc * jnp.exp(lgb * pos)).astype(BF16), vc)

    def chunk_rows(i):
        return pl.ds(pl.multiple_of(i * c, c), c)

    def kv_step(i, carry):
        rows = chunk_rows(i)
        k_i = k_ref[0, rows, :].astype(F32)
        kk = jnp.concatenate([(k_i * k_dec_f).astype(BF16), (k_i * k_dec_b).astype(BF16)], axis=1)
        u_ref[i] = _dot_tn(kk, v_ref[0, rows, :])
        return carry

    lax.fori_loop(0, nc, kv_step, 0, unroll=True)

    def scan_f(i, s):
        s_ref[i, :, 0:dk] = s.astype(BF16)
        return s * chunk_dec_f + u_ref[i, 0:dk, :]

    lax.fori_loop(0, nc, scan_f, s_f0, unroll=True)

    def scan_b(t, s):
        i = nc - 1 - t
        s_ref[i, :, dk:2 * dk] = s.astype(BF16)
        return s * chunk_dec_b + u_ref[i, dk:2 * dk, :]

    lax.fori_loop(0, nc, scan_b, s_b0, unroll=True)

    def out_step(i, carry):
        rows = chunk_rows(i)
        q_i = q_ref[0, rows, :]
        v_i = v_ref[0, rows, :]
        att = (_dot_nt(q_i, k_ref[0, rows, :]) * decay).astype(BF16)
        inter = _dot(q_i, s_ref[i])
        o = _dot(att, v_i) + inter[:, 0:dk] * q_dec_f + inter[:, dk:2 * dk] * q_dec_b
        mu = jnp.mean(o, axis=-1, keepdims=True)
        var = jnp.mean(jnp.square(o - mu), axis=-1, keepdims=True)
        o_n = (o - mu) * lax.rsqrt(var + LN_EPS)
        gate = g_ref[0, rows, :].astype(F32)
        o_ref[0, rows, :] = (_silu(gate) * o_n).astype(BF16)
        return carry

    lax.fori_loop(0, nc, out_step, 0, unroll=True)


def _retention(log_gamma2, rq, rk, rv, rg, rk_c, rv_c):
    b, n, _ = rq.shape
    l = rk_c.shape[1]
    nc = n // RET_CHUNK
    seq_spec = pl.BlockSpec((1, n, RET_DK), lambda i, h: (i, 0, h))
    ctx_spec = pl.BlockSpec((1, l, RET_DK), lambda i, h: (i, 0, h))
    return pl.pallas_call(
        _retention_kernel,
        out_shape=jax.ShapeDtypeStruct((b, n, HEAD_W), BF16),
        grid=(b, RET_HEADS),
        in_specs=[pl.BlockSpec(memory_space=pltpu.SMEM),
                  seq_spec, seq_spec, seq_spec, seq_spec, ctx_spec, ctx_spec],
        out_specs=seq_spec,
        scratch_shapes=[pltpu.VMEM((nc, 2 * RET_DK, RET_DK), F32),
                        pltpu.VMEM((nc, RET_DK, 2 * RET_DK), BF16)],
        compiler_params=_cparams("arbitrary", "arbitrary"),
        name="retention",
    )(log_gamma2, rq, rk, rv, rg, rk_c, rv_c)


def _na_geometry(rows):
    rb = NA_ROWS_PER_BLOCK
    wr = min(NA_WIN_R, rows)
    key_rows = min(rows, rb + wr - 1)
    variants, block_variant, block_start = [], [], []
    for kb in range(rows // rb):
        q_rows = kb * rb + np.arange(rb)
        r0 = np.clip(q_rows - wr // 2, 0, rows - wr)
        ks = int(np.clip(r0.min(), 0, rows - key_rows))
        assert r0.max() + wr <= ks + key_rows
        kr = ks + np.arange(key_rows)
        valid = (kr[None, :] >= r0[:, None]) & (kr[None, :] < r0[:, None] + wr)
        ridx = np.where(valid, kr[None, :] - q_rows[:, None] + NA_WIN_R - 1, N_BIAS_ROWS).astype(np.int32)
        for vi, v in enumerate(variants):
            if np.array_equal(v, ridx):
                block_variant.append(vi)
                break
        else:
            variants.append(ridx)
            block_variant.append(len(variants) - 1)
        block_start.append(ks)
    return key_rows, np.stack(variants), block_variant, block_start


def _na_bias_table(rpb, variant_rows):
    n_heads = rpb.shape[0]
    qc = np.arange(GRID_W)[:, None]
    kc = np.arange(GRID_W)[None, :]
    wstart = np.clip(qc - NA_WIN_C // 2, 0, GRID_W - NA_WIN_C)
    col_ok = (kc >= wstart) & (kc < wstart + NA_WIN_C)
    cidx = np.clip(kc - qc + NA_WIN_C - 1, 0, 2 * NA_WIN_C - 2)
    onehot = ((cidx[None] == np.arange(2 * NA_WIN_C - 1)[:, None, None]) & col_ok[None]).astype(np.float32)
    blocks = jnp.einsum("hrd,dqk->hqrk", rpb.astype(F32) * LOG2E, jnp.asarray(onehot), precision=HIGHEST)
    blocks = jnp.where(col_ok[None, :, None, :], blocks, MASK_VALUE)
    masked = jnp.full((n_heads, GRID_W, GRID_W), MASK_VALUE, F32)
    nvar, rb, key_rows = variant_rows.shape
    rows = [jnp.concatenate([masked if r == N_BIAS_ROWS else blocks[:, :, r, :] for r in variant_rows[v, a]], axis=-1)
            for v in range(nvar) for a in range(rb)]
    return jnp.stack(rows, axis=1).reshape(n_heads, nvar, rb * GRID_W, key_rows * GRID_W)


def _na_kernel(q_ref, k_ref, v_ref, kc_ref, vc_ref, bias_ref, o_ref, *, block_variant, block_start, key_rows):
    qb = NA_ROWS_PER_BLOCK * GRID_W
    nk = key_rows * GRID_W
    first_head = lax.broadcasted_iota(jnp.int32, (qb, LANES), 1) < NA_DH
    kc = kc_ref[0]
    vc = vc_ref[0]
    for kb, (var, ks) in enumerate(zip(block_variant, block_start)):
        q = q_ref[0, kb * qb:(kb + 1) * qb, :]
        k_win = k_ref[0, ks * GRID_W:ks * GRID_W + nk, :]
        v_win = v_ref[0, ks * GRID_W:ks * GRID_W + nk, :]
        zero = jnp.zeros_like(q)
        q2 = jnp.concatenate([jnp.where(first_head, q, zero), jnp.where(first_head, zero, q)], axis=0)
        s_win = _dot_nt(q2, k_win) + jnp.concatenate([bias_ref[0, var], bias_ref[1, var]], axis=0)
        s_ctx = _dot_nt(q2, kc)
        m = jnp.maximum(jnp.max(s_win, axis=-1, keepdims=True), jnp.max(s_ctx, axis=-1, keepdims=True))
        p_win = jnp.exp2(s_win - m)
        p_ctx = jnp.exp2(s_ctx - m)
        denom = jnp.sum(p_win, axis=-1, keepdims=True) + jnp.sum(p_ctx, axis=-1, keepdims=True)
        o = (_dot(p_win.astype(BF16), v_win) + _dot(p_ctx.astype(BF16), vc)) / denom
        o_ref[0, kb * qb:(kb + 1) * qb, :] = jnp.where(first_head, o[0:qb], o[qb:2 * qb]).astype(BF16)


def _neighbourhood_attention(nq, nk, nv, nk_c, nv_c, rpb):
    b, n, _ = nq.shape
    l = nk_c.shape[1]
    rows = n // GRID_W
    key_rows, variant_rows, block_variant, block_start = _na_geometry(rows)
    bias = _na_bias_table(rpb, variant_rows)
    nvar = variant_rows.shape[0]
    qb = NA_ROWS_PER_BLOCK * GRID_W
    nkeys = key_rows * GRID_W
    seq_spec = pl.BlockSpec((1, n, LANES), lambda h, i: (i, 0, h))
    ctx_spec = pl.BlockSpec((1, l, LANES), lambda h, i: (i, 0, h))
    kern = functools.partial(_na_kernel, block_variant=tuple(block_variant), block_start=tuple(block_start),
                             key_rows=key_rows)
    return pl.pallas_call(
        kern,
        out_shape=jax.ShapeDtypeStruct((b, n, HEAD_W), BF16),
        grid=(NA_HEADS // 2, b),
        in_specs=[seq_spec, seq_spec, seq_spec, ctx_spec, ctx_spec,
                  pl.BlockSpec((2, nvar, qb, nkeys), lambda h, i: (h, 0, 0, 0))],
        out_specs=seq_spec,
        compiler_params=_cparams("arbitrary", "arbitrary"),
        name="neighbourhood_attention",
    )(nq, nk, nv, nk_c, nv_c, bias)


def _layer_norm(z, g, b):
    mu = jnp.mean(z, axis=-1, keepdims=True)
    var = jnp.mean(jnp.square(z - mu), axis=-1, keepdims=True)
    return (z - mu) * lax.rsqrt(var + LN_EPS) * g + b


def _split_bf16(v):
    hi = v.astype(BF16)
    return hi, (v - hi.astype(F32)).astype(BF16)


def _route(t, wr_ref, br_ref):
    t_hi, t_lo = _split_bf16(t)
    w_hi, w_lo = _split_bf16(wr_ref[...])
    hi = _dot(t_hi, jnp.concatenate([w_hi, w_lo], axis=1))
    logits = hi[:, 0:LANES] + (_dot(t_lo, w_hi) + hi[:, LANES:2 * LANES]) + br_ref[...]
    lane = lax.broadcasted_iota(jnp.int32, logits.shape, 1).astype(F32)
    neg = -jnp.inf
    big = float(LANES)

    def first_max(vals):
        vmax = jnp.max(vals, axis=-1, keepdims=True)
        return vmax, jnp.min(jnp.where(vals == vmax, lane, big), axis=-1, keepdims=True)

    is_grp = lane < MOE_GROUPS
    g_max, grp = first_max(jnp.where(is_grp, logits, neg))
    g_sum = jnp.sum(jnp.where(is_grp, jnp.exp(logits - g_max), 0.0), axis=-1, keepdims=True)
    gate_g = 1.0 / g_sum
    lo = MOE_GROUPS + grp * MOE_PER_GROUP
    in_grp = (lane >= lo) & (lane < lo + MOE_PER_GROUP)
    le = jnp.where(in_grp, logits, neg)
    v1, i1 = first_max(le)
    v2, i2 = first_max(jnp.where(lane == i1, neg, le))
    e21 = jnp.exp(v2 - v1)
    w1 = gate_g / (1.0 + e21)
    w2 = gate_g * e21 / (1.0 + e21)
    return jnp.where(lane == 0, w1,
                     jnp.where(lane == 1, w2,
                               jnp.where(lane == 2, i1 - MOE_GROUPS,
                                         jnp.where(lane == 3, i2 - MOE_GROUPS, 0.0))))


def _post_norm_route(h, y, m_ref, ln_ref, wr_ref, br_ref, h_out, t_out, r_out):
    gate = m_ref[0, 2:3, :]
    shift = m_ref[0, 3:4, :]
    scale = m_ref[0, 4:5, :]
    h1 = _layer_norm(DEEPNORM_ALPHA * h + gate * y, ln_ref[0:1, :], ln_ref[1:2, :])
    t = h1 * (1.0 + scale) + shift
    h_out[0] = h1
    t_out[0] = t.astype(BF16)
    r_out[0] = _route(t, wr_ref, br_ref)


def _epilogue_specs(b, n, d, tm):
    in_specs = [
        pl.BlockSpec((1, N_MOD, d), lambda i, j: (i, 0, 0)),
        pl.BlockSpec((2, d), lambda i, j: (0, 0)),
        pl.BlockSpec((d, LANES), lambda i, j: (0, 0)),
        pl.BlockSpec((1, LANES), lambda i, j: (0, 0)),
    ]
    out_shape = [jax.ShapeDtypeStruct((b, n, d), F32), jax.ShapeDtypeStruct((b, n, d), BF16),
                 jax.ShapeDtypeStruct((b, n, LANES), F32)]
    out_specs = [pl.BlockSpec((1, tm, d), lambda i, j: (i, j, 0)),
                 pl.BlockSpec((1, tm, d), lambda i, j: (i, j, 0)),
                 pl.BlockSpec((1, tm, LANES), lambda i, j: (i, j, 0))]
    return in_specs, out_shape, out_specs


def _out_proj_kernel(x_ref, yr_ref, yn_ref, w_ref, m_ref, ln_ref, wr_ref, br_ref, h_out, t_out, r_out):
    y = _dot(yr_ref[0], w_ref[0:HEAD_W, :]) + _dot(yn_ref[0], w_ref[HEAD_W:2 * HEAD_W, :])
    _post_norm_route(x_ref[0], y, m_ref, ln_ref, wr_ref, br_ref, h_out, t_out, r_out)


def _out_proj(x, y_ret, y_na, w_out_bf, m, ln, router):
    b, n, d = x.shape
    tm = EPILOGUE_TILE
    ep_in, out_shape, out_specs = _epilogue_specs(b, n, d, tm)
    return pl.pallas_call(
        _out_proj_kernel,
        out_shape=out_shape,
        grid=(b, n // tm),
        in_specs=[
            pl.BlockSpec((1, tm, d), lambda i, j: (i, j, 0)),
            pl.BlockSpec((1, tm, HEAD_W), lambda i, j: (i, j, 0)),
            pl.BlockSpec((1, tm, HEAD_W), lambda i, j: (i, j, 0)),
            pl.BlockSpec((2 * HEAD_W, d), lambda i, j: (0, 0)),
        ] + ep_in,
        out_specs=out_specs,
        compiler_params=_cparams("arbitrary", "arbitrary"),
        name="out_proj",
    )(x, y_ret, y_na, w_out_bf, m, ln, *router)


POOL_HALO = max(POOL_SIZES) // 2
POOL_TILE = EPILOGUE_TILE


def _pool_kernel(x_ref, prev_ref, next_ref, pw_ref, ps_ref, m_ref, ln_ref, wr_ref, br_ref,
                 h_out, t_out, r_out, *, n):
    j = pl.program_id(1)
    nj = pl.num_programs(1)
    tm = x_ref.shape[1]
    halo = POOL_HALO
    shift = m_ref[0, 0:1, :]
    scale = m_ref[0, 1:2, :]
    x = x_ref[0]
    hm = x * (1.0 + scale) + shift
    prev = jnp.where(j > 0, prev_ref[0] * (1.0 + scale) + shift, 0.0)
    nxt = jnp.where(j < nj - 1, next_ref[0] * (1.0 + scale) + shift, 0.0)
    ext = jnp.concatenate([prev, hm, nxt], axis=0)
    pos = (j * tm + lax.broadcasted_iota(jnp.int32, (tm, 1), 0))
    ys = []
    for g, w in enumerate(POOL_SIZES):
        cols = slice(g * POOL_GROUP, (g + 1) * POOL_GROUP)
        s = ext[:, cols]
        span = 1
        while span < w:
            s = s[:s.shape[0] - span] + s[span:]
            span *= 2
        off = halo - w // 2
        win = s[off:off + tm]
        cnt = (jnp.minimum(pos + (w - w // 2), n) - jnp.maximum(pos - w // 2, 0)).astype(F32)
        z = (win / cnt - hm[:, cols]).astype(BF16)
        ys.append(_dot(z, pw_ref[g]))
    y = jnp.concatenate(ys, axis=-1) * ps_ref[...]
    _post_norm_route(x, y, m_ref, ln_ref, wr_ref, br_ref, h_out, t_out, r_out)


def _pool_mixer(h, pool_w_bf, pool_scale, m, ln, router):
    b, n, d = h.shape
    tm = POOL_TILE
    halo = POOL_HALO
    blocks_per_tile = tm // halo
    n_halo_blocks = n // halo
    ep_in, out_shape, out_specs = _epilogue_specs(b, n, d, tm)
    return pl.pallas_call(
        functools.partial(_pool_kernel, n=n),
        out_shape=out_shape,
        grid=(b, n // tm),
        in_specs=[
            pl.BlockSpec((1, tm, d), lambda i, j: (i, j, 0)),
            pl.BlockSpec((1, halo, d), lambda i, j: (i, jnp.maximum(j * blocks_per_tile - 1, 0), 0)),
            pl.BlockSpec((1, halo, d),
                         lambda i, j: (i, jnp.minimum((j + 1) * blocks_per_tile, n_halo_blocks - 1), 0)),
            pl.BlockSpec((len(POOL_SIZES), POOL_GROUP, POOL_GROUP), lambda i, j: (0, 0, 0)),
            pl.BlockSpec((1, d), lambda i, j: (0, 0)),
        ] + ep_in,
        out_specs=out_specs,
        compiler_params=_cparams("arbitrary", "arbitrary"),
        name="pool_mixer",
    )(h, h, h, pool_w_bf, pool_scale.reshape(1, d), m, ln, *router)


def _expert_kernel(tile_ref, exp_ref, lo_ref, hi_ref, x_ref, wg_ref, wu_ref, wd_ref, buf_ref, o_ref,
                   wg_s, wu_s, wd_s):
    del buf_ref
    g = pl.program_id(0)
    prev_e = exp_ref[jnp.maximum(g - 1, 0)]

    @pl.when((g == 0) | (exp_ref[g] != prev_e))
    def _():
        wg_s[...] = wg_ref[0, 0].astype(BF16)
        wu_s[...] = wu_ref[0, 0].astype(BF16)
        wd_s[...] = wd_ref[0, 0].astype(BF16)

    lo = lo_ref[g]
    hi = hi_ref[g]

    def expert_rows():
        x = x_ref[...]
        gate = _dot(x, wg_s[...])
        up = _dot(x, wu_s[...])
        act = (_silu(gate) * up).astype(BF16)
        return _dot(act, wd_s[...]).astype(BF16)

    @pl.when((hi > lo) & (lo == 0))
    def _():
        row = lax.broadcasted_iota(jnp.int32, (o_ref.shape[0], 1), 0)
        o_ref[...] = jnp.where(row < hi, expert_rows(), jnp.zeros((), BF16))

    @pl.when((hi > lo) & (lo > 0))
    def _():
        row = lax.broadcasted_iota(jnp.int32, (o_ref.shape[0], 1), 0)
        o_ref[...] = jnp.where((row >= lo) & (row < hi), expert_rows(), o_ref[...])


def _experts(steps, x_chunk, y_prev, w_gate, w_up, w_down, layer, chunk):
    pc, d = x_chunk.shape
    tm = MOE_TILE
    tiles = pc // tm
    hid = w_gate.shape[-1]
    n_steps = steps[0].shape[0]
    w_spec = lambda shape: pl.BlockSpec(shape, lambda g, tile, exp, lo, hi: (layer, exp[g], 0, 0))
    in_specs = [
        pl.BlockSpec((tm, d), lambda g, tile, exp, lo, hi: (tile[g], 0)),
        w_spec((1, 1, d, hid)), w_spec((1, 1, d, hid)), w_spec((1, 1, hid, d)),
        pl.BlockSpec(memory_space=pl.ANY),
    ]
    args = [*steps, x_chunk, w_gate, w_up, w_down, y_prev]
    grid_spec = pltpu.PrefetchScalarGridSpec(
        num_scalar_prefetch=4,
        grid=(n_steps,),
        in_specs=in_specs,
        out_specs=pl.BlockSpec((tm, d), lambda g, tile, exp, lo, hi: (chunk * tiles + tile[g], 0)),
        scratch_shapes=[pltpu.VMEM((d, hid), BF16), pltpu.VMEM((d, hid), BF16), pltpu.VMEM((hid, d), BF16)],
    )
    return pl.pallas_call(
        _expert_kernel,
        out_shape=jax.ShapeDtypeStruct(y_prev.shape, BF16),
        grid_spec=grid_spec,
        input_output_aliases={len(args) - 1: 0},
        compiler_params=_cparams("arbitrary"),
        name="experts",
    )(*args)


ASSIGN_BITS = 16


def _dispatch_plan(route, tm, n_chunks):
    tok = route.shape[0]
    n_assign = 2 * tok
    assert n_assign <= 1 << ASSIGN_BITS
    i32 = jnp.int32
    low_mask = (1 << ASSIGN_BITS) - 1
    eid = jnp.transpose(route[:, 2:4]).astype(i32).reshape(n_assign)
    experts = jnp.arange(MOE_EXPERTS, dtype=i32)
    counts = jnp.sum((eid[:, None] == experts[None, :]).astype(i32), axis=0)
    end = jnp.cumsum(counts)
    start = end - counts
    order = lax.sort((eid << ASSIGN_BITS) | jnp.arange(n_assign, dtype=i32)) & low_mask
    tok_of_pos = order % tok
    _, pos_of_assign = lax.sort((order, jnp.arange(n_assign, dtype=i32)), num_keys=1)
    rows_per_chunk = n_assign // n_chunks
    tiles_per_chunk = rows_per_chunk // tm
    chunk_lo = (jnp.arange(n_chunks, dtype=i32) * rows_per_chunk)[:, None]
    tile_starts = chunk_lo + jnp.arange(tiles_per_chunk, dtype=i32)[None, :] * tm
    cuts = jnp.concatenate([tile_starts, jnp.clip(start[None, :], chunk_lo, chunk_lo + rows_per_chunk)], axis=1)
    cuts = jnp.sort(cuts, axis=1)
    nxt = jnp.concatenate([cuts[:, 1:], chunk_lo + rows_per_chunk], axis=1)
    tile = jnp.minimum((cuts - chunk_lo) // tm, tiles_per_chunk - 1)
    tile_row = chunk_lo + tile * tm
    lo = cuts - tile_row
    hi = nxt - tile_row
    first_row = jnp.minimum(cuts, chunk_lo + rows_per_chunk - 1)
    expert = jnp.sum((end[None, None, :] <= first_row[:, :, None]).astype(i32), axis=2)
    return (tile, expert, lo, hi), tok_of_pos, pos_of_assign


def _n_chunks(b):
    return 4 if b % 4 == 0 else (2 if b % 2 == 0 else 1)


def _post_norm_kernel(h_ref, y0_ref, y1_ref, r_ref, m_ref, ln_ref, o_ref):
    gate = m_ref[0, 5:6, :]
    r = r_ref[0]
    y = r[:, 0:1] * y0_ref[...].astype(F32) + r[:, 1:2] * y1_ref[...].astype(F32)
    o_ref[0] = _layer_norm(DEEPNORM_ALPHA * h_ref[0] + gate * y, ln_ref[0:1, :], ln_ref[1:2, :])


def _post_norm(h, y2, route, m, ln, chunk, n_chunks):
    b, n, d = h.shape
    tm = ROW_TILE
    bc = b // n_chunks
    b0 = chunk * bc
    tiles_per_seq = n // tm
    tiles_per_k = bc * tiles_per_seq
    row_spec = pl.BlockSpec((1, tm, d), lambda i, j: (b0 + i, j, 0))
    return pl.pallas_call(
        _post_norm_kernel,
        out_shape=jax.ShapeDtypeStruct((b, n, d), F32),
        grid=(bc, tiles_per_seq),
        in_specs=[row_spec,
                  pl.BlockSpec((tm, d), lambda i, j: (i * tiles_per_seq + j, 0)),
                  pl.BlockSpec((tm, d), lambda i, j: (tiles_per_k + i * tiles_per_seq + j, 0)),
                  pl.BlockSpec((1, tm, LANES), lambda i, j: (b0 + i, j, 0)),
                  pl.BlockSpec((1, N_MOD, d), lambda i, j: (b0 + i, 0, 0)),
                  pl.BlockSpec((2, d), lambda i, j: (0, 0))],
        out_specs=row_spec,
        input_output_aliases={0: 0},
        compiler_params=_cparams("arbitrary", "arbitrary"),
        name="post_norm",
    )(h, y2, y2, route, m, ln)


def _moe_rows(tok):
    return 2 * tok


def _moe_post_norm(h1, t, route, m, ln, w_gate, w_up, w_down, layer, row_buf):
    b, n, d = t.shape
    tok = b * n
    nch = _n_chunks(b)
    steps, tok_of_pos, pos_of_assign = _dispatch_plan(route.reshape(tok, LANES), MOE_TILE, nch)
    t2 = t.reshape(tok, d)
    pc = 2 * tok // nch
    y_sorted = row_buf
    for c in range(nch):
        x_c = t2.at[tok_of_pos[c * pc:(c + 1) * pc]].get(mode="promise_in_bounds")
        y_sorted = _experts(tuple(v[c] for v in steps), x_c, y_sorted, w_gate, w_up, w_down, layer, c)
    bc = b // nch
    pos3 = pos_of_assign.reshape(2, b, n)
    h = h1
    for c in range(nch):
        pos_c = pos3[:, c * bc:(c + 1) * bc].reshape(2 * bc * n)
        y2_c = y_sorted.at[pos_c].get(mode="promise_in_bounds")
        h = _post_norm(h, y2_c, route, m, ln, c, nch)
    return h, y_sorted


def _rope_tables(n):
    t = jnp.arange(n)
    rows = (t // GRID_W).astype(F32)
    cols = (t % GRID_W).astype(F32)
    n_freq = RET_DK // 4
    inv_freq = ROPE_BASE ** (-jnp.arange(n_freq, dtype=F32) / n_freq)
    ang = jnp.concatenate([rows[:, None] * inv_freq, cols[:, None] * inv_freq], axis=-1)
    cos, sin = jnp.cos(ang), jnp.sin(ang)
    cos2 = jnp.concatenate([cos, cos], axis=-1)
    sin2 = jnp.concatenate([-sin, sin], axis=-1)
    q_scale = RET_DK ** -0.5
    return cos2 * q_scale, sin2 * q_scale, cos2, sin2


def _router_params(w_r1, b_r1, w_r2, b_r2):
    d = w_r1.shape[0]
    w2 = jnp.transpose(w_r2, (1, 0, 2)).reshape(d, MOE_EXPERTS)
    pad = LANES - MOE_GROUPS - MOE_EXPERTS
    wr = jnp.concatenate([w_r1, w2, jnp.zeros((d, pad), F32)], axis=-1)
    br = jnp.concatenate([b_r1, b_r2.reshape(MOE_EXPERTS), jnp.zeros((pad,), F32)]).reshape(1, LANES)
    return wr, br


def kernel(x, c, ctx, c_ctx, w_mod, b_mod, ln_g, ln_b, ab_w_in, ab_w_out, ab_log_decay, ab_rpb, pool_w, pool_scale, moe_w_r1, moe_b_r1, moe_w_r2, moe_b_r2, moe_w_gate, moe_w_up, moe_w_down):
    b, n, d = x.shape
    cc = jnp.concatenate([c, c_ctx[None, :], jnp.zeros((MOD_ROWS - b - 1, d), F32)], axis=0)
    mod = _modulation(cc, w_mod, b_mod)
    h = x
    row_buf = None
    for i in range(DEPTH):
        j = i // 2
        m = mod[i, :b].reshape(b, N_MOD, d)
        ln1 = jnp.stack([ln_g[i, 0], ln_b[i, 0]])
        ln2 = jnp.stack([ln_g[i, 1], ln_b[i, 1]])
        router = _router_params(moe_w_r1[i], moe_b_r1[i], moe_w_r2[i], moe_b_r2[i])
        if i % 2 == 0:
            m_ctx = mod[i, b].reshape(1, N_MOD, d)
            w_in_bf = ab_w_in[j].astype(BF16)
            log_gamma2 = jnp.log1p(-jnp.exp(ab_log_decay[j].astype(F32)))
            rk, rv, nk, nv, rq, rg, nq, row_buf = _in_proj(h, m, w_in_bf, _rope_tables(n), _moe_rows(b * n))
            rk_c, rv_c, nk_c, nv_c = _ctx_proj(ctx, m_ctx, w_in_bf[:, :N_KV_GROUPS * HEAD_W])
            y_ret = _retention(log_gamma2, rq, rk, rv, rg, rk_c, rv_c)
            y_na = _neighbourhood_attention(nq, nk, nv, nk_c, nv_c, ab_rpb[j])
            h1, t, route = _out_proj(h, y_ret, y_na, ab_w_out[j].astype(BF16), m, ln1, router)
        else:
            h1, t, route = _pool_mixer(h, pool_w[j].astype(BF16), pool_scale[j], m, ln1, router)
        h, row_buf = _moe_post_norm(h1, t, route, m, ln2, moe_w_gate, moe_w_up, moe_w_down, i, row_buf)
    return h
```

```python
import functools

import numpy as np
import jax
import jax.numpy as jnp
from jax import lax
from jax.experimental import pallas as pl
from jax.experimental.pallas import tpu as pltpu

F32 = jnp.float32
BF16 = jnp.bfloat16
HIGHEST = lax.Precision.HIGHEST

D_MODEL = 1024
DEPTH = 2
GRID_W = 64
RET_HEADS = 4
RET_DK = 128
RET_CHUNK = 128
NA_HEADS = 8
NA_DH = 64
NA_WIN_R = 8
NA_WIN_C = 16
N_BIAS_ROWS = 2 * NA_WIN_R - 1
POOL_SIZES = (2, 4, 8, 16)
POOL_GROUP = D_MODEL // len(POOL_SIZES)
MOE_GROUPS = 4
MOE_PER_GROUP = 8
MOE_EXPERTS = MOE_GROUPS * MOE_PER_GROUP
MOE_HIDDEN = D_MODEL // 2
ROPE_BASE = 10000.0
LN_EPS = 1e-5
N_MOD = 6
DEEPNORM_ALPHA = (2 * DEPTH) ** 0.25
HEAD_W = 512
N_IN_GROUPS = 7
N_KV_GROUPS = 4
LOG2E = float(np.log2(np.e))
MASK_VALUE = -1e30

LANES = 128
VMEM_LIMIT = 56 * 1024 * 1024

NA_ROWS_PER_BLOCK = 4
ROW_TILE = 1024
EPILOGUE_TILE = 1024
MOE_TILE = 1024
MOD_ROWS = 24
MOD_COL_TILE = 1536


def _cparams(*sem):
    return pltpu.CompilerParams(dimension_semantics=sem, vmem_limit_bytes=VMEM_LIMIT)


def _silu(v):
    return v / (1.0 + jnp.exp(-v))


def _dot(a, b):
    return jnp.dot(a, b, preferred_element_type=F32)


def _dot_nt(a, b):
    return lax.dot_general(a, b, (((1,), (1,)), ((), ())), preferred_element_type=F32)


def _dot_tn(a, b):
    return lax.dot_general(a, b, (((0,), (0,)), ((), ())), preferred_element_type=F32)


def _mod_kernel(c_ref, w_ref, b_ref, o_ref):
    s = _silu(c_ref[...])
    o_ref[0] = jnp.dot(s, w_ref[0], precision=HIGHEST, preferred_element_type=F32) + b_ref[0]


def _modulation(cc, w_mod, b_mod):
    depth, d, n = w_mod.shape
    return pl.pallas_call(
        _mod_kernel,
        out_shape=jax.ShapeDtypeStruct((depth, MOD_ROWS, n), F32),
        grid=(depth, n // MOD_COL_TILE),
        in_specs=[
            pl.BlockSpec((MOD_ROWS, d), lambda i, j: (0, 0)),
            pl.BlockSpec((1, d, MOD_COL_TILE), lambda i, j: (i, 0, j)),
            pl.BlockSpec((1, 1, MOD_COL_TILE), lambda i, j: (i, 0, j)),
        ],
        out_specs=pl.BlockSpec((1, MOD_ROWS, MOD_COL_TILE), lambda i, j: (i, 0, j)),
        compiler_params=_cparams("arbitrary", "arbitrary"),
        name="modulation",
    )(cc, w_mod, b_mod.reshape(depth, 1, n))


def _rope(v, cos2, sin2):
    return v * cos2 + pltpu.roll(v, RET_DK // 2, axis=1) * sin2


def _in_proj_kernel(x_ref, m_ref, w_ref, cq_ref, sq_ref, ck_ref, sk_ref,
                    rk_ref, rv_ref, nk_ref, nv_ref, rq_ref, rg_ref, nq_ref, buf_ref):
    buf_ref[...] = jnp.zeros_like(buf_ref)
    shift = m_ref[0, 0:1, :]
    scale = m_ref[0, 1:2, :]
    hm = (x_ref[0] * (1.0 + scale) + shift).astype(BF16)
    outs = (rk_ref, rv_ref, nk_ref, nv_ref, rq_ref, rg_ref, nq_ref)
    for g, o_ref in enumerate(outs):
        p = _dot(hm, w_ref[:, g * HEAD_W:(g + 1) * HEAD_W])
        if g == 0 or g == 4:
            cos2 = (ck_ref if g == 0 else cq_ref)[...]
            sin2 = (sk_ref if g == 0 else sq_ref)[...]
            for hd in range(RET_HEADS):
                sl = slice(hd * RET_DK, (hd + 1) * RET_DK)
                o_ref[0, :, sl] = _rope(p[:, sl], cos2, sin2).astype(BF16)
        elif g == 6:
            o_ref[0] = (p * (NA_DH ** -0.5 * LOG2E)).astype(BF16)
        else:
            o_ref[0] = p.astype(BF16)


def _in_proj(x, m, w_in_bf, rope_tabs, buf_rows):
    b, n, d = x.shape
    tm = ROW_TILE
    tiles_per_seq = n // tm
    buf_tile = buf_rows // (b * tiles_per_seq)
    assert buf_tile * b * tiles_per_seq == buf_rows and buf_tile % 16 == 0
    tab_spec = pl.BlockSpec((tm, RET_DK), lambda i, j: (j, 0))
    out_spec = pl.BlockSpec((1, tm, HEAD_W), lambda i, j: (i, j, 0))
    return pl.pallas_call(
        _in_proj_kernel,
        out_shape=[jax.ShapeDtypeStruct((b, n, HEAD_W), BF16)] * N_IN_GROUPS
        + [jax.ShapeDtypeStruct((buf_rows, d), BF16)],
        grid=(b, tiles_per_seq),
        in_specs=[
            pl.BlockSpec((1, tm, d), lambda i, j: (i, j, 0)),
            pl.BlockSpec((1, N_MOD, d), lambda i, j: (i, 0, 0)),
            pl.BlockSpec((d, N_IN_GROUPS * HEAD_W), lambda i, j: (0, 0)),
            tab_spec, tab_spec, tab_spec, tab_spec,
        ],
        out_specs=[out_spec] * N_IN_GROUPS + [pl.BlockSpec((buf_tile, d), lambda i, j: (i * tiles_per_seq + j, 0))],
        compiler_params=_cparams("arbitrary", "arbitrary"),
        name="in_proj",
    )(x, m, w_in_bf, *rope_tabs)


def _ctx_proj_kernel(x_ref, m_ref, w_ref, rk_ref, rv_ref, nk_ref, nv_ref):
    shift = m_ref[0, 0:1, :]
    scale = m_ref[0, 1:2, :]
    hm = (x_ref[0] * (1.0 + scale) + shift).astype(BF16)
    for g, o_ref in enumerate((rk_ref, rv_ref, nk_ref, nv_ref)):
        o_ref[0] = _dot(hm, w_ref[:, g * HEAD_W:(g + 1) * HEAD_W]).astype(BF16)


def _ctx_proj(ctx, m_ctx, w_in_bf):
    b, l, d = ctx.shape
    out_spec = pl.BlockSpec((1, l, HEAD_W), lambda i: (i, 0, 0))
    return pl.pallas_call(
        _ctx_proj_kernel,
        out_shape=[jax.ShapeDtypeStruct((b, l, HEAD_W), BF16)] * N_KV_GROUPS,
        grid=(b,),
        in_specs=[
            pl.BlockSpec((1, l, d), lambda i: (i, 0, 0)),
            pl.BlockSpec((1, N_MOD, d), lambda i: (0, 0, 0)),
            pl.BlockSpec((d, N_KV_GROUPS * HEAD_W), lambda i: (0, 0)),
        ],
        out_specs=[out_spec] * N_KV_GROUPS,
        compiler_params=_cparams("arbitrary"),
        name="ctx_proj",
    )(ctx, m_ctx, w_in_bf)


def _retention_kernel(lg_ref, q_ref, k_ref, v_ref, g_ref, kc_ref, vc_ref, o_ref, u_ref, s_ref):
    hd = pl.program_id(1)
    n = q_ref.shape[1]
    c = RET_CHUNK
    dk = RET_DK
    nc = n // c
    l = kc_ref.shape[1]
    lgf = lg_ref[0, hd]
    lgb = lg_ref[1, hd]

    ii = lax.broadcasted_iota(jnp.int32, (c, c), 0).astype(F32)
    jj = lax.broadcasted_iota(jnp.int32, (c, c), 1).astype(F32)
    diff = ii - jj
    decay = (jnp.where(diff >= 0, jnp.exp(lgf * jnp.maximum(diff, 0.0)), 0.0)
             + jnp.where(diff <= 0, jnp.exp(lgb * jnp.maximum(-diff, 0.0)), 0.0))
    idx = lax.broadcasted_iota(jnp.int32, (c, 1), 0).astype(F32)
    q_dec_f = jnp.exp(lgf * (idx + 1.0))
    k_dec_f = jnp.exp(lgf * (c - 1.0 - idx))
    q_dec_b = jnp.exp(lgb * (c - idx))
    k_dec_b = jnp.exp(lgb * idx)
    ones = jnp.ones((1, dk), F32)
    chunk_dec_f = jnp.exp(ones * (lgf * c))
    chunk_dec_b = jnp.exp(ones * (lgb * c))

    pos = lax.broadcasted_iota(jnp.int32, (l, 1), 0).astype(F32)
    kc = kc_ref[0].astype(F32)
    vc = vc_ref[0]
    s_f0 = _dot_tn((kc * jnp.exp(lgf * (l - 1.0 - pos))).astype(BF16), vc)
    s_b0 = _dot_tn((kc * jnp.exp(lgb * pos)).astype(BF16), vc)

    def chunk_rows(i):
        return pl.ds(pl.multiple_of(i * c, c), c)

    def kv_step(i, carry):
        rows = chunk_rows(i)
        k_i = k_ref[0, rows, :].astype(F32)
        kk = jnp.concatenate([(k_i * k_dec_f).astype(BF16), (k_i * k_dec_b).astype(BF16)], axis=1)
        u_ref[i] = _dot_tn(kk, v_ref[0, rows, :])
        return carry

    lax.fori_loop(0, nc, kv_step, 0, unroll=True)

    def scan_f(i, s):
        s_ref[i, :, 0:dk] = s.astype(BF16)
        return s * chunk_dec_f + u_ref[i, 0:dk, :]

    lax.fori_loop(0, nc, scan_f, s_f0, unroll=True)

    def scan_b(t, s):
        i = nc - 1 - t
        s_ref[i, :, dk:2 * dk] = s.astype(BF16)
        return s * chunk_dec_b + u_ref[i, dk:2 * dk, :]

    lax.fori_loop(0, nc, scan_b, s_b0, unroll=True)

    def out_step(i, carry):
        rows = chunk_rows(i)
        q_i = q_ref[0, rows, :]
        v_i = v_ref[0, rows, :]
        att = (_dot_nt(q_i, k_ref[0, rows, :]) * decay).astype(BF16)
        inter = _dot(q_i, s_ref[i])
        o = _dot(att, v_i) + inter[:, 0:dk] * q_dec_f + inter[:, dk:2 * dk] * q_dec_b
        mu = jnp.mean(o, axis=-1, keepdims=True)
        var = jnp.mean(jnp.square(o - mu), axis=-1, keepdims=True)
        o_n = (o - mu) * lax.rsqrt(var + LN_EPS)
        gate = g_ref[0, rows, :].astype(F32)
        o_ref[0, rows, :] = (_silu(gate) * o_n).astype(BF16)
        return carry

    lax.fori_loop(0, nc, out_step, 0, unroll=True)


def _retention(log_gamma2, rq, rk, rv, rg, rk_c, rv_c):
    b, n, _ = rq.shape
    l = rk_c.shape[1]
    nc = n // RET_CHUNK
    seq_spec = pl.BlockSpec((1, n, RET_DK), lambda i, h: (i, 0, h))
    ctx_spec = pl.BlockSpec((1, l, RET_DK), lambda i, h: (i, 0, h))
    return pl.pallas_call(
        _retention_kernel,
        out_shape=jax.ShapeDtypeStruct((b, n, HEAD_W), BF16),
        grid=(b, RET_HEADS),
        in_specs=[pl.BlockSpec(memory_space=pltpu.SMEM),
                  seq_spec, seq_spec, seq_spec, seq_spec, ctx_spec, ctx_spec],
        out_specs=seq_spec,
        scratch_shapes=[pltpu.VMEM((nc, 2 * RET_DK, RET_DK), F32),
                        pltpu.VMEM((nc, RET_DK, 2 * RET_DK), BF16)],
        compiler_params=_cparams("arbitrary", "arbitrary"),
        name="retention",
    )(log_gamma2, rq, rk, rv, rg, rk_c, rv_c)


def _na_geometry(rows):
    rb = NA_ROWS_PER_BLOCK
    wr = min(NA_WIN_R, rows)
    key_rows = min(rows, rb + wr - 1)
    variants, block_variant, block_start = [], [], []
    for kb in range(rows // rb):
        q_rows = kb * rb + np.arange(rb)
        r0 = np.clip(q_rows - wr // 2, 0, rows - wr)
        ks = int(np.clip(r0.min(), 0, rows - key_rows))
        assert r0.max() + wr <= ks + key_rows
        kr = ks + np.arange(key_rows)
        valid = (kr[None, :] >= r0[:, None]) & (kr[None, :] < r0[:, None] + wr)
        ridx = np.where(valid, kr[None, :] - q_rows[:, None] + NA_WIN_R - 1, N_BIAS_ROWS).astype(np.int32)
        for vi, v in enumerate(variants):
            if np.array_equal(v, ridx):
                block_variant.append(vi)
                break
        else:
            variants.append(ridx)
            block_variant.append(len(variants) - 1)
        block_start.append(ks)
    return key_rows, np.stack(variants), block_variant, block_start


def _na_bias_table(rpb, variant_rows):
    n_heads = rpb.shape[0]
    qc = np.arange(GRID_W)[:, None]
    kc = np.arange(GRID_W)[None, :]
    wstart = np.clip(qc - NA_WIN_C // 2, 0, GRID_W - NA_WIN_C)
    col_ok = (kc >= wstart) & (kc < wstart + NA_WIN_C)
    cidx = np.clip(kc - qc + NA_WIN_C - 1, 0, 2 * NA_WIN_C - 2)
    onehot = ((cidx[None] == np.arange(2 * NA_WIN_C - 1)[:, None, None]) & col_ok[None]).astype(np.float32)
    blocks = jnp.einsum("hrd,dqk->hqrk", rpb.astype(F32) * LOG2E, jnp.asarray(onehot), precision=HIGHEST)
    blocks = jnp.where(col_ok[None, :, None, :], blocks, MASK_VALUE)
    masked = jnp.full((n_heads, GRID_W, GRID_W), MASK_VALUE, F32)
    nvar, rb, key_rows = variant_rows.shape
    rows = [jnp.concatenate([masked if r == N_BIAS_ROWS else blocks[:, :, r, :] for r in variant_rows[v, a]], axis=-1)
            for v in range(nvar) for a in range(rb)]
    return jnp.stack(rows, axis=1).reshape(n_heads, nvar, rb * GRID_W, key_rows * GRID_W)


def _na_kernel(q_ref, k_ref, v_ref, kc_ref, vc_ref, bias_ref, o_ref, *, block_variant, block_start, key_rows):
    qb = NA_ROWS_PER_BLOCK * GRID_W
    nk = key_rows * GRID_W
    first_head = lax.broadcasted_iota(jnp.int32, (qb, LANES), 1) < NA_DH
    kc = kc_ref[0]
    vc = vc_ref[0]
    for kb, (var, ks) in enumerate(zip(block_variant, block_start)):
        q = q_ref[0, kb * qb:(kb + 1) * qb, :]
        k_win = k_ref[0, ks * GRID_W:ks * GRID_W + nk, :]
        v_win = v_ref[0, ks * GRID_W:ks * GRID_W + nk, :]
        zero = jnp.zeros_like(q)
        q2 = jnp.concatenate([jnp.where(first_head, q, zero), jnp.where(first_head, zero, q)], axis=0)
        s_win = _dot_nt(q2, k_win) + jnp.concatenate([bias_ref[0, var], bias_ref[1, var]], axis=0)
        s_ctx = _dot_nt(q2, kc)
        m = jnp.maximum(jnp.max(s_win, axis=-1, keepdims=True), jnp.max(s_ctx, axis=-1, keepdims=True))
        p_win = jnp.exp2(s_win - m)
        p_ctx = jnp.exp2(s_ctx - m)
        denom = jnp.sum(p_win, axis=-1, keepdims=True) + jnp.sum(p_ctx, axis=-1, keepdims=True)
        o = (_dot(p_win.astype(BF16), v_win) + _dot(p_ctx.astype(BF16), vc)) / denom
        o_ref[0, kb * qb:(kb + 1) * qb, :] = jnp.where(first_head, o[0:qb], o[qb:2 * qb]).astype(BF16)


def _neighbourhood_attention(nq, nk, nv, nk_c, nv_c, rpb):
    b, n, _ = nq.shape
    l = nk_c.shape[1]
    rows = n // GRID_W
    key_rows, variant_rows, block_variant, block_start = _na_geometry(rows)
    bias = _na_bias_table(rpb, variant_rows)
    nvar = variant_rows.shape[0]
    qb = NA_ROWS_PER_BLOCK * GRID_W
    nkeys = key_rows * GRID_W
    seq_spec = pl.BlockSpec((1, n, LANES), lambda h, i: (i, 0, h))
    ctx_spec = pl.BlockSpec((1, l, LANES), lambda h, i: (i, 0, h))
    kern = functools.partial(_na_kernel, block_variant=tuple(block_variant), block_start=tuple(block_start),
                             key_rows=key_rows)
    return pl.pallas_call(
        kern,
        out_shape=jax.ShapeDtypeStruct((b, n, HEAD_W), BF16),
        grid=(NA_HEADS // 2, b),
        in_specs=[seq_spec, seq_spec, seq_spec, ctx_spec, ctx_spec,
                  pl.BlockSpec((2, nvar, qb, nkeys), lambda h, i: (h, 0, 0, 0))],
        out_specs=seq_spec,
        compiler_params=_cparams("arbitrary", "arbitrary"),
        name="neighbourhood_attention",
    )(nq, nk, nv, nk_c, nv_c, bias)


def _layer_norm(z, g, b):
    mu = jnp.mean(z, axis=-1, keepdims=True)
    var = jnp.mean(jnp.square(z - mu), axis=-1, keepdims=True)
    return (z - mu) * lax.rsqrt(var + LN_EPS) * g + b


def _split_bf16(v):
    hi = v.astype(BF16)
    return hi, (v - hi.astype(F32)).astype(BF16)


def _route(t, wr_ref, br_ref):
    t_hi, t_lo = _split_bf16(t)
    w_hi, w_lo = _split_bf16(wr_ref[...])
    hi = _dot(t_hi, jnp.concatenate([w_hi, w_lo], axis=1))
    logits = hi[:, 0:LANES] + (_dot(t_lo, w_hi) + hi[:, LANES:2 * LANES]) + br_ref[...]
    lane = lax.broadcasted_iota(jnp.int32, logits.shape, 1).astype(F32)
    neg = -jnp.inf
    big = float(LANES)

    def first_max(vals):
        vmax = jnp.max(vals, axis=-1, keepdims=True)
        return vmax, jnp.min(jnp.where(vals == vmax, lane, big), axis=-1, keepdims=True)

    is_grp = lane < MOE_GROUPS
    g_max, grp = first_max(jnp.where(is_grp, logits, neg))
    g_sum = jnp.sum(jnp.where(is_grp, jnp.exp(logits - g_max), 0.0), axis=-1, keepdims=True)
    gate_g = 1.0 / g_sum
    lo = MOE_GROUPS + grp * MOE_PER_GROUP
    in_grp = (lane >= lo) & (lane < lo + MOE_PER_GROUP)
    le = jnp.where(in_grp, logits, neg)
    v1, i1 = first_max(le)
    v2, i2 = first_max(jnp.where(lane == i1, neg, le))
    e21 = jnp.exp(v2 - v1)
    w1 = gate_g / (1.0 + e21)
    w2 = gate_g * e21 / (1.0 + e21)
    return jnp.where(lane == 0, w1,
                     jnp.where(lane == 1, w2,
                               jnp.where(lane == 2, i1 - MOE_GROUPS,
                                         jnp.where(lane == 3, i2 - MOE_GROUPS, 0.0))))


def _post_norm_route(h, y, m_ref, ln_ref, wr_ref, br_ref, h_out, t_out, r_out):
    gate = m_ref[0, 2:3, :]
    shift = m_ref[0, 3:4, :]
    scale = m_ref[0, 4:5, :]
    h1 = _layer_norm(DEEPNORM_ALPHA * h + gate * y, ln_ref[0:1, :], ln_ref[1:2, :])
    t = h1 * (1.0 + scale) + shift
    h_out[0] = h1
    t_out[0] = t.astype(BF16)
    r_out[0] = _route(t, wr_ref, br_ref)


def _epilogue_specs(b, n, d, tm):
    in_specs = [
        pl.BlockSpec((1, N_MOD, d), lambda i, j: (i, 0, 0)),
        pl.BlockSpec((2, d), lambda i, j: (0, 0)),
        pl.BlockSpec((d, LANES), lambda i, j: (0, 0)),
        pl.BlockSpec((1, LANES), lambda i, j: (0, 0)),
    ]
    out_shape = [jax.ShapeDtypeStruct((b, n, d), F32), jax.ShapeDtypeStruct((b, n, d), BF16),
                 jax.ShapeDtypeStruct((b, n, LANES), F32)]
    out_specs = [pl.BlockSpec((1, tm, d), lambda i, j: (i, j, 0)),
                 pl.BlockSpec((1, tm, d), lambda i, j: (i, j, 0)),
                 pl.BlockSpec((1, tm, LANES), lambda i, j: (i, j, 0))]
    return in_specs, out_shape, out_specs


def _out_proj_kernel(x_ref, yr_ref, yn_ref, w_ref, m_ref, ln_ref, wr_ref, br_ref, h_out, t_out, r_out):
    y = _dot(yr_ref[0], w_ref[0:HEAD_W, :]) + _dot(yn_ref[0], w_ref[HEAD_W:2 * HEAD_W, :])
    _post_norm_route(x_ref[0], y, m_ref, ln_ref, wr_ref, br_ref, h_out, t_out, r_out)


def _out_proj(x, y_ret, y_na, w_out_bf, m, ln, router):
    b, n, d = x.shape
    tm = EPILOGUE_TILE
    ep_in, out_shape, out_specs = _epilogue_specs(b, n, d, tm)
    return pl.pallas_call(
        _out_proj_kernel,
        out_shape=out_shape,
        grid=(b, n // tm),
        in_specs=[
            pl.BlockSpec((1, tm, d), lambda i, j: (i, j, 0)),
            pl.BlockSpec((1, tm, HEAD_W), lambda i, j: (i, j, 0)),
            pl.BlockSpec((1, tm, HEAD_W), lambda i, j: (i, j, 0)),
            pl.BlockSpec((2 * HEAD_W, d), lambda i, j: (0, 0)),
        ] + ep_in,
        out_specs=out_specs,
        compiler_params=_cparams("arbitrary", "arbitrary"),
        name="out_proj",
    )(x, y_ret, y_na, w_out_bf, m, ln, *router)


POOL_HALO = max(POOL_SIZES) // 2
POOL_TILE = EPILOGUE_TILE


def _pool_kernel(x_ref, prev_ref, next_ref, pw_ref, ps_ref, m_ref, ln_ref, wr_ref, br_ref,
                 h_out, t_out, r_out, *, n):
    j = pl.program_id(1)
    nj = pl.num_programs(1)
    tm = x_ref.shape[1]
    halo = POOL_HALO
    shift = m_ref[0, 0:1, :]
    scale = m_ref[0, 1:2, :]
    x = x_ref[0]
    hm = x * (1.0 + scale) + shift
    prev = jnp.where(j > 0, prev_ref[0] * (1.0 + scale) + shift, 0.0)
    nxt = jnp.where(j < nj - 1, next_ref[0] * (1.0 + scale) + shift, 0.0)
    ext = jnp.concatenate([prev, hm, nxt], axis=0)
    pos = (j * tm + lax.broadcasted_iota(jnp.int32, (tm, 1), 0))
    ys = []
    for g, w in enumerate(POOL_SIZES):
        cols = slice(g * POOL_GROUP, (g + 1) * POOL_GROUP)
        s = ext[:, cols]
        span = 1
        while span < w:
            s = s[:s.shape[0] - span] + s[span:]
            span *= 2
        off = halo - w // 2
        win = s[off:off + tm]
        cnt = (jnp.minimum(pos + (w - w // 2), n) - jnp.maximum(pos - w // 2, 0)).astype(F32)
        z = (win / cnt - hm[:, cols]).astype(BF16)
        ys.append(_dot(z, pw_ref[g]))
    y = jnp.concatenate(ys, axis=-1) * ps_ref[...]
    _post_norm_route(x, y, m_ref, ln_ref, wr_ref, br_ref, h_out, t_out, r_out)


def _pool_mixer(h, pool_w_bf, pool_scale, m, ln, router):
    b, n, d = h.shape
    tm = POOL_TILE
    halo = POOL_HALO
    blocks_per_tile = tm // halo
    n_halo_blocks = n // halo
    ep_in, out_shape, out_specs = _epilogue_specs(b, n, d, tm)
    return pl.pallas_call(
        functools.partial(_pool_kernel, n=n),
        out_shape=out_shape,
        grid=(b, n // tm),
        in_specs=[
            pl.BlockSpec((1, tm, d), lambda i, j: (i, j, 0)),
            pl.BlockSpec((1, halo, d), lambda i, j: (i, jnp.maximum(j * blocks_per_tile - 1, 0), 0)),
            pl.BlockSpec((1, halo, d),
                         lambda i, j: (i, jnp.minimum((j + 1) * blocks_per_tile, n_halo_blocks - 1), 0)),
            pl.BlockSpec((len(POOL_SIZES), POOL_GROUP, POOL_GROUP), lambda i, j: (0, 0, 0)),
            pl.BlockSpec((1, d), lambda i, j: (0, 0)),
        ] + ep_in,
        out_specs=out_specs,
        compiler_params=_cparams("arbitrary", "arbitrary"),
        name="pool_mixer",
    )(h, h, h, pool_w_bf, pool_scale.reshape(1, d), m, ln, *router)


def _expert_kernel(tile_ref, exp_ref, lo_ref, hi_ref, x_ref, wg_ref, wu_ref, wd_ref, buf_ref, o_ref,
                   wgu_s, wd_s):
    del buf_ref
    g = pl.program_id(0)
    prev_e = exp_ref[jnp.maximum(g - 1, 0)]

    @pl.when((g == 0) | (exp_ref[g] != prev_e))
    def _():
        wgu_s[:, 0:MOE_HIDDEN] = wg_ref[0, 0].astype(BF16)
        wgu_s[:, MOE_HIDDEN:2 * MOE_HIDDEN] = wu_ref[0, 0].astype(BF16)
        wd_s[...] = wd_ref[0, 0].astype(BF16)

    lo = lo_ref[g]
    hi = hi_ref[g]

    def expert_rows():
        x = x_ref[...]
        gate_up = _dot(x, wgu_s[...])
        act = (_silu(gate_up[:, 0:MOE_HIDDEN]) * gate_up[:, MOE_HIDDEN:2 * MOE_HIDDEN]).astype(BF16)
        return _dot(act, wd_s[...]).astype(BF16)

    @pl.when((hi > lo) & (lo == 0))
    def _():
        row = lax.broadcasted_iota(jnp.int32, (o_ref.shape[0], 1), 0)
        o_ref[...] = jnp.where(row < hi, expert_rows(), jnp.zeros((), BF16))

    @pl.when((hi > lo) & (lo > 0))
    def _():
        row = lax.broadcasted_iota(jnp.int32, (o_ref.shape[0], 1), 0)
        o_ref[...] = jnp.where((row >= lo) & (row < hi), expert_rows(), o_ref[...])


def _experts(steps, x_chunk, y_prev, w_gate, w_up, w_down, layer, chunk):
    pc, d = x_chunk.shape
    tm = MOE_TILE
    tiles = pc // tm
    hid = w_gate.shape[-1]
    n_steps = steps[0].shape[0]
    w_spec = lambda shape: pl.BlockSpec(shape, lambda g, tile, exp, lo, hi: (layer, exp[g], 0, 0))
    in_specs = [
        pl.BlockSpec((tm, d), lambda g, tile, exp, lo, hi: (tile[g], 0)),
        w_spec((1, 1, d, hid)), w_spec((1, 1, d, hid)), w_spec((1, 1, hid, d)),
        pl.BlockSpec(memory_space=pl.ANY),
    ]
    args = [*steps, x_chunk, w_gate, w_up, w_down, y_prev]
    grid_spec = pltpu.PrefetchScalarGridSpec(
        num_scalar_prefetch=4,
        grid=(n_steps,),
        in_specs=in_specs,
        out_specs=pl.BlockSpec((tm, d), lambda g, tile, exp, lo, hi: (chunk * tiles + tile[g], 0)),
        scratch_shapes=[pltpu.VMEM((d, 2 * hid), BF16), pltpu.VMEM((hid, d), BF16)],
    )
    return pl.pallas_call(
        _expert_kernel,
        out_shape=jax.ShapeDtypeStruct(y_prev.shape, BF16),
        grid_spec=grid_spec,
        input_output_aliases={len(args) - 1: 0},
        compiler_params=_cparams("arbitrary"),
        name="experts",
    )(*args)


ASSIGN_BITS = 16


def _dispatch_plan(route, tm, n_chunks):
    tok = route.shape[0]
    n_assign = 2 * tok
    assert n_assign <= 1 << ASSIGN_BITS
    i32 = jnp.int32
    low_mask = (1 << ASSIGN_BITS) - 1
    eid = jnp.transpose(route[:, 2:4]).astype(i32).reshape(n_assign)
    experts = jnp.arange(MOE_EXPERTS, dtype=i32)
    counts = jnp.sum((eid[:, None] == experts[None, :]).astype(i32), axis=0)
    end = jnp.cumsum(counts)
    start = end - counts
    order = lax.sort((eid << ASSIGN_BITS) | jnp.arange(n_assign, dtype=i32)) & low_mask
    tok_of_pos = order % tok
    _, pos_of_assign = lax.sort((order, jnp.arange(n_assign, dtype=i32)), num_keys=1)
    rows_per_chunk = n_assign // n_chunks
    tiles_per_chunk = rows_per_chunk // tm
    chunk_lo = (jnp.arange(n_chunks, dtype=i32) * rows_per_chunk)[:, None]
    tile_starts = chunk_lo + jnp.arange(tiles_per_chunk, dtype=i32)[None, :] * tm
    cuts = jnp.concatenate([tile_starts, jnp.clip(start[None, :], chunk_lo, chunk_lo + rows_per_chunk)], axis=1)
    cuts = jnp.sort(cuts, axis=1)
    nxt = jnp.concatenate([cuts[:, 1:], chunk_lo + rows_per_chunk], axis=1)
    tile = jnp.minimum((cuts - chunk_lo) // tm, tiles_per_chunk - 1)
    tile_row = chunk_lo + tile * tm
    lo = cuts - tile_row
    hi = nxt - tile_row
    first_row = jnp.minimum(cuts, chunk_lo + rows_per_chunk - 1)
    expert = jnp.sum((end[None, None, :] <= first_row[:, :, None]).astype(i32), axis=2)
    return (tile, expert, lo, hi), tok_of_pos, pos_of_assign


def _n_chunks(b):
    return 4 if b % 4 == 0 else (2 if b % 2 == 0 else 1)


def _post_norm_kernel(h_ref, y0_ref, y1_ref, r_ref, m_ref, ln_ref, o_ref):
    gate = m_ref[0, 5:6, :]
    r = r_ref[0]
    y = r[:, 0:1] * y0_ref[...].astype(F32) + r[:, 1:2] * y1_ref[...].astype(F32)
    o_ref[0] = _layer_norm(DEEPNORM_ALPHA * h_ref[0] + gate * y, ln_ref[0:1, :], ln_ref[1:2, :])


def _post_norm(h, y2, route, m, ln, chunk, n_chunks):
    b, n, d = h.shape
    tm = ROW_TILE
    bc = b // n_chunks
    b0 = chunk * bc
    tiles_per_seq = n // tm
    tiles_per_k = bc * tiles_per_seq
    row_spec = pl.BlockSpec((1, tm, d), lambda i, j: (b0 + i, j, 0))
    return pl.pallas_call(
        _post_norm_kernel,
        out_shape=jax.ShapeDtypeStruct((b, n, d), F32),
        grid=(bc, tiles_per_seq),
        in_specs=[row_spec,
                  pl.BlockSpec((tm, d), lambda i, j: (i * tiles_per_seq + j, 0)),
                  pl.BlockSpec((tm, d), lambda i, j: (tiles_per_k + i * tiles_per_seq + j, 0)),
                  pl.BlockSpec((1, tm, LANES), lambda i, j: (b0 + i, j, 0)),
                  pl.BlockSpec((1, N_MOD, d), lambda i, j: (b0 + i, 0, 0)),
                  pl.BlockSpec((2, d), lambda i, j: (0, 0))],
        out_specs=row_spec,
        input_output_aliases={0: 0},
        compiler_params=_cparams("arbitrary", "arbitrary"),
        name="post_norm",
    )(h, y2, y2, route, m, ln)


def _moe_rows(tok):
    return 2 * tok


def _moe_post_norm(h1, t, route, m, ln, w_gate, w_up, w_down, layer, row_buf):
    b, n, d = t.shape
    tok = b * n
    nch = _n_chunks(b)
    steps, tok_of_pos, pos_of_assign = _dispatch_plan(route.reshape(tok, LANES), MOE_TILE, nch)
    t2 = t.reshape(tok, d)
    pc = 2 * tok // nch
    y_sorted = row_buf
    for c in range(nch):
        x_c = t2.at[tok_of_pos[c * pc:(c + 1) * pc]].get(mode="promise_in_bounds")
        y_sorted = _experts(tuple(v[c] for v in steps), x_c, y_sorted, w_gate, w_up, w_down, layer, c)
    bc = b // nch
    pos3 = pos_of_assign.reshape(2, b, n)
    h = h1
    for c in range(nch):
        pos_c = pos3[:, c * bc:(c + 1) * bc].reshape(2 * bc * n)
        y2_c = y_sorted.at[pos_c].get(mode="promise_in_bounds")
        h = _post_norm(h, y2_c, route, m, ln, c, nch)
    return h, y_sorted


def _rope_tables(n):
    t = jnp.arange(n)
    rows = (t // GRID_W).astype(F32)
    cols = (t % GRID_W).astype(F32)
    n_freq = RET_DK // 4
    inv_freq = ROPE_BASE ** (-jnp.arange(n_freq, dtype=F32) / n_freq)
    ang = jnp.concatenate([rows[:, None] * inv_freq, cols[:, None] * inv_freq], axis=-1)
    cos, sin = jnp.cos(ang), jnp.sin(ang)
    cos2 = jnp.concatenate([cos, cos], axis=-1)
    sin2 = jnp.concatenate([-sin, sin], axis=-1)
    q_scale = RET_DK ** -0.5
    return cos2 * q_scale, sin2 * q_scale, cos2, sin2


def _router_params(w_r1, b_r1, w_r2, b_r2):
    d = w_r1.shape[0]
    w2 = jnp.transpose(w_r2, (1, 0, 2)).reshape(d, MOE_EXPERTS)
    pad = LANES - MOE_GROUPS - MOE_EXPERTS
    wr = jnp.concatenate([w_r1, w2, jnp.zeros((d, pad), F32)], axis=-1)
    br = jnp.concatenate([b_r1, b_r2.reshape(MOE_EXPERTS), jnp.zeros((pad,), F32)]).reshape(1, LANES)
    return wr, br


def kernel(x, c, ctx, c_ctx, w_mod, b_mod, ln_g, ln_b, ab_w_in, ab_w_out, ab_log_decay, ab_rpb, pool_w, pool_scale, moe_w_r1, moe_b_r1, moe_w_r2, moe_b_r2, moe_w_gate, moe_w_up, moe_w_down):
    b, n, d = x.shape
    cc = jnp.concatenate([c, c_ctx[None, :], jnp.zeros((MOD_ROWS - b - 1, d), F32)], axis=0)
    mod = _modulation(cc, w_mod, b_mod)
    h = x
    row_buf = None
    for i in range(DEPTH):
        j = i // 2
        m = mod[i, :b].reshape(b, N_MOD, d)
        ln1 = jnp.stack([ln_g[i, 0], ln_b[i, 0]])
        ln2 = jnp.stack([ln_g[i, 1], ln_b[i, 1]])
        router = _router_params(moe_w_r1[i], moe_b_r1[i], moe_w_r2[i], moe_b_r2[i])
        if i % 2 == 0:
            m_ctx = mod[i, b].reshape(1, N_MOD, d)
            w_in_bf = ab_w_in[j].astype(BF16)
            log_gamma2 = jnp.log1p(-jnp.exp(ab_log_decay[j].astype(F32)))
            rk, rv, nk, nv, rq, rg, nq, row_buf = _in_proj(h, m, w_in_bf, _rope_tables(n), _moe_rows(b * n))
            rk_c, rv_c, nk_c, nv_c = _ctx_proj(ctx, m_ctx, w_in_bf[:, :N_KV_GROUPS * HEAD_W])
            y_ret = _retention(log_gamma2, rq, rk, rv, rg, rk_c, rv_c)
            y_na = _neighbourhood_attention(nq, nk, nv, nk_c, nv_c, ab_rpb[j])
            h1, t, route = _out_proj(h, y_ret, y_na, ab_w_out[j].astype(BF16), m, ln1, router)
        else:
            h1, t, route = _pool_mixer(h, pool_w[j].astype(BF16), pool_scale[j], m, ln1, router)
        h, row_buf = _moe_post_norm(h1, t, route, m, ln2, moe_w_gate, moe_w_up, moe_w_down, i, row_buf)
    return h
```
